```python
import jax
import jax.numpy as jnp
from jax import lax
import numpy as np

D_MODEL = 1024
BATCH = 32
SEQ = 2048
DEPTH = 1

HEAD_DIM = 64
N_HEADS_DIL = 8
DIL_PATTERNS = ((128, 1), (512, 4), (2048, 16))
DIL_BLOCK = 128
N_HEADS_NSA = 8
N_KV_NSA = 2
CMP_STRIDE = 16
CMP_LEN = 2 * CMP_STRIDE
CMP_HIDDEN = 256
SEL_LEN = 64
N_SEL = 8
SEL_QCHUNK = 64
WIN = 512
WIN_BLOCK = 128
ROPE_THETA = 500000.0
ROT_DIM = HEAD_DIM // 4
N_GROUPS = 8
EXPERTS_PER_GROUP = 8
N_EXPERTS = N_GROUPS * EXPERTS_PER_GROUP
TOP_K_INNER = 2
D_FF_EXPERT = 256
EPS = 1e-6
NEG = -1e30
BIG = 1e9
SPLIT_SIZES = (N_HEADS_DIL * HEAD_DIM, N_HEADS_DIL * HEAD_DIM, N_HEADS_DIL * HEAD_DIM,
               N_HEADS_NSA * HEAD_DIM,
               N_KV_NSA * HEAD_DIM, N_KV_NSA * HEAD_DIM, N_KV_NSA * HEAD_DIM,
               N_KV_NSA * HEAD_DIM, N_KV_NSA * HEAD_DIM, N_KV_NSA * HEAD_DIM,
               N_HEADS_NSA * 3)
D_IN = sum(SPLIT_SIZES)
D_MIX = (N_HEADS_DIL + N_HEADS_NSA) * HEAD_DIM

kernel_name = 'hymba_dilated_nsa_hmoe_block'


def rmsnorm(x, g):
    xf = x.astype(jnp.float32)
    y = xf * lax.rsqrt(jnp.mean(xf * xf, axis=-1, keepdims=True) + EPS) * g.astype(jnp.float32)
    return y.astype(x.dtype)


def rope_tables(pos):
    inv_freq = ROPE_THETA ** (-jnp.arange(0, ROT_DIM, 2, dtype=jnp.float32) / ROT_DIM)
    ang = pos.astype(jnp.float32)[:, None] * inv_freq[None, :]
    return jnp.cos(ang), jnp.sin(ang)


def apply_rope(x, cos, sin):
    half = ROT_DIM // 2
    xf = x.astype(jnp.float32)
    x1, x2 = xf[..., :half], xf[..., half:ROT_DIM]
    out = jnp.concatenate([x1 * cos - x2 * sin, x2 * cos + x1 * sin, xf[..., ROT_DIM:]], axis=-1)
    return out.astype(x.dtype)


def banded_attention(q, k, v, max_dist, block):
    B_, G, R, L, hd = q.shape
    n_prev = -(-max_dist // block)
    nb = -(-L // block)
    Lp = nb * block
    span = (n_prev + 1) * block
    qp = jnp.pad(q, ((0, 0), (0, 0), (0, 0), (0, Lp - L), (0, 0)))
    kp = jnp.pad(k, ((0, 0), (0, 0), (n_prev * block, Lp - L), (0, 0)))
    vp = jnp.pad(v, ((0, 0), (0, 0), (n_prev * block, Lp - L), (0, 0)))
    scale = hd ** -0.5

    def one_block(i):
        start = i * block
        qb = lax.dynamic_slice_in_dim(qp, start, block, axis=3)
        kb = lax.dynamic_slice_in_dim(kp, start, span, axis=2)
        vb = lax.dynamic_slice_in_dim(vp, start, span, axis=2)
        qpos = start + jnp.arange(block)
        kpos = start - n_prev * block + jnp.arange(span)
        dist = qpos[:, None] - kpos[None, :]
        mask = (dist >= 0) & (dist <= max_dist) & (kpos[None, :] >= 0)
        s = jnp.einsum('bgrqd,bgkd->bgrqk', qb, kb).astype(jnp.float32) * scale
        s = jnp.where(mask, s, NEG)
        m = jnp.max(s, axis=-1, keepdims=True)
        p = jnp.exp(s - m)
        l = jnp.sum(p, axis=-1, keepdims=True)
        o = jnp.einsum('bgrqk,bgkd->bgrqd', p.astype(vb.dtype), vb).astype(jnp.float32) / l
        return o.astype(q.dtype), (m + jnp.log(l))[..., 0]

    o, lse = lax.map(one_block, jnp.arange(nb))
    o = jnp.moveaxis(o, 0, 3).reshape(B_, G, R, Lp, hd)[:, :, :, :L]
    lse = jnp.moveaxis(lse, 0, 3).reshape(B_, G, R, Lp)[..., :L]
    return o, lse


def dilated_mixer(q, k, v):
    B_, H, S, hd = q.shape
    outs, lses = [], []
    for window, dil in DIL_PATTERNS:
        L = S // dil

        def to_sub(t, L=L, dil=dil):
            return t.reshape(B_, H, L, dil, hd).transpose(0, 1, 3, 2, 4).reshape(B_, H * dil, L, hd)

        o, lse = banded_attention(to_sub(q)[:, :, None], to_sub(k), to_sub(v), window // dil, DIL_BLOCK)
        outs.append(o[:, :, 0].reshape(B_, H, dil, L, hd).transpose(0, 1, 3, 2, 4).reshape(B_, H, S, hd))
        lses.append(lse[:, :, 0].reshape(B_, H, dil, L).transpose(0, 1, 3, 2).reshape(B_, H, S))
    w = jax.nn.softmax(jnp.stack(lses, axis=0), axis=0)
    out = jnp.sum(w[..., None] * jnp.stack(outs, axis=0).astype(jnp.float32), axis=0)
    return out.astype(q.dtype)


def compress_kv(kv, pe, w1, w2):
    B_, S, G, hd = kv.shape
    chunks = kv.reshape(B_, S // CMP_STRIDE, CMP_STRIDE, G, hd)
    blocks = jnp.concatenate([chunks[:, :-1], chunks[:, 1:]], axis=2) + pe[None, None, :, None, :]
    n_c = blocks.shape[1]
    flat = blocks.transpose(0, 1, 3, 2, 4).reshape(B_, n_c, G, CMP_LEN * hd)
    out = jax.nn.gelu(flat @ w1) @ w2
    return out.transpose(0, 2, 1, 3)


def nsa_mixer(q, kc_raw, vc_raw, ks, vs, kw, vw, gate_logits, pe_kc, w_kc1, w_kc2, pe_vc, w_vc1, w_vc2):
    B_, G, R, S, hd = q.shape
    scale = hd ** -0.5
    t = jnp.arange(S)

    n_c = S // CMP_STRIDE - 1
    end_pos = jnp.arange(n_c) * CMP_STRIDE + CMP_LEN - 1
    cos_c, sin_c = rope_tables(end_pos)
    kc = apply_rope(compress_kv(kc_raw, pe_kc, w_kc1, w_kc2), cos_c, sin_c)
    vc = compress_kv(vc_raw, pe_vc, w_vc1, w_vc2)
    cmask = end_pos[None, :] <= t[:, None]
    s = jnp.einsum('bgrsd,bgcd->bgrsc', q, kc).astype(jnp.float32) * scale
    s = jnp.where(cmask, s, NEG)
    m = jnp.max(s, axis=-1, keepdims=True)
    p = jnp.where(cmask, jnp.exp(s - m), 0.0)
    l = jnp.sum(p, axis=-1, keepdims=True)
    p_cmp = p / jnp.where(l > 0, l, 1.0)
    o_cmp = jnp.einsum('bgrsc,bgcd->bgrsd', p_cmp.astype(vc.dtype), vc)

    n_s = S // SEL_LEN
    cs = jnp.arange(n_c) * CMP_STRIDE
    js = jnp.arange(n_s) * SEL_LEN
    overlap = jnp.clip(jnp.minimum(cs[:, None] + CMP_LEN, js[None, :] + SEL_LEN)
                       - jnp.maximum(cs[:, None], js[None, :]), 0, None).astype(jnp.float32) / CMP_LEN
    imp = jnp.einsum('bgrsc,cj->bgsj', p_cmp, overlap)
    j = jnp.arange(n_s)[None, :]
    blk_t = (t // SEL_LEN)[:, None]
    forced = (j == 0) | (j == blk_t) | (j == blk_t - 1)
    valid = j * SEL_LEN <= t[:, None]
    imp = jnp.where(valid, jnp.where(forced, BIG, imp), -BIG)
    k_sel = min(N_SEL, n_s)
    _, idx = lax.top_k(imp, k_sel)

    ks_blocks = ks.reshape(B_, G, n_s, SEL_LEN, hd)
    vs_blocks = vs.reshape(B_, G, n_s, SEL_LEN, hd)
    gather = jax.vmap(jax.vmap(lambda blk, ids: blk[ids]))

    def sel_chunk(c):
        q0 = c * SEL_QCHUNK
        qc = lax.dynamic_slice_in_dim(q, q0, SEL_QCHUNK, axis=3)
        ic = lax.dynamic_slice_in_dim(idx, q0, SEL_QCHUNK, axis=2)
        kb = gather(ks_blocks, ic)
        vb = gather(vs_blocks, ic)
        kpos = ic[..., None] * SEL_LEN + jnp.arange(SEL_LEN)
        qpos = q0 + jnp.arange(SEL_QCHUNK)
        msk = kpos <= qpos[None, None, :, None, None]
        sc = jnp.einsum('bgrqd,bgqkjd->bgrqkj', qc, kb).astype(jnp.float32) * scale
        sc = jnp.where(msk[:, :, None], sc, NEG).reshape(B_, G, R, SEL_QCHUNK, k_sel * SEL_LEN)
        pc = jax.nn.softmax(sc, axis=-1)
        return jnp.einsum('bgrqn,bgqnd->bgrqd', pc.astype(vb.dtype),
                          vb.reshape(B_, G, SEL_QCHUNK, k_sel * SEL_LEN, hd))

    o_sel = lax.map(sel_chunk, jnp.arange(S // SEL_QCHUNK))
    o_sel = jnp.moveaxis(o_sel, 0, 3).reshape(B_, G, R, S, hd)

    o_win, _ = banded_attention(q, kw, vw, WIN - 1, WIN_BLOCK)

    g = jax.nn.sigmoid(gate_logits.astype(jnp.float32)).reshape(B_, S, G, R, 3).transpose(0, 2, 3, 1, 4)
    out = (g[..., 0:1] * o_cmp.astype(jnp.float32) + g[..., 1:2] * o_sel.astype(jnp.float32)
           + g[..., 2:3] * o_win.astype(jnp.float32))
    return out.astype(q.dtype)


def hier_moe(h, w_rg, b_rg, w_re, b_re, w_gate, w_up, w_down):
    T = h.shape[0]
    pg = jax.nn.softmax((h @ w_rg + b_rg).astype(jnp.float32), axis=-1)
    g_star = jnp.argmax(pg, axis=-1)
    p_g = jnp.max(pg, axis=-1, keepdims=True)
    g_onehot = jax.nn.one_hot(g_star, N_GROUPS, dtype=jnp.float32)
    el = (jnp.einsum('td,dge->tge', h, w_re) + b_re).astype(jnp.float32)
    el_sel = jnp.einsum('tge,tg->te', el, g_onehot)
    pe = jax.nn.softmax(el_sel, axis=-1)
    top_v, top_i = lax.top_k(pe, TOP_K_INNER)
    top_w = top_v / jnp.sum(top_v, axis=-1, keepdims=True) * p_g
    inner = jnp.sum(jax.nn.one_hot(top_i, EXPERTS_PER_GROUP, dtype=jnp.float32) * top_w[..., None], axis=1)
    comb = (g_onehot[:, :, None] * inner[:, None, :]).reshape(T, N_EXPERTS)
    y = jnp.zeros((T, h.shape[1]), jnp.float32)
    for e in range(N_EXPERTS):
        he = jax.nn.silu(h @ w_gate[e]) * (h @ w_up[e])
        y = y + comb[:, e:e + 1] * (he @ w_down[e]).astype(jnp.float32)
    return y.astype(h.dtype)


def hybrid_layer(x, cos_t, sin_t, norm1_g, w_in, pe_kc, w_kc1, w_kc2, pe_vc, w_vc1, w_vc2, w_o,
                 norm2_g, w_rg, b_rg, w_re, b_re, w_gate, w_up, w_down):
    B_, S, D = x.shape
    hd = HEAD_DIM
    h = rmsnorm(x, norm1_g)
    proj = h @ w_in
    offsets = [int(o) for o in np.cumsum(SPLIT_SIZES)[:-1]]
    aq, ak, av, bq, kc_raw, vc_raw, ks, vs, kw, vw, gate_logits = jnp.split(proj, offsets, axis=-1)

    def heads(t, n):
        return t.reshape(B_, S, n, hd).transpose(0, 2, 1, 3)

    o_a = dilated_mixer(apply_rope(heads(aq, N_HEADS_DIL), cos_t, sin_t),
                        apply_rope(heads(ak, N_HEADS_DIL), cos_t, sin_t),
                        heads(av, N_HEADS_DIL))
    R = N_HEADS_NSA // N_KV_NSA
    qb = apply_rope(bq.reshape(B_, S, N_KV_NSA, R, hd).transpose(0, 2, 3, 1, 4), cos_t, sin_t)
    o_b = nsa_mixer(qb, kc_raw.reshape(B_, S, N_KV_NSA, hd), vc_raw.reshape(B_, S, N_KV_NSA, hd),
                    apply_rope(heads(ks, N_KV_NSA), cos_t, sin_t), heads(vs, N_KV_NSA),
                    apply_rope(heads(kw, N_KV_NSA), cos_t, sin_t), heads(vw, N_KV_NSA),
                    gate_logits, pe_kc, w_kc1, w_kc2, pe_vc, w_vc1, w_vc2)
    mix = jnp.concatenate([o_a.transpose(0, 2, 1, 3).reshape(B_, S, -1),
                           o_b.transpose(0, 3, 1, 2, 4).reshape(B_, S, -1)], axis=-1)
    x = x + mix @ w_o
    h2 = rmsnorm(x, norm2_g).reshape(B_ * S, D)
    return x + hier_moe(h2, w_rg, b_rg, w_re, b_re, w_gate, w_up, w_down).reshape(B_, S, D)


def setup_inputs(seed: int = 0) -> dict:
    key = jax.random.key(seed)
    ks = jax.random.split(key, 20)
    f32 = jnp.float32

    def nrm(k, shape, fan_in):
        return jax.random.normal(k, shape, f32) * fan_in ** -0.5

    return {
        'x': jax.random.normal(ks[0], (BATCH, SEQ, D_MODEL), f32),
        'norm1_g': 1.0 + 0.02 * jax.random.normal(ks[1], (DEPTH, D_MODEL), f32),
        'w_in': nrm(ks[2], (DEPTH, D_MODEL, D_IN), D_MODEL),
        'pe_kc': 0.1 * jax.random.normal(ks[3], (DEPTH, CMP_LEN, HEAD_DIM), f32),
        'w_kc1': nrm(ks[4], (DEPTH, CMP_LEN * HEAD_DIM, CMP_HIDDEN), CMP_LEN * HEAD_DIM),
        'w_kc2': nrm(ks[5], (DEPTH, CMP_HIDDEN, HEAD_DIM), CMP_HIDDEN),
        'pe_vc': 0.1 * jax.random.normal(ks[6], (DEPTH, CMP_LEN, HEAD_DIM), f32),
        'w_vc1': nrm(ks[7], (DEPTH, CMP_LEN * HEAD_DIM, CMP_HIDDEN), CMP_LEN * HEAD_DIM),
        'w_vc2': nrm(ks[8], (DEPTH, CMP_HIDDEN, HEAD_DIM), CMP_HIDDEN),
        'w_o': nrm(ks[9], (DEPTH, D_MIX, D_MODEL), D_MIX),
        'norm2_g': 1.0 + 0.02 * jax.random.normal(ks[10], (DEPTH, D_MODEL), f32),
        'w_rg': nrm(ks[11], (DEPTH, D_MODEL, N_GROUPS), D_MODEL),
        'b_rg': 0.01 * jax.random.normal(ks[12], (DEPTH, N_GROUPS), f32),
        'w_re': nrm(ks[13], (DEPTH, D_MODEL, N_GROUPS, EXPERTS_PER_GROUP), D_MODEL),
        'b_re': 0.01 * jax.random.normal(ks[14], (DEPTH, N_GROUPS, EXPERTS_PER_GROUP), f32),
        'w_gate': nrm(ks[15], (DEPTH, N_EXPERTS, D_MODEL, D_FF_EXPERT), D_MODEL),
        'w_up': nrm(ks[16], (DEPTH, N_EXPERTS, D_MODEL, D_FF_EXPERT), D_MODEL),
        'w_down': nrm(ks[17], (DEPTH, N_EXPERTS, D_FF_EXPERT, D_MODEL), D_FF_EXPERT),
        'norm_f_g': 1.0 + 0.02 * jax.random.normal(ks[18], (D_MODEL,), f32),
    }


def reference(x, norm1_g, w_in, pe_kc, w_kc1, w_kc2, pe_vc, w_vc1, w_vc2, w_o, norm2_g,
              w_rg, b_rg, w_re, b_re, w_gate, w_up, w_down, norm_f_g):
    S = x.shape[1]
    cos_t, sin_t = rope_tables(jnp.arange(S))
    for l in range(DEPTH):
        x = hybrid_layer(x, cos_t, sin_t, norm1_g[l], w_in[l], pe_kc[l], w_kc1[l], w_kc2[l],
                         pe_vc[l], w_vc1[l], w_vc2[l], w_o[l], norm2_g[l], w_rg[l], b_rg[l],
                         w_re[l], b_re[l], w_gate[l], w_up[l], w_down[l])
    return rmsnorm(x, norm_f_g)
```

```python
import functools

import numpy as np
import jax
import jax.numpy as jnp
from jax import lax
from jax.experimental import pallas as pl
from jax.experimental.pallas import tpu as pltpu

F32 = jnp.float32
BF16 = jnp.bfloat16

D_MODEL = 1024
SEQ = 2048
HEAD_DIM = 64
N_HEADS_DIL = 8
DIL_PATTERNS = ((128, 1), (512, 4), (2048, 16))
N_HEADS_NSA = 8
N_KV_NSA = 2
NSA_REP = N_HEADS_NSA // N_KV_NSA
CMP_STRIDE = 16
CMP_LEN = 32
CMP_HIDDEN = 256
SEL_LEN = 64
N_SEL = 8
WIN = 512
ROPE_THETA = 500000.0
ROT_DIM = HEAD_DIM // 4
N_GROUPS = 8
EXPERTS_PER_GROUP = 8
D_FF_EXPERT = 256
EPS = 1e-6
NEG = -1e30
BIG = 1e9
LOWEST = -3.0e38

LANES = 128
QBLK = 128
N_QBLK = SEQ // QBLK
N_CMP = SEQ // CMP_STRIDE
N_SELBLK = SEQ // SEL_LEN
D_IN_PAD = 23 * LANES
PROJ_TM = 512
MOE_TM = 1024
MOE_CHUNK = 128
MOE_SLOTS = 16
VMEM_LIMIT = 56 * 1024 * 1024

_NT = (((1,), (1,)), ((), ()))
_TN = (((0,), (0,)), ((), ()))


def _lane_iota(shape):
    return lax.broadcasted_iota(jnp.int32, shape, 1)


def _row_iota(shape):
    return lax.broadcasted_iota(jnp.int32, shape, 0)


def _rope_lanes(y, c, s1, s2):
    return y * c + pltpu.roll(y, LANES - ROT_DIM // 2, axis=1) * s1 + pltpu.roll(y, ROT_DIM // 2, axis=1) * s2


def _proj_kernel(x_ref, g_ref, w_ref, c_ref, s1_ref, s2_ref,
                 aq_ref, ak_ref, av_ref, bq_ref, kcv_ref, ksw_ref, gate_ref):
    x = x_ref[...]
    ms = jnp.mean(x * x, axis=-1, keepdims=True)
    h = (x * lax.rsqrt(ms + EPS) * g_ref[...]).astype(BF16)
    c = c_ref[...]
    s1 = s1_ref[...]
    s2 = s2_ref[...]

    def seg(lo, width):
        return jnp.dot(h, w_ref[:, lo:lo + width], preferred_element_type=F32)

    def store(dst, col, y, rope, scale):
        for j in range(y.shape[1] // LANES):
            yj = y[:, j * LANES:(j + 1) * LANES]
            if rope:
                yj = _rope_lanes(yj, c, s1, s2)
            if scale != 1.0:
                yj = yj * scale
            dst[:, col + j * LANES:col + (j + 1) * LANES] = yj.astype(dst.dtype)

    qscale = HEAD_DIM ** -0.5
    store(aq_ref, 0, seg(0, 512), True, qscale)
    store(ak_ref, 0, seg(512, 512), True, 1.0)
    store(av_ref, 0, seg(1024, 512), False, 1.0)
    store(bq_ref, 0, seg(1536, 512), True, qscale)
    store(kcv_ref, 0, seg(2048, 256), False, 1.0)
    store(ksw_ref, 0, seg(2304, 128), True, 1.0)
    store(ksw_ref, 128, seg(2432, 128), False, 1.0)
    store(ksw_ref, 256, seg(2560, 128), True, 1.0)
    store(ksw_ref, 384, seg(2688, 128), False, 1.0)
    gate_ref[...] = jax.nn.sigmoid(seg(2816, 128))


def _proj(x2, g1, w_in_p, rc, rs1, rs2):
    t = x2.shape[0]
    nblk_s = SEQ // PROJ_TM
    row = lambda i: (i, 0)
    pos = lambda i: (i % nblk_s, 0)
    const = lambda i: (0, 0)
    outs = [jax.ShapeDtypeStruct((t, 512), BF16)] * 4 + [
        jax.ShapeDtypeStruct((t, 256), BF16), jax.ShapeDtypeStruct((t, 512), BF16),
        jax.ShapeDtypeStruct((t, LANES), F32)]
    out_specs = [pl.BlockSpec((PROJ_TM, 512), row)] * 4 + [
        pl.BlockSpec((PROJ_TM, 256), row), pl.BlockSpec((PROJ_TM, 512), row),
        pl.BlockSpec((PROJ_TM, LANES), row)]
    return pl.pallas_call(
        _proj_kernel,
        grid=(t // PROJ_TM,),
        in_specs=[pl.BlockSpec((PROJ_TM, D_MODEL), row), pl.BlockSpec((1, D_MODEL), const),
                  pl.BlockSpec((D_MODEL, D_IN_PAD), const),
                  pl.BlockSpec((PROJ_TM, LANES), pos), pl.BlockSpec((PROJ_TM, LANES), pos),
                  pl.BlockSpec((PROJ_TM, LANES), pos)],
        out_specs=out_specs,
        out_shape=outs,
        compiler_params=pltpu.CompilerParams(dimension_semantics=("arbitrary",),
                                             vmem_limit_bytes=VMEM_LIMIT),
        name="proj",
    )(x2, g1, w_in_p, rc, rs1, rs2)


def _dilated_kernel(q_ref, k_ref, v_ref, o_ref,
                    qf, kf, vf, qp, k0p, k1p, vp, op, lp, on0, on1, on2, ln0, ln1, ln2):
    qf[...] = q_ref[...].astype(F32)
    kf[...] = k_ref[...].astype(F32)
    vf[...] = v_ref[...].astype(F32)
    zpad = jnp.zeros((QBLK, LANES), BF16)
    k0p[0:QBLK, :] = zpad
    k1p[0:QBLK, :] = zpad
    vp[0:QBLK, :] = zpad

    a = _row_iota((QBLK, 2 * QBLK))
    cc = _lane_iota((QBLK, 2 * QBLK))
    band = jnp.where((cc >= a) & (cc <= a + QBLK), 0.0, NEG)
    band_noprev = jnp.where(cc < QBLK, NEG, band)
    lane_q = _lane_iota((QBLK, LANES))
    left = lane_q < HEAD_DIM

    for pi, (window, dil) in enumerate(DIL_PATTERNS):
        seg_len = SEQ // dil
        nseg_blk = seg_len // QBLK
        for r in range(dil):
            src = pl.ds(r, seg_len, stride=dil) if dil > 1 else pl.ds(0, SEQ)
            dst = pl.ds(r * seg_len, seg_len)
            dstp = pl.ds(QBLK + r * seg_len, seg_len)
            qp[dst, :] = qf[src, :].astype(BF16)
            kk = kf[src, :]
            hm = _lane_iota((seg_len, LANES)) < HEAD_DIM
            k0p[dstp, :] = jnp.where(hm, kk, 0.0).astype(BF16)
            k1p[dstp, :] = jnp.where(hm, 0.0, kk).astype(BF16)
            vp[dstp, :] = vf[src, :].astype(BF16)

        def body(j, carry, nseg_blk=nseg_blk):
            row0 = pl.multiple_of(j * QBLK, QBLK)
            q = qp[pl.ds(row0, QBLK), :]
            if nseg_blk > 1:
                span = pl.ds(row0, 2 * QBLK)
                has_prev = (j % nseg_blk) != 0
                bias = jnp.where(has_prev, band, band_noprev)
            else:
                span = pl.ds(row0 + QBLK, QBLK)
                bias = band[:, QBLK:]
            vv = vp[span, :]
            res = []
            for kref in (k0p, k1p):
                s = lax.dot_general(q, kref[span, :], _NT, preferred_element_type=F32) + bias
                m = jnp.max(s, axis=1, keepdims=True)
                p = jnp.exp(s - m)
                l = jnp.sum(p, axis=1, keepdims=True)
                o = jnp.dot(p.astype(BF16), vv, preferred_element_type=F32) / l
                res.append((o, jnp.broadcast_to(m + jnp.log(l), (QBLK, LANES))))
            op[pl.ds(row0, QBLK), :] = jnp.where(left, res[0][0], res[1][0])
            lp[pl.ds(row0, QBLK), :] = jnp.where(left, res[0][1], res[1][1])
            return carry

        lax.fori_loop(0, N_QBLK, body, 0)

        for r in range(dil):
            dst = pl.ds(r, seg_len, stride=dil) if dil > 1 else pl.ds(0, SEQ)
            src = pl.ds(r * seg_len, seg_len)
            (on0, on1, on2)[pi][dst, :] = op[src, :]
            (ln0, ln1, ln2)[pi][dst, :] = lp[src, :]

    l0, l1, l2 = ln0[...], ln1[...], ln2[...]
    mx = jnp.maximum(jnp.maximum(l0, l1), l2)
    e0, e1, e2 = jnp.exp(l0 - mx), jnp.exp(l1 - mx), jnp.exp(l2 - mx)
    den = e0 + e1 + e2
    out = (e0 / den) * on0[...] + (e1 / den) * on1[...] + (e2 / den) * on2[...]
    o_ref[...] = out.astype(o_ref.dtype)


def _dilated(aq, ak, av):
    t = aq.shape[0]
    nb = t // SEQ
    spec = pl.BlockSpec((SEQ, LANES), lambda b, hp: (b, hp))
    return pl.pallas_call(
        _dilated_kernel,
        grid=(nb, N_HEADS_DIL // 2),
        in_specs=[spec, spec, spec],
        out_specs=spec,
        out_shape=jax.ShapeDtypeStruct((t, 512), BF16),
        scratch_shapes=[pltpu.VMEM((SEQ, LANES), F32)] * 3
        + [pltpu.VMEM((SEQ, LANES), BF16)]
        + [pltpu.VMEM((SEQ + QBLK, LANES), BF16)] * 3
        + [pltpu.VMEM((SEQ, LANES), F32)] * 8,
        compiler_params=pltpu.CompilerParams(dimension_semantics=("arbitrary", "arbitrary"),
                                             vmem_limit_bytes=VMEM_LIMIT),
        name="dilated",
    )(aq, ak, av)


def _gelu_tanh(x):
    return 0.5 * x * (1.0 + jnp.tanh(np.sqrt(2.0 / np.pi).astype(np.float32) * (x + 0.044715 * (x * x * x))))


def _softmax_step(state, s, vd):
    m, l, acc = state
    mn = jnp.maximum(m, jnp.max(s, axis=1, keepdims=True))
    al = jnp.exp(m - mn)
    p = jnp.exp(s - mn)
    l = al * l + jnp.sum(p, axis=1, keepdims=True)
    acc = al * acc + jnp.dot(p.astype(BF16), vd, preferred_element_type=F32)
    return mn, l, acc


def _nsa_kernel(q_ref, kflat_ref, vflat_ref, ksw_ref, gate_ref,
                wk1_ref, wk2a_ref, wk2b_ref, pek_ref, wv1_ref, wv2d_ref, pev_ref,
                cc_ref, cs1_ref, cs2_ref, ovl_ref, exp_ref,
                o_ref,
                ks0, ks1, vsd, kw0, kw1, vwd, kc0, kc1, vcd):
    g = pl.program_id(1)
    lane_s = _lane_iota((SEQ, LANES))
    left_s = lane_s < HEAD_DIM

    def split_pair(col):
        blk = ksw_ref[:, col:col + LANES].astype(F32)
        swapped = pltpu.roll(blk, HEAD_DIM, axis=1)
        lo = jnp.where(g == 0, blk, swapped)
        hi = jnp.where(g == 0, swapped, blk)
        return lo, hi

    lo, hi = split_pair(0)
    ks0[...] = jnp.where(left_s, lo, 0.0).astype(BF16)
    ks1[...] = jnp.where(left_s, 0.0, hi).astype(BF16)
    lo, hi = split_pair(128)
    vsd[...] = jnp.where(left_s, lo, hi).astype(BF16)
    lo, hi = split_pair(256)
    kw0[...] = jnp.where(left_s, lo, 0.0).astype(BF16)
    kw1[...] = jnp.where(left_s, 0.0, hi).astype(BF16)
    lo, hi = split_pair(384)
    vwd[...] = jnp.where(left_s, lo, hi).astype(BF16)

    def hidden(flat_ref, w1_ref, pe_ref):
        ab = jnp.dot(flat_ref[...], w1_ref[...], preferred_element_type=F32)
        pb = (jnp.dot(pe_ref[0:16, :], w1_ref[:, 0:CMP_HIDDEN], preferred_element_type=F32)
              + jnp.dot(pe_ref[16:32, :], w1_ref[:, CMP_HIDDEN:], preferred_element_type=F32))
        hid = ab[:, 0:CMP_HIDDEN] + pltpu.roll(ab[:, CMP_HIDDEN:], N_CMP - 1, axis=0) + pb[0:1, :]
        return _gelu_tanh(hid).astype(BF16)

    hk = hidden(kflat_ref, wk1_ref, pek_ref)
    cc, cs1, cs2 = cc_ref[...], cs1_ref[...], cs2_ref[...]
    kc0[...] = _rope_lanes(jnp.dot(hk, wk2a_ref[...], preferred_element_type=F32), cc, cs1, cs2).astype(BF16)
    kc1[...] = _rope_lanes(jnp.dot(hk, wk2b_ref[...], preferred_element_type=F32), cc, cs1, cs2).astype(BF16)
    hv = hidden(vflat_ref, wv1_ref, pev_ref)
    vcd[...] = jnp.dot(hv, wv2d_ref[...], preferred_element_type=F32).astype(BF16)

    a2 = _row_iota((2 * QBLK, LANES)) % QBLK
    c2 = _lane_iota((2 * QBLK, LANES))
    causal = jnp.where(c2 <= a2, 0.0, NEG)
    upper = jnp.where(c2 > a2, 0.0, NEG)
    left2 = c2 < HEAD_DIM
    a1 = _row_iota((QBLK, LANES))
    c1 = _lane_iota((QBLK, LANES))
    left1 = c1 < HEAD_DIM

    def qblock(i, carry):
        row0 = pl.multiple_of(i * QBLK, QBLK)
        rows = pl.ds(row0, QBLK)
        q2 = jnp.concatenate([q_ref[rows, 0:LANES], q_ref[rows, LANES:2 * LANES]], axis=0)

        def scores(kref, span):
            return lax.dot_general(q2, kref[span, :], _NT, preferred_element_type=F32)

        def fresh():
            return (jnp.full((2 * QBLK, 1), LOWEST, F32), jnp.zeros((2 * QBLK, 1), F32),
                    jnp.zeros((2 * QBLK, LANES), F32))

        def finish(st0, st1):
            return jnp.where(left2, st0[2] / st0[1], st1[2] / st1[1])

        t2 = row0 + a2
        cvalid = (CMP_STRIDE * c2 + (CMP_LEN - 1) <= t2) & (c2 < N_CMP - 1)
        pcs = []
        for kref in (kc0, kc1):
            s = jnp.where(cvalid, scores(kref, pl.ds(0, N_CMP)), NEG)
            m = jnp.max(s, axis=1, keepdims=True)
            p = jnp.where(cvalid, jnp.exp(s - m), 0.0)
            l = jnp.sum(p, axis=1, keepdims=True)
            pcs.append(p / jnp.where(l > 0, l, 1.0))
        vcv = vcd[...]
        o_cmp = jnp.where(left2, jnp.dot(pcs[0].astype(BF16), vcv, preferred_element_type=F32),
                          jnp.dot(pcs[1].astype(BF16), vcv, preferred_element_type=F32))

        psum = pcs[0][0:QBLK] + pcs[0][QBLK:] + pcs[1][0:QBLK] + pcs[1][QBLK:]
        p_hi = psum.astype(BF16)
        p_lo = (psum - p_hi.astype(F32)).astype(BF16)
        ovl = ovl_ref[...]
        imp = (jnp.dot(p_hi, ovl, preferred_element_type=F32) + jnp.dot(p_lo, ovl, preferred_element_type=F32))
        t1 = row0 + a1
        blk_t = t1 // SEL_LEN
        validj = (c1 * SEL_LEN <= t1) & (c1 < N_SELBLK)
        forced = (c1 == 0) | (c1 == blk_t) | (c1 == blk_t - 1)
        v = jnp.where(validj, jnp.where(forced, BIG, imp), -BIG)
        v = jnp.where(c1 < N_SELBLK, v, LOWEST)
        selm = jnp.zeros((QBLK, LANES), F32)
        for _ in range(N_SEL):
            mx = jnp.max(v, axis=1, keepdims=True)
            first = jnp.min(jnp.where(v == mx, c1, 4 * LANES), axis=1, keepdims=True)
            hit = c1 == first
            selm = jnp.where(hit, 1.0, selm)
            v = jnp.where(hit, LOWEST, v)
        notsel = (selm - 1.0).astype(BF16)

        def sel_bias(kb):
            b = jnp.dot(notsel, exp_ref[kb], preferred_element_type=F32)
            return jnp.concatenate([b, b], axis=0)

        def sel_step(span, bias, st):
            vd = vsd[span, :]
            return (_softmax_step(st[0], scores(ks0, span) + bias, vd),
                    _softmax_step(st[1], scores(ks1, span) + bias, vd))

        def sel_body(kb, st):
            return sel_step(pl.ds(pl.multiple_of(kb * QBLK, QBLK), QBLK), sel_bias(kb), st)

        st = lax.fori_loop(0, i, sel_body, (fresh(), fresh()))
        st = sel_step(rows, sel_bias(i) + causal, st)
        o_sel = finish(*st)

        def win_step(span, bias, st):
            vd = vwd[span, :]
            s0, s1 = scores(kw0, span), scores(kw1, span)
            if bias is not None:
                s0, s1 = s0 + bias, s1 + bias
            return (_softmax_step(st[0], s0, vd), _softmax_step(st[1], s1, vd))

        st = win_step(rows, causal, (fresh(), fresh()))

        def win_body(kb, st):
            return win_step(pl.ds(pl.multiple_of(kb * QBLK, QBLK), QBLK), None, st)

        st = lax.fori_loop(jnp.maximum(i - (WIN // QBLK - 1), 0), i, win_body, st)
        far = jnp.maximum(i - WIN // QBLK, 0)
        far_bias = jnp.where(i >= WIN // QBLK, upper, NEG)
        st = win_step(pl.ds(pl.multiple_of(far * QBLK, QBLK), QBLK), far_bias, st)
        o_win = finish(*st)

        gt = gate_ref[rows, :]
        gt = jnp.where(g == 0, gt, pltpu.roll(gt, LANES - 3 * NSA_REP, axis=1))

        def gate(r, br):
            col = r * 3 + br
            return jnp.broadcast_to(gt[:, col:col + 1], (QBLK, LANES))

        for pair in range(2):
            r0, r1 = 2 * pair, 2 * pair + 1
            rs = slice(pair * QBLK, (pair + 1) * QBLK)
            out = (jnp.where(left1, gate(r0, 0), gate(r1, 0)) * o_cmp[rs]
                   + jnp.where(left1, gate(r0, 1), gate(r1, 1)) * o_sel[rs]
                   + jnp.where(left1, gate(r0, 2), gate(r1, 2)) * o_win[rs])
            o_ref[rows, pair * LANES:(pair + 1) * LANES] = out.astype(o_ref.dtype)
        return carry

    lax.fori_loop(0, N_QBLK, qblock, 0)


def _nsa(bq, kvflat, ksw, gates, wk1, wk2a, wk2b, pek, wv1, wv2d, pev, cc, cs1, cs2, ovl, expand):
    t = bq.shape[0]
    nb = t // SEQ
    const2 = lambda b, g: (0, 0)
    return pl.pallas_call(
        _nsa_kernel,
        grid=(nb, N_KV_NSA),
        in_specs=[
            pl.BlockSpec((SEQ, 2 * LANES), lambda b, g: (b, g)),
            pl.BlockSpec((None, None, N_CMP, CMP_STRIDE * HEAD_DIM), lambda b, g: (b, g, 0, 0)),
            pl.BlockSpec((None, None, N_CMP, CMP_STRIDE * HEAD_DIM), lambda b, g: (b, N_KV_NSA + g, 0, 0)),
            pl.BlockSpec((SEQ, 4 * LANES), lambda b, g: (b, 0)),
            pl.BlockSpec((SEQ, LANES), lambda b, g: (b, 0)),
            pl.BlockSpec((CMP_STRIDE * HEAD_DIM, 2 * CMP_HIDDEN), const2),
            pl.BlockSpec((CMP_HIDDEN, LANES), const2),
            pl.BlockSpec((CMP_HIDDEN, LANES), const2),
            pl.BlockSpec((32, CMP_STRIDE * HEAD_DIM), const2),
            pl.BlockSpec((CMP_STRIDE * HEAD_DIM, 2 * CMP_HIDDEN), const2),
            pl.BlockSpec((CMP_HIDDEN, LANES), const2),
            pl.BlockSpec((32, CMP_STRIDE * HEAD_DIM), const2),
            pl.BlockSpec((N_CMP, LANES), const2),
            pl.BlockSpec((N_CMP, LANES), const2),
            pl.BlockSpec((N_CMP, LANES), const2),
            pl.BlockSpec((LANES, LANES), const2),
            pl.BlockSpec((N_QBLK, LANES, LANES), lambda b, g: (0, 0, 0)),
        ],
        out_specs=pl.BlockSpec((SEQ, 2 * LANES), lambda b, g: (b, g)),
        out_shape=jax.ShapeDtypeStruct((t, 512), BF16),
        scratch_shapes=[pltpu.VMEM((SEQ, LANES), BF16)] * 6 + [pltpu.VMEM((N_CMP, LANES), BF16)] * 3,
        compiler_params=pltpu.CompilerParams(dimension_semantics=("arbitrary", "arbitrary"),
                                             vmem_limit_bytes=VMEM_LIMIT),
        name="nsa",
    )(bq, kvflat, kvflat, ksw, gates, wk1, wk2a, wk2b, pek, wv1, wv2d, pev, cc, cs1, cs2, ovl, expand)


def _post_kernel(x_ref, ma_ref, mb_ref, wo_ref, g2_ref, wr_ref, br_ref, tri_ref,
                 x1_ref, h2_ref, dest_ref, cw_ref, cnt_ref):
    x1 = (x_ref[...] + jnp.dot(ma_ref[...], wo_ref[0:512, :], preferred_element_type=F32)
          + jnp.dot(mb_ref[...], wo_ref[512:1024, :], preferred_element_type=F32))
    x1_ref[...] = x1
    ms = jnp.mean(x1 * x1, axis=-1, keepdims=True)
    h2 = x1 * lax.rsqrt(ms + EPS) * g2_ref[...]
    h2_ref[...] = h2.astype(BF16)

    lg = lax.dot_general(wr_ref[...], h2, _NT, precision=lax.Precision.HIGHEST,
                         preferred_element_type=F32) + br_ref[:, 0:1]
    gl = lg[0:N_GROUPS, :]
    sub = _row_iota(gl.shape)
    gmax = jnp.max(gl, axis=0, keepdims=True)
    p_g = 1.0 / jnp.sum(jnp.exp(gl - gmax), axis=0, keepdims=True)
    g_star = jnp.min(jnp.where(gl == gmax, sub, N_GROUPS), axis=0, keepdims=True)
    onehot = sub == g_star
    el = jnp.zeros_like(gl)
    for gi in range(N_GROUPS):
        lo = N_GROUPS + gi * EXPERTS_PER_GROUP
        el = el + jnp.where(g_star == gi, lg[lo:lo + EXPERTS_PER_GROUP, :], 0.0)
    emax = jnp.max(el, axis=0, keepdims=True)
    ee = jnp.exp(el - emax)
    pe = ee / jnp.sum(ee, axis=0, keepdims=True)
    v1 = jnp.max(pe, axis=0, keepdims=True)
    i1 = jnp.min(jnp.where(pe == v1, sub, EXPERTS_PER_GROUP), axis=0, keepdims=True)
    pe2 = jnp.where(sub == i1, -1.0, pe)
    v2 = jnp.max(pe2, axis=0, keepdims=True)
    i2 = jnp.min(jnp.where(pe2 == v2, sub, EXPERTS_PER_GROUP), axis=0, keepdims=True)
    tot = v1 + v2
    cw_ref[...] = jnp.where(sub == i1, v1 / tot * p_g, jnp.where(sub == i2, v2 / tot * p_g, 0.0))

    oh = jnp.where(onehot, 1.0, 0.0)
    before = jnp.dot(oh.astype(BF16), tri_ref[...], preferred_element_type=F32)
    rank = jnp.sum(oh * before, axis=0, keepdims=True)
    dest_ref[0] = g_star * MOE_TM + rank.astype(jnp.int32)
    cnt = jnp.sum(oh, axis=1, keepdims=True).astype(jnp.int32)
    cnt_ref[0] = jnp.broadcast_to(cnt, (N_GROUPS, LANES))


def _post(x2, mix_a, mix_b, w_o, g2, wr_t, br, tri):
    t = x2.shape[0]
    nt = t // MOE_TM
    row = lambda i: (i, 0)
    const = lambda i: (0, 0)
    return pl.pallas_call(
        _post_kernel,
        grid=(nt,),
        in_specs=[pl.BlockSpec((MOE_TM, D_MODEL), row), pl.BlockSpec((MOE_TM, 512), row),
                  pl.BlockSpec((MOE_TM, 512), row), pl.BlockSpec((D_MODEL, D_MODEL), const),
                  pl.BlockSpec((1, D_MODEL), const), pl.BlockSpec((72, D_MODEL), const),
                  pl.BlockSpec((72, LANES), const), pl.BlockSpec((MOE_TM, MOE_TM), const)],
        out_specs=[pl.BlockSpec((MOE_TM, D_MODEL), row), pl.BlockSpec((MOE_TM, D_MODEL), row),
                   pl.BlockSpec((1, 1, MOE_TM), lambda i: (i, 0, 0)),
                   pl.BlockSpec((N_GROUPS, MOE_TM), lambda i: (0, i)),
                   pl.BlockSpec((1, N_GROUPS, LANES), lambda i: (i, 0, 0))],
        out_shape=[jax.ShapeDtypeStruct((t, D_MODEL), F32), jax.ShapeDtypeStruct((t, D_MODEL), BF16),
                   jax.ShapeDtypeStruct((nt, 1, MOE_TM), jnp.int32),
                   jax.ShapeDtypeStruct((N_GROUPS, t), F32),
                   jax.ShapeDtypeStruct((nt, N_GROUPS, LANES), jnp.int32)],
        compiler_params=pltpu.CompilerParams(dimension_semantics=("arbitrary",),
                                             vmem_limit_bytes=VMEM_LIMIT),
        name="post",
    )(x2, mix_a, mix_b, w_o, g2, wr_t, br, tri)


def _gather_matrix(dest_row, target0):
    p = _row_iota((MOE_CHUNK, MOE_TM))
    return jnp.where(dest_row == target0 + p, 1.0, 0.0).astype(BF16)


def _moe_kernel(ig_ref, it_ref, ic_ref, islot_ref, ivalid_ref,
                h2_ref, dest_ref, cw_ref, wgu_ref, wd_ref, y_ref):
    k = pl.program_id(0)

    @pl.when(ivalid_ref[k] == 0)
    def _():
        y_ref[...] = jnp.zeros_like(y_ref)

    @pl.when(ivalid_ref[k] != 0)
    def _():
        pm = _gather_matrix(dest_ref[0], ig_ref[k] * MOE_TM + ic_ref[k] * MOE_CHUNK)
        xg = jnp.dot(pm, h2_ref[...], preferred_element_type=F32).astype(BF16)
        cw = jnp.concatenate([cw_ref[...], jnp.zeros((N_GROUPS, MOE_TM), F32)], axis=0)
        cw_hi = cw.astype(BF16)
        cw_lo = (cw - cw_hi.astype(F32)).astype(BF16)
        cws = (lax.dot_general(cw_hi, pm, _NT, preferred_element_type=F32)
               + lax.dot_general(cw_lo, pm, _NT, preferred_element_type=F32)).T
        dff = EXPERTS_PER_GROUP * D_FF_EXPERT
        parts = []
        for e in range(EXPERTS_PER_GROUP):
            lo = e * D_FF_EXPERT
            gate = jnp.dot(xg, wgu_ref[:, lo:lo + D_FF_EXPERT], preferred_element_type=F32)
            up = jnp.dot(xg, wgu_ref[:, dff + lo:dff + lo + D_FF_EXPERT], preferred_element_type=F32)
            he = gate * jax.nn.sigmoid(gate) * up
            parts.append((he * cws[:, e:e + 1]).astype(BF16))
        hid = jnp.concatenate(parts, axis=1)
        y_ref[...] = jnp.dot(hid, wd_ref[...], preferred_element_type=F32).astype(y_ref.dtype)


def _moe(items, h2, dest, cw_t, wgu, wd):
    t = h2.shape[0]
    nt = t // MOE_TM
    dff = EXPERTS_PER_GROUP * D_FF_EXPERT
    grid_spec = pltpu.PrefetchScalarGridSpec(
        num_scalar_prefetch=5,
        grid=(nt * MOE_SLOTS,),
        in_specs=[
            pl.BlockSpec((MOE_TM, D_MODEL), lambda k, ig, it, ic, isl, iv: (it[k], 0)),
            pl.BlockSpec((1, 1, MOE_TM), lambda k, ig, it, ic, isl, iv: (it[k], 0, 0)),
            pl.BlockSpec((N_GROUPS, MOE_TM), lambda k, ig, it, ic, isl, iv: (0, it[k])),
            pl.BlockSpec((None, D_MODEL, 2 * dff), lambda k, ig, it, ic, isl, iv: (ig[k], 0, 0)),
            pl.BlockSpec((None, dff, D_MODEL), lambda k, ig, it, ic, isl, iv: (ig[k], 0, 0)),
        ],
        out_specs=pl.BlockSpec((MOE_CHUNK, D_MODEL), lambda k, ig, it, ic, isl, iv: (isl[k], 0)),
    )
    return pl.pallas_call(
        _moe_kernel,
        grid_spec=grid_spec,
        out_shape=jax.ShapeDtypeStruct((nt * MOE_SLOTS * MOE_CHUNK, D_MODEL), BF16),
        compiler_params=pltpu.CompilerParams(dimension_semantics=("arbitrary",),
                                             vmem_limit_bytes=VMEM_LIMIT),
        name="moe",
    )(*items, h2, dest, cw_t, wgu, wd)


def _combine_kernel(nused_ref, sg_ref, sc_ref, x1_ref, ys_ref, dest_ref, gf_ref, o_ref, acc_ref):
    i = pl.program_id(0)
    acc_ref[...] = x1_ref[...]

    def body(s, carry):
        pm = _gather_matrix(dest_ref[0], sg_ref[i, s] * MOE_TM + sc_ref[i, s] * MOE_CHUNK)
        yc = ys_ref[pl.ds(pl.multiple_of(s * MOE_CHUNK, MOE_CHUNK), MOE_CHUNK), :]
        acc_ref[...] += lax.dot_general(pm, yc, _TN, preferred_element_type=F32)
        return carry

    lax.fori_loop(0, nused_ref[i], body, 0)
    xo = acc_ref[...]
    ms = jnp.mean(xo * xo, axis=-1, keepdims=True)
    o_ref[...] = xo * lax.rsqrt(ms + EPS) * gf_ref[...]


def _combine(nused, slot_g, slot_c, x1, ys, dest, gf):
    t = x1.shape[0]
    nt = t // MOE_TM
    grid_spec = pltpu.PrefetchScalarGridSpec(
        num_scalar_prefetch=3,
        grid=(nt,),
        in_specs=[
            pl.BlockSpec((MOE_TM, D_MODEL), lambda i, a, b, c: (i, 0)),
            pl.BlockSpec((MOE_SLOTS * MOE_CHUNK, D_MODEL), lambda i, a, b, c: (i, 0)),
            pl.BlockSpec((1, 1, MOE_TM), lambda i, a, b, c: (i, 0, 0)),
            pl.BlockSpec((1, D_MODEL), lambda i, a, b, c: (0, 0)),
        ],
        out_specs=pl.BlockSpec((MOE_TM, D_MODEL), lambda i, a, b, c: (i, 0)),
        scratch_shapes=[pltpu.VMEM((MOE_TM, D_MODEL), F32)],
    )
    return pl.pallas_call(
        _combine_kernel,
        grid_spec=grid_spec,
        out_shape=jax.ShapeDtypeStruct((t, D_MODEL), F32),
        compiler_params=pltpu.CompilerParams(dimension_semantics=("arbitrary",),
                                             vmem_limit_bytes=VMEM_LIMIT),
        name="combine",
    )(nused, slot_g, slot_c, x1, ys, dest, gf)


def _moe_plan(cnt):
    nt = cnt.shape[0]
    nch = (cnt + MOE_CHUNK - 1) // MOE_CHUNK
    nused = jnp.sum(nch, axis=1)
    base = jnp.cumsum(nch, axis=1) - nch
    spare = MOE_SLOTS - nused
    n_items = nch.T + jnp.where(jnp.arange(N_GROUPS)[:, None] == N_GROUPS - 1, spare[None, :], 0)
    flat = n_items.reshape(-1)
    starts = jnp.cumsum(flat) - flat
    k = jnp.arange(nt * MOE_SLOTS)
    gt = jnp.searchsorted(starts, k, side="right") - 1
    ig = gt // nt
    it = gt % nt
    local = k - starts[gt]
    real = nch[it, ig]
    valid = local < real
    islot = it * MOE_SLOTS + jnp.where(valid, base[it, ig] + local, nused[it] + local - real)
    ic = jnp.where(valid, local, 0)
    items = tuple(a.astype(jnp.int32) for a in (ig, it, ic, islot, valid))
    s = jnp.arange(MOE_SLOTS)
    slot_g = jnp.sum(s[None, :, None] >= (base + nch)[:, None, :], axis=2)
    slot_g = jnp.minimum(slot_g, N_GROUPS - 1)
    slot_c = s[None, :] - jnp.take_along_axis(base, slot_g, axis=1)
    return items, nused.astype(jnp.int32), slot_g.astype(jnp.int32), slot_c.astype(jnp.int32)


def _rope_tables(pos):
    half = ROT_DIM // 2
    inv_freq = ROPE_THETA ** (-jnp.arange(0, ROT_DIM, 2, dtype=F32) / ROT_DIM)
    ang = pos.astype(F32)[:, None] * inv_freq[None, :]
    cos, sin = jnp.cos(ang), jnp.sin(ang)
    l64 = np.arange(LANES) % HEAD_DIM
    f = l64 % half
    first = jnp.asarray(l64 < half)[None, :]
    second = jnp.asarray((l64 >= half) & (l64 < ROT_DIM))[None, :]
    c = jnp.where(first | second, cos[:, f], 1.0)
    s1 = jnp.where(first, -sin[:, f], 0.0)
    s2 = jnp.where(second, sin[:, f], 0.0)
    return c.astype(F32), s1.astype(F32), s2.astype(F32)


def _overlap_table():
    cs = np.arange(N_CMP)[:, None] * CMP_STRIDE
    js = np.arange(N_SELBLK)[None, :] * SEL_LEN
    ov = np.clip(np.minimum(cs + CMP_LEN, js + SEL_LEN) - np.maximum(cs, js), 0, None) / CMP_LEN
    ov[N_CMP - 1, :] = 0.0
    out = np.zeros((LANES, LANES), np.float32)
    out[:N_CMP, :N_SELBLK] = ov
    return out


def _expand_table():
    out = np.zeros((N_QBLK, LANES, LANES), np.float32)
    for kb in range(N_QBLK):
        for c in range(QBLK):
            out[kb, (kb * QBLK + c) // SEL_LEN, c] = -NEG
    return out


def kernel(x, norm1_g, w_in, pe_kc, w_kc1, w_kc2, pe_vc, w_vc1, w_vc2, w_o, norm2_g, w_rg, b_rg, w_re, b_re,
           w_gate, w_up, w_down, norm_f_g):
    nb, s, d = x.shape
    assert (s, d) == (SEQ, D_MODEL) and norm1_g.shape[0] == 1
    t = nb * s
    x2 = x.reshape(t, d)

    w_in_p = jnp.pad(w_in[0], ((0, 0), (0, D_IN_PAD - w_in.shape[2]))).astype(BF16)
    rc, rs1, rs2 = _rope_tables(jnp.arange(SEQ))
    cc, cs1, cs2 = _rope_tables(jnp.arange(N_CMP) * CMP_STRIDE + CMP_LEN - 1)
    half_flat = CMP_STRIDE * HEAD_DIM

    def cmp_w1(w1):
        return jnp.concatenate([w1[:half_flat], w1[half_flat:]], axis=1).astype(BF16)

    def cmp_pe(pe):
        rows = pe.reshape(2, half_flat)
        return jnp.concatenate([jnp.broadcast_to(rows[0:1], (16, half_flat)),
                                jnp.broadcast_to(rows[1:2], (16, half_flat))], axis=0).astype(BF16)

    zeros_half = jnp.zeros((CMP_HIDDEN, HEAD_DIM), F32)
    wk2a = jnp.concatenate([w_kc2[0], zeros_half], axis=1).astype(BF16)
    wk2b = jnp.concatenate([zeros_half, w_kc2[0]], axis=1).astype(BF16)
    wv2d = jnp.concatenate([w_vc2[0], w_vc2[0]], axis=1).astype(BF16)
    ovl = jnp.asarray(_overlap_table(), BF16)
    expand = jnp.asarray(_expand_table(), BF16)

    aq, ak, av, bq, kcv, ksw, gates = _proj(x2, norm1_g, w_in_p, rc, rs1, rs2)
    mix_a = _dilated(aq, ak, av)

    kvflat = kcv.reshape(nb, N_CMP, CMP_STRIDE, 2 * N_KV_NSA, HEAD_DIM).transpose(0, 3, 1, 2, 4)
    kvflat = kvflat.reshape(nb, 2 * N_KV_NSA, N_CMP, half_flat)
    mix_b = _nsa(bq, kvflat, ksw, gates, cmp_w1(w_kc1[0]), wk2a, wk2b, cmp_pe(pe_kc[0]),
                 cmp_w1(w_vc1[0]), wv2d, cmp_pe(pe_vc[0]), cc, cs1, cs2, ovl, expand)

    wr_t = jnp.concatenate([w_rg[0].T, w_re[0].reshape(d, -1).T], axis=0)
    br = jnp.broadcast_to(jnp.concatenate([b_rg[0], b_re[0].reshape(-1)])[:, None], (72, LANES))
    tri = jnp.asarray(np.triu(np.ones((MOE_TM, MOE_TM), np.float32), 1), BF16)
    x1, h2, dest, cw_t, cnt = _post(x2, mix_a, mix_b, w_o[0].astype(BF16), norm2_g, wr_t, br, tri)

    dff = EXPERTS_PER_GROUP * D_FF_EXPERT

    def group_cols(w):
        return w.reshape(N_GROUPS, EXPERTS_PER_GROUP, d, D_FF_EXPERT).transpose(0, 2, 1, 3).reshape(N_GROUPS, d, dff)

    wgu = jnp.concatenate([group_cols(w_gate[0]), group_cols(w_up[0])], axis=2).astype(BF16)
    wd = w_down[0].reshape(N_GROUPS, dff, d).astype(BF16)
    items, nused, slot_g, slot_c = _moe_plan(cnt[:, :, 0])
    ys = _moe(items, h2, dest, cw_t, wgu, wd)
    out = _combine(nused, slot_g, slot_c, x1, ys, dest, norm_f_g.reshape(1, d))
    return out.reshape(nb, s, d)
```

```python
import numpy as np
import jax
import jax.numpy as jnp
from jax import lax
from jax.experimental import pallas as pl
from jax.experimental.pallas import tpu as pltpu

F32 = jnp.float32
BF16 = jnp.bfloat16

D_MODEL = 1024
SEQ = 2048
HEAD_DIM = 64
N_HEADS_DIL = 8
DIL_PATTERNS = ((128, 1), (512, 4), (2048, 16))
N_HEADS_NSA = 8
N_KV_NSA = 2
NSA_REP = N_HEADS_NSA // N_KV_NSA
CMP_STRIDE = 16
CMP_LEN = 32
CMP_HIDDEN = 256
SEL_LEN = 64
N_SEL = 8
WIN = 512
ROPE_THETA = 500000.0
ROT_DIM = HEAD_DIM // 4
N_GROUPS = 8
EXPERTS_PER_GROUP = 8
D_FF_EXPERT = 256
EPS = 1e-6
NEG = -1e30
BIG = 1e9
LOWEST = -3.0e38

LANES = 128
QBLK = 128
N_QBLK = SEQ // QBLK
N_CMP = SEQ // CMP_STRIDE
N_SELBLK = SEQ // SEL_LEN
D_IN_PAD = 23 * LANES
PROJ_TM = 512
MOE_TM = 1024
MOE_CHUNK = 128
MOE_SLOTS = 16
VMEM_LIMIT = 56 * 1024 * 1024

_NT = (((1,), (1,)), ((), ()))
_TN = (((0,), (0,)), ((), ()))


def _lane_iota(shape):
    return lax.broadcasted_iota(jnp.int32, shape, 1)


def _row_iota(shape):
    return lax.broadcasted_iota(jnp.int32, shape, 0)


def _rope_lanes(y, c, s1, s2):
    return y * c + pltpu.roll(y, LANES - ROT_DIM // 2, axis=1) * s1 + pltpu.roll(y, ROT_DIM // 2, axis=1) * s2


def _proj_kernel(x_ref, g_ref, w_ref, c_ref, s1_ref, s2_ref,
                 aq_ref, ak_ref, av_ref, bq_ref, kcv_ref, ksw_ref, gate_ref):
    x = x_ref[...]
    ms = jnp.mean(x * x, axis=-1, keepdims=True)
    h = (x * lax.rsqrt(ms + EPS) * g_ref[...]).astype(BF16)
    c = c_ref[...]
    s1 = s1_ref[...]
    s2 = s2_ref[...]

    def seg(lo, width):
        return jnp.dot(h, w_ref[:, lo:lo + width], preferred_element_type=F32)

    def store(dst, col, y, rope, scale):
        for j in range(y.shape[1] // LANES):
            yj = y[:, j * LANES:(j + 1) * LANES]
            if rope:
                yj = _rope_lanes(yj, c, s1, s2)
            if scale != 1.0:
                yj = yj * scale
            dst[:, col + j * LANES:col + (j + 1) * LANES] = yj.astype(dst.dtype)

    qscale = HEAD_DIM ** -0.5
    store(aq_ref, 0, seg(0, 512), True, qscale)
    store(ak_ref, 0, seg(512, 512), True, 1.0)
    store(av_ref, 0, seg(1024, 512), False, 1.0)
    store(bq_ref, 0, seg(1536, 512), True, qscale)
    store(kcv_ref, 0, seg(2048, 256), False, 1.0)
    store(ksw_ref, 0, seg(2304, 128), True, 1.0)
    store(ksw_ref, 128, seg(2432, 128), False, 1.0)
    store(ksw_ref, 256, seg(2560, 128), True, 1.0)
    store(ksw_ref, 384, seg(2688, 128), False, 1.0)
    gate_ref[...] = jax.nn.sigmoid(seg(2816, 128))


def _proj(x2, g1, w_in_p, rc, rs1, rs2):
    t = x2.shape[0]
    nblk_s = SEQ // PROJ_TM
    row = lambda i: (i, 0)
    pos = lambda i: (i % nblk_s, 0)
    const = lambda i: (0, 0)
    outs = [jax.ShapeDtypeStruct((t, 512), BF16)] * 4 + [
        jax.ShapeDtypeStruct((t, 256), BF16), jax.ShapeDtypeStruct((t, 512), BF16),
        jax.ShapeDtypeStruct((t, LANES), F32)]
    out_specs = [pl.BlockSpec((PROJ_TM, 512), row)] * 4 + [
        pl.BlockSpec((PROJ_TM, 256), row), pl.BlockSpec((PROJ_TM, 512), row),
        pl.BlockSpec((PROJ_TM, LANES), row)]
    return pl.pallas_call(
        _proj_kernel,
        grid=(t // PROJ_TM,),
        in_specs=[pl.BlockSpec((PROJ_TM, D_MODEL), row), pl.BlockSpec((1, D_MODEL), const),
                  pl.BlockSpec((D_MODEL, D_IN_PAD), const),
                  pl.BlockSpec((PROJ_TM, LANES), pos), pl.BlockSpec((PROJ_TM, LANES), pos),
                  pl.BlockSpec((PROJ_TM, LANES), pos)],
        out_specs=out_specs,
        out_shape=outs,
        compiler_params=pltpu.CompilerParams(dimension_semantics=("arbitrary",),
                                             vmem_limit_bytes=VMEM_LIMIT),
        name="proj",
    )(x2, g1, w_in_p, rc, rs1, rs2)


def _dilated_kernel(q_ref, k_ref, v_ref, o_ref,
                    qf, kf, vf, qp, kp, vp, qt, vt, op, lp, on0, on1, on2, ln0, ln1, ln2):
    qf[...] = q_ref[...].astype(F32)
    kf[...] = k_ref[...].astype(F32)
    vf[...] = v_ref[...].astype(F32)
    kp[0:QBLK, :] = jnp.zeros((QBLK, LANES), BF16)
    vt[0] = jnp.zeros((LANES, QBLK), BF16)

    c2 = _row_iota((2 * QBLK, 2 * QBLK))
    a2 = _lane_iota((2 * QBLK, 2 * QBLK)) % QBLK
    band = jnp.where((c2 >= a2) & (c2 <= a2 + QBLK), 0.0, NEG)
    band_noprev = jnp.where(c2 < QBLK, NEG, band)
    c1 = _row_iota((QBLK, 2 * QBLK))
    a1 = _lane_iota((QBLK, 2 * QBLK)) % QBLK
    band_own = jnp.where(c1 <= a1, 0.0, NEG)
    top_sq = _row_iota((LANES, QBLK)) < HEAD_DIM

    for pi, (window, dil) in enumerate(DIL_PATTERNS):
        seg_len = SEQ // dil
        nseg_blk = seg_len // QBLK
        for r in range(dil):
            src = pl.ds(r, seg_len, stride=dil) if dil > 1 else pl.ds(0, SEQ)
            qp[pl.ds(r * seg_len, seg_len), :] = qf[src, :].astype(BF16)
            kp[pl.ds(QBLK + r * seg_len, seg_len), :] = kf[src, :].astype(BF16)
            vp[pl.ds(r * seg_len, seg_len), :] = vf[src, :].astype(BF16)

        def to_channel_major(j, carry):
            rows = pl.ds(pl.multiple_of(j * QBLK, QBLK), QBLK)
            qt[j] = qp[rows, :].T
            vt[j + 1] = vp[rows, :].T
            return carry

        lax.fori_loop(0, N_QBLK, to_channel_major, 0)

        def body(j, carry, nseg_blk=nseg_blk):
            row0 = pl.multiple_of(j * QBLK, QBLK)
            qtb = qt[j]
            zero = jnp.zeros_like(qtb)
            q_both = jnp.concatenate([jnp.where(top_sq, qtb, zero), jnp.where(top_sq, zero, qtb)], axis=1)
            if nseg_blk > 1:
                keys = kp[pl.ds(row0, 2 * QBLK), :]
                v_span = jnp.concatenate([vt[j], vt[j + 1]], axis=1)
                bias = jnp.where((j % nseg_blk) != 0, band, band_noprev)
            else:
                keys = kp[pl.ds(row0 + QBLK, QBLK), :]
                v_span = vt[j + 1]
                bias = band_own
            s = jnp.dot(keys, q_both, preferred_element_type=F32) + bias
            m = jnp.max(s, axis=0, keepdims=True)
            p = jnp.exp(s - m)
            l = jnp.sum(p, axis=0, keepdims=True)
            res = jnp.dot(v_span, p.astype(BF16), preferred_element_type=F32)
            lse = m + jnp.log(l)
            o_t = jnp.where(top_sq, res[:, 0:QBLK] / l[:, 0:QBLK], res[:, QBLK:] / l[:, QBLK:])
            lse_t = jnp.where(top_sq, jnp.broadcast_to(lse[:, 0:QBLK], (LANES, QBLK)),
                              jnp.broadcast_to(lse[:, QBLK:], (LANES, QBLK)))
            op[pl.ds(row0, QBLK), :] = o_t.T
            lp[pl.ds(row0, QBLK), :] = lse_t.T
            return carry

        lax.fori_loop(0, N_QBLK, body, 0)

        for r in range(dil):
            dst = pl.ds(r, seg_len, stride=dil) if dil > 1 else pl.ds(0, SEQ)
            src = pl.ds(r * seg_len, seg_len)
            (on0, on1, on2)[pi][dst, :] = op[src, :]
            (ln0, ln1, ln2)[pi][dst, :] = lp[src, :]

    l0, l1, l2 = ln0[...], ln1[...], ln2[...]
    mx = jnp.maximum(jnp.maximum(l0, l1), l2)
    e0, e1, e2 = jnp.exp(l0 - mx), jnp.exp(l1 - mx), jnp.exp(l2 - mx)
    den = e0 + e1 + e2
    out = (e0 / den) * on0[...] + (e1 / den) * on1[...] + (e2 / den) * on2[...]
    o_ref[...] = out.astype(o_ref.dtype)


def _dilated(aq, ak, av):
    t = aq.shape[0]
    nb = t // SEQ
    spec = pl.BlockSpec((SEQ, LANES), lambda b, hp: (b, hp))
    return pl.pallas_call(
        _dilated_kernel,
        grid=(nb, N_HEADS_DIL // 2),
        in_specs=[spec, spec, spec],
        out_specs=spec,
        out_shape=jax.ShapeDtypeStruct((t, 512), BF16),
        scratch_shapes=[pltpu.VMEM((SEQ, LANES), F32)] * 3
        + [pltpu.VMEM((SEQ, LANES), BF16), pltpu.VMEM((SEQ + QBLK, LANES), BF16),
           pltpu.VMEM((SEQ, LANES), BF16),
           pltpu.VMEM((N_QBLK, LANES, QBLK), BF16), pltpu.VMEM((N_QBLK + 1, LANES, QBLK), BF16)]
        + [pltpu.VMEM((SEQ, LANES), F32)] * 8,
        compiler_params=pltpu.CompilerParams(dimension_semantics=("arbitrary", "arbitrary"),
                                             vmem_limit_bytes=VMEM_LIMIT),
        name="dilated",
    )(aq, ak, av)


def _gelu_tanh(x):
    return 0.5 * x * (1.0 + jnp.tanh(np.sqrt(2.0 / np.pi).astype(np.float32) * (x + 0.044715 * (x * x * x))))


def _softmax_step(state, s, v_t):
    m, l, acc = state
    mn = jnp.maximum(m, jnp.max(s, axis=0, keepdims=True))
    al = jnp.exp(m - mn)
    p = jnp.exp(s - mn)
    l = al * l + jnp.sum(p, axis=0, keepdims=True)
    acc = al * acc + jnp.dot(v_t, p.astype(BF16), preferred_element_type=F32)
    return mn, l, acc


def _nsa_kernel(q_ref, kflat_ref, vflat_ref, ksw_ref, gate_ref,
                wk1_ref, wk2d_ref, pek_ref, wv1_ref, wv2t_ref, pev_ref,
                cc_ref, cs1_ref, cs2_ref, ovl_ref, exp_ref,
                o_ref,
                qt, vst, vwt, gtt, kc2, vct):
    g = pl.program_id(1)
    nq = NSA_REP * QBLK

    def to_channel_major(i, carry):
        rows = pl.ds(pl.multiple_of(i * QBLK, QBLK), QBLK)
        qt[i, 0:LANES, :] = q_ref[rows, 0:LANES].T
        qt[i, LANES:2 * LANES, :] = q_ref[rows, LANES:2 * LANES].T
        vs_t = ksw_ref[rows, LANES:2 * LANES].T
        vst[i] = jnp.where(g == 0, vs_t[0:HEAD_DIM], vs_t[HEAD_DIM:])
        vw_t = ksw_ref[rows, 3 * LANES:4 * LANES].T
        vwt[i] = jnp.where(g == 0, vw_t[0:HEAD_DIM], vw_t[HEAD_DIM:])
        gt = gate_ref[rows, :]
        gt = jnp.where(g == 0, gt, pltpu.roll(gt, LANES - 3 * NSA_REP, axis=1))
        gtt[i] = gt.T[0:16, :]
        return carry

    lax.fori_loop(0, N_QBLK, to_channel_major, 0)

    def hidden(flat_ref, w1_ref, pe_ref):
        ab = jnp.dot(flat_ref[...], w1_ref[...], preferred_element_type=F32)
        pb = (jnp.dot(pe_ref[0:16, :], w1_ref[:, 0:CMP_HIDDEN], preferred_element_type=F32)
              + jnp.dot(pe_ref[16:32, :], w1_ref[:, CMP_HIDDEN:], preferred_element_type=F32))
        hid = ab[:, 0:CMP_HIDDEN] + pltpu.roll(ab[:, CMP_HIDDEN:], N_CMP - 1, axis=0) + pb[0:1, :]
        return _gelu_tanh(hid).astype(BF16)

    hk = hidden(kflat_ref, wk1_ref, pek_ref)
    kc2[...] = _rope_lanes(jnp.dot(hk, wk2d_ref[...], preferred_element_type=F32),
                           cc_ref[...], cs1_ref[...], cs2_ref[...]).astype(BF16)
    hv = hidden(vflat_ref, wv1_ref, pev_ref)
    vct[...] = lax.dot_general(wv2t_ref[...], hv, _NT, preferred_element_type=F32).astype(BF16)

    key = _row_iota((QBLK, nq))
    qry = _lane_iota((QBLK, nq)) % QBLK
    causal = jnp.where(key <= qry, 0.0, NEG)
    upper = jnp.where(key > qry, 0.0, NEG)
    blk = _row_iota((N_SELBLK, QBLK))
    qry1 = _lane_iota((N_SELBLK, QBLK))

    def qblock(i, carry):
        row0 = pl.multiple_of(i * QBLK, QBLK)
        rows = pl.ds(row0, QBLK)
        q_t = jnp.concatenate([qt[i, r * HEAD_DIM:(r + 1) * HEAD_DIM, :] for r in range(NSA_REP)], axis=1)
        zero = jnp.zeros_like(q_t)
        q_pad = jnp.where(g == 0, jnp.concatenate([q_t, zero], axis=0), jnp.concatenate([zero, q_t], axis=0))

        def scores(keys):
            return jnp.dot(keys, q_pad, preferred_element_type=F32)

        def fresh():
            return (jnp.full((1, nq), LOWEST, F32), jnp.zeros((1, nq), F32), jnp.zeros((HEAD_DIM, nq), F32))

        cvalid = (CMP_STRIDE * key + (CMP_LEN - 1) <= row0 + qry) & (key < N_CMP - 1)
        s = jnp.where(cvalid, scores(kc2[...]), NEG)
        m = jnp.max(s, axis=0, keepdims=True)
        p = jnp.where(cvalid, jnp.exp(s - m), 0.0)
        l = jnp.sum(p, axis=0, keepdims=True)
        pc = p / jnp.where(l > 0, l, 1.0)
        o_cmp = jnp.dot(vct[...], pc.astype(BF16), preferred_element_type=F32)

        psum = pc[:, 0:QBLK] + pc[:, QBLK:2 * QBLK] + pc[:, 2 * QBLK:3 * QBLK] + pc[:, 3 * QBLK:]
        p_hi = psum.astype(BF16)
        p_lo = (psum - p_hi.astype(F32)).astype(BF16)
        ovl = ovl_ref[...]
        imp = (jnp.dot(ovl, p_hi, preferred_element_type=F32) + jnp.dot(ovl, p_lo, preferred_element_type=F32))
        t1 = row0 + qry1
        blk_t = t1 // SEL_LEN
        forced = (blk == 0) | (blk == blk_t) | (blk == blk_t - 1)
        v = jnp.where(blk * SEL_LEN <= t1, jnp.where(forced, BIG, imp), -BIG)
        selm = jnp.zeros((N_SELBLK, QBLK), F32)
        for _ in range(N_SEL):
            mx = jnp.max(v, axis=0, keepdims=True)
            first = jnp.min(jnp.where(v == mx, blk, N_SELBLK), axis=0, keepdims=True)
            hit = blk == first
            selm = jnp.where(hit, 1.0, selm)
            v = jnp.where(hit, LOWEST, v)
        notsel = jnp.concatenate([selm - 1.0, jnp.zeros((LANES - N_SELBLK, QBLK), F32)], axis=0).astype(BF16)

        def sel_bias(kb):
            b = jnp.dot(exp_ref[kb], notsel, preferred_element_type=F32)
            return jnp.concatenate([b] * NSA_REP, axis=1)

        def sel_body(kb, st):
            keys = ksw_ref[pl.ds(pl.multiple_of(kb * QBLK, QBLK), QBLK), 0:LANES]
            return _softmax_step(st, scores(keys) + sel_bias(kb), vst[kb])

        st = lax.fori_loop(0, i, sel_body, fresh())
        st = _softmax_step(st, scores(ksw_ref[rows, 0:LANES]) + sel_bias(i) + causal, vst[i])
        o_sel = st[2] / st[1]

        st = _softmax_step(fresh(), scores(ksw_ref[rows, 2 * LANES:3 * LANES]) + causal, vwt[i])

        def win_body(kb, st):
            keys = ksw_ref[pl.ds(pl.multiple_of(kb * QBLK, QBLK), QBLK), 2 * LANES:3 * LANES]
            return _softmax_step(st, scores(keys), vwt[kb])

        st = lax.fori_loop(jnp.maximum(i - (WIN // QBLK - 1), 0), i, win_body, st)
        far = jnp.maximum(i - WIN // QBLK, 0)
        far_bias = jnp.where(i >= WIN // QBLK, upper, NEG)
        keys = ksw_ref[pl.ds(pl.multiple_of(far * QBLK, QBLK), QBLK), 2 * LANES:3 * LANES]
        st = _softmax_step(st, scores(keys) + far_bias, vwt[far])
        o_win = st[2] / st[1]

        gti = gtt[i]

        def gate(br):
            return jnp.concatenate([gti[r * 3 + br:r * 3 + br + 1, :] for r in range(NSA_REP)], axis=1)

        out_t = gate(0) * o_cmp + gate(1) * o_sel + gate(2) * o_win
        for pair in range(NSA_REP // 2):
            both = jnp.concatenate([out_t[:, (2 * pair) * QBLK:(2 * pair + 1) * QBLK],
                                    out_t[:, (2 * pair + 1) * QBLK:(2 * pair + 2) * QBLK]], axis=0)
            o_ref[rows, pair * LANES:(pair + 1) * LANES] = both.T.astype(o_ref.dtype)
        return carry

    lax.fori_loop(0, N_QBLK, qblock, 0)


def _nsa(bq, kvflat, ksw, gates, wk1, wk2d, pek, wv1, wv2t, pev, cc, cs1, cs2, ovl_t, expand_t):
    t = bq.shape[0]
    nb = t // SEQ
    const2 = lambda b, g: (0, 0)
    return pl.pallas_call(
        _nsa_kernel,
        grid=(nb, N_KV_NSA),
        in_specs=[
            pl.BlockSpec((SEQ, 2 * LANES), lambda b, g: (b, g)),
            pl.BlockSpec((None, None, N_CMP, CMP_STRIDE * HEAD_DIM), lambda b, g: (b, g, 0, 0)),
            pl.BlockSpec((None, None, N_CMP, CMP_STRIDE * HEAD_DIM), lambda b, g: (b, N_KV_NSA + g, 0, 0)),
            pl.BlockSpec((SEQ, 4 * LANES), lambda b, g: (b, 0)),
            pl.BlockSpec((SEQ, LANES), lambda b, g: (b, 0)),
            pl.BlockSpec((CMP_STRIDE * HEAD_DIM, 2 * CMP_HIDDEN), const2),
            pl.BlockSpec((CMP_HIDDEN, LANES), const2),
            pl.BlockSpec((32, CMP_STRIDE * HEAD_DIM), const2),
            pl.BlockSpec((CMP_STRIDE * HEAD_DIM, 2 * CMP_HIDDEN), const2),
            pl.BlockSpec((HEAD_DIM, CMP_HIDDEN), const2),
            pl.BlockSpec((32, CMP_STRIDE * HEAD_DIM), const2),
            pl.BlockSpec((N_CMP, LANES), const2),
            pl.BlockSpec((N_CMP, LANES), const2),
            pl.BlockSpec((N_CMP, LANES), const2),
            pl.BlockSpec((N_SELBLK, LANES), const2),
            pl.BlockSpec((N_QBLK, QBLK, LANES), lambda b, g: (0, 0, 0)),
        ],
        out_specs=pl.BlockSpec((SEQ, 2 * LANES), lambda b, g: (b, g)),
        out_shape=jax.ShapeDtypeStruct((t, 512), BF16),
        scratch_shapes=[pltpu.VMEM((N_QBLK, 2 * LANES, QBLK), BF16),
                        pltpu.VMEM((N_QBLK, HEAD_DIM, QBLK), BF16),
                        pltpu.VMEM((N_QBLK, HEAD_DIM, QBLK), BF16),
                        pltpu.VMEM((N_QBLK, 16, QBLK), F32),
                        pltpu.VMEM((N_CMP, LANES), BF16),
                        pltpu.VMEM((HEAD_DIM, N_CMP), BF16)],
        compiler_params=pltpu.CompilerParams(dimension_semantics=("arbitrary", "arbitrary"),
                                             vmem_limit_bytes=VMEM_LIMIT),
        name="nsa",
    )(bq, kvflat, kvflat, ksw, gates, wk1, wk2d, pek, wv1, wv2t, pev, cc, cs1, cs2, ovl_t, expand_t)


def _post_kernel(x_ref, ma_ref, mb_ref, wo_ref, g2_ref, wr_ref, br_ref, tri_ref,
                 x1_ref, h2_ref, dest_ref, cw_ref, cnt_ref):
    x1 = (x_ref[...] + jnp.dot(ma_ref[...], wo_ref[0:512, :], preferred_element_type=F32)
          + jnp.dot(mb_ref[...], wo_ref[512:1024, :], preferred_element_type=F32))
    x1_ref[...] = x1
    ms = jnp.mean(x1 * x1, axis=-1, keepdims=True)
    h2 = x1 * lax.rsqrt(ms + EPS) * g2_ref[...]
    h2_ref[...] = h2.astype(BF16)

    lg = lax.dot_general(wr_ref[...], h2, _NT, precision=lax.Precision.HIGHEST,
                         preferred_element_type=F32) + br_ref[:, 0:1]
    gl = lg[0:N_GROUPS, :]
    sub = _row_iota(gl.shape)
    gmax = jnp.max(gl, axis=0, keepdims=True)
    p_g = 1.0 / jnp.sum(jnp.exp(gl - gmax), axis=0, keepdims=True)
    g_star = jnp.min(jnp.where(gl == gmax, sub, N_GROUPS), axis=0, keepdims=True)
    onehot = sub == g_star
    el = jnp.zeros_like(gl)
    for gi in range(N_GROUPS):
        lo = N_GROUPS + gi * EXPERTS_PER_GROUP
        el = el + jnp.where(g_star == gi, lg[lo:lo + EXPERTS_PER_GROUP, :], 0.0)
    emax = jnp.max(el, axis=0, keepdims=True)
    ee = jnp.exp(el - emax)
    pe = ee / jnp.sum(ee, axis=0, keepdims=True)
    v1 = jnp.max(pe, axis=0, keepdims=True)
    i1 = jnp.min(jnp.where(pe == v1, sub, EXPERTS_PER_GROUP), axis=0, keepdims=True)
    pe2 = jnp.where(sub == i1, -1.0, pe)
    v2 = jnp.max(pe2, axis=0, keepdims=True)
    i2 = jnp.min(jnp.where(pe2 == v2, sub, EXPERTS_PER_GROUP), axis=0, keepdims=True)
    tot = v1 + v2
    cw_ref[...] = jnp.where(sub == i1, v1 / tot * p_g, jnp.where(sub == i2, v2 / tot * p_g, 0.0))

    oh = jnp.where(onehot, 1.0, 0.0)
    before = jnp.dot(oh.astype(BF16), tri_ref[...], preferred_element_type=F32)
    rank = jnp.sum(oh * before, axis=0, keepdims=True)
    dest_ref[0] = g_star * MOE_TM + rank.astype(jnp.int32)
    cnt = jnp.sum(oh, axis=1, keepdims=True).astype(jnp.int32)
    cnt_ref[0] = jnp.broadcast_to(cnt, (N_GROUPS, LANES))


def _post(x2, mix_a, mix_b, w_o, g2, wr_t, br, tri):
    t = x2.shape[0]
    nt = t // MOE_TM
    row = lambda i: (i, 0)
    const = lambda i: (0, 0)
    return pl.pallas_call(
        _post_kernel,
        grid=(nt,),
        in_specs=[pl.BlockSpec((MOE_TM, D_MODEL), row), pl.BlockSpec((MOE_TM, 512), row),
                  pl.BlockSpec((MOE_TM, 512), row), pl.BlockSpec((D_MODEL, D_MODEL), const),
                  pl.BlockSpec((1, D_MODEL), const), pl.BlockSpec((72, D_MODEL), const),
                  pl.BlockSpec((72, LANES), const), pl.BlockSpec((MOE_TM, MOE_TM), const)],
        out_specs=[pl.BlockSpec((MOE_TM, D_MODEL), row), pl.BlockSpec((MOE_TM, D_MODEL), row),
                   pl.BlockSpec((1, 1, MOE_TM), lambda i: (i, 0, 0)),
                   pl.BlockSpec((N_GROUPS, MOE_TM), lambda i: (0, i)),
                   pl.BlockSpec((1, N_GROUPS, LANES), lambda i: (i, 0, 0))],
        out_shape=[jax.ShapeDtypeStruct((t, D_MODEL), F32), jax.ShapeDtypeStruct((t, D_MODEL), BF16),
                   jax.ShapeDtypeStruct((nt, 1, MOE_TM), jnp.int32),
                   jax.ShapeDtypeStruct((N_GROUPS, t), F32),
                   jax.ShapeDtypeStruct((nt, N_GROUPS, LANES), jnp.int32)],
        compiler_params=pltpu.CompilerParams(dimension_semantics=("arbitrary",),
                                             vmem_limit_bytes=VMEM_LIMIT),
        name="post",
    )(x2, mix_a, mix_b, w_o, g2, wr_t, br, tri)


def _gather_matrix(dest_row, target0):
    p = _row_iota((MOE_CHUNK, MOE_TM))
    return jnp.where(dest_row == target0 + p, 1.0, 0.0).astype(BF16)


def _moe_kernel(ig_ref, it_ref, ic_ref, islot_ref, ivalid_ref,
                h2_ref, dest_ref, cw_ref, wgu_ref, wd_ref, y_ref):
    k = pl.program_id(0)

    @pl.when(ivalid_ref[k] == 0)
    def _():
        y_ref[...] = jnp.zeros_like(y_ref)

    @pl.when(ivalid_ref[k] != 0)
    def _():
        pm = _gather_matrix(dest_ref[0], ig_ref[k] * MOE_TM + ic_ref[k] * MOE_CHUNK)
        xg = jnp.dot(pm, h2_ref[...], preferred_element_type=F32).astype(BF16)
        cw = jnp.concatenate([cw_ref[...], jnp.zeros((N_GROUPS, MOE_TM), F32)], axis=0)
        cw_hi = cw.astype(BF16)
        cw_lo = (cw - cw_hi.astype(F32)).astype(BF16)
        cws = (lax.dot_general(cw_hi, pm, _NT, preferred_element_type=F32)
               + lax.dot_general(cw_lo, pm, _NT, preferred_element_type=F32)).T
        dff = EXPERTS_PER_GROUP * D_FF_EXPERT
        parts = []
        for e in range(EXPERTS_PER_GROUP):
            lo = e * D_FF_EXPERT
            gate = jnp.dot(xg, wgu_ref[:, lo:lo + D_FF_EXPERT], preferred_element_type=F32)
            up = jnp.dot(xg, wgu_ref[:, dff + lo:dff + lo + D_FF_EXPERT], preferred_element_type=F32)
            he = gate * jax.nn.sigmoid(gate) * up
            parts.append((he * cws[:, e:e + 1]).astype(BF16))
        hid = jnp.concatenate(parts, axis=1)
        y_ref[...] = jnp.dot(hid, wd_ref[...], preferred_element_type=F32).astype(y_ref.dtype)


def _moe(items, h2, dest, cw_t, wgu, wd):
    t = h2.shape[0]
    nt = t // MOE_TM
    dff = EXPERTS_PER_GROUP * D_FF_EXPERT
    grid_spec = pltpu.PrefetchScalarGridSpec(
        num_scalar_prefetch=5,
        grid=(nt * MOE_SLOTS,),
        in_specs=[
            pl.BlockSpec((MOE_TM, D_MODEL), lambda k, ig, it, ic, isl, iv: (it[k], 0)),
            pl.BlockSpec((1, 1, MOE_TM), lambda k, ig, it, ic, isl, iv: (it[k], 0, 0)),
            pl.BlockSpec((N_GROUPS, MOE_TM), lambda k, ig, it, ic, isl, iv: (0, it[k])),
            pl.BlockSpec((None, D_MODEL, 2 * dff), lambda k, ig, it, ic, isl, iv: (ig[k], 0, 0)),
            pl.BlockSpec((None, dff, D_MODEL), lambda k, ig, it, ic, isl, iv: (ig[k], 0, 0)),
        ],
        out_specs=pl.BlockSpec((MOE_CHUNK, D_MODEL), lambda k, ig, it, ic, isl, iv: (isl[k], 0)),
    )
    return pl.pallas_call(
        _moe_kernel,
        grid_spec=grid_spec,
        out_shape=jax.ShapeDtypeStruct((nt * MOE_SLOTS * MOE_CHUNK, D_MODEL), BF16),
        compiler_params=pltpu.CompilerParams(dimension_semantics=("arbitrary",),
                                             vmem_limit_bytes=VMEM_LIMIT),
        name="moe",
    )(*items, h2, dest, cw_t, wgu, wd)


def _combine_kernel(nused_ref, sg_ref, sc_ref, x1_ref, ys_ref, dest_ref, gf_ref, o_ref, acc_ref):
    i = pl.program_id(0)
    acc_ref[...] = x1_ref[...]

    def body(s, carry):
        pm = _gather_matrix(dest_ref[0], sg_ref[i, s] * MOE_TM + sc_ref[i, s] * MOE_CHUNK)
        yc = ys_ref[pl.ds(pl.multiple_of(s * MOE_CHUNK, MOE_CHUNK), MOE_CHUNK), :]
        acc_ref[...] += lax.dot_general(pm, yc, _TN, preferred_element_type=F32)
        return carry

    lax.fori_loop(0, nused_ref[i], body, 0)
    xo = acc_ref[...]
    ms = jnp.mean(xo * xo, axis=-1, keepdims=True)
    o_ref[...] = xo * lax.rsqrt(ms + EPS) * gf_ref[...]


def _combine(nused, slot_g, slot_c, x1, ys, dest, gf):
    t = x1.shape[0]
    nt = t // MOE_TM
    grid_spec = pltpu.PrefetchScalarGridSpec(
        num_scalar_prefetch=3,
        grid=(nt,),
        in_specs=[
            pl.BlockSpec((MOE_TM, D_MODEL), lambda i, a, b, c: (i, 0)),
            pl.BlockSpec((MOE_SLOTS * MOE_CHUNK, D_MODEL), lambda i, a, b, c: (i, 0)),
            pl.BlockSpec((1, 1, MOE_TM), lambda i, a, b, c: (i, 0, 0)),
            pl.BlockSpec((1, D_MODEL), lambda i, a, b, c: (0, 0)),
        ],
        out_specs=pl.BlockSpec((MOE_TM, D_MODEL), lambda i, a, b, c: (i, 0)),
        scratch_shapes=[pltpu.VMEM((MOE_TM, D_MODEL), F32)],
    )
    return pl.pallas_call(
        _combine_kernel,
        grid_spec=grid_spec,
        out_shape=jax.ShapeDtypeStruct((t, D_MODEL), F32),
        compiler_params=pltpu.CompilerParams(dimension_semantics=("arbitrary",),
                                             vmem_limit_bytes=VMEM_LIMIT),
        name="combine",
    )(nused, slot_g, slot_c, x1, ys, dest, gf)


def _moe_plan(cnt):
    nt = cnt.shape[0]
    nch = (cnt + MOE_CHUNK - 1) // MOE_CHUNK
    nused = jnp.sum(nch, axis=1)
    base = jnp.cumsum(nch, axis=1) - nch
    spare = MOE_SLOTS - nused
    n_items = nch.T + jnp.where(jnp.arange(N_GROUPS)[:, None] == N_GROUPS - 1, spare[None, :], 0)
    flat = n_items.reshape(-1)
    ends = jnp.cumsum(flat)
    starts = ends - flat
    k = jnp.arange(nt * MOE_SLOTS)
    gt = jnp.sum((ends[None, :] <= k[:, None]).astype(jnp.int32), axis=1)
    ig = gt // nt
    it = gt % nt
    local = k - starts[gt]
    real = nch[it, ig]
    valid = local < real
    islot = it * MOE_SLOTS + jnp.where(valid, base[it, ig] + local, nused[it] + local - real)
    ic = jnp.where(valid, local, 0)
    items = tuple(a.astype(jnp.int32) for a in (ig, it, ic, islot, valid))
    s = jnp.arange(MOE_SLOTS)
    slot_g = jnp.sum((s[None, :, None] >= (base + nch)[:, None, :]).astype(jnp.int32), axis=2)
    slot_g = jnp.minimum(slot_g, N_GROUPS - 1)
    slot_c = s[None, :] - jnp.take_along_axis(base, slot_g, axis=1)
    return items, nused.astype(jnp.int32), slot_g.astype(jnp.int32), slot_c.astype(jnp.int32)


def _rope_tables(pos):
    half = ROT_DIM // 2
    inv_freq = ROPE_THETA ** (-jnp.arange(0, ROT_DIM, 2, dtype=F32) / ROT_DIM)
    ang = pos.astype(F32)[:, None] * inv_freq[None, :]
    cos, sin = jnp.cos(ang), jnp.sin(ang)
    l64 = np.arange(LANES) % HEAD_DIM
    f = l64 % half
    first = jnp.asarray(l64 < half)[None, :]
    second = jnp.asarray((l64 >= half) & (l64 < ROT_DIM))[None, :]
    c = jnp.where(first | second, cos[:, f], 1.0)
    s1 = jnp.where(first, -sin[:, f], 0.0)
    s2 = jnp.where(second, sin[:, f], 0.0)
    return c.astype(F32), s1.astype(F32), s2.astype(F32)


def _overlap_table_t():
    cs = np.arange(N_CMP)[None, :] * CMP_STRIDE
    js = np.arange(N_SELBLK)[:, None] * SEL_LEN
    ov = np.clip(np.minimum(cs + CMP_LEN, js + SEL_LEN) - np.maximum(cs, js), 0, None) / CMP_LEN
    ov[:, N_CMP - 1] = 0.0
    return ov.astype(np.float32)


def _expand_table_t():
    out = np.zeros((N_QBLK, QBLK, LANES), np.float32)
    for kb in range(N_QBLK):
        for c in range(QBLK):
            out[kb, c, (kb * QBLK + c) // SEL_LEN] = -NEG
    return out


def kernel(x, norm1_g, w_in, pe_kc, w_kc1, w_kc2, pe_vc, w_vc1, w_vc2, w_o, norm2_g, w_rg, b_rg, w_re, b_re,
           w_gate, w_up, w_down, norm_f_g):
    nb, s, d = x.shape
    assert (s, d) == (SEQ, D_MODEL) and norm1_g.shape[0] == 1
    t = nb * s
    x2 = x.reshape(t, d)

    w_in_p = jnp.pad(w_in[0], ((0, 0), (0, D_IN_PAD - w_in.shape[2]))).astype(BF16)
    rc, rs1, rs2 = _rope_tables(jnp.arange(SEQ))
    cc, cs1, cs2 = _rope_tables(jnp.arange(N_CMP) * CMP_STRIDE + CMP_LEN - 1)
    half_flat = CMP_STRIDE * HEAD_DIM

    def cmp_w1(w1):
        return jnp.concatenate([w1[:half_flat], w1[half_flat:]], axis=1).astype(BF16)

    def cmp_pe(pe):
        rows = pe.reshape(2, half_flat)
        return jnp.concatenate([jnp.broadcast_to(rows[0:1], (16, half_flat)),
                                jnp.broadcast_to(rows[1:2], (16, half_flat))], axis=0).astype(BF16)

    wk2d = jnp.concatenate([w_kc2[0], w_kc2[0]], axis=1).astype(BF16)
    wv2t = w_vc2[0].T.astype(BF16)
    ovl_t = jnp.asarray(_overlap_table_t(), BF16)
    expand_t = jnp.asarray(_expand_table_t(), BF16)

    aq, ak, av, bq, kcv, ksw, gates = _proj(x2, norm1_g, w_in_p, rc, rs1, rs2)
    mix_a = _dilated(aq, ak, av)

    kvflat = kcv.reshape(nb, N_CMP, CMP_STRIDE, 2 * N_KV_NSA, HEAD_DIM).transpose(0, 3, 1, 2, 4)
    kvflat = kvflat.reshape(nb, 2 * N_KV_NSA, N_CMP, half_flat)
    mix_b = _nsa(bq, kvflat, ksw, gates, cmp_w1(w_kc1[0]), wk2d, cmp_pe(pe_kc[0]),
                 cmp_w1(w_vc1[0]), wv2t, cmp_pe(pe_vc[0]), cc, cs1, cs2, ovl_t, expand_t)

    wr_t = jnp.concatenate([w_rg[0].T, w_re[0].reshape(d, -1).T], axis=0)
    br = jnp.broadcast_to(jnp.concatenate([b_rg[0], b_re[0].reshape(-1)])[:, None], (72, LANES))
    tri = jnp.asarray(np.triu(np.ones((MOE_TM, MOE_TM), np.float32), 1), BF16)
    x1, h2, dest, cw_t, cnt = _post(x2, mix_a, mix_b, w_o[0].astype(BF16), norm2_g, wr_t, br, tri)

    dff = EXPERTS_PER_GROUP * D_FF_EXPERT

    def group_cols(w):
        return w.reshape(N_GROUPS, EXPERTS_PER_GROUP, d, D_FF_EXPERT).transpose(0, 2, 1, 3).reshape(N_GROUPS, d, dff)

    wgu = jnp.concatenate([group_cols(w_gate[0]), group_cols(w_up[0])], axis=2).astype(BF16)
    wd = w_down[0].reshape(N_GROUPS, dff, d).astype(BF16)
    items, nused, slot_g, slot_c = _moe_plan(cnt[:, :, 0])
    ys = _moe(items, h2, dest, cw_t, wgu, wd)
    out = _combine(nused, slot_g, slot_c, x1, ys, dest, norm_f_g.reshape(1, d))
    return out.reshape(nb, s, d)
```

```python
import numpy as np
import jax
import jax.numpy as jnp
from jax import lax
from jax.experimental import pallas as pl
from jax.experimental.pallas import tpu as pltpu

F32 = jnp.float32
BF16 = jnp.bfloat16

D_MODEL = 1024
SEQ = 2048
HEAD_DIM = 64
N_HEADS_DIL = 8
DIL_PATTERNS = ((128, 1), (512, 4), (2048, 16))
N_HEADS_NSA = 8
N_KV_NSA = 2
NSA_REP = N_HEADS_NSA // N_KV_NSA
CMP_STRIDE = 16
CMP_LEN = 32
CMP_HIDDEN = 256
SEL_LEN = 64
N_SEL = 8
WIN = 512
ROPE_THETA = 500000.0
ROT_DIM = HEAD_DIM // 4
N_GROUPS = 8
EXPERTS_PER_GROUP = 8
D_FF_EXPERT = 256
EPS = 1e-6
NEG = -1e30
BIG = 1e9
LOWEST = -3.0e38

LANES = 128
QBLK = 128
N_QBLK = SEQ // QBLK
N_CMP = SEQ // CMP_STRIDE
N_SELBLK = SEQ // SEL_LEN
D_IN_PAD = 23 * LANES
PROJ_TM = 512
SEL_GROUP = 4
DIL_UNROLL = 8
MOE_TM = 1024
MOE_CHUNK = 128
MOE_SLOTS = 16
VMEM_LIMIT = 56 * 1024 * 1024

_NT = (((1,), (1,)), ((), ()))
_TN = (((0,), (0,)), ((), ()))


def _lane_iota(shape):
    return lax.broadcasted_iota(jnp.int32, shape, 1)


def _row_iota(shape):
    return lax.broadcasted_iota(jnp.int32, shape, 0)


def _rope_lanes(y, c, s1, s2):
    return y * c + pltpu.roll(y, LANES - ROT_DIM // 2, axis=1) * s1 + pltpu.roll(y, ROT_DIM // 2, axis=1) * s2


def _proj_kernel(x_ref, g_ref, w_ref, c_ref, s1_ref, s2_ref,
                 aq_ref, ak_ref, av_ref, bq_ref, kcv_ref, ksw_ref, gate_ref):
    x = x_ref[...]
    ms = jnp.mean(x * x, axis=-1, keepdims=True)
    h = (x * lax.rsqrt(ms + EPS) * g_ref[...]).astype(BF16)
    c = c_ref[...]
    s1 = s1_ref[...]
    s2 = s2_ref[...]

    def seg(lo, width):
        return jnp.dot(h, w_ref[:, lo:lo + width], preferred_element_type=F32)

    def store(dst, col, y, rope, scale):
        for j in range(y.shape[1] // LANES):
            yj = y[:, j * LANES:(j + 1) * LANES]
            if rope:
                yj = _rope_lanes(yj, c, s1, s2)
            if scale != 1.0:
                yj = yj * scale
            dst[:, col + j * LANES:col + (j + 1) * LANES] = yj.astype(dst.dtype)

    qscale = HEAD_DIM ** -0.5
    store(aq_ref, 0, seg(0, 512), True, qscale)
    store(ak_ref, 0, seg(512, 512), True, 1.0)
    store(av_ref, 0, seg(1024, 512), False, 1.0)
    store(bq_ref, 0, seg(1536, 512), True, qscale)
    store(kcv_ref, 0, seg(2048, 256), False, 1.0)
    store(ksw_ref, 0, seg(2304, 128), True, 1.0)
    store(ksw_ref, 128, seg(2432, 128), False, 1.0)
    store(ksw_ref, 256, seg(2560, 128), True, 1.0)
    store(ksw_ref, 384, seg(2688, 128), False, 1.0)
    gate_ref[...] = jax.nn.sigmoid(seg(2816, 128))


def _proj(x2, g1, w_in_p, rc, rs1, rs2):
    t = x2.shape[0]
    nblk_s = SEQ // PROJ_TM
    row = lambda i: (i, 0)
    pos = lambda i: (i % nblk_s, 0)
    const = lambda i: (0, 0)
    outs = [jax.ShapeDtypeStruct((t, 512), BF16)] * 4 + [
        jax.ShapeDtypeStruct((t, 256), BF16), jax.ShapeDtypeStruct((t, 512), BF16),
        jax.ShapeDtypeStruct((t, LANES), F32)]
    out_specs = [pl.BlockSpec((PROJ_TM, 512), row)] * 4 + [
        pl.BlockSpec((PROJ_TM, 256), row), pl.BlockSpec((PROJ_TM, 512), row),
        pl.BlockSpec((PROJ_TM, LANES), row)]
    return pl.pallas_call(
        _proj_kernel,
        grid=(t // PROJ_TM,),
        in_specs=[pl.BlockSpec((PROJ_TM, D_MODEL), row), pl.BlockSpec((1, D_MODEL), const),
                  pl.BlockSpec((D_MODEL, D_IN_PAD), const),
                  pl.BlockSpec((PROJ_TM, LANES), pos), pl.BlockSpec((PROJ_TM, LANES), pos),
                  pl.BlockSpec((PROJ_TM, LANES), pos)],
        out_specs=out_specs,
        out_shape=outs,
        compiler_params=pltpu.CompilerParams(dimension_semantics=("arbitrary",),
                                             vmem_limit_bytes=VMEM_LIMIT),
        name="proj",
    )(x2, g1, w_in_p, rc, rs1, rs2)


def _dilated_kernel(q_ref, k_ref, v_ref, o_ref,
                    qf, kf, vf, qp, kp, vp, qt, vt, op, lp, on0, on1, on2, ln0, ln1, ln2):
    qf[...] = q_ref[...].astype(F32)
    kf[...] = k_ref[...].astype(F32)
    vf[...] = v_ref[...].astype(F32)
    kp[0:QBLK, :] = jnp.zeros((QBLK, LANES), BF16)
    vt[0] = jnp.zeros((LANES, QBLK), BF16)

    c2 = _row_iota((2 * QBLK, 2 * QBLK))
    a2 = _lane_iota((2 * QBLK, 2 * QBLK)) % QBLK
    band = jnp.where((c2 >= a2) & (c2 <= a2 + QBLK), 0.0, NEG)
    band_noprev = jnp.where(c2 < QBLK, NEG, band)
    c1 = _row_iota((QBLK, 2 * QBLK))
    a1 = _lane_iota((QBLK, 2 * QBLK)) % QBLK
    band_own = jnp.where(c1 <= a1, 0.0, NEG)
    top_sq = _row_iota((LANES, QBLK)) < HEAD_DIM

    for pi, (window, dil) in enumerate(DIL_PATTERNS):
        seg_len = SEQ // dil
        nseg_blk = seg_len // QBLK
        for r in range(dil):
            src = pl.ds(r, seg_len, stride=dil) if dil > 1 else pl.ds(0, SEQ)
            qp[pl.ds(r * seg_len, seg_len), :] = qf[src, :].astype(BF16)
            kp[pl.ds(QBLK + r * seg_len, seg_len), :] = kf[src, :].astype(BF16)
            vp[pl.ds(r * seg_len, seg_len), :] = vf[src, :].astype(BF16)

        for j in range(N_QBLK):
            rows = pl.ds(j * QBLK, QBLK)
            qt[j] = qp[rows, :].T
            vt[j + 1] = vp[rows, :].T

        def body(u, carry, nseg_blk=nseg_blk):
            js = [u * DIL_UNROLL + d for d in range(DIL_UNROLL)]
            scores = []
            for d, j in enumerate(js):
                row0 = pl.multiple_of(j * QBLK, QBLK)
                qtb = qt[j]
                zero = jnp.zeros_like(qtb)
                q_both = jnp.concatenate([jnp.where(top_sq, qtb, zero), jnp.where(top_sq, zero, qtb)], axis=1)
                if nseg_blk == 1:
                    keys, bias = kp[pl.ds(row0 + QBLK, QBLK), :], band_own
                else:
                    keys = kp[pl.ds(row0, 2 * QBLK), :]
                    if nseg_blk <= DIL_UNROLL:
                        bias = band_noprev if d % nseg_blk == 0 else band
                    else:
                        bias = jnp.where(u == 0, band_noprev, band) if d == 0 else band
                scores.append(jnp.dot(keys, q_both, preferred_element_type=F32) + bias)
            stats = []
            for s in scores:
                m = jnp.max(s, axis=0, keepdims=True)
                p = jnp.exp(s - m)
                stats.append((m, jnp.sum(p, axis=0, keepdims=True), p.astype(BF16)))
            outs = []
            for j, (m, l, p) in zip(js, stats):
                v_span = vt[j + 1] if nseg_blk == 1 else jnp.concatenate([vt[j], vt[j + 1]], axis=1)
                outs.append(jnp.dot(v_span, p, preferred_element_type=F32))
            for j, (m, l, p), res in zip(js, stats, outs):
                row0 = pl.multiple_of(j * QBLK, QBLK)
                lse = m + jnp.log(l)
                o_t = jnp.where(top_sq, res[:, 0:QBLK] / l[:, 0:QBLK], res[:, QBLK:] / l[:, QBLK:])
                lse_t = jnp.where(top_sq, jnp.broadcast_to(lse[:, 0:QBLK], (LANES, QBLK)),
                                  jnp.broadcast_to(lse[:, QBLK:], (LANES, QBLK)))
                op[pl.ds(row0, QBLK), :] = o_t.T
                lp[pl.ds(row0, QBLK), :] = lse_t.T
            return carry

        lax.fori_loop(0, N_QBLK // DIL_UNROLL, body, 0)

        for r in range(dil):
            dst = pl.ds(r, seg_len, stride=dil) if dil > 1 else pl.ds(0, SEQ)
            src = pl.ds(r * seg_len, seg_len)
            (on0, on1, on2)[pi][dst, :] = op[src, :]
            (ln0, ln1, ln2)[pi][dst, :] = lp[src, :]

    l0, l1, l2 = ln0[...], ln1[...], ln2[...]
    mx = jnp.maximum(jnp.maximum(l0, l1), l2)
    e0, e1, e2 = jnp.exp(l0 - mx), jnp.exp(l1 - mx), jnp.exp(l2 - mx)
    den = e0 + e1 + e2
    out = (e0 / den) * on0[...] + (e1 / den) * on1[...] + (e2 / den) * on2[...]
    o_ref[...] = out.astype(o_ref.dtype)


def _dilated(aq, ak, av):
    t = aq.shape[0]
    nb = t // SEQ
    spec = pl.BlockSpec((SEQ, LANES), lambda b, hp: (b, hp))
    return pl.pallas_call(
        _dilated_kernel,
        grid=(nb, N_HEADS_DIL // 2),
        in_specs=[spec, spec, spec],
        out_specs=spec,
        out_shape=jax.ShapeDtypeStruct((t, 512), BF16),
        scratch_shapes=[pltpu.VMEM((SEQ, LANES), F32)] * 3
        + [pltpu.VMEM((SEQ, LANES), BF16), pltpu.VMEM((SEQ + QBLK, LANES), BF16),
           pltpu.VMEM((SEQ, LANES), BF16),
           pltpu.VMEM((N_QBLK, LANES, QBLK), BF16), pltpu.VMEM((N_QBLK + 1, LANES, QBLK), BF16)]
        + [pltpu.VMEM((SEQ, LANES), F32)] * 8,
        compiler_params=pltpu.CompilerParams(dimension_semantics=("arbitrary", "arbitrary"),
                                             vmem_limit_bytes=VMEM_LIMIT),
        name="dilated",
    )(aq, ak, av)


def _gelu_tanh(x):
    return 0.5 * x * (1.0 + jnp.tanh(np.sqrt(2.0 / np.pi).astype(np.float32) * (x + 0.044715 * (x * x * x))))


def _softmax_step(state, s, v_t):
    m, l, acc = state
    mn = jnp.maximum(m, jnp.max(s, axis=0, keepdims=True))
    al = jnp.exp(m - mn)
    p = jnp.exp(s - mn)
    l = al * l + jnp.sum(p, axis=0, keepdims=True)
    acc = al * acc + jnp.dot(v_t, p.astype(BF16), preferred_element_type=F32)
    return mn, l, acc


def _nsa_kernel(q_ref, kflat_ref, vflat_ref, ksw_ref, gate_ref,
                wk1_ref, wk2d_ref, pek_ref, wv1_ref, wv2t_ref, pev_ref,
                cc_ref, cs1_ref, cs2_ref, ovl_ref, exp_ref,
                o_ref,
                qt, vst, vwt, gtt, kc2, vct):
    g = pl.program_id(1)
    nq = NSA_REP * QBLK

    for i in range(N_QBLK):
        rows = pl.ds(i * QBLK, QBLK)
        qt[i, 0:LANES, :] = q_ref[rows, 0:LANES].T
        qt[i, LANES:2 * LANES, :] = q_ref[rows, LANES:2 * LANES].T
        vs_t = ksw_ref[rows, LANES:2 * LANES].T
        vst[i] = jnp.where(g == 0, vs_t[0:HEAD_DIM], vs_t[HEAD_DIM:])
        vw_t = ksw_ref[rows, 3 * LANES:4 * LANES].T
        vwt[i] = jnp.where(g == 0, vw_t[0:HEAD_DIM], vw_t[HEAD_DIM:])
        gt = gate_ref[rows, :]
        gt = jnp.where(g == 0, gt, pltpu.roll(gt, LANES - 3 * NSA_REP, axis=1))
        gtt[i] = gt.T[0:16, :]

    def hidden(flat_ref, w1_ref, pe_ref):
        ab = jnp.dot(flat_ref[...], w1_ref[...], preferred_element_type=F32)
        pb = (jnp.dot(pe_ref[0:16, :], w1_ref[:, 0:CMP_HIDDEN], preferred_element_type=F32)
              + jnp.dot(pe_ref[16:32, :], w1_ref[:, CMP_HIDDEN:], preferred_element_type=F32))
        hid = ab[:, 0:CMP_HIDDEN] + pltpu.roll(ab[:, CMP_HIDDEN:], N_CMP - 1, axis=0) + pb[0:1, :]
        return _gelu_tanh(hid).astype(BF16)

    hk = hidden(kflat_ref, wk1_ref, pek_ref)
    kc2[...] = _rope_lanes(jnp.dot(hk, wk2d_ref[...], preferred_element_type=F32),
                           cc_ref[...], cs1_ref[...], cs2_ref[...]).astype(BF16)
    hv = hidden(vflat_ref, wv1_ref, pev_ref)
    vct[...] = lax.dot_general(wv2t_ref[...], hv, _NT, preferred_element_type=F32).astype(BF16)

    key = _row_iota((QBLK, nq))
    qry = _lane_iota((QBLK, nq)) % QBLK
    causal = jnp.where(key <= qry, 0.0, NEG)
    upper = jnp.where(key > qry, 0.0, NEG)
    masked = jnp.full((QBLK, nq), NEG, F32)
    blk = _row_iota((N_SELBLK, QBLK))
    qry1 = _lane_iota((N_SELBLK, QBLK))
    grp_keys = SEL_GROUP * QBLK
    win_blocks = WIN // QBLK

    def group_mask(i, first_blk):
        rel = i - first_blk
        return jnp.concatenate(
            [jnp.where(d > rel, masked, jnp.where(d == rel, causal, 0.0)) for d in range(SEL_GROUP)], axis=0)

    def qpair(u, full_window):
        blocks = [2 * u, 2 * u + 1]
        row0s = [pl.multiple_of(i * QBLK, QBLK) for i in blocks]

        q_pads = []
        for i in blocks:
            q_t = jnp.concatenate([qt[i, r * HEAD_DIM:(r + 1) * HEAD_DIM, :] for r in range(NSA_REP)], axis=1)
            zero = jnp.zeros_like(q_t)
            q_pads.append(jnp.where(g == 0, jnp.concatenate([q_t, zero], axis=0),
                                    jnp.concatenate([zero, q_t], axis=0)))
        s_cmp = [jnp.dot(kc2[...], qp, preferred_element_type=F32) for qp in q_pads]
        s_win, v_win = [], []
        for i, qp in zip(blocks, q_pads):
            if full_window:
                first = i - win_blocks
                keys = ksw_ref[pl.ds(pl.multiple_of(first * QBLK, QBLK), (win_blocks + 1) * QBLK), 2 * LANES:3 * LANES]
                s = jnp.dot(keys, qp, preferred_element_type=F32)
                s = jnp.concatenate([s[0:QBLK] + upper, s[QBLK:win_blocks * QBLK], s[win_blocks * QBLK:] + causal],
                                    axis=0)
                v_t = jnp.concatenate([vwt[first + d] for d in range(win_blocks + 1)], axis=1)
            else:
                s = jnp.dot(ksw_ref[0:grp_keys, 2 * LANES:3 * LANES], qp, preferred_element_type=F32) + group_mask(i, 0)
                v_t = jnp.concatenate([vwt[d] for d in range(SEL_GROUP)], axis=1)
            s_win.append(s)
            v_win.append(v_t)

        pcs = []
        for row0, s in zip(row0s, s_cmp):
            cvalid = (CMP_STRIDE * key + (CMP_LEN - 1) <= row0 + qry) & (key < N_CMP - 1)
            s = jnp.where(cvalid, s, NEG)
            m = jnp.max(s, axis=0, keepdims=True)
            p = jnp.where(cvalid, jnp.exp(s - m), 0.0)
            l = jnp.sum(p, axis=0, keepdims=True)
            pcs.append(p / jnp.where(l > 0, l, 1.0))
        o_cmp = [jnp.dot(vct[...], pc.astype(BF16), preferred_element_type=F32) for pc in pcs]

        ovl = ovl_ref[...]
        imps = []
        for pc in pcs:
            psum = pc[:, 0:QBLK] + pc[:, QBLK:2 * QBLK] + pc[:, 2 * QBLK:3 * QBLK] + pc[:, 3 * QBLK:]
            p_hi = psum.astype(BF16)
            p_lo = (psum - p_hi.astype(F32)).astype(BF16)
            imps.append(jnp.dot(ovl, p_hi, preferred_element_type=F32) + jnp.dot(ovl, p_lo, preferred_element_type=F32))

        o_win = []
        for s, v_t in zip(s_win, v_win):
            m = jnp.max(s, axis=0, keepdims=True)
            p = jnp.exp(s - m)
            l = jnp.sum(p, axis=0, keepdims=True)
            o_win.append(jnp.dot(v_t, p.astype(BF16), preferred_element_type=F32) / l)

        notsels = []
        for row0, imp in zip(row0s, imps):
            t1 = row0 + qry1
            blk_t = t1 // SEL_LEN
            forced = (blk == 0) | (blk == blk_t) | (blk == blk_t - 1)
            v = jnp.where(blk * SEL_LEN <= t1, jnp.where(forced, BIG, imp), -BIG)
            selm = jnp.zeros((N_SELBLK, QBLK), F32)
            for _ in range(N_SEL):
                mx = jnp.max(v, axis=0, keepdims=True)
                first_hit = jnp.min(jnp.where(v == mx, blk, N_SELBLK), axis=0, keepdims=True)
                hit = blk == first_hit
                selm = jnp.where(hit, 1.0, selm)
                v = jnp.where(hit, LOWEST, v)
            notsels.append(jnp.concatenate([selm - 1.0, jnp.zeros((LANES - N_SELBLK, QBLK), F32)], axis=0).astype(BF16))

        q_both = jnp.concatenate(q_pads, axis=1)
        ns_both = jnp.concatenate(notsels, axis=1)

        def sel_scores(t):
            keys = ksw_ref[pl.ds(pl.multiple_of(t * grp_keys, grp_keys), grp_keys), 0:LANES]
            b = jnp.dot(exp_ref[t], ns_both, preferred_element_type=F32)
            bias = jnp.concatenate([b[:, 0:QBLK]] * NSA_REP + [b[:, QBLK:]] * NSA_REP, axis=1)
            return jnp.dot(keys, q_both, preferred_element_type=F32) + bias

        def sel_values(t):
            return jnp.concatenate([vst[t * SEL_GROUP + d] for d in range(SEL_GROUP)], axis=1)

        def sel_body(t, st):
            return _softmax_step(st, sel_scores(t), sel_values(t))

        n_full = blocks[0] // SEL_GROUP
        st = (jnp.full((1, 2 * nq), LOWEST, F32), jnp.zeros((1, 2 * nq), F32), jnp.zeros((HEAD_DIM, 2 * nq), F32))
        st = lax.fori_loop(0, n_full, sel_body, st)
        mask = jnp.concatenate([group_mask(i, n_full * SEL_GROUP) for i in blocks], axis=1)
        st = _softmax_step(st, sel_scores(n_full) + mask, sel_values(n_full))
        o_sel_both = st[2] / st[1]

        for n, (i, row0) in enumerate(zip(blocks, row0s)):
            gti = gtt[i]

            def gate(br):
                return jnp.concatenate([gti[r * 3 + br:r * 3 + br + 1, :] for r in range(NSA_REP)], axis=1)

            out_t = gate(0) * o_cmp[n] + gate(1) * o_sel_both[:, n * nq:(n + 1) * nq] + gate(2) * o_win[n]
            for pair in range(NSA_REP // 2):
                both = jnp.concatenate([out_t[:, (2 * pair) * QBLK:(2 * pair + 1) * QBLK],
                                        out_t[:, (2 * pair + 1) * QBLK:(2 * pair + 2) * QBLK]], axis=0)
                o_ref[pl.ds(row0, QBLK), pair * LANES:(pair + 1) * LANES] = both.T.astype(o_ref.dtype)

    def head_pairs(u, carry):
        qpair(u, False)
        return carry

    def tail_pairs(u, carry):
        qpair(u, True)
        return carry

    lax.fori_loop(0, win_blocks // 2, head_pairs, 0)
    lax.fori_loop(win_blocks // 2, N_QBLK // 2, tail_pairs, 0)


def _nsa(bq, kvflat, ksw, gates, wk1, wk2d, pek, wv1, wv2t, pev, cc, cs1, cs2, ovl_t, expand_t):
    t = bq.shape[0]
    nb = t // SEQ
    const2 = lambda b, g: (0, 0)
    return pl.pallas_call(
        _nsa_kernel,
        grid=(nb, N_KV_NSA),
        in_specs=[
            pl.BlockSpec((SEQ, 2 * LANES), lambda b, g: (b, g)),
            pl.BlockSpec((None, None, N_CMP, CMP_STRIDE * HEAD_DIM), lambda b, g: (b, g, 0, 0)),
            pl.BlockSpec((None, None, N_CMP, CMP_STRIDE * HEAD_DIM), lambda b, g: (b, N_KV_NSA + g, 0, 0)),
            pl.BlockSpec((SEQ, 4 * LANES), lambda b, g: (b, 0)),
            pl.BlockSpec((SEQ, LANES), lambda b, g: (b, 0)),
            pl.BlockSpec((CMP_STRIDE * HEAD_DIM, 2 * CMP_HIDDEN), const2),
            pl.BlockSpec((CMP_HIDDEN, LANES), const2),
            pl.BlockSpec((32, CMP_STRIDE * HEAD_DIM), const2),
            pl.BlockSpec((CMP_STRIDE * HEAD_DIM, 2 * CMP_HIDDEN), const2),
            pl.BlockSpec((HEAD_DIM, CMP_HIDDEN), const2),
            pl.BlockSpec((32, CMP_STRIDE * HEAD_DIM), const2),
            pl.BlockSpec((N_CMP, LANES), const2),
            pl.BlockSpec((N_CMP, LANES), const2),
            pl.BlockSpec((N_CMP, LANES), const2),
            pl.BlockSpec((N_SELBLK, LANES), const2),
            pl.BlockSpec((N_QBLK // SEL_GROUP, SEL_GROUP * QBLK, LANES), lambda b, g: (0, 0, 0)),
        ],
        out_specs=pl.BlockSpec((SEQ, 2 * LANES), lambda b, g: (b, g)),
        out_shape=jax.ShapeDtypeStruct((t, 512), BF16),
        scratch_shapes=[pltpu.VMEM((N_QBLK, 2 * LANES, QBLK), BF16),
                        pltpu.VMEM((N_QBLK, HEAD_DIM, QBLK), BF16),
                        pltpu.VMEM((N_QBLK, HEAD_DIM, QBLK), BF16),
                        pltpu.VMEM((N_QBLK, 16, QBLK), F32),
                        pltpu.VMEM((N_CMP, LANES), BF16),
                        pltpu.VMEM((HEAD_DIM, N_CMP), BF16)],
        compiler_params=pltpu.CompilerParams(dimension_semantics=("arbitrary", "arbitrary"),
                                             vmem_limit_bytes=VMEM_LIMIT),
        name="nsa",
    )(bq, kvflat, kvflat, ksw, gates, wk1, wk2d, pek, wv1, wv2t, pev, cc, cs1, cs2, ovl_t, expand_t)


def _post_kernel(x_ref, ma_ref, mb_ref, wo_ref, g2_ref, wr_ref, br_ref, tri_ref,
                 x1_ref, h2_ref, dest_ref, cw_ref, cnt_ref):
    x1 = (x_ref[...] + jnp.dot(ma_ref[...], wo_ref[0:512, :], preferred_element_type=F32)
          + jnp.dot(mb_ref[...], wo_ref[512:1024, :], preferred_element_type=F32))
    x1_ref[...] = x1
    ms = jnp.mean(x1 * x1, axis=-1, keepdims=True)
    h2 = x1 * lax.rsqrt(ms + EPS) * g2_ref[...]
    h2_ref[...] = h2.astype(BF16)

    lg = lax.dot_general(wr_ref[...], h2, _NT, precision=lax.Precision.HIGHEST,
                         preferred_element_type=F32) + br_ref[:, 0:1]
    gl = lg[0:N_GROUPS, :]
    sub = _row_iota(gl.shape)
    gmax = jnp.max(gl, axis=0, keepdims=True)
    p_g = 1.0 / jnp.sum(jnp.exp(gl - gmax), axis=0, keepdims=True)
    g_star = jnp.min(jnp.where(gl == gmax, sub, N_GROUPS), axis=0, keepdims=True)
    onehot = sub == g_star
    el = jnp.zeros_like(gl)
    for gi in range(N_GROUPS):
        lo = N_GROUPS + gi * EXPERTS_PER_GROUP
        el = el + jnp.where(g_star == gi, lg[lo:lo + EXPERTS_PER_GROUP, :], 0.0)
    emax = jnp.max(el, axis=0, keepdims=True)
    ee = jnp.exp(el - emax)
    pe = ee / jnp.sum(ee, axis=0, keepdims=True)
    v1 = jnp.max(pe, axis=0, keepdims=True)
    i1 = jnp.min(jnp.where(pe == v1, sub, EXPERTS_PER_GROUP), axis=0, keepdims=True)
    pe2 = jnp.where(sub == i1, -1.0, pe)
    v2 = jnp.max(pe2, axis=0, keepdims=True)
    i2 = jnp.min(jnp.where(pe2 == v2, sub, EXPERTS_PER_GROUP), axis=0, keepdims=True)
    tot = v1 + v2
    cw_ref[...] = jnp.where(sub == i1, v1 / tot * p_g, jnp.where(sub == i2, v2 / tot * p_g, 0.0))

    oh = jnp.where(onehot, 1.0, 0.0)
    before = jnp.dot(oh.astype(BF16), tri_ref[...], preferred_element_type=F32)
    rank = jnp.sum(oh * before, axis=0, keepdims=True)
    dest_ref[0] = g_star * MOE_TM + rank.astype(jnp.int32)
    cnt = jnp.sum(oh, axis=1, keepdims=True).astype(jnp.int32)
    cnt_ref[0] = jnp.broadcast_to(cnt, (N_GROUPS, LANES))


def _post(x2, mix_a, mix_b, w_o, g2, wr_t, br, tri):
    t = x2.shape[0]
    nt = t // MOE_TM
    row = lambda i: (i, 0)
    const = lambda i: (0, 0)
    return pl.pallas_call(
        _post_kernel,
        grid=(nt,),
        in_specs=[pl.BlockSpec((MOE_TM, D_MODEL), row), pl.BlockSpec((MOE_TM, 512), row),
                  pl.BlockSpec((MOE_TM, 512), row), pl.BlockSpec((D_MODEL, D_MODEL), const),
                  pl.BlockSpec((1, D_MODEL), const), pl.BlockSpec((72, D_MODEL), const),
                  pl.BlockSpec((72, LANES), const), pl.BlockSpec((MOE_TM, MOE_TM), const)],
        out_specs=[pl.BlockSpec((MOE_TM, D_MODEL), row), pl.BlockSpec((MOE_TM, D_MODEL), row),
                   pl.BlockSpec((1, 1, MOE_TM), lambda i: (i, 0, 0)),
                   pl.BlockSpec((N_GROUPS, MOE_TM), lambda i: (0, i)),
                   pl.BlockSpec((1, N_GROUPS, LANES), lambda i: (i, 0, 0))],
        out_shape=[jax.ShapeDtypeStruct((t, D_MODEL), F32), jax.ShapeDtypeStruct((t, D_MODEL), BF16),
                   jax.ShapeDtypeStruct((nt, 1, MOE_TM), jnp.int32),
                   jax.ShapeDtypeStruct((N_GROUPS, t), F32),
                   jax.ShapeDtypeStruct((nt, N_GROUPS, LANES), jnp.int32)],
        compiler_params=pltpu.CompilerParams(dimension_semantics=("arbitrary",),
                                             vmem_limit_bytes=VMEM_LIMIT),
        name="post",
    )(x2, mix_a, mix_b, w_o, g2, wr_t, br, tri)


def _gather_matrix(dest_row, target0):
    p = _row_iota((MOE_CHUNK, MOE_TM))
    return jnp.where(dest_row == target0 + p, 1.0, 0.0).astype(BF16)


def _moe_kernel(ig_ref, it_ref, ic_ref, islot_ref, ivalid_ref,
                h2_ref, dest_ref, cw_ref, wg_ref, wu_ref, wd_ref, y_ref):
    k = pl.program_id(0)

    @pl.when(ivalid_ref[k] == 0)
    def _():
        y_ref[...] = jnp.zeros_like(y_ref)

    @pl.when(ivalid_ref[k] != 0)
    def _():
        pm = _gather_matrix(dest_ref[0], ig_ref[k] * MOE_TM + ic_ref[k] * MOE_CHUNK)
        xg = jnp.dot(pm, h2_ref[...], preferred_element_type=F32).astype(BF16)
        cw = jnp.concatenate([cw_ref[...], jnp.zeros((N_GROUPS, MOE_TM), F32)], axis=0)
        cw_hi = cw.astype(BF16)
        cw_lo = (cw - cw_hi.astype(F32)).astype(BF16)
        cws = (lax.dot_general(cw_hi, pm, _NT, preferred_element_type=F32)
               + lax.dot_general(cw_lo, pm, _NT, preferred_element_type=F32)).T
        parts = []
        for e in range(EXPERTS_PER_GROUP):
            gate = jnp.dot(xg, wg_ref[e], preferred_element_type=F32)
            up = jnp.dot(xg, wu_ref[e], preferred_element_type=F32)
            he = gate * jax.nn.sigmoid(gate) * up
            parts.append((he * cws[:, e:e + 1]).astype(BF16))
        hid = jnp.concatenate(parts, axis=1)
        y_ref[...] = jnp.dot(hid, wd_ref[...], preferred_element_type=F32).astype(y_ref.dtype)


def _moe(items, h2, dest, cw_t, wg, wu, wd):
    t = h2.shape[0]
    nt = t // MOE_TM
    dff = EXPERTS_PER_GROUP * D_FF_EXPERT
    grid_spec = pltpu.PrefetchScalarGridSpec(
        num_scalar_prefetch=5,
        grid=(nt * MOE_SLOTS,),
        in_specs=[
            pl.BlockSpec((MOE_TM, D_MODEL), lambda k, ig, it, ic, isl, iv: (it[k], 0)),
            pl.BlockSpec((1, 1, MOE_TM), lambda k, ig, it, ic, isl, iv: (it[k], 0, 0)),
            pl.BlockSpec((N_GROUPS, MOE_TM), lambda k, ig, it, ic, isl, iv: (0, it[k])),
            pl.BlockSpec((EXPERTS_PER_GROUP, D_MODEL, D_FF_EXPERT), lambda k, ig, it, ic, isl, iv: (ig[k], 0, 0)),
            pl.BlockSpec((EXPERTS_PER_GROUP, D_MODEL, D_FF_EXPERT), lambda k, ig, it, ic, isl, iv: (ig[k], 0, 0)),
            pl.BlockSpec((None, dff, D_MODEL), lambda k, ig, it, ic, isl, iv: (ig[k], 0, 0)),
        ],
        out_specs=pl.BlockSpec((MOE_CHUNK, D_MODEL), lambda k, ig, it, ic, isl, iv: (isl[k], 0)),
    )
    return pl.pallas_call(
        _moe_kernel,
        grid_spec=grid_spec,
        out_shape=jax.ShapeDtypeStruct((nt * MOE_SLOTS * MOE_CHUNK, D_MODEL), BF16),
        compiler_params=pltpu.CompilerParams(dimension_semantics=("arbitrary",),
                                             vmem_limit_bytes=VMEM_LIMIT),
        name="moe",
    )(*items, h2, dest, cw_t, wg, wu, wd)


def _combine_kernel(nused_ref, sg_ref, sc_ref, x1_ref, ys_ref, dest_ref, gf_ref, o_ref, acc_ref):
    i = pl.program_id(0)
    acc_ref[...] = x1_ref[...]

    def body(s, carry):
        pm = _gather_matrix(dest_ref[0], sg_ref[i, s] * MOE_TM + sc_ref[i, s] * MOE_CHUNK)
        yc = ys_ref[pl.ds(pl.multiple_of(s * MOE_CHUNK, MOE_CHUNK), MOE_CHUNK), :]
        acc_ref[...] += lax.dot_general(pm, yc, _TN, preferred_element_type=F32)
        return carry

    lax.fori_loop(0, nused_ref[i], body, 0)
    xo = acc_ref[...]
    ms = jnp.mean(xo * xo, axis=-1, keepdims=True)
    o_ref[...] = xo * lax.rsqrt(ms + EPS) * gf_ref[...]


def _combine(nused, slot_g, slot_c, x1, ys, dest, gf):
    t = x1.shape[0]
    nt = t // MOE_TM
    grid_spec = pltpu.PrefetchScalarGridSpec(
        num_scalar_prefetch=3,
        grid=(nt,),
        in_specs=[
            pl.BlockSpec((MOE_TM, D_MODEL), lambda i, a, b, c: (i, 0)),
            pl.BlockSpec((MOE_SLOTS * MOE_CHUNK, D_MODEL), lambda i, a, b, c: (i, 0)),
            pl.BlockSpec((1, 1, MOE_TM), lambda i, a, b, c: (i, 0, 0)),
            pl.BlockSpec((1, D_MODEL), lambda i, a, b, c: (0, 0)),
        ],
        out_specs=pl.BlockSpec((MOE_TM, D_MODEL), lambda i, a, b, c: (i, 0)),
        scratch_shapes=[pltpu.VMEM((MOE_TM, D_MODEL), F32)],
    )
    return pl.pallas_call(
        _combine_kernel,
        grid_spec=grid_spec,
        out_shape=jax.ShapeDtypeStruct((t, D_MODEL), F32),
        compiler_params=pltpu.CompilerParams(dimension_semantics=("arbitrary",),
                                             vmem_limit_bytes=VMEM_LIMIT),
        name="combine",
    )(nused, slot_g, slot_c, x1, ys, dest, gf)


def _moe_plan(cnt):
    nt = cnt.shape[0]
    nch = (cnt + MOE_CHUNK - 1) // MOE_CHUNK
    nused = jnp.sum(nch, axis=1)
    base = jnp.cumsum(nch, axis=1) - nch
    spare = MOE_SLOTS - nused
    n_items = nch.T + jnp.where(jnp.arange(N_GROUPS)[:, None] == N_GROUPS - 1, spare[None, :], 0)
    flat = n_items.reshape(-1)
    ends = jnp.cumsum(flat)
    starts = ends - flat
    k = jnp.arange(nt * MOE_SLOTS)
    gt = jnp.sum((ends[None, :] <= k[:, None]).astype(jnp.int32), axis=1)
    ig = gt // nt
    it = gt % nt
    local = k - starts[gt]
    real = nch[it, ig]
    valid = local < real
    islot = it * MOE_SLOTS + jnp.where(valid, base[it, ig] + local, nused[it] + local - real)
    ic = jnp.where(valid, local, 0)
    items = tuple(a.astype(jnp.int32) for a in (ig, it, ic, islot, valid))
    s = jnp.arange(MOE_SLOTS)
    slot_g = jnp.sum((s[None, :, None] >= (base + nch)[:, None, :]).astype(jnp.int32), axis=2)
    slot_g = jnp.minimum(slot_g, N_GROUPS - 1)
    slot_c = s[None, :] - jnp.take_along_axis(base, slot_g, axis=1)
    return items, nused.astype(jnp.int32), slot_g.astype(jnp.int32), slot_c.astype(jnp.int32)


def _rope_tables(pos):
    half = ROT_DIM // 2
    inv_freq = ROPE_THETA ** (-jnp.arange(0, ROT_DIM, 2, dtype=F32) / ROT_DIM)
    ang = pos.astype(F32)[:, None] * inv_freq[None, :]
    cos, sin = jnp.cos(ang), jnp.sin(ang)
    l64 = np.arange(LANES) % HEAD_DIM
    f = l64 % half
    first = jnp.asarray(l64 < half)[None, :]
    second = jnp.asarray((l64 >= half) & (l64 < ROT_DIM))[None, :]
    c = jnp.where(first | second, cos[:, f], 1.0)
    s1 = jnp.where(first, -sin[:, f], 0.0)
    s2 = jnp.where(second, sin[:, f], 0.0)
    return c.astype(F32), s1.astype(F32), s2.astype(F32)


def _overlap_table_t():
    cs = np.arange(N_CMP)[None, :] * CMP_STRIDE
    js = np.arange(N_SELBLK)[:, None] * SEL_LEN
    ov = np.clip(np.minimum(cs + CMP_LEN, js + SEL_LEN) - np.maximum(cs, js), 0, None) / CMP_LEN
    ov[:, N_CMP - 1] = 0.0
    return ov.astype(np.float32)


def _expand_table_t():
    out = np.zeros((SEQ, LANES), np.float32)
    out[np.arange(SEQ), np.arange(SEQ) // SEL_LEN] = -NEG
    return out.reshape(N_QBLK // SEL_GROUP, SEL_GROUP * QBLK, LANES)


def kernel(x, norm1_g, w_in, pe_kc, w_kc1, w_kc2, pe_vc, w_vc1, w_vc2, w_o, norm2_g, w_rg, b_rg, w_re, b_re,
           w_gate, w_up, w_down, norm_f_g):
    nb, s, d = x.shape
    assert (s, d) == (SEQ, D_MODEL) and norm1_g.shape[0] == 1
    t = nb * s
    x2 = x.reshape(t, d)

    w_in_p = jnp.pad(w_in[0], ((0, 0), (0, D_IN_PAD - w_in.shape[2]))).astype(BF16)
    rc, rs1, rs2 = _rope_tables(jnp.arange(SEQ))
    cc, cs1, cs2 = _rope_tables(jnp.arange(N_CMP) * CMP_STRIDE + CMP_LEN - 1)
    half_flat = CMP_STRIDE * HEAD_DIM

    def cmp_w1(w1):
        return jnp.concatenate([w1[:half_flat], w1[half_flat:]], axis=1).astype(BF16)

    def cmp_pe(pe):
        rows = pe.reshape(2, half_flat)
        return jnp.concatenate([jnp.broadcast_to(rows[0:1], (16, half_flat)),
                                jnp.broadcast_to(rows[1:2], (16, half_flat))], axis=0).astype(BF16)

    wk2d = jnp.concatenate([w_kc2[0], w_kc2[0]], axis=1).astype(BF16)
    wv2t = w_vc2[0].T.astype(BF16)
    ovl_t = jnp.asarray(_overlap_table_t(), BF16)
    expand_t = jnp.asarray(_expand_table_t(), BF16)

    aq, ak, av, bq, kcv, ksw, gates = _proj(x2, norm1_g, w_in_p, rc, rs1, rs2)
    mix_a = _dilated(aq, ak, av)

    kvflat = kcv.reshape(nb, N_CMP, CMP_STRIDE, 2 * N_KV_NSA, HEAD_DIM).transpose(0, 3, 1, 2, 4)
    kvflat = kvflat.reshape(nb, 2 * N_KV_NSA, N_CMP, half_flat)
    mix_b = _nsa(bq, kvflat, ksw, gates, cmp_w1(w_kc1[0]), wk2d, cmp_pe(pe_kc[0]),
                 cmp_w1(w_vc1[0]), wv2t, cmp_pe(pe_vc[0]), cc, cs1, cs2, ovl_t, expand_t)

    wr_t = jnp.concatenate([w_rg[0].T, w_re[0].reshape(d, -1).T], axis=0)
    br = jnp.broadcast_to(jnp.concatenate([b_rg[0], b_re[0].reshape(-1)])[:, None], (72, LANES))
    tri = jnp.asarray(np.triu(np.ones((MOE_TM, MOE_TM), np.float32), 1), BF16)
    x1, h2, dest, cw_t, cnt = _post(x2, mix_a, mix_b, w_o[0].astype(BF16), norm2_g, wr_t, br, tri)

    dff = EXPERTS_PER_GROUP * D_FF_EXPERT

    wd = w_down[0].reshape(N_GROUPS, dff, d).astype(BF16)
    items, nused, slot_g, slot_c = _moe_plan(cnt[:, :, 0])
    ys = _moe(items, h2, dest, cw_t, w_gate[0].astype(BF16), w_up[0].astype(BF16), wd)
    out = _combine(nused, slot_g, slot_c, x1, ys, dest, norm_f_g.reshape(1, d))
    return out.reshape(nb, s, d)
```

```python
import numpy as np
import jax
import jax.numpy as jnp
from jax import lax
from jax.experimental import pallas as pl
from jax.experimental.pallas import tpu as pltpu
from jax.experimental.pallas import tpu_sc as plsc

F32 = jnp.float32
BF16 = jnp.bfloat16

D_MODEL = 1024
SEQ = 2048
HEAD_DIM = 64
N_HEADS_DIL = 8
DIL_PATTERNS = ((128, 1), (512, 4), (2048, 16))
N_HEADS_NSA = 8
N_KV_NSA = 2
NSA_REP = N_HEADS_NSA // N_KV_NSA
CMP_STRIDE = 16
CMP_LEN = 32
CMP_HIDDEN = 256
SEL_LEN = 64
N_SEL = 8
WIN = 512
ROPE_THETA = 500000.0
ROT_DIM = HEAD_DIM // 4
N_GROUPS = 8
EXPERTS_PER_GROUP = 8
D_FF_EXPERT = 256
EPS = 1e-6
NEG = -1e30
BIG = 1e9
LOWEST = -3.0e38
LOG2_E = 1.4426950408889634

LANES = 128
QBLK = 128
N_QBLK = SEQ // QBLK
N_CMP = SEQ // CMP_STRIDE
N_SELBLK = SEQ // SEL_LEN
D_IN_PAD = 23 * LANES
PROJ_TM = 512
SEL_GROUP = 4
DIL_UNROLL = 8
MOE_TM = 1024
FFN_ROWS = 256
FINAL_TM = 512
PACK_W = D_MODEL // 4
SC_WINDOW = 128
VMEM_LIMIT = 56 * 1024 * 1024

_NT = (((1,), (1,)), ((), ()))


def _lane_iota(shape):
    return lax.broadcasted_iota(jnp.int32, shape, 1)


def _row_iota(shape):
    return lax.broadcasted_iota(jnp.int32, shape, 0)


def _rope_lanes(y, c, s1, s2):
    return y * c + pltpu.roll(y, LANES - ROT_DIM // 2, axis=1) * s1 + pltpu.roll(y, ROT_DIM // 2, axis=1) * s2


def _proj_kernel(x_ref, g_ref, w_ref, c_ref, s1_ref, s2_ref,
                 aq_ref, ak_ref, av_ref, bq_ref, kcv_ref, ksw_ref, gate_ref):
    x = x_ref[...]
    ms = jnp.mean(x * x, axis=-1, keepdims=True)
    h = (x * lax.rsqrt(ms + EPS) * g_ref[...]).astype(BF16)
    c = c_ref[...]
    s1 = s1_ref[...]
    s2 = s2_ref[...]

    def seg(lo, width):
        return jnp.dot(h, w_ref[:, lo:lo + width], preferred_element_type=F32)

    def store(dst, col, y, rope, scale):
        for j in range(y.shape[1] // LANES):
            yj = y[:, j * LANES:(j + 1) * LANES]
            if rope:
                yj = _rope_lanes(yj, c, s1, s2)
            if scale != 1.0:
                yj = yj * scale
            dst[:, col + j * LANES:col + (j + 1) * LANES] = yj.astype(dst.dtype)

    qscale = HEAD_DIM ** -0.5 * LOG2_E
    store(aq_ref, 0, seg(0, 512), True, qscale)
    store(ak_ref, 0, seg(512, 512), True, 1.0)
    store(av_ref, 0, seg(1024, 512), False, 1.0)
    store(bq_ref, 0, seg(1536, 512), True, qscale)
    store(kcv_ref, 0, seg(2048, 256), False, 1.0)
    store(ksw_ref, 0, seg(2304, 128), True, 1.0)
    store(ksw_ref, 128, seg(2432, 128), False, 1.0)
    store(ksw_ref, 256, seg(2560, 128), True, 1.0)
    store(ksw_ref, 384, seg(2688, 128), False, 1.0)
    gate_ref[...] = jax.nn.sigmoid(seg(2816, 128))


def _proj(x2, g1, w_in_p, rc, rs1, rs2):
    t = x2.shape[0]
    nblk_s = SEQ // PROJ_TM
    row = lambda i: (i, 0)
    pos = lambda i: (i % nblk_s, 0)
    const = lambda i: (0, 0)
    outs = [jax.ShapeDtypeStruct((t, 512), BF16)] * 4 + [
        jax.ShapeDtypeStruct((t, 256), BF16), jax.ShapeDtypeStruct((t, 512), BF16),
        jax.ShapeDtypeStruct((t, LANES), F32)]
    out_specs = [pl.BlockSpec((PROJ_TM, 512), row)] * 4 + [
        pl.BlockSpec((PROJ_TM, 256), row), pl.BlockSpec((PROJ_TM, 512), row),
        pl.BlockSpec((PROJ_TM, LANES), row)]
    return pl.pallas_call(
        _proj_kernel,
        grid=(t // PROJ_TM,),
        in_specs=[pl.BlockSpec((PROJ_TM, D_MODEL), row), pl.BlockSpec((1, D_MODEL), const),
                  pl.BlockSpec((D_MODEL, D_IN_PAD), const),
                  pl.BlockSpec((PROJ_TM, LANES), pos), pl.BlockSpec((PROJ_TM, LANES), pos),
                  pl.BlockSpec((PROJ_TM, LANES), pos)],
        out_specs=out_specs,
        out_shape=outs,
        compiler_params=pltpu.CompilerParams(dimension_semantics=("arbitrary",),
                                             vmem_limit_bytes=VMEM_LIMIT),
        name="proj",
    )(x2, g1, w_in_p, rc, rs1, rs2)


def _dilated_kernel(q_ref, k_ref, v_ref, o_ref,
                    qf, kf, vf, qp, kp, vp, qt, vt, op, lp, on0, on1, on2, ln0, ln1, ln2):
    qf[...] = q_ref[...].astype(F32)
    kf[...] = k_ref[...].astype(F32)
    vf[...] = v_ref[...].astype(F32)
    kp[0:QBLK, :] = jnp.zeros((QBLK, LANES), BF16)
    vt[0] = jnp.zeros((LANES, QBLK), BF16)

    c2 = _row_iota((2 * QBLK, 2 * QBLK))
    a2 = _lane_iota((2 * QBLK, 2 * QBLK)) % QBLK
    band = jnp.where((c2 >= a2) & (c2 <= a2 + QBLK), 0.0, NEG)
    band_noprev = jnp.where(c2 < QBLK, NEG, band)
    c1 = _row_iota((QBLK, 2 * QBLK))
    a1 = _lane_iota((QBLK, 2 * QBLK)) % QBLK
    band_own = jnp.where(c1 <= a1, 0.0, NEG)
    top_sq = _row_iota((LANES, QBLK)) < HEAD_DIM

    for pi, (window, dil) in enumerate(DIL_PATTERNS):
        seg_len = SEQ // dil
        nseg_blk = seg_len // QBLK
        for r in range(dil):
            src = pl.ds(r, seg_len, stride=dil) if dil > 1 else pl.ds(0, SEQ)
            qp[pl.ds(r * seg_len, seg_len), :] = qf[src, :].astype(BF16)
            kp[pl.ds(QBLK + r * seg_len, seg_len), :] = kf[src, :].astype(BF16)
            vp[pl.ds(r * seg_len, seg_len), :] = vf[src, :].astype(BF16)

        for j in range(N_QBLK):
            rows = pl.ds(j * QBLK, QBLK)
            qt[j] = qp[rows, :].T
            vt[j + 1] = vp[rows, :].T

        def body(u, carry, nseg_blk=nseg_blk):
            js = [u * DIL_UNROLL + d for d in range(DIL_UNROLL)]
            scores = []
            for d, j in enumerate(js):
                row0 = pl.multiple_of(j * QBLK, QBLK)
                qtb = qt[j]
                zero = jnp.zeros_like(qtb)
                q_both = jnp.concatenate([jnp.where(top_sq, qtb, zero), jnp.where(top_sq, zero, qtb)], axis=1)
                if nseg_blk == 1:
                    keys, bias = kp[pl.ds(row0 + QBLK, QBLK), :], band_own
                else:
                    keys = kp[pl.ds(row0, 2 * QBLK), :]
                    if nseg_blk <= DIL_UNROLL:
                        bias = band_noprev if d % nseg_blk == 0 else band
                    else:
                        bias = jnp.where(u == 0, band_noprev, band) if d == 0 else band
                scores.append(jnp.dot(keys, q_both, preferred_element_type=F32) + bias)
            stats = []
            for s in scores:
                m = jnp.max(s, axis=0, keepdims=True)
                p = jnp.exp2(s - m)
                stats.append((m, jnp.sum(p, axis=0, keepdims=True), p.astype(BF16)))
            outs = []
            for j, (m, l, p) in zip(js, stats):
                v_span = vt[j + 1] if nseg_blk == 1 else jnp.concatenate([vt[j], vt[j + 1]], axis=1)
                outs.append(jnp.dot(v_span, p, preferred_element_type=F32))
            for j, (m, l, p), res in zip(js, stats, outs):
                row0 = pl.multiple_of(j * QBLK, QBLK)
                lse = m + jnp.log2(l)
                o_t = jnp.where(top_sq, res[:, 0:QBLK] / l[:, 0:QBLK], res[:, QBLK:] / l[:, QBLK:])
                lse_t = jnp.where(top_sq, jnp.broadcast_to(lse[:, 0:QBLK], (LANES, QBLK)),
                                  jnp.broadcast_to(lse[:, QBLK:], (LANES, QBLK)))
                op[pl.ds(row0, QBLK), :] = o_t.T
                lp[pl.ds(row0, QBLK), :] = lse_t.T
            return carry

        lax.fori_loop(0, N_QBLK // DIL_UNROLL, body, 0)

        for r in range(dil):
            dst = pl.ds(r, seg_len, stride=dil) if dil > 1 else pl.ds(0, SEQ)
            src = pl.ds(r * seg_len, seg_len)
            (on0, on1, on2)[pi][dst, :] = op[src, :]
            (ln0, ln1, ln2)[pi][dst, :] = lp[src, :]

    l0, l1, l2 = ln0[...], ln1[...], ln2[...]
    mx = jnp.maximum(jnp.maximum(l0, l1), l2)
    e0, e1, e2 = jnp.exp2(l0 - mx), jnp.exp2(l1 - mx), jnp.exp2(l2 - mx)
    den = e0 + e1 + e2
    out = (e0 / den) * on0[...] + (e1 / den) * on1[...] + (e2 / den) * on2[...]
    o_ref[...] = out.astype(o_ref.dtype)


def _dilated(aq, ak, av):
    t = aq.shape[0]
    nb = t // SEQ
    spec = pl.BlockSpec((SEQ, LANES), lambda b, hp: (b, hp))
    return pl.pallas_call(
        _dilated_kernel,
        grid=(nb, N_HEADS_DIL // 2),
        in_specs=[spec, spec, spec],
        out_specs=spec,
        out_shape=jax.ShapeDtypeStruct((t, 512), BF16),
        scratch_shapes=[pltpu.VMEM((SEQ, LANES), F32)] * 3
        + [pltpu.VMEM((SEQ, LANES), BF16), pltpu.VMEM((SEQ + QBLK, LANES), BF16),
           pltpu.VMEM((SEQ, LANES), BF16),
           pltpu.VMEM((N_QBLK, LANES, QBLK), BF16), pltpu.VMEM((N_QBLK + 1, LANES, QBLK), BF16)]
        + [pltpu.VMEM((SEQ, LANES), F32)] * 8,
        compiler_params=pltpu.CompilerParams(dimension_semantics=("arbitrary", "arbitrary"),
                                             vmem_limit_bytes=VMEM_LIMIT),
        name="dilated",
    )(aq, ak, av)


def _gelu_tanh(x):
    return 0.5 * x * (1.0 + jnp.tanh(np.sqrt(2.0 / np.pi).astype(np.float32) * (x + 0.044715 * (x * x * x))))


def _softmax_step(state, s, v_t):
    m, l, acc = state
    mn = jnp.maximum(m, jnp.max(s, axis=0, keepdims=True))
    al = jnp.exp2(m - mn)
    p = jnp.exp2(s - mn)
    l = al * l + jnp.sum(p, axis=0, keepdims=True)
    acc = al * acc + jnp.dot(v_t, p.astype(BF16), preferred_element_type=F32)
    return mn, l, acc


def _nsa_kernel(q_ref, kflat_ref, vflat_ref, ksw_ref, gate_ref,
                wk1_ref, wk2d_ref, pek_ref, wv1_ref, wv2t_ref, pev_ref,
                cc_ref, cs1_ref, cs2_ref, ovl_ref, exp_ref,
                o_ref,
                qt, vst, vwt, gtt, kc2, vct):
    g = pl.program_id(1)
    nq = NSA_REP * QBLK

    for i in range(N_QBLK):
        rows = pl.ds(i * QBLK, QBLK)
        qt[i, 0:LANES, :] = q_ref[rows, 0:LANES].T
        qt[i, LANES:2 * LANES, :] = q_ref[rows, LANES:2 * LANES].T
        vs_t = ksw_ref[rows, LANES:2 * LANES].T
        vst[i] = jnp.where(g == 0, vs_t[0:HEAD_DIM], vs_t[HEAD_DIM:])
        vw_t = ksw_ref[rows, 3 * LANES:4 * LANES].T
        vwt[i] = jnp.where(g == 0, vw_t[0:HEAD_DIM], vw_t[HEAD_DIM:])
        gt = gate_ref[rows, :]
        gt = jnp.where(g == 0, gt, pltpu.roll(gt, LANES - 3 * NSA_REP, axis=1))
        gtt[i] = gt.T[0:16, :]

    def hidden(flat_ref, w1_ref, pe_ref):
        ab = jnp.dot(flat_ref[...], w1_ref[...], preferred_element_type=F32)
        pb = (jnp.dot(pe_ref[0:16, :], w1_ref[:, 0:CMP_HIDDEN], preferred_element_type=F32)
              + jnp.dot(pe_ref[16:32, :], w1_ref[:, CMP_HIDDEN:], preferred_element_type=F32))
        hid = ab[:, 0:CMP_HIDDEN] + pltpu.roll(ab[:, CMP_HIDDEN:], N_CMP - 1, axis=0) + pb[0:1, :]
        return _gelu_tanh(hid).astype(BF16)

    hk = hidden(kflat_ref, wk1_ref, pek_ref)
    kc2[...] = _rope_lanes(jnp.dot(hk, wk2d_ref[...], preferred_element_type=F32),
                           cc_ref[...], cs1_ref[...], cs2_ref[...]).astype(BF16)
    hv = hidden(vflat_ref, wv1_ref, pev_ref)
    vct[...] = lax.dot_general(wv2t_ref[...], hv, _NT, preferred_element_type=F32).astype(BF16)

    key = _row_iota((QBLK, nq))
    qry = _lane_iota((QBLK, nq)) % QBLK
    causal = jnp.where(key <= qry, 0.0, NEG)
    upper = jnp.where(key > qry, 0.0, NEG)
    blk = _row_iota((N_SELBLK, QBLK))
    qry1 = _lane_iota((N_SELBLK, QBLK))
    grp_keys = SEL_GROUP * QBLK
    win_blocks = WIN // QBLK

    def qgroup(w, full_window):
        blocks = [SEL_GROUP * w + d for d in range(SEL_GROUP)]
        row0s = [pl.multiple_of(i * QBLK, QBLK) for i in blocks]

        q_pads = []
        for i in blocks:
            q_t = jnp.concatenate([qt[i, r * HEAD_DIM:(r + 1) * HEAD_DIM, :] for r in range(NSA_REP)], axis=1)
            zero = jnp.zeros_like(q_t)
            q_pads.append(jnp.where(g == 0, jnp.concatenate([q_t, zero], axis=0),
                                    jnp.concatenate([zero, q_t], axis=0)))
        s_cmp = [jnp.dot(kc2[...], qp, preferred_element_type=F32) for qp in q_pads]
        s_win, v_win = [], []
        for d, (i, qp) in enumerate(zip(blocks, q_pads)):
            if full_window:
                first = i - win_blocks
                keys = ksw_ref[pl.ds(pl.multiple_of(first * QBLK, QBLK), (win_blocks + 1) * QBLK), 2 * LANES:3 * LANES]
                s = jnp.dot(keys, qp, preferred_element_type=F32)
                s = jnp.concatenate([s[0:QBLK] + upper, s[QBLK:win_blocks * QBLK], s[win_blocks * QBLK:] + causal],
                                    axis=0)
                v_t = jnp.concatenate([vwt[first + e] for e in range(win_blocks + 1)], axis=1)
            else:
                s = jnp.dot(ksw_ref[0:(d + 1) * QBLK, 2 * LANES:3 * LANES], qp, preferred_element_type=F32)
                s = jnp.concatenate([s[0:d * QBLK], s[d * QBLK:] + causal], axis=0) if d else s + causal
                v_t = jnp.concatenate([vwt[e] for e in range(d + 1)], axis=1) if d else vwt[0]
            s_win.append(s)
            v_win.append(v_t)

        pcs = []
        for row0, s in zip(row0s, s_cmp):
            cvalid = (CMP_STRIDE * key + (CMP_LEN - 1) <= row0 + qry) & (key < N_CMP - 1)
            s = jnp.where(cvalid, s, NEG)
            m = jnp.max(s, axis=0, keepdims=True)
            p = jnp.where(cvalid, jnp.exp2(s - m), 0.0)
            l = jnp.sum(p, axis=0, keepdims=True)
            pcs.append(p / jnp.where(l > 0, l, 1.0))
        o_cmp = [jnp.dot(vct[...], pc.astype(BF16), preferred_element_type=F32) for pc in pcs]

        ovl = ovl_ref[...]
        imps = []
        for pc in pcs:
            psum = pc[:, 0:QBLK] + pc[:, QBLK:2 * QBLK] + pc[:, 2 * QBLK:3 * QBLK] + pc[:, 3 * QBLK:]
            p_hi = psum.astype(BF16)
            p_lo = (psum - p_hi.astype(F32)).astype(BF16)
            imps.append(jnp.dot(ovl, p_hi, preferred_element_type=F32) + jnp.dot(ovl, p_lo, preferred_element_type=F32))

        o_win = []
        for s, v_t in zip(s_win, v_win):
            m = jnp.max(s, axis=0, keepdims=True)
            p = jnp.exp2(s - m)
            l = jnp.sum(p, axis=0, keepdims=True)
            o_win.append(jnp.dot(v_t, p.astype(BF16), preferred_element_type=F32) / l)

        q_sel = []
        for row0, imp, qp in zip(row0s, imps, q_pads):
            t1 = row0 + qry1
            blk_t = t1 // SEL_LEN
            forced = (blk == 0) | (blk == blk_t) | (blk == blk_t - 1)
            v = jnp.where(blk * SEL_LEN <= t1, jnp.where(forced, BIG, imp), -BIG)
            selm = jnp.zeros((N_SELBLK, QBLK), F32)
            for _ in range(N_SEL):
                mx = jnp.max(v, axis=0, keepdims=True)
                first_hit = jnp.min(jnp.where(v == mx, blk, N_SELBLK), axis=0, keepdims=True)
                hit = blk == first_hit
                selm = jnp.where(hit, 1.0, selm)
                v = jnp.where(hit, LOWEST, v)
            notsel = jnp.concatenate([selm - 1.0, jnp.zeros((LANES - N_SELBLK, QBLK), F32)], axis=0).astype(BF16)
            q_sel.append(jnp.concatenate([qp, jnp.concatenate([notsel] * NSA_REP, axis=1)], axis=0))

        def sel_keys(t, n_rows):
            rows_t = pl.ds(pl.multiple_of(t * grp_keys, grp_keys), n_rows)
            return jnp.concatenate([ksw_ref[rows_t, 0:LANES], exp_ref[t, 0:n_rows, :]], axis=1)

        def sel_values(t, n_blk):
            return jnp.concatenate([vst[t * SEL_GROUP + e] for e in range(n_blk)], axis=1) if n_blk > 1 else vst[t * SEL_GROUP]

        wide = SEL_GROUP * nq
        st = (jnp.full((1, wide), LOWEST, F32), jnp.zeros((1, wide), F32), jnp.zeros((HEAD_DIM, wide), F32))
        if full_window:
            q_all = jnp.concatenate(q_sel, axis=1)

            def sel_body(t, st):
                s = jnp.dot(sel_keys(t, grp_keys), q_all, preferred_element_type=F32)
                return _softmax_step(st, s, sel_values(t, SEL_GROUP))

            st = lax.fori_loop(0, w, sel_body, st)
        o_sel = []
        for d, qs in enumerate(q_sel):
            s = jnp.dot(sel_keys(w, (d + 1) * QBLK), qs, preferred_element_type=F32)
            s = jnp.concatenate([s[0:d * QBLK], s[d * QBLK:] + causal], axis=0) if d else s + causal
            lanes = slice(d * nq, (d + 1) * nq)
            m, l, acc = _softmax_step((st[0][:, lanes], st[1][:, lanes], st[2][:, lanes]), s, sel_values(w, d + 1))
            o_sel.append(acc / l)

        for n, (i, row0) in enumerate(zip(blocks, row0s)):
            gti = gtt[i]

            def gate(br):
                return jnp.concatenate([gti[r * 3 + br:r * 3 + br + 1, :] for r in range(NSA_REP)], axis=1)

            out_t = gate(0) * o_cmp[n] + gate(1) * o_sel[n] + gate(2) * o_win[n]
            for pair in range(NSA_REP // 2):
                both = jnp.concatenate([out_t[:, (2 * pair) * QBLK:(2 * pair + 1) * QBLK],
                                        out_t[:, (2 * pair + 1) * QBLK:(2 * pair + 2) * QBLK]], axis=0)
                o_ref[pl.ds(row0, QBLK), pair * LANES:(pair + 1) * LANES] = both.T.astype(o_ref.dtype)

    def tail_groups(w, carry):
        qgroup(w, True)
        return carry

    qgroup(0, False)
    lax.fori_loop(1, N_QBLK // SEL_GROUP, tail_groups, 0)


def _nsa(bq, kvflat, ksw, gates, wk1, wk2d, pek, wv1, wv2t, pev, cc, cs1, cs2, ovl_t, expand_t):
    t = bq.shape[0]
    nb = t // SEQ
    const2 = lambda b, g: (0, 0)
    return pl.pallas_call(
        _nsa_kernel,
        grid=(nb, N_KV_NSA),
        in_specs=[
            pl.BlockSpec((SEQ, 2 * LANES), lambda b, g: (b, g)),
            pl.BlockSpec((None, None, N_CMP, CMP_STRIDE * HEAD_DIM), lambda b, g: (b, g, 0, 0)),
            pl.BlockSpec((None, None, N_CMP, CMP_STRIDE * HEAD_DIM), lambda b, g: (b, N_KV_NSA + g, 0, 0)),
            pl.BlockSpec((SEQ, 4 * LANES), lambda b, g: (b, 0)),
            pl.BlockSpec((SEQ, LANES), lambda b, g: (b, 0)),
            pl.BlockSpec((CMP_STRIDE * HEAD_DIM, 2 * CMP_HIDDEN), const2),
            pl.BlockSpec((CMP_HIDDEN, LANES), const2),
            pl.BlockSpec((32, CMP_STRIDE * HEAD_DIM), const2),
            pl.BlockSpec((CMP_STRIDE * HEAD_DIM, 2 * CMP_HIDDEN), const2),
            pl.BlockSpec((HEAD_DIM, CMP_HIDDEN), const2),
            pl.BlockSpec((32, CMP_STRIDE * HEAD_DIM), const2),
            pl.BlockSpec((N_CMP, LANES), const2),
            pl.BlockSpec((N_CMP, LANES), const2),
            pl.BlockSpec((N_CMP, LANES), const2),
            pl.BlockSpec((N_SELBLK, LANES), const2),
            pl.BlockSpec((N_QBLK // SEL_GROUP, SEL_GROUP * QBLK, LANES), lambda b, g: (0, 0, 0)),
        ],
        out_specs=pl.BlockSpec((SEQ, 2 * LANES), lambda b, g: (b, g)),
        out_shape=jax.ShapeDtypeStruct((t, 512), BF16),
        scratch_shapes=[pltpu.VMEM((N_QBLK, 2 * LANES, QBLK), BF16),
                        pltpu.VMEM((N_QBLK, HEAD_DIM, QBLK), BF16),
                        pltpu.VMEM((N_QBLK, HEAD_DIM, QBLK), BF16),
                        pltpu.VMEM((N_QBLK, 16, QBLK), F32),
                        pltpu.VMEM((N_CMP, LANES), BF16),
                        pltpu.VMEM((HEAD_DIM, N_CMP), BF16)],
        compiler_params=pltpu.CompilerParams(dimension_semantics=("arbitrary", "arbitrary"),
                                             vmem_limit_bytes=VMEM_LIMIT),
        name="nsa",
    )(bq, kvflat, kvflat, ksw, gates, wk1, wk2d, pek, wv1, wv2t, pev, cc, cs1, cs2, ovl_t, expand_t)


def _pack_bf16_pairs(lo, hi):
    lo_b = lax.bitcast_convert_type(lo.astype(BF16).astype(F32), jnp.uint32)
    hi_b = lax.bitcast_convert_type(hi.astype(BF16).astype(F32), jnp.uint32)
    return (hi_b & jnp.uint32(0xFFFF0000)) | (lo_b >> 16)


def _unpack_bf16_pairs(p):
    lo = lax.bitcast_convert_type(p << 16, F32)
    hi = lax.bitcast_convert_type(p & jnp.uint32(0xFFFF0000), F32)
    return lo, hi


def _post_kernel(x_ref, ma_ref, mb_ref, wo_ref, g2_ref, wr_ref, br_ref, tri_ref,
                 x1_ref, ha_ref, hb_ref, e1_ref, e2_ref, r1_ref, r2_ref, w_ref, cnt_ref):
    x1 = (x_ref[...] + jnp.dot(ma_ref[...], wo_ref[0:512, :], preferred_element_type=F32)
          + jnp.dot(mb_ref[...], wo_ref[512:1024, :], preferred_element_type=F32))
    x1_ref[...] = x1
    ms = jnp.mean(x1 * x1, axis=-1, keepdims=True)
    h2 = x1 * lax.rsqrt(ms + EPS) * g2_ref[...]
    ha_ref[...] = _pack_bf16_pairs(h2[:, 0:PACK_W], h2[:, 2 * PACK_W:3 * PACK_W])
    hb_ref[...] = _pack_bf16_pairs(h2[:, PACK_W:2 * PACK_W], h2[:, 3 * PACK_W:])

    lg = lax.dot_general(wr_ref[...], h2, _NT, precision=lax.Precision.HIGHEST,
                         preferred_element_type=F32) + br_ref[:, 0:1]
    gl = lg[0:N_GROUPS, :]
    sub = _row_iota(gl.shape)
    gmax = jnp.max(gl, axis=0, keepdims=True)
    p_g = 1.0 / jnp.sum(jnp.exp(gl - gmax), axis=0, keepdims=True)
    g_star = jnp.min(jnp.where(gl == gmax, sub, N_GROUPS), axis=0, keepdims=True)
    el = jnp.zeros_like(gl)
    for gi in range(N_GROUPS):
        lo = N_GROUPS + gi * EXPERTS_PER_GROUP
        el = el + jnp.where(g_star == gi, lg[lo:lo + EXPERTS_PER_GROUP, :], 0.0)
    emax = jnp.max(el, axis=0, keepdims=True)
    ee = jnp.exp(el - emax)
    pe = ee / jnp.sum(ee, axis=0, keepdims=True)
    v1 = jnp.max(pe, axis=0, keepdims=True)
    i1 = jnp.min(jnp.where(pe == v1, sub, EXPERTS_PER_GROUP), axis=0, keepdims=True)
    pe2 = jnp.where(sub == i1, -1.0, pe)
    v2 = jnp.max(pe2, axis=0, keepdims=True)
    i2 = jnp.min(jnp.where(pe2 == v2, sub, EXPERTS_PER_GROUP), axis=0, keepdims=True)
    tot = v1 + v2
    e1 = g_star * EXPERTS_PER_GROUP + i1
    e2 = g_star * EXPERTS_PER_GROUP + i2
    e1_ref[0] = e1
    e2_ref[0] = e2

    wslab = jnp.concatenate([v1 / tot * p_g, v2 / tot * p_g, jnp.zeros((LANES - 2, MOE_TM), F32)], axis=0)
    for j in range(MOE_TM // LANES):
        w_ref[j * LANES:(j + 1) * LANES, :] = wslab[:, j * LANES:(j + 1) * LANES].T

    n_exp = N_GROUPS * EXPERTS_PER_GROUP
    sub_e = _row_iota((n_exp, MOE_TM))
    hit1 = sub_e == e1
    hit2 = sub_e == e2
    assigned = jnp.where(hit1, 1.0, 0.0) + jnp.where(hit2, 1.0, 0.0)
    before = jnp.dot(assigned.astype(BF16), tri_ref[...], preferred_element_type=F32)
    r1_ref[0] = jnp.sum(jnp.where(hit1, before, 0.0), axis=0, keepdims=True).astype(jnp.int32)
    r2_ref[0] = jnp.sum(jnp.where(hit2, before, 0.0), axis=0, keepdims=True).astype(jnp.int32)
    cnt = jnp.sum(assigned, axis=1, keepdims=True).astype(jnp.int32)
    cnt_ref[0] = jnp.broadcast_to(cnt, (n_exp, LANES))


def _post(x2, mix_a, mix_b, w_o, g2, wr_t, br, tri):
    t = x2.shape[0]
    nt = t // MOE_TM
    n_exp = N_GROUPS * EXPERTS_PER_GROUP
    row = lambda i: (i, 0)
    const = lambda i: (0, 0)
    tok = lambda i: (i, 0, 0)
    tok_spec = pl.BlockSpec((1, 1, MOE_TM), tok)
    tok_shape = jax.ShapeDtypeStruct((nt, 1, MOE_TM), jnp.int32)
    return pl.pallas_call(
        _post_kernel,
        grid=(nt,),
        in_specs=[pl.BlockSpec((MOE_TM, D_MODEL), row), pl.BlockSpec((MOE_TM, 512), row),
                  pl.BlockSpec((MOE_TM, 512), row), pl.BlockSpec((D_MODEL, D_MODEL), const),
                  pl.BlockSpec((1, D_MODEL), const), pl.BlockSpec((72, D_MODEL), const),
                  pl.BlockSpec((72, LANES), const), pl.BlockSpec((MOE_TM, MOE_TM), const)],
        out_specs=[pl.BlockSpec((MOE_TM, D_MODEL), row),
                   pl.BlockSpec((MOE_TM, PACK_W), row), pl.BlockSpec((MOE_TM, PACK_W), row),
                   tok_spec, tok_spec, tok_spec, tok_spec,
                   pl.BlockSpec((MOE_TM, LANES), row),
                   pl.BlockSpec((1, n_exp, LANES), tok)],
        out_shape=[jax.ShapeDtypeStruct((t, D_MODEL), F32),
                   jax.ShapeDtypeStruct((t, PACK_W), jnp.uint32), jax.ShapeDtypeStruct((t, PACK_W), jnp.uint32),
                   tok_shape, tok_shape, tok_shape, tok_shape,
                   jax.ShapeDtypeStruct((t, LANES), F32),
                   jax.ShapeDtypeStruct((nt, n_exp, LANES), jnp.int32)],
        compiler_params=pltpu.CompilerParams(dimension_semantics=("arbitrary",),
                                             vmem_limit_bytes=VMEM_LIMIT),
        name="post",
    )(x2, mix_a, mix_b, w_o, g2, wr_t, br, tri)


def _sc_mesh():
    return plsc.VectorSubcoreMesh(core_axis_name="core", subcore_axis_name="subcore")


def _sc_scatter_rows(x, pos1, pos2, n_out):
    r, c = x.shape

    @pl.kernel(out_type=jax.ShapeDtypeStruct((n_out, c), x.dtype), mesh=_sc_mesh(), scratch_types=[])
    def scatter_kernel(x_hbm, p1_hbm, p2_hbm, o_hbm):
        def body(x_vmem, p1_vmem, p2_vmem):
            pltpu.sync_copy(x_vmem, o_hbm.at[p1_vmem.at[0]])
            pltpu.sync_copy(x_vmem, o_hbm.at[p2_vmem.at[0]])

        idx = pl.BlockSpec((1, SC_WINDOW), lambda i: (0, i))
        pltpu.emit_pipeline(body, grid=(r // SC_WINDOW,),
                            in_specs=[pl.BlockSpec((SC_WINDOW, c), lambda i: (i, 0)), idx, idx], out_specs=[],
                            core_axis_name=("core", "subcore"),
                            dimension_semantics=(pltpu.PARALLEL,))(x_hbm, p1_hbm, p2_hbm)

    return scatter_kernel(x, pos1.reshape(1, r), pos2.reshape(1, r))


def _sc_gather_rows(y, idx):
    n = idx.shape[0]
    c = y.shape[1]

    @pl.kernel(out_type=jax.ShapeDtypeStruct((n, c), y.dtype), mesh=_sc_mesh(), scratch_types=[])
    def gather_kernel(y_hbm, i_hbm, o_hbm):
        def body(i_vmem, o_vmem):
            pltpu.sync_copy(y_hbm.at[i_vmem.at[0]], o_vmem)

        pltpu.emit_pipeline(body, grid=(n // SC_WINDOW,),
                            in_specs=[pl.BlockSpec((1, SC_WINDOW), lambda i: (0, i))],
                            out_specs=[pl.BlockSpec((SC_WINDOW, c), lambda i: (i, 0))],
                            core_axis_name=("core", "subcore"),
                            dimension_semantics=(pltpu.PARALLEL,))(i_hbm, o_hbm)

    return gather_kernel(y, idx.reshape(1, n))


def _ffn_kernel(ce_ref, cv_ref, xa_ref, xb_ref, wg_ref, wu_ref, wd_ref, ya_ref, yb_ref):
    c = pl.program_id(0)

    @pl.when(cv_ref[c] == 0)
    def _():
        ya_ref[...] = jnp.zeros_like(ya_ref)
        yb_ref[...] = jnp.zeros_like(yb_ref)

    @pl.when(cv_ref[c] != 0)
    def _():
        a_lo, a_hi = _unpack_bf16_pairs(xa_ref[...])
        b_lo, b_hi = _unpack_bf16_pairs(xb_ref[...])
        xs = jnp.concatenate([a_lo, b_lo, a_hi, b_hi], axis=1).astype(BF16)
        gate = jnp.dot(xs, wg_ref[0], preferred_element_type=F32)
        up = jnp.dot(xs, wu_ref[0], preferred_element_type=F32)
        he = (gate * jax.nn.sigmoid(gate) * up).astype(BF16)
        y = jnp.dot(he, wd_ref[0], preferred_element_type=F32)
        ya_ref[...] = _pack_bf16_pairs(y[:, 0:PACK_W], y[:, 2 * PACK_W:3 * PACK_W])
        yb_ref[...] = _pack_bf16_pairs(y[:, PACK_W:2 * PACK_W], y[:, 3 * PACK_W:])


def _ffn(chunk_expert, chunk_valid, xsa, xsb, wg, wu, wd):
    n_pad = xsa.shape[0]
    rows = pl.BlockSpec((FFN_ROWS, PACK_W), lambda c, ce, cv: (c, 0))
    grid_spec = pltpu.PrefetchScalarGridSpec(
        num_scalar_prefetch=2,
        grid=(n_pad // FFN_ROWS,),
        in_specs=[rows, rows,
                  pl.BlockSpec((1, D_MODEL, D_FF_EXPERT), lambda c, ce, cv: (ce[c], 0, 0)),
                  pl.BlockSpec((1, D_MODEL, D_FF_EXPERT), lambda c, ce, cv: (ce[c], 0, 0)),
                  pl.BlockSpec((1, D_FF_EXPERT, D_MODEL), lambda c, ce, cv: (ce[c], 0, 0))],
        out_specs=[rows, rows],
    )
    out = jax.ShapeDtypeStruct((n_pad, PACK_W), jnp.uint32)
    return pl.pallas_call(
        _ffn_kernel,
        grid_spec=grid_spec,
        out_shape=[out, out],
        compiler_params=pltpu.CompilerParams(dimension_semantics=("arbitrary",),
                                             vmem_limit_bytes=VMEM_LIMIT),
        name="ffn",
    )(chunk_expert, chunk_valid, xsa, xsb, wg, wu, wd)


def _final_kernel(x1_ref, y1a_ref, y1b_ref, y2a_ref, y2b_ref, w_ref, gf_ref, o_ref):
    w1 = w_ref[:, 0:1]
    w2 = w_ref[:, 1:2]
    a1_lo, a1_hi = _unpack_bf16_pairs(y1a_ref[...])
    b1_lo, b1_hi = _unpack_bf16_pairs(y1b_ref[...])
    a2_lo, a2_hi = _unpack_bf16_pairs(y2a_ref[...])
    b2_lo, b2_hi = _unpack_bf16_pairs(y2b_ref[...])
    y = jnp.concatenate([w1 * a1_lo + w2 * a2_lo, w1 * b1_lo + w2 * b2_lo,
                         w1 * a1_hi + w2 * a2_hi, w1 * b1_hi + w2 * b2_hi], axis=1)
    xo = x1_ref[...] + y
    ms = jnp.mean(xo * xo, axis=-1, keepdims=True)
    o_ref[...] = xo * lax.rsqrt(ms + EPS) * gf_ref[...]


def _final(x1, yga, ygb, wtok, gf):
    t = x1.shape[0]
    row = lambda i: (i, 0)
    first = pl.BlockSpec((None, FINAL_TM, PACK_W), lambda i: (0, i, 0))
    second = pl.BlockSpec((None, FINAL_TM, PACK_W), lambda i: (1, i, 0))
    return pl.pallas_call(
        _final_kernel,
        grid=(t // FINAL_TM,),
        in_specs=[pl.BlockSpec((FINAL_TM, D_MODEL), row), first, first, second, second,
                  pl.BlockSpec((FINAL_TM, LANES), row), pl.BlockSpec((1, D_MODEL), lambda i: (0, 0))],
        out_specs=pl.BlockSpec((FINAL_TM, D_MODEL), row),
        out_shape=jax.ShapeDtypeStruct((t, D_MODEL), F32),
        compiler_params=pltpu.CompilerParams(dimension_semantics=("arbitrary",),
                                             vmem_limit_bytes=VMEM_LIMIT),
        name="final",
    )(x1, yga, ygb, yga, ygb, wtok, gf)


def _moe_plan(cnt, e1, e2, r1, r2):
    nt, n_exp = cnt.shape
    tot = jnp.sum(cnt, axis=0)
    padded = (tot + FFN_ROWS - 1) // FFN_ROWS * FFN_ROWS
    seg_end = jnp.cumsum(padded)
    seg_start = seg_end - padded
    off = seg_start[None, :] + jnp.cumsum(cnt, axis=0) - cnt
    experts = jnp.arange(n_exp, dtype=jnp.int32)

    def lookup(e):
        return jnp.sum(jnp.where(e[:, :, None] == experts, off[:, None, :], 0), axis=2)

    pos1 = (lookup(e1) + r1).reshape(-1).astype(jnp.int32)
    pos2 = (lookup(e2) + r2).reshape(-1).astype(jnp.int32)
    n_chunks = (nt * MOE_TM * 2 + n_exp * FFN_ROWS) // FFN_ROWS
    cstart = jnp.arange(n_chunks, dtype=jnp.int32) * FFN_ROWS
    ce = jnp.sum((seg_end[None, :] <= cstart[:, None]).astype(jnp.int32), axis=1)
    cec = jnp.minimum(ce, n_exp - 1)
    valid = (ce < n_exp) & (cstart < seg_start[cec] + tot[cec])
    return pos1, pos2, cec.astype(jnp.int32), valid.astype(jnp.int32)


def _rope_tables(pos):
    half = ROT_DIM // 2
    inv_freq = ROPE_THETA ** (-jnp.arange(0, ROT_DIM, 2, dtype=F32) / ROT_DIM)
    ang = pos.astype(F32)[:, None] * inv_freq[None, :]
    cos, sin = jnp.cos(ang), jnp.sin(ang)
    l64 = np.arange(LANES) % HEAD_DIM
    f = l64 % half
    first = jnp.asarray(l64 < half)[None, :]
    second = jnp.asarray((l64 >= half) & (l64 < ROT_DIM))[None, :]
    c = jnp.where(first | second, cos[:, f], 1.0)
    s1 = jnp.where(first, -sin[:, f], 0.0)
    s2 = jnp.where(second, sin[:, f], 0.0)
    return c.astype(F32), s1.astype(F32), s2.astype(F32)


def _overlap_table_t():
    cs = np.arange(N_CMP)[None, :] * CMP_STRIDE
    js = np.arange(N_SELBLK)[:, None] * SEL_LEN
    ov = np.clip(np.minimum(cs + CMP_LEN, js + SEL_LEN) - np.maximum(cs, js), 0, None) / CMP_LEN
    ov[:, N_CMP - 1] = 0.0
    return ov.astype(np.float32)


def _expand_table_t():
    out = np.zeros((SEQ, LANES), np.float32)
    out[np.arange(SEQ), np.arange(SEQ) // SEL_LEN] = -NEG
    return out.reshape(N_QBLK // SEL_GROUP, SEL_GROUP * QBLK, LANES)


def kernel(x, norm1_g, w_in, pe_kc, w_kc1, w_kc2, pe_vc, w_vc1, w_vc2, w_o, norm2_g, w_rg, b_rg, w_re, b_re,
           w_gate, w_up, w_down, norm_f_g):
    nb, s, d = x.shape
    assert (s, d) == (SEQ, D_MODEL) and norm1_g.shape[0] == 1
    t = nb * s
    x2 = x.reshape(t, d)

    w_in_p = jnp.pad(w_in[0], ((0, 0), (0, D_IN_PAD - w_in.shape[2]))).astype(BF16)
    rc, rs1, rs2 = _rope_tables(jnp.arange(SEQ))
    cc, cs1, cs2 = _rope_tables(jnp.arange(N_CMP) * CMP_STRIDE + CMP_LEN - 1)
    half_flat = CMP_STRIDE * HEAD_DIM

    def cmp_w1(w1):
        return jnp.concatenate([w1[:half_flat], w1[half_flat:]], axis=1).astype(BF16)

    def cmp_pe(pe):
        rows = pe.reshape(2, half_flat)
        return jnp.concatenate([jnp.broadcast_to(rows[0:1], (16, half_flat)),
                                jnp.broadcast_to(rows[1:2], (16, half_flat))], axis=0).astype(BF16)

    wk2d = jnp.concatenate([w_kc2[0], w_kc2[0]], axis=1).astype(BF16)
    wv2t = w_vc2[0].T.astype(BF16)
    ovl_t = jnp.asarray(_overlap_table_t(), BF16)
    expand_t = jnp.asarray(_expand_table_t(), BF16)

    aq, ak, av, bq, kcv, ksw, gates = _proj(x2, norm1_g, w_in_p, rc, rs1, rs2)
    mix_a = _dilated(aq, ak, av)

    kvflat = kcv.reshape(nb, N_CMP, CMP_STRIDE, 2 * N_KV_NSA, HEAD_DIM).transpose(0, 3, 1, 2, 4)
    kvflat = kvflat.reshape(nb, 2 * N_KV_NSA, N_CMP, half_flat)
    mix_b = _nsa(bq, kvflat, ksw, gates, cmp_w1(w_kc1[0]), wk2d, cmp_pe(pe_kc[0]),
                 cmp_w1(w_vc1[0]), wv2t, cmp_pe(pe_vc[0]), cc, cs1, cs2, ovl_t, expand_t)

    wr_t = jnp.concatenate([w_rg[0].T, w_re[0].reshape(d, -1).T], axis=0)
    br = jnp.broadcast_to(jnp.concatenate([b_rg[0], b_re[0].reshape(-1)])[:, None], (72, LANES))
    tri = jnp.asarray(np.triu(np.ones((MOE_TM, MOE_TM), np.float32), 1), BF16)
    x1, hpa, hpb, e1, e2, r1, r2, wtok, cnt = _post(x2, mix_a, mix_b, w_o[0].astype(BF16), norm2_g, wr_t, br, tri)

    pos1, pos2, chunk_expert, chunk_valid = _moe_plan(cnt[:, :, 0], e1[:, 0], e2[:, 0], r1[:, 0], r2[:, 0])
    n_pad = 2 * t + N_GROUPS * EXPERTS_PER_GROUP * FFN_ROWS
    xsa = _sc_scatter_rows(hpa, pos1, pos2, n_pad)
    xsb = _sc_scatter_rows(hpb, pos1, pos2, n_pad)
    ysa, ysb = _ffn(chunk_expert, chunk_valid, xsa, xsb,
                    w_gate[0].astype(BF16), w_up[0].astype(BF16), w_down[0].astype(BF16))
    pos12 = jnp.concatenate([pos1, pos2])
    yga = _sc_gather_rows(ysa, pos12)
    ygb = _sc_gather_rows(ysb, pos12)
    out = _final(x1, yga.reshape(2, t, PACK_W), ygb.reshape(2, t, PACK_W), wtok, norm_f_g.reshape(1, d))
    return out.reshape(nb, s, d)
```

```python
import numpy as np
import jax
import jax.numpy as jnp
from jax import lax
from jax.experimental import pallas as pl
from jax.experimental.pallas import tpu as pltpu
from jax.experimental.pallas import tpu_sc as plsc

F32 = jnp.float32
BF16 = jnp.bfloat16

D_MODEL = 1024
SEQ = 2048
HEAD_DIM = 64
N_HEADS_DIL = 8
DIL_PATTERNS = ((128, 1), (512, 4), (2048, 16))
N_HEADS_NSA = 8
N_KV_NSA = 2
NSA_REP = N_HEADS_NSA // N_KV_NSA
CMP_STRIDE = 16
CMP_LEN = 32
CMP_HIDDEN = 256
SEL_LEN = 64
N_SEL = 8
WIN = 512
ROPE_THETA = 500000.0
ROT_DIM = HEAD_DIM // 4
N_GROUPS = 8
EXPERTS_PER_GROUP = 8
D_FF_EXPERT = 256
EPS = 1e-6
NEG = -1e30
BIG = 1e9
LOWEST = -3.0e38
LOG2_E = 1.4426950408889634

LANES = 128
QBLK = 128
N_QBLK = SEQ // QBLK
N_CMP = SEQ // CMP_STRIDE
N_SELBLK = SEQ // SEL_LEN
D_IN_PAD = 23 * LANES
PROJ_TM = 512
SEL_GROUP = 4
DIL_UNROLL = 8
MOE_TM = 1024
ROUTER_ROWS = 80
FFN_ROWS = 512
FINAL_TM = 512
PACK_W = D_MODEL // 4
SC_WINDOW = 128
VMEM_LIMIT = 56 * 1024 * 1024

_NT = (((1,), (1,)), ((), ()))


def _lane_iota(shape):
    return lax.broadcasted_iota(jnp.int32, shape, 1)


def _row_iota(shape):
    return lax.broadcasted_iota(jnp.int32, shape, 0)


def _rope_lanes(y, c, s1, s2):
    return y * c + pltpu.roll(y, LANES - ROT_DIM // 2, axis=1) * s1 + pltpu.roll(y, ROT_DIM // 2, axis=1) * s2


def _proj_kernel(x_ref, g_ref, w_ref, c_ref, s1_ref, s2_ref,
                 aq_ref, ak_ref, av_ref, aq4_ref, ak4_ref, av4_ref, aq16_ref, ak16_ref, av16_ref,
                 bq_ref, kcv_ref, ksw_ref, gate_ref, nat_f32, d4_f32):
    x = x_ref[...]
    ms = jnp.mean(x * x, axis=-1, keepdims=True)
    h = (x * lax.rsqrt(ms + EPS) * g_ref[...]).astype(BF16)
    c = c_ref[...]
    s1 = s1_ref[...]
    s2 = s2_ref[...]

    def seg(lo, width):
        return jnp.dot(h, w_ref[:, lo:lo + width], preferred_element_type=F32)

    def store(dst, col, y, rope, scale, dilated=None):
        for j in range(y.shape[1] // LANES):
            yj = y[:, j * LANES:(j + 1) * LANES]
            if rope:
                yj = _rope_lanes(yj, c, s1, s2)
            if scale != 1.0:
                yj = yj * scale
            cols = slice(col + j * LANES, col + (j + 1) * LANES)
            dst[:, cols] = yj.astype(dst.dtype)
            if dilated is not None:
                d4_ref, d16_ref = dilated
                quarter = PROJ_TM // 4
                nat_f32[...] = yj
                for r in range(4):
                    part = nat_f32[pl.ds(r, quarter, stride=4), :]
                    d4_ref[r, :, cols] = part.astype(d4_ref.dtype)
                    d4_f32[r * quarter:(r + 1) * quarter, :] = part
                for r in range(4):
                    for a_ in range(4):
                        part = d4_f32[pl.ds(r * quarter + a_, quarter // 4, stride=4), :]
                        d16_ref[4 * a_ + r, :, cols] = part.astype(d16_ref.dtype)

    qscale = HEAD_DIM ** -0.5 * LOG2_E
    store(aq_ref, 0, seg(0, 512), True, qscale, (aq4_ref, aq16_ref))
    store(ak_ref, 0, seg(512, 512), True, 1.0, (ak4_ref, ak16_ref))
    store(av_ref, 0, seg(1024, 512), False, 1.0, (av4_ref, av16_ref))
    store(bq_ref, 0, seg(1536, 512), True, qscale)
    store(kcv_ref, 0, seg(2048, 256), False, 1.0)
    store(ksw_ref, 0, seg(2304, 128), True, 1.0)
    store(ksw_ref, 128, seg(2432, 128), False, 1.0)
    store(ksw_ref, 256, seg(2560, 128), True, 1.0)
    store(ksw_ref, 384, seg(2688, 128), False, 1.0)
    gate_ref[...] = jax.nn.sigmoid(seg(2816, 128))


def _proj(x2, g1, w_in_p, rc, rs1, rs2):
    t = x2.shape[0]
    nb = t // SEQ
    nblk_s = SEQ // PROJ_TM
    row = lambda i: (i, 0)
    pos = lambda i: (i % nblk_s, 0)
    const = lambda i: (0, 0)
    perm = lambda i: (i // nblk_s, 0, i % nblk_s, 0)
    wide = jax.ShapeDtypeStruct((t, 512), BF16)
    wide_spec = pl.BlockSpec((PROJ_TM, 512), row)
    d4 = jax.ShapeDtypeStruct((nb, 4, SEQ // 4, 512), BF16)
    d4_spec = pl.BlockSpec((None, 4, PROJ_TM // 4, 512), perm)
    d16 = jax.ShapeDtypeStruct((nb, 16, SEQ // 16, 512), BF16)
    d16_spec = pl.BlockSpec((None, 16, PROJ_TM // 16, 512), perm)
    outs = [wide] * 3 + [d4] * 3 + [d16] * 3 + [wide, jax.ShapeDtypeStruct((t, 256), BF16), wide,
                                               jax.ShapeDtypeStruct((t, LANES), F32)]
    out_specs = [wide_spec] * 3 + [d4_spec] * 3 + [d16_spec] * 3 + [
        wide_spec, pl.BlockSpec((PROJ_TM, 256), row), wide_spec, pl.BlockSpec((PROJ_TM, LANES), row)]
    return pl.pallas_call(
        _proj_kernel,
        grid=(t // PROJ_TM,),
        in_specs=[pl.BlockSpec((PROJ_TM, D_MODEL), row), pl.BlockSpec((1, D_MODEL), const),
                  pl.BlockSpec((D_MODEL, D_IN_PAD), const),
                  pl.BlockSpec((PROJ_TM, LANES), pos), pl.BlockSpec((PROJ_TM, LANES), pos),
                  pl.BlockSpec((PROJ_TM, LANES), pos)],
        out_specs=out_specs,
        out_shape=outs,
        scratch_shapes=[pltpu.VMEM((PROJ_TM, LANES), F32)] * 2,
        compiler_params=pltpu.CompilerParams(dimension_semantics=("arbitrary",),
                                             vmem_limit_bytes=VMEM_LIMIT),
        name="proj",
    )(x2, g1, w_in_p, rc, rs1, rs2)


def _dilated_kernel(q1_ref, k1_ref, v1_ref, q4_ref, k4_ref, v4_ref, q16_ref, k16_ref, v16_ref, o_ref,
                    qt, vt, op, lp, on0, on1, on2, ln0, ln1, ln2):
    c2 = _row_iota((2 * QBLK, 2 * QBLK))
    a2 = _lane_iota((2 * QBLK, 2 * QBLK)) % QBLK
    band = jnp.where((c2 >= a2) & (c2 <= a2 + QBLK), 0.0, NEG)
    band_noprev = jnp.where(c2 < QBLK, NEG, band)
    c1 = _row_iota((QBLK, 2 * QBLK))
    a1 = _lane_iota((QBLK, 2 * QBLK)) % QBLK
    band_own = jnp.where(c1 <= a1, 0.0, NEG)
    top_sq = _row_iota((LANES, QBLK)) < HEAD_DIM

    inputs = ((q1_ref, k1_ref, v1_ref), (q4_ref, k4_ref, v4_ref), (q16_ref, k16_ref, v16_ref))
    for pi, (window, dil) in enumerate(DIL_PATTERNS):
        seg_len = SEQ // dil
        nseg_blk = seg_len // QBLK
        q_ref, k_ref, v_ref = inputs[pi]

        def block(ref, j, d=None, nseg_blk=nseg_blk, dil=dil):
            if dil == 1:
                return ref[pl.ds(pl.multiple_of(j * QBLK, QBLK), QBLK), :]
            if d is None:
                return ref[j // nseg_blk, (j % nseg_blk) * QBLK:(j % nseg_blk + 1) * QBLK, :]
            within = d % nseg_blk
            return ref[j // nseg_blk, within * QBLK:(within + 1) * QBLK, :]

        for j in range(N_QBLK):
            qt[j] = block(q_ref, j).T if dil > 1 else q_ref[j * QBLK:(j + 1) * QBLK, :].T
            vt[j] = block(v_ref, j).T if dil > 1 else v_ref[j * QBLK:(j + 1) * QBLK, :].T

        def body(u, carry, nseg_blk=nseg_blk, k_ref=k_ref, block=block):
            js = [u * DIL_UNROLL + d for d in range(DIL_UNROLL)]
            prevs, scores = [], []
            for d, j in enumerate(js):
                qtb = qt[j]
                zero = jnp.zeros_like(qtb)
                q_both = jnp.concatenate([jnp.where(top_sq, qtb, zero), jnp.where(top_sq, zero, qtb)], axis=1)
                if nseg_blk == 1:
                    keys, bias, jp = block(k_ref, j, d), band_own, j
                else:
                    if nseg_blk <= DIL_UNROLL:
                        first = d % nseg_blk == 0
                        jp = j if first else j - 1
                        dp = d if first else d - 1
                        bias = band_noprev if first else band
                    else:
                        jp = jnp.maximum(j - 1, 0) if d == 0 else j - 1
                        dp = d - 1
                        bias = jnp.where(u == 0, band_noprev, band) if d == 0 else band
                    keys = jnp.concatenate([block(k_ref, jp, dp), block(k_ref, j, d)], axis=0)
                prevs.append(jp)
                scores.append(jnp.dot(keys, q_both, preferred_element_type=F32) + bias)
            stats = []
            for s in scores:
                m = jnp.max(s, axis=0, keepdims=True)
                p = jnp.exp2(s - m)
                stats.append((m, jnp.sum(p, axis=0, keepdims=True), p.astype(BF16)))
            outs = []
            for j, jp, (m, l, p) in zip(js, prevs, stats):
                v_span = vt[j] if nseg_blk == 1 else jnp.concatenate([vt[jp], vt[j]], axis=1)
                outs.append(jnp.dot(v_span, p, preferred_element_type=F32))
            for j, (m, l, p), res in zip(js, stats, outs):
                row0 = pl.multiple_of(j * QBLK, QBLK)
                lse = m + jnp.log2(l)
                o_t = jnp.where(top_sq, res[:, 0:QBLK] / l[:, 0:QBLK], res[:, QBLK:] / l[:, QBLK:])
                lse_t = jnp.where(top_sq, jnp.broadcast_to(lse[:, 0:QBLK], (LANES, QBLK)),
                                  jnp.broadcast_to(lse[:, QBLK:], (LANES, QBLK)))
                op[pl.ds(row0, QBLK), :] = o_t.T
                lp[pl.ds(row0, QBLK), :] = lse_t.T
            return carry

        lax.fori_loop(0, N_QBLK // DIL_UNROLL, body, 0)

        for r in range(dil):
            dst = pl.ds(r, seg_len, stride=dil) if dil > 1 else pl.ds(0, SEQ)
            src = pl.ds(r * seg_len, seg_len)
            (on0, on1, on2)[pi][dst, :] = op[src, :]
            (ln0, ln1, ln2)[pi][dst, :] = lp[src, :]

    l0, l1, l2 = ln0[...], ln1[...], ln2[...]
    mx = jnp.maximum(jnp.maximum(l0, l1), l2)
    e0, e1, e2 = jnp.exp2(l0 - mx), jnp.exp2(l1 - mx), jnp.exp2(l2 - mx)
    den = e0 + e1 + e2
    out = (e0 / den) * on0[...] + (e1 / den) * on1[...] + (e2 / den) * on2[...]
    o_ref[...] = out.astype(o_ref.dtype)


def _dilated(a1, a4, a16):
    t = a1[0].shape[0]
    nb = t // SEQ
    spec1 = pl.BlockSpec((SEQ, LANES), lambda b, hp: (b, hp))
    spec4 = pl.BlockSpec((None, 4, SEQ // 4, LANES), lambda b, hp: (b, 0, 0, hp))
    spec16 = pl.BlockSpec((None, 16, SEQ // 16, LANES), lambda b, hp: (b, 0, 0, hp))
    return pl.pallas_call(
        _dilated_kernel,
        grid=(nb, N_HEADS_DIL // 2),
        in_specs=[spec1] * 3 + [spec4] * 3 + [spec16] * 3,
        out_specs=spec1,
        out_shape=jax.ShapeDtypeStruct((t, 512), BF16),
        scratch_shapes=[pltpu.VMEM((N_QBLK, LANES, QBLK), BF16), pltpu.VMEM((N_QBLK, LANES, QBLK), BF16)]
        + [pltpu.VMEM((SEQ, LANES), F32)] * 8,
        compiler_params=pltpu.CompilerParams(dimension_semantics=("arbitrary", "arbitrary"),
                                             vmem_limit_bytes=VMEM_LIMIT),
        name="dilated",
    )(*a1, *a4, *a16)


def _gelu_tanh(x):
    return 0.5 * x * (1.0 + jnp.tanh(np.sqrt(2.0 / np.pi).astype(np.float32) * (x + 0.044715 * (x * x * x))))


def _softmax_step(state, s, v_t):
    m, l, acc = state
    mn = jnp.maximum(m, jnp.max(s, axis=0, keepdims=True))
    al = jnp.exp2(m - mn)
    p = jnp.exp2(s - mn)
    l = al * l + jnp.sum(p, axis=0, keepdims=True)
    acc = al * acc + jnp.dot(v_t, p.astype(BF16), preferred_element_type=F32)
    return mn, l, acc


def _nsa_kernel(q_ref, kflat_ref, vflat_ref, ksw_ref, gate_ref,
                wk1_ref, wk2d_ref, pek_ref, wv1_ref, wv2t_ref, pev_ref,
                cc_ref, cs1_ref, cs2_ref, ovl_ref, exp_ref,
                o_ref,
                qt, vst, vwt, gtt, kc2, vct):
    g = pl.program_id(1)
    nq = NSA_REP * QBLK

    for i in range(N_QBLK):
        rows = pl.ds(i * QBLK, QBLK)
        qt[i, 0:LANES, :] = q_ref[rows, 0:LANES].T
        qt[i, LANES:2 * LANES, :] = q_ref[rows, LANES:2 * LANES].T
        vs_t = ksw_ref[rows, LANES:2 * LANES].T
        vst[i] = jnp.where(g == 0, vs_t[0:HEAD_DIM], vs_t[HEAD_DIM:])
        vw_t = ksw_ref[rows, 3 * LANES:4 * LANES].T
        vwt[i] = jnp.where(g == 0, vw_t[0:HEAD_DIM], vw_t[HEAD_DIM:])
        gt = gate_ref[rows, :]
        gt = jnp.where(g == 0, gt, pltpu.roll(gt, LANES - 3 * NSA_REP, axis=1))
        gtt[i] = gt.T[0:16, :]

    def hidden(flat_ref, w1_ref, pe_ref):
        ab = jnp.dot(flat_ref[...], w1_ref[...], preferred_element_type=F32)
        pb = (jnp.dot(pe_ref[0:16, :], w1_ref[:, 0:CMP_HIDDEN], preferred_element_type=F32)
              + jnp.dot(pe_ref[16:32, :], w1_ref[:, CMP_HIDDEN:], preferred_element_type=F32))
        hid = ab[:, 0:CMP_HIDDEN] + pltpu.roll(ab[:, CMP_HIDDEN:], N_CMP - 1, axis=0) + pb[0:1, :]
        return _gelu_tanh(hid).astype(BF16)

    hk = hidden(kflat_ref, wk1_ref, pek_ref)
    kc2[...] = _rope_lanes(jnp.dot(hk, wk2d_ref[...], preferred_element_type=F32),
                           cc_ref[...], cs1_ref[...], cs2_ref[...]).astype(BF16)
    hv = hidden(vflat_ref, wv1_ref, pev_ref)
    vct[...] = lax.dot_general(wv2t_ref[...], hv, _NT, preferred_element_type=F32).astype(BF16)

    key = _row_iota((QBLK, nq))
    qry = _lane_iota((QBLK, nq)) % QBLK
    causal = jnp.where(key <= qry, 0.0, NEG)
    upper = jnp.where(key > qry, 0.0, NEG)
    blk = _row_iota((N_SELBLK, QBLK))
    qry1 = _lane_iota((N_SELBLK, QBLK))
    grp_keys = SEL_GROUP * QBLK
    win_blocks = WIN // QBLK

    def qgroup(w, full_window):
        blocks = [SEL_GROUP * w + d for d in range(SEL_GROUP)]
        row0s = [pl.multiple_of(i * QBLK, QBLK) for i in blocks]

        q_pads = []
        for i in blocks:
            q_t = jnp.concatenate([qt[i, r * HEAD_DIM:(r + 1) * HEAD_DIM, :] for r in range(NSA_REP)], axis=1)
            zero = jnp.zeros_like(q_t)
            q_pads.append(jnp.where(g == 0, jnp.concatenate([q_t, zero], axis=0),
                                    jnp.concatenate([zero, q_t], axis=0)))
        s_cmp = [jnp.dot(kc2[...], qp, preferred_element_type=F32) for qp in q_pads]
        s_win, v_win = [], []
        for d, (i, qp) in enumerate(zip(blocks, q_pads)):
            if full_window:
                first = i - win_blocks
                keys = ksw_ref[pl.ds(pl.multiple_of(first * QBLK, QBLK), (win_blocks + 1) * QBLK), 2 * LANES:3 * LANES]
                s = jnp.dot(keys, qp, preferred_element_type=F32)
                s = jnp.concatenate([s[0:QBLK] + upper, s[QBLK:win_blocks * QBLK], s[win_blocks * QBLK:] + causal],
                                    axis=0)
                v_t = jnp.concatenate([vwt[first + e] for e in range(win_blocks + 1)], axis=1)
            else:
                s = jnp.dot(ksw_ref[0:(d + 1) * QBLK, 2 * LANES:3 * LANES], qp, preferred_element_type=F32)
                s = jnp.concatenate([s[0:d * QBLK], s[d * QBLK:] + causal], axis=0) if d else s + causal
                v_t = jnp.concatenate([vwt[e] for e in range(d + 1)], axis=1) if d else vwt[0]
            s_win.append(s)
            v_win.append(v_t)

        pcs = []
        for row0, s in zip(row0s, s_cmp):
            cvalid = (CMP_STRIDE * key + (CMP_LEN - 1) <= row0 + qry) & (key < N_CMP - 1)
            s = jnp.where(cvalid, s, NEG)
            m = jnp.max(s, axis=0, keepdims=True)
            p = jnp.where(cvalid, jnp.exp2(s - m), 0.0)
            l = jnp.sum(p, axis=0, keepdims=True)
            pcs.append(p / jnp.where(l > 0, l, 1.0))
        o_cmp = [jnp.dot(vct[...], pc.astype(BF16), preferred_element_type=F32) for pc in pcs]

        ovl = ovl_ref[...]
        imps = []
        for pc in pcs:
            psum = pc[:, 0:QBLK] + pc[:, QBLK:2 * QBLK] + pc[:, 2 * QBLK:3 * QBLK] + pc[:, 3 * QBLK:]
            p_hi = psum.astype(BF16)
            p_lo = (psum - p_hi.astype(F32)).astype(BF16)
            imps.append(jnp.dot(ovl, p_hi, preferred_element_type=F32) + jnp.dot(ovl, p_lo, preferred_element_type=F32))

        o_win = []
        for s, v_t in zip(s_win, v_win):
            m = jnp.max(s, axis=0, keepdims=True)
            p = jnp.exp2(s - m)
            l = jnp.sum(p, axis=0, keepdims=True)
            o_win.append(jnp.dot(v_t, p.astype(BF16), preferred_element_type=F32) / l)

        q_sel = []
        for row0, imp, qp in zip(row0s, imps, q_pads):
            t1 = row0 + qry1
            blk_t = t1 // SEL_LEN
            forced = (blk == 0) | (blk == blk_t) | (blk == blk_t - 1)
            v = jnp.where(blk * SEL_LEN <= t1, jnp.where(forced, BIG, imp), -BIG)
            selm = jnp.zeros((N_SELBLK, QBLK), F32)
            for _ in range(N_SEL):
                mx = jnp.max(v, axis=0, keepdims=True)
                first_hit = jnp.min(jnp.where(v == mx, blk, N_SELBLK), axis=0, keepdims=True)
                hit = blk == first_hit
                selm = jnp.where(hit, 1.0, selm)
                v = jnp.where(hit, LOWEST, v)
            notsel = jnp.concatenate([selm - 1.0, jnp.zeros((LANES - N_SELBLK, QBLK), F32)], axis=0).astype(BF16)
            q_sel.append(jnp.concatenate([qp, jnp.concatenate([notsel] * NSA_REP, axis=1)], axis=0))

        def sel_keys(t, n_rows):
            rows_t = pl.ds(pl.multiple_of(t * grp_keys, grp_keys), n_rows)
            return jnp.concatenate([ksw_ref[rows_t, 0:LANES], exp_ref[t, 0:n_rows, :]], axis=1)

        def sel_values(t, n_blk):
            return jnp.concatenate([vst[t * SEL_GROUP + e] for e in range(n_blk)], axis=1) if n_blk > 1 else vst[t * SEL_GROUP]

        wide = SEL_GROUP * nq
        st = (jnp.full((1, wide), LOWEST, F32), jnp.zeros((1, wide), F32), jnp.zeros((HEAD_DIM, wide), F32))
        if full_window:
            q_all = jnp.concatenate(q_sel, axis=1)

            def sel_body(t, st):
                s = jnp.dot(sel_keys(t, grp_keys), q_all, preferred_element_type=F32)
                return _softmax_step(st, s, sel_values(t, SEL_GROUP))

            st = lax.fori_loop(0, w, sel_body, st)
        o_sel = []
        for d, qs in enumerate(q_sel):
            s = jnp.dot(sel_keys(w, (d + 1) * QBLK), qs, preferred_element_type=F32)
            s = jnp.concatenate([s[0:d * QBLK], s[d * QBLK:] + causal], axis=0) if d else s + causal
            lanes = slice(d * nq, (d + 1) * nq)
            m, l, acc = _softmax_step((st[0][:, lanes], st[1][:, lanes], st[2][:, lanes]), s, sel_values(w, d + 1))
            o_sel.append(acc / l)

        for n, (i, row0) in enumerate(zip(blocks, row0s)):
            gti = gtt[i]

            def gate(br):
                return jnp.concatenate([gti[r * 3 + br:r * 3 + br + 1, :] for r in range(NSA_REP)], axis=1)

            out_t = gate(0) * o_cmp[n] + gate(1) * o_sel[n] + gate(2) * o_win[n]
            for pair in range(NSA_REP // 2):
                both = jnp.concatenate([out_t[:, (2 * pair) * QBLK:(2 * pair + 1) * QBLK],
                                        out_t[:, (2 * pair + 1) * QBLK:(2 * pair + 2) * QBLK]], axis=0)
                o_ref[pl.ds(row0, QBLK), pair * LANES:(pair + 1) * LANES] = both.T.astype(o_ref.dtype)

    def tail_groups(w, carry):
        qgroup(w, True)
        return carry

    qgroup(0, False)
    lax.fori_loop(1, N_QBLK // SEL_GROUP, tail_groups, 0)


def _nsa(bq, kvflat, ksw, gates, wk1, wk2d, pek, wv1, wv2t, pev, cc, cs1, cs2, ovl_t, expand_t):
    t = bq.shape[0]
    nb = t // SEQ
    const2 = lambda b, g: (0, 0)
    return pl.pallas_call(
        _nsa_kernel,
        grid=(nb, N_KV_NSA),
        in_specs=[
            pl.BlockSpec((SEQ, 2 * LANES), lambda b, g: (b, g)),
            pl.BlockSpec((None, None, N_CMP, CMP_STRIDE * HEAD_DIM), lambda b, g: (b, g, 0, 0)),
            pl.BlockSpec((None, None, N_CMP, CMP_STRIDE * HEAD_DIM), lambda b, g: (b, N_KV_NSA + g, 0, 0)),
            pl.BlockSpec((SEQ, 4 * LANES), lambda b, g: (b, 0)),
            pl.BlockSpec((SEQ, LANES), lambda b, g: (b, 0)),
            pl.BlockSpec((CMP_STRIDE * HEAD_DIM, 2 * CMP_HIDDEN), const2),
            pl.BlockSpec((CMP_HIDDEN, LANES), const2),
            pl.BlockSpec((32, CMP_STRIDE * HEAD_DIM), const2),
            pl.BlockSpec((CMP_STRIDE * HEAD_DIM, 2 * CMP_HIDDEN), const2),
            pl.BlockSpec((HEAD_DIM, CMP_HIDDEN), const2),
            pl.BlockSpec((32, CMP_STRIDE * HEAD_DIM), const2),
            pl.BlockSpec((N_CMP, LANES), const2),
            pl.BlockSpec((N_CMP, LANES), const2),
            pl.BlockSpec((N_CMP, LANES), const2),
            pl.BlockSpec((N_SELBLK, LANES), const2),
            pl.BlockSpec((N_QBLK // SEL_GROUP, SEL_GROUP * QBLK, LANES), lambda b, g: (0, 0, 0)),
        ],
        out_specs=pl.BlockSpec((SEQ, 2 * LANES), lambda b, g: (b, g)),
        out_shape=jax.ShapeDtypeStruct((t, 512), BF16),
        scratch_shapes=[pltpu.VMEM((N_QBLK, 2 * LANES, QBLK), BF16),
                        pltpu.VMEM((N_QBLK, HEAD_DIM, QBLK), BF16),
                        pltpu.VMEM((N_QBLK, HEAD_DIM, QBLK), BF16),
                        pltpu.VMEM((N_QBLK, 16, QBLK), F32),
                        pltpu.VMEM((N_CMP, LANES), BF16),
                        pltpu.VMEM((HEAD_DIM, N_CMP), BF16)],
        compiler_params=pltpu.CompilerParams(dimension_semantics=("arbitrary", "arbitrary"),
                                             vmem_limit_bytes=VMEM_LIMIT),
        name="nsa",
    )(bq, kvflat, kvflat, ksw, gates, wk1, wk2d, pek, wv1, wv2t, pev, cc, cs1, cs2, ovl_t, expand_t)


def _pack_bf16_pairs(lo, hi):
    lo_b = lax.bitcast_convert_type(lo.astype(BF16).astype(F32), jnp.uint32)
    hi_b = lax.bitcast_convert_type(hi.astype(BF16).astype(F32), jnp.uint32)
    return (hi_b & jnp.uint32(0xFFFF0000)) | (lo_b >> 16)


def _unpack_bf16_pairs(p):
    lo = lax.bitcast_convert_type(p << 16, F32)
    hi = lax.bitcast_convert_type(p & jnp.uint32(0xFFFF0000), F32)
    return lo, hi


def _post_kernel(x_ref, ma_ref, mb_ref, wo_ref, g2_ref, wr_ref, br_ref, tri_ref,
                 x1_ref, ha_ref, hb_ref, e1_ref, e2_ref, r1_ref, r2_ref, w_ref, cnt_ref):
    x1 = (x_ref[...] + jnp.dot(ma_ref[...], wo_ref[0:512, :], preferred_element_type=F32)
          + jnp.dot(mb_ref[...], wo_ref[512:1024, :], preferred_element_type=F32))
    x1_ref[...] = x1
    ms = jnp.mean(x1 * x1, axis=-1, keepdims=True)
    h2 = x1 * lax.rsqrt(ms + EPS) * g2_ref[...]
    ha_ref[...] = _pack_bf16_pairs(h2[:, 0:PACK_W], h2[:, 2 * PACK_W:3 * PACK_W])
    hb_ref[...] = _pack_bf16_pairs(h2[:, PACK_W:2 * PACK_W], h2[:, 3 * PACK_W:])

    h_hi = h2.astype(BF16)
    h_lo = (h2 - h_hi.astype(F32)).astype(BF16)
    wr = wr_ref[...]
    w_hi = wr.astype(BF16)
    w_lo = (wr - w_hi.astype(F32)).astype(BF16)
    lg = (lax.dot_general(w_hi, h_hi, _NT, preferred_element_type=F32)
          + lax.dot_general(w_hi, h_lo, _NT, preferred_element_type=F32)
          + lax.dot_general(w_lo, h_hi, _NT, preferred_element_type=F32)) + br_ref[:, 0:1]
    gl = lg[0:N_GROUPS, :]
    sub = _row_iota(gl.shape)
    gmax = jnp.max(gl, axis=0, keepdims=True)
    p_g = 1.0 / jnp.sum(jnp.exp(gl - gmax), axis=0, keepdims=True)
    g_star = jnp.min(jnp.where(gl == gmax, sub, N_GROUPS), axis=0, keepdims=True)
    el = jnp.zeros_like(gl)
    for gi in range(N_GROUPS):
        lo = N_GROUPS + gi * EXPERTS_PER_GROUP
        el = el + jnp.where(g_star == gi, lg[lo:lo + EXPERTS_PER_GROUP, :], 0.0)
    emax = jnp.max(el, axis=0, keepdims=True)
    ee = jnp.exp(el - emax)
    pe = ee / jnp.sum(ee, axis=0, keepdims=True)
    v1 = jnp.max(pe, axis=0, keepdims=True)
    i1 = jnp.min(jnp.where(pe == v1, sub, EXPERTS_PER_GROUP), axis=0, keepdims=True)
    pe2 = jnp.where(sub == i1, -1.0, pe)
    v2 = jnp.max(pe2, axis=0, keepdims=True)
    i2 = jnp.min(jnp.where(pe2 == v2, sub, EXPERTS_PER_GROUP), axis=0, keepdims=True)
    tot = v1 + v2
    e1 = g_star * EXPERTS_PER_GROUP + i1
    e2 = g_star * EXPERTS_PER_GROUP + i2
    e1_ref[0] = e1
    e2_ref[0] = e2

    wslab = jnp.concatenate([v1 / tot * p_g, v2 / tot * p_g, jnp.zeros((LANES - 2, MOE_TM), F32)], axis=0)
    for j in range(MOE_TM // LANES):
        w_ref[j * LANES:(j + 1) * LANES, :] = wslab[:, j * LANES:(j + 1) * LANES].T

    n_exp = N_GROUPS * EXPERTS_PER_GROUP
    sub_e = _row_iota((n_exp, MOE_TM))
    hit1 = sub_e == e1
    hit2 = sub_e == e2
    assigned = jnp.where(hit1, 1.0, 0.0) + jnp.where(hit2, 1.0, 0.0)
    before = jnp.dot(assigned.astype(BF16), tri_ref[...], preferred_element_type=F32)
    r1_ref[0] = jnp.sum(jnp.where(hit1, before, 0.0), axis=0, keepdims=True).astype(jnp.int32)
    r2_ref[0] = jnp.sum(jnp.where(hit2, before, 0.0), axis=0, keepdims=True).astype(jnp.int32)
    cnt = jnp.sum(assigned, axis=1, keepdims=True).astype(jnp.int32)
    cnt_ref[0] = jnp.broadcast_to(cnt, (n_exp, LANES))


def _post(x2, mix_a, mix_b, w_o, g2, wr_t, br, tri):
    t = x2.shape[0]
    nt = t // MOE_TM
    n_exp = N_GROUPS * EXPERTS_PER_GROUP
    row = lambda i: (i, 0)
    const = lambda i: (0, 0)
    tok = lambda i: (i, 0, 0)
    tok_spec = pl.BlockSpec((1, 1, MOE_TM), tok)
    tok_shape = jax.ShapeDtypeStruct((nt, 1, MOE_TM), jnp.int32)
    return pl.pallas_call(
        _post_kernel,
        grid=(nt,),
        in_specs=[pl.BlockSpec((MOE_TM, D_MODEL), row), pl.BlockSpec((MOE_TM, 512), row),
                  pl.BlockSpec((MOE_TM, 512), row), pl.BlockSpec((D_MODEL, D_MODEL), const),
                  pl.BlockSpec((1, D_MODEL), const), pl.BlockSpec((ROUTER_ROWS, D_MODEL), const),
                  pl.BlockSpec((ROUTER_ROWS, LANES), const), pl.BlockSpec((MOE_TM, MOE_TM), const)],
        out_specs=[pl.BlockSpec((MOE_TM, D_MODEL), row),
                   pl.BlockSpec((MOE_TM, PACK_W), row), pl.BlockSpec((MOE_TM, PACK_W), row),
                   tok_spec, tok_spec, tok_spec, tok_spec,
                   pl.BlockSpec((MOE_TM, LANES), row),
                   pl.BlockSpec((1, n_exp, LANES), tok)],
        out_shape=[jax.ShapeDtypeStruct((t, D_MODEL), F32),
                   jax.ShapeDtypeStruct((t, PACK_W), jnp.uint32), jax.ShapeDtypeStruct((t, PACK_W), jnp.uint32),
                   tok_shape, tok_shape, tok_shape, tok_shape,
                   jax.ShapeDtypeStruct((t, LANES), F32),
                   jax.ShapeDtypeStruct((nt, n_exp, LANES), jnp.int32)],
        compiler_params=pltpu.CompilerParams(dimension_semantics=("arbitrary",),
                                             vmem_limit_bytes=VMEM_LIMIT),
        name="post",
    )(x2, mix_a, mix_b, w_o, g2, wr_t, br, tri)


def _sc_mesh():
    return plsc.VectorSubcoreMesh(core_axis_name="core", subcore_axis_name="subcore")


def _sc_scatter_rows(x, pos1, pos2, n_out):
    r, c = x.shape

    @pl.kernel(out_type=jax.ShapeDtypeStruct((n_out, c), x.dtype), mesh=_sc_mesh(), scratch_types=[])
    def scatter_kernel(x_hbm, p1_hbm, p2_hbm, o_hbm):
        def body(x_vmem, p1_vmem, p2_vmem):
            pltpu.sync_copy(x_vmem, o_hbm.at[p1_vmem.at[0]])
            pltpu.sync_copy(x_vmem, o_hbm.at[p2_vmem.at[0]])

        idx = pl.BlockSpec((1, SC_WINDOW), lambda i: (0, i))
        pltpu.emit_pipeline(body, grid=(r // SC_WINDOW,),
                            in_specs=[pl.BlockSpec((SC_WINDOW, c), lambda i: (i, 0)), idx, idx], out_specs=[],
                            core_axis_name=("core", "subcore"),
                            dimension_semantics=(pltpu.PARALLEL,))(x_hbm, p1_hbm, p2_hbm)

    return scatter_kernel(x, pos1.reshape(1, r), pos2.reshape(1, r))


def _sc_gather_rows(y, idx):
    n = idx.shape[0]
    c = y.shape[1]

    @pl.kernel(out_type=jax.ShapeDtypeStruct((n, c), y.dtype), mesh=_sc_mesh(), scratch_types=[])
    def gather_kernel(y_hbm, i_hbm, o_hbm):
        def body(i_vmem, o_vmem):
            pltpu.sync_copy(y_hbm.at[i_vmem.at[0]], o_vmem)

        pltpu.emit_pipeline(body, grid=(n // SC_WINDOW,),
                            in_specs=[pl.BlockSpec((1, SC_WINDOW), lambda i: (0, i))],
                            out_specs=[pl.BlockSpec((SC_WINDOW, c), lambda i: (i, 0))],
                            core_axis_name=("core", "subcore"),
                            dimension_semantics=(pltpu.PARALLEL,))(i_hbm, o_hbm)

    return gather_kernel(y, idx.reshape(1, n))


def _ffn_kernel(ce_ref, cv_ref, xa_ref, xb_ref, wg_ref, wu_ref, wd_ref, ya_ref, yb_ref, wg_bf, wu_bf, wd_bf):
    c = pl.program_id(0)
    prev = ce_ref[jnp.maximum(c - 1, 0)]

    @pl.when((c == 0) | (ce_ref[c] != prev))
    def _():
        wg_bf[...] = wg_ref[0].astype(BF16)
        wu_bf[...] = wu_ref[0].astype(BF16)
        wd_bf[...] = wd_ref[0].astype(BF16)

    @pl.when(cv_ref[c] == 0)
    def _():
        ya_ref[...] = jnp.zeros_like(ya_ref)
        yb_ref[...] = jnp.zeros_like(yb_ref)

    @pl.when(cv_ref[c] != 0)
    def _():
        a_lo, a_hi = _unpack_bf16_pairs(xa_ref[...])
        b_lo, b_hi = _unpack_bf16_pairs(xb_ref[...])
        xs = jnp.concatenate([a_lo, b_lo, a_hi, b_hi], axis=1).astype(BF16)
        gate = jnp.dot(xs, wg_bf[...], preferred_element_type=F32)
        up = jnp.dot(xs, wu_bf[...], preferred_element_type=F32)
        he = (gate * jax.nn.sigmoid(gate) * up).astype(BF16)
        y = jnp.dot(he, wd_bf[...], preferred_element_type=F32)
        ya_ref[...] = _pack_bf16_pairs(y[:, 0:PACK_W], y[:, 2 * PACK_W:3 * PACK_W])
        yb_ref[...] = _pack_bf16_pairs(y[:, PACK_W:2 * PACK_W], y[:, 3 * PACK_W:])


def _ffn(chunk_expert, chunk_valid, xsa, xsb, wg, wu, wd):
    n_pad = xsa.shape[0]
    rows = pl.BlockSpec((FFN_ROWS, PACK_W), lambda c, ce, cv: (c, 0))
    grid_spec = pltpu.PrefetchScalarGridSpec(
        num_scalar_prefetch=2,
        grid=(n_pad // FFN_ROWS,),
        in_specs=[rows, rows,
                  pl.BlockSpec((1, D_MODEL, D_FF_EXPERT), lambda c, ce, cv: (ce[c], 0, 0)),
                  pl.BlockSpec((1, D_MODEL, D_FF_EXPERT), lambda c, ce, cv: (ce[c], 0, 0)),
                  pl.BlockSpec((1, D_FF_EXPERT, D_MODEL), lambda c, ce, cv: (ce[c], 0, 0))],
        out_specs=[rows, rows],
        scratch_shapes=[pltpu.VMEM((D_MODEL, D_FF_EXPERT), BF16), pltpu.VMEM((D_MODEL, D_FF_EXPERT), BF16),
                        pltpu.VMEM((D_FF_EXPERT, D_MODEL), BF16)],
    )
    out = jax.ShapeDtypeStruct((n_pad, PACK_W), jnp.uint32)
    return pl.pallas_call(
        _ffn_kernel,
        grid_spec=grid_spec,
        out_shape=[out, out],
        compiler_params=pltpu.CompilerParams(dimension_semantics=("arbitrary",),
                                             vmem_limit_bytes=VMEM_LIMIT),
        name="ffn",
    )(chunk_expert, chunk_valid, xsa, xsb, wg, wu, wd)


def _final_kernel(x1_ref, y1a_ref, y1b_ref, y2a_ref, y2b_ref, w_ref, gf_ref, o_ref):
    w1 = w_ref[:, 0:1]
    w2 = w_ref[:, 1:2]
    a1_lo, a1_hi = _unpack_bf16_pairs(y1a_ref[...])
    b1_lo, b1_hi = _unpack_bf16_pairs(y1b_ref[...])
    a2_lo, a2_hi = _unpack_bf16_pairs(y2a_ref[...])
    b2_lo, b2_hi = _unpack_bf16_pairs(y2b_ref[...])
    y = jnp.concatenate([w1 * a1_lo + w2 * a2_lo, w1 * b1_lo + w2 * b2_lo,
                         w1 * a1_hi + w2 * a2_hi, w1 * b1_hi + w2 * b2_hi], axis=1)
    xo = x1_ref[...] + y
    ms = jnp.mean(xo * xo, axis=-1, keepdims=True)
    o_ref[...] = xo * lax.rsqrt(ms + EPS) * gf_ref[...]


def _final(x1, yga, ygb, wtok, gf):
    t = x1.shape[0]
    row = lambda i: (i, 0)
    first = pl.BlockSpec((None, FINAL_TM, PACK_W), lambda i: (0, i, 0))
    second = pl.BlockSpec((None, FINAL_TM, PACK_W), lambda i: (1, i, 0))
    return pl.pallas_call(
        _final_kernel,
        grid=(t // FINAL_TM,),
        in_specs=[pl.BlockSpec((FINAL_TM, D_MODEL), row), first, first, second, second,
                  pl.BlockSpec((FINAL_TM, LANES), row), pl.BlockSpec((1, D_MODEL), lambda i: (0, 0))],
        out_specs=pl.BlockSpec((FINAL_TM, D_MODEL), row),
        out_shape=jax.ShapeDtypeStruct((t, D_MODEL), F32),
        compiler_params=pltpu.CompilerParams(dimension_semantics=("arbitrary",),
                                             vmem_limit_bytes=VMEM_LIMIT),
        name="final",
    )(x1, yga, ygb, yga, ygb, wtok, gf)


def _moe_plan(cnt, e1, e2, r1, r2):
    nt, n_exp = cnt.shape
    tot = jnp.sum(cnt, axis=0)
    padded = (tot + FFN_ROWS - 1) // FFN_ROWS * FFN_ROWS
    seg_end = jnp.cumsum(padded)
    seg_start = seg_end - padded
    off = seg_start[None, :] + jnp.cumsum(cnt, axis=0) - cnt
    experts = jnp.arange(n_exp, dtype=jnp.int32)

    def lookup(e):
        return jnp.sum(jnp.where(e[:, :, None] == experts, off[:, None, :], 0), axis=2)

    pos1 = (lookup(e1) + r1).reshape(-1).astype(jnp.int32)
    pos2 = (lookup(e2) + r2).reshape(-1).astype(jnp.int32)
    n_chunks = (nt * MOE_TM * 2 + n_exp * FFN_ROWS) // FFN_ROWS
    cstart = jnp.arange(n_chunks, dtype=jnp.int32) * FFN_ROWS
    ce = jnp.sum((seg_end[None, :] <= cstart[:, None]).astype(jnp.int32), axis=1)
    cec = jnp.minimum(ce, n_exp - 1)
    valid = (ce < n_exp) & (cstart < seg_start[cec] + tot[cec])
    return pos1, pos2, cec.astype(jnp.int32), valid.astype(jnp.int32)


def _rope_tables(pos):
    half = ROT_DIM // 2
    inv_freq = ROPE_THETA ** (-jnp.arange(0, ROT_DIM, 2, dtype=F32) / ROT_DIM)
    ang = pos.astype(F32)[:, None] * inv_freq[None, :]
    cos, sin = jnp.cos(ang), jnp.sin(ang)
    l64 = np.arange(LANES) % HEAD_DIM
    f = l64 % half
    first = jnp.asarray(l64 < half)[None, :]
    second = jnp.asarray((l64 >= half) & (l64 < ROT_DIM))[None, :]
    c = jnp.where(first | second, cos[:, f], 1.0)
    s1 = jnp.where(first, -sin[:, f], 0.0)
    s2 = jnp.where(second, sin[:, f], 0.0)
    return c.astype(F32), s1.astype(F32), s2.astype(F32)


def _overlap_table_t():
    cs = np.arange(N_CMP)[None, :] * CMP_STRIDE
    js = np.arange(N_SELBLK)[:, None] * SEL_LEN
    ov = np.clip(np.minimum(cs + CMP_LEN, js + SEL_LEN) - np.maximum(cs, js), 0, None) / CMP_LEN
    ov[:, N_CMP - 1] = 0.0
    return ov.astype(np.float32)


def _expand_table_t():
    out = np.zeros((SEQ, LANES), np.float32)
    out[np.arange(SEQ), np.arange(SEQ) // SEL_LEN] = -NEG
    return out.reshape(N_QBLK // SEL_GROUP, SEL_GROUP * QBLK, LANES)


def kernel(x, norm1_g, w_in, pe_kc, w_kc1, w_kc2, pe_vc, w_vc1, w_vc2, w_o, norm2_g, w_rg, b_rg, w_re, b_re,
           w_gate, w_up, w_down, norm_f_g):
    nb, s, d = x.shape
    assert (s, d) == (SEQ, D_MODEL) and norm1_g.shape[0] == 1
    t = nb * s
    x2 = x.reshape(t, d)

    w_in_p = jnp.pad(w_in[0], ((0, 0), (0, D_IN_PAD - w_in.shape[2]))).astype(BF16)
    rc, rs1, rs2 = _rope_tables(jnp.arange(SEQ))
    cc, cs1, cs2 = _rope_tables(jnp.arange(N_CMP) * CMP_STRIDE + CMP_LEN - 1)
    half_flat = CMP_STRIDE * HEAD_DIM

    def cmp_w1(w1):
        return jnp.concatenate([w1[:half_flat], w1[half_flat:]], axis=1).astype(BF16)

    def cmp_pe(pe):
        rows = pe.reshape(2, half_flat)
        return jnp.concatenate([jnp.broadcast_to(rows[0:1], (16, half_flat)),
                                jnp.broadcast_to(rows[1:2], (16, half_flat))], axis=0).astype(BF16)

    wk2d = jnp.concatenate([w_kc2[0], w_kc2[0]], axis=1).astype(BF16)
    wv2t = w_vc2[0].T.astype(BF16)
    ovl_t = jnp.asarray(_overlap_table_t(), BF16)
    expand_t = jnp.asarray(_expand_table_t(), BF16)

    (aq, ak, av, aq4, ak4, av4, aq16, ak16, av16,
     bq, kcv, ksw, gates) = _proj(x2, norm1_g, w_in_p, rc, rs1, rs2)
    mix_a = _dilated((aq, ak, av), (aq4, ak4, av4), (aq16, ak16, av16))

    kvflat = kcv.reshape(nb, N_CMP, CMP_STRIDE, 2 * N_KV_NSA, HEAD_DIM).transpose(0, 3, 1, 2, 4)
    kvflat = kvflat.reshape(nb, 2 * N_KV_NSA, N_CMP, half_flat)
    mix_b = _nsa(bq, kvflat, ksw, gates, cmp_w1(w_kc1[0]), wk2d, cmp_pe(pe_kc[0]),
                 cmp_w1(w_vc1[0]), wv2t, cmp_pe(pe_vc[0]), cc, cs1, cs2, ovl_t, expand_t)

    n_router = N_GROUPS + N_GROUPS * EXPERTS_PER_GROUP
    wr_t = jnp.pad(jnp.concatenate([w_rg[0].T, w_re[0].reshape(d, -1).T], axis=0),
                   ((0, ROUTER_ROWS - n_router), (0, 0)))
    br = jnp.broadcast_to(jnp.pad(jnp.concatenate([b_rg[0], b_re[0].reshape(-1)]),
                                  (0, ROUTER_ROWS - n_router))[:, None], (ROUTER_ROWS, LANES))
    tri = jnp.asarray(np.triu(np.ones((MOE_TM, MOE_TM), np.float32), 1), BF16)
    x1, hpa, hpb, e1, e2, r1, r2, wtok, cnt = _post(x2, mix_a, mix_b, w_o[0].astype(BF16), norm2_g, wr_t, br, tri)

    pos1, pos2, chunk_expert, chunk_valid = _moe_plan(cnt[:, :, 0], e1[:, 0], e2[:, 0], r1[:, 0], r2[:, 0])
    n_pad = 2 * t + N_GROUPS * EXPERTS_PER_GROUP * FFN_ROWS
    xsa = _sc_scatter_rows(hpa, pos1, pos2, n_pad)
    xsb = _sc_scatter_rows(hpb, pos1, pos2, n_pad)
    ysa, ysb = _ffn(chunk_expert, chunk_valid, xsa, xsb,
                    w_gate[0], w_up[0], w_down[0])
    pos12 = jnp.concatenate([pos1, pos2])
    yga = _sc_gather_rows(ysa, pos12)
    ygb = _sc_gather_rows(ysb, pos12)
    out = _final(x1, yga.reshape(2, t, PACK_W), ygb.reshape(2, t, PACK_W), wtok, norm_f_g.reshape(1, d))
    return out.reshape(nb, s, d)
```

```python
import numpy as np
import jax
import jax.numpy as jnp
from jax import lax
from jax.experimental import pallas as pl
from jax.experimental.pallas import tpu as pltpu
from jax.experimental.pallas import tpu_sc as plsc

F32 = jnp.float32
BF16 = jnp.bfloat16

D_MODEL = 1024
SEQ = 2048
HEAD_DIM = 64
N_HEADS_DIL = 8
DIL_PATTERNS = ((128, 1), (512, 4), (2048, 16))
N_HEADS_NSA = 8
N_KV_NSA = 2
NSA_REP = N_HEADS_NSA // N_KV_NSA
CMP_STRIDE = 16
CMP_LEN = 32
CMP_HIDDEN = 256
SEL_LEN = 64
N_SEL = 8
WIN = 512
ROPE_THETA = 500000.0
ROT_DIM = HEAD_DIM // 4
N_GROUPS = 8
EXPERTS_PER_GROUP = 8
D_FF_EXPERT = 256
EPS = 1e-6
NEG = -1e30
BIG = 1e9
LOWEST = -3.0e38
LOG2_E = 1.4426950408889634

LANES = 128
QBLK = 128
N_QBLK = SEQ // QBLK
N_CMP = SEQ // CMP_STRIDE
N_SELBLK = SEQ // SEL_LEN
D_IN_PAD = 23 * LANES
PROJ_TM = 512
SEL_GROUP = 4
DIL_UNROLL = 8
MOE_TM = 1024
MOE_PARTS = 2
ROUTER_ROWS = 80
FFN_ROWS = 512
FINAL_TM = 512
PACK_W = D_MODEL // 4
SC_WINDOW = 128
VMEM_LIMIT = 56 * 1024 * 1024

_NT = (((1,), (1,)), ((), ()))


def _lane_iota(shape):
    return lax.broadcasted_iota(jnp.int32, shape, 1)


def _row_iota(shape):
    return lax.broadcasted_iota(jnp.int32, shape, 0)


def _rope_lanes(y, c, s1, s2):
    return y * c + pltpu.roll(y, LANES - ROT_DIM // 2, axis=1) * s1 + pltpu.roll(y, ROT_DIM // 2, axis=1) * s2


def _proj_kernel(x_ref, g_ref, w_ref, c_ref, s1_ref, s2_ref,
                 aq_ref, ak_ref, av_ref, aq4_ref, ak4_ref, av4_ref, aq16_ref, ak16_ref, av16_ref,
                 bq_ref, kcv_ref, ksw_ref, gate_ref, nat_f32, d4_f32):
    x = x_ref[...]
    ms = jnp.mean(x * x, axis=-1, keepdims=True)
    h = (x * lax.rsqrt(ms + EPS) * g_ref[...]).astype(BF16)
    c = c_ref[...]
    s1 = s1_ref[...]
    s2 = s2_ref[...]

    def seg(lo, width):
        return jnp.dot(h, w_ref[:, lo:lo + width], preferred_element_type=F32)

    def store(dst, col, y, rope, scale, dilated=None):
        for j in range(y.shape[1] // LANES):
            yj = y[:, j * LANES:(j + 1) * LANES]
            if rope:
                yj = _rope_lanes(yj, c, s1, s2)
            if scale != 1.0:
                yj = yj * scale
            cols = slice(col + j * LANES, col + (j + 1) * LANES)
            dst[:, cols] = yj.astype(dst.dtype)
            if dilated is not None:
                d4_ref, d16_ref = dilated
                quarter = PROJ_TM // 4
                nat_f32[...] = yj
                for r in range(4):
                    part = nat_f32[pl.ds(r, quarter, stride=4), :]
                    d4_ref[r, :, cols] = part.astype(d4_ref.dtype)
                    d4_f32[r * quarter:(r + 1) * quarter, :] = part
                for r in range(4):
                    for a_ in range(4):
                        part = d4_f32[pl.ds(r * quarter + a_, quarter // 4, stride=4), :]
                        d16_ref[4 * a_ + r, :, cols] = part.astype(d16_ref.dtype)

    qscale = HEAD_DIM ** -0.5 * LOG2_E
    store(aq_ref, 0, seg(0, 512), True, qscale, (aq4_ref, aq16_ref))
    store(ak_ref, 0, seg(512, 512), True, 1.0, (ak4_ref, ak16_ref))
    store(av_ref, 0, seg(1024, 512), False, 1.0, (av4_ref, av16_ref))
    store(bq_ref, 0, seg(1536, 512), True, qscale)
    store(kcv_ref, 0, seg(2048, 256), False, 1.0)
    store(ksw_ref, 0, seg(2304, 128), True, 1.0)
    store(ksw_ref, 128, seg(2432, 128), False, 1.0)
    store(ksw_ref, 256, seg(2560, 128), True, 1.0)
    store(ksw_ref, 384, seg(2688, 128), False, 1.0)
    gate_ref[...] = jax.nn.sigmoid(seg(2816, 128))


def _proj(x2, g1, w_in_p, rc, rs1, rs2):
    t = x2.shape[0]
    nb = t // SEQ
    nblk_s = SEQ // PROJ_TM
    row = lambda i: (i, 0)
    pos = lambda i: (i % nblk_s, 0)
    const = lambda i: (0, 0)
    perm = lambda i: (i // nblk_s, 0, i % nblk_s, 0)
    wide = jax.ShapeDtypeStruct((t, 512), BF16)
    wide_spec = pl.BlockSpec((PROJ_TM, 512), row)
    d4 = jax.ShapeDtypeStruct((nb, 4, SEQ // 4, 512), BF16)
    d4_spec = pl.BlockSpec((None, 4, PROJ_TM // 4, 512), perm)
    d16 = jax.ShapeDtypeStruct((nb, 16, SEQ // 16, 512), BF16)
    d16_spec = pl.BlockSpec((None, 16, PROJ_TM // 16, 512), perm)
    outs = [wide] * 3 + [d4] * 3 + [d16] * 3 + [wide, jax.ShapeDtypeStruct((t, 256), BF16), wide,
                                               jax.ShapeDtypeStruct((t, LANES), F32)]
    out_specs = [wide_spec] * 3 + [d4_spec] * 3 + [d16_spec] * 3 + [
        wide_spec, pl.BlockSpec((PROJ_TM, 256), row), wide_spec, pl.BlockSpec((PROJ_TM, LANES), row)]
    return pl.pallas_call(
        _proj_kernel,
        grid=(t // PROJ_TM,),
        in_specs=[pl.BlockSpec((PROJ_TM, D_MODEL), row), pl.BlockSpec((1, D_MODEL), const),
                  pl.BlockSpec((D_MODEL, D_IN_PAD), const),
                  pl.BlockSpec((PROJ_TM, LANES), pos), pl.BlockSpec((PROJ_TM, LANES), pos),
                  pl.BlockSpec((PROJ_TM, LANES), pos)],
        out_specs=out_specs,
        out_shape=outs,
        scratch_shapes=[pltpu.VMEM((PROJ_TM, LANES), F32)] * 2,
        compiler_params=pltpu.CompilerParams(dimension_semantics=("arbitrary",),
                                             vmem_limit_bytes=VMEM_LIMIT),
        name="proj",
    )(x2, g1, w_in_p, rc, rs1, rs2)


def _dilated_kernel(q1_ref, k1_ref, v1_ref, q4_ref, k4_ref, v4_ref, q16_ref, k16_ref, v16_ref, o_ref,
                    qt, vt, op, lp, on0, on1, on2, ln0, ln1, ln2):
    c2 = _row_iota((2 * QBLK, 2 * QBLK))
    a2 = _lane_iota((2 * QBLK, 2 * QBLK)) % QBLK
    band = jnp.where((c2 >= a2) & (c2 <= a2 + QBLK), 0.0, NEG)
    band_noprev = jnp.where(c2 < QBLK, NEG, band)
    c1 = _row_iota((QBLK, 2 * QBLK))
    a1 = _lane_iota((QBLK, 2 * QBLK)) % QBLK
    band_own = jnp.where(c1 <= a1, 0.0, NEG)
    top_sq = _row_iota((LANES, QBLK)) < HEAD_DIM

    inputs = ((q1_ref, k1_ref, v1_ref), (q4_ref, k4_ref, v4_ref), (q16_ref, k16_ref, v16_ref))
    for pi, (window, dil) in enumerate(DIL_PATTERNS):
        seg_len = SEQ // dil
        nseg_blk = seg_len // QBLK
        q_ref, k_ref, v_ref = inputs[pi]

        def block(ref, j, d=None, nseg_blk=nseg_blk, dil=dil):
            if dil == 1:
                return ref[pl.ds(pl.multiple_of(j * QBLK, QBLK), QBLK), :]
            if d is None:
                return ref[j // nseg_blk, (j % nseg_blk) * QBLK:(j % nseg_blk + 1) * QBLK, :]
            within = d % nseg_blk
            return ref[j // nseg_blk, within * QBLK:(within + 1) * QBLK, :]

        for j in range(N_QBLK):
            qt[j] = block(q_ref, j).T if dil > 1 else q_ref[j * QBLK:(j + 1) * QBLK, :].T
            vt[j] = block(v_ref, j).T if dil > 1 else v_ref[j * QBLK:(j + 1) * QBLK, :].T

        def body(u, carry, nseg_blk=nseg_blk, k_ref=k_ref, block=block):
            js = [u * DIL_UNROLL + d for d in range(DIL_UNROLL)]
            prevs, scores = [], []
            for d, j in enumerate(js):
                qtb = qt[j]
                zero = jnp.zeros_like(qtb)
                q_both = jnp.concatenate([jnp.where(top_sq, qtb, zero), jnp.where(top_sq, zero, qtb)], axis=1)
                if nseg_blk == 1:
                    keys, bias, jp = block(k_ref, j, d), band_own, j
                else:
                    if nseg_blk <= DIL_UNROLL:
                        first = d % nseg_blk == 0
                        jp = j if first else j - 1
                        dp = d if first else d - 1
                        bias = band_noprev if first else band
                    else:
                        jp = jnp.maximum(j - 1, 0) if d == 0 else j - 1
                        dp = d - 1
                        bias = jnp.where(u == 0, band_noprev, band) if d == 0 else band
                    keys = jnp.concatenate([block(k_ref, jp, dp), block(k_ref, j, d)], axis=0)
                prevs.append(jp)
                scores.append(jnp.dot(keys, q_both, preferred_element_type=F32) + bias)
            stats = []
            for s in scores:
                m = jnp.max(s, axis=0, keepdims=True)
                p = jnp.exp2(s - m)
                stats.append((m, jnp.sum(p, axis=0, keepdims=True), p.astype(BF16)))
            outs = []
            for j, jp, (m, l, p) in zip(js, prevs, stats):
                v_span = vt[j] if nseg_blk == 1 else jnp.concatenate([vt[jp], vt[j]], axis=1)
                outs.append(jnp.dot(v_span, p, preferred_element_type=F32))
            for j, (m, l, p), res in zip(js, stats, outs):
                row0 = pl.multiple_of(j * QBLK, QBLK)
                lse = m + jnp.log2(l)
                o_t = jnp.where(top_sq, res[:, 0:QBLK] / l[:, 0:QBLK], res[:, QBLK:] / l[:, QBLK:])
                lse_t = jnp.where(top_sq, jnp.broadcast_to(lse[:, 0:QBLK], (LANES, QBLK)),
                                  jnp.broadcast_to(lse[:, QBLK:], (LANES, QBLK)))
                op[pl.ds(row0, QBLK), :] = o_t.T
                lp[pl.ds(row0, QBLK), :] = lse_t.T
            return carry

        lax.fori_loop(0, N_QBLK // DIL_UNROLL, body, 0)

        for r in range(dil):
            dst = pl.ds(r, seg_len, stride=dil) if dil > 1 else pl.ds(0, SEQ)
            src = pl.ds(r * seg_len, seg_len)
            (on0, on1, on2)[pi][dst, :] = op[src, :]
            (ln0, ln1, ln2)[pi][dst, :] = lp[src, :]

    l0, l1, l2 = ln0[...], ln1[...], ln2[...]
    mx = jnp.maximum(jnp.maximum(l0, l1), l2)
    e0, e1, e2 = jnp.exp2(l0 - mx), jnp.exp2(l1 - mx), jnp.exp2(l2 - mx)
    den = e0 + e1 + e2
    out = (e0 / den) * on0[...] + (e1 / den) * on1[...] + (e2 / den) * on2[...]
    o_ref[...] = out.astype(o_ref.dtype)


def _dilated(a1, a4, a16):
    t = a1[0].shape[0]
    nb = t // SEQ
    spec1 = pl.BlockSpec((SEQ, LANES), lambda b, hp: (b, hp))
    spec4 = pl.BlockSpec((None, 4, SEQ // 4, LANES), lambda b, hp: (b, 0, 0, hp))
    spec16 = pl.BlockSpec((None, 16, SEQ // 16, LANES), lambda b, hp: (b, 0, 0, hp))
    return pl.pallas_call(
        _dilated_kernel,
        grid=(nb, N_HEADS_DIL // 2),
        in_specs=[spec1] * 3 + [spec4] * 3 + [spec16] * 3,
        out_specs=spec1,
        out_shape=jax.ShapeDtypeStruct((t, 512), BF16),
        scratch_shapes=[pltpu.VMEM((N_QBLK, LANES, QBLK), BF16), pltpu.VMEM((N_QBLK, LANES, QBLK), BF16)]
        + [pltpu.VMEM((SEQ, LANES), F32)] * 8,
        compiler_params=pltpu.CompilerParams(dimension_semantics=("arbitrary", "arbitrary"),
                                             vmem_limit_bytes=VMEM_LIMIT),
        name="dilated",
    )(*a1, *a4, *a16)


def _gelu_tanh(x):
    return 0.5 * x * (1.0 + jnp.tanh(np.sqrt(2.0 / np.pi).astype(np.float32) * (x + 0.044715 * (x * x * x))))


def _softmax_step(state, s, v_t):
    m, l, acc = state
    mn = jnp.maximum(m, jnp.max(s, axis=0, keepdims=True))
    al = jnp.exp2(m - mn)
    p = jnp.exp2(s - mn)
    l = al * l + jnp.sum(p, axis=0, keepdims=True)
    acc = al * acc + jnp.dot(v_t, p.astype(BF16), preferred_element_type=F32)
    return mn, l, acc


def _nsa_kernel(q_ref, kflat_ref, vflat_ref, ksw_ref, gate_ref,
                wk1_ref, wk2d_ref, pek_ref, wv1_ref, wv2t_ref, pev_ref,
                cc_ref, cs1_ref, cs2_ref, ovl_ref, exp_ref,
                o_ref,
                qt, vst, vwt, gtt, kc2, vct):
    g = pl.program_id(1)
    nq = NSA_REP * QBLK

    for i in range(N_QBLK):
        rows = pl.ds(i * QBLK, QBLK)
        qt[i, 0:LANES, :] = q_ref[rows, 0:LANES].T
        qt[i, LANES:2 * LANES, :] = q_ref[rows, LANES:2 * LANES].T
        vs_t = ksw_ref[rows, LANES:2 * LANES].T
        vst[i] = jnp.where(g == 0, vs_t[0:HEAD_DIM], vs_t[HEAD_DIM:])
        vw_t = ksw_ref[rows, 3 * LANES:4 * LANES].T
        vwt[i] = jnp.where(g == 0, vw_t[0:HEAD_DIM], vw_t[HEAD_DIM:])
        gt = gate_ref[rows, :]
        gt = jnp.where(g == 0, gt, pltpu.roll(gt, LANES - 3 * NSA_REP, axis=1))
        gtt[i] = gt.T[0:16, :]

    def hidden(flat_ref, w1_ref, pe_ref):
        ab = jnp.dot(flat_ref[...], w1_ref[...], preferred_element_type=F32)
        pb = (jnp.dot(pe_ref[0:16, :], w1_ref[:, 0:CMP_HIDDEN], preferred_element_type=F32)
              + jnp.dot(pe_ref[16:32, :], w1_ref[:, CMP_HIDDEN:], preferred_element_type=F32))
        hid = ab[:, 0:CMP_HIDDEN] + pltpu.roll(ab[:, CMP_HIDDEN:], N_CMP - 1, axis=0) + pb[0:1, :]
        return _gelu_tanh(hid).astype(BF16)

    hk = hidden(kflat_ref, wk1_ref, pek_ref)
    kc2[...] = _rope_lanes(jnp.dot(hk, wk2d_ref[...], preferred_element_type=F32),
                           cc_ref[...], cs1_ref[...], cs2_ref[...]).astype(BF16)
    hv = hidden(vflat_ref, wv1_ref, pev_ref)
    vct[...] = lax.dot_general(wv2t_ref[...], hv, _NT, preferred_element_type=F32).astype(BF16)

    key = _row_iota((QBLK, nq))
    qry = _lane_iota((QBLK, nq)) % QBLK
    causal = jnp.where(key <= qry, 0.0, NEG)
    upper = jnp.where(key > qry, 0.0, NEG)
    blk = _row_iota((N_SELBLK, QBLK))
    qry1 = _lane_iota((N_SELBLK, QBLK))
    grp_keys = SEL_GROUP * QBLK
    win_blocks = WIN // QBLK

    def qgroup(w, full_window):
        blocks = [SEL_GROUP * w + d for d in range(SEL_GROUP)]
        row0s = [pl.multiple_of(i * QBLK, QBLK) for i in blocks]

        q_pads = []
        for i in blocks:
            q_t = jnp.concatenate([qt[i, r * HEAD_DIM:(r + 1) * HEAD_DIM, :] for r in range(NSA_REP)], axis=1)
            zero = jnp.zeros_like(q_t)
            q_pads.append(jnp.where(g == 0, jnp.concatenate([q_t, zero], axis=0),
                                    jnp.concatenate([zero, q_t], axis=0)))
        s_cmp = [jnp.dot(kc2[...], qp, preferred_element_type=F32) for qp in q_pads]
        s_win, v_win = [], []
        for d, (i, qp) in enumerate(zip(blocks, q_pads)):
            if full_window:
                first = i - win_blocks
                keys = ksw_ref[pl.ds(pl.multiple_of(first * QBLK, QBLK), (win_blocks + 1) * QBLK), 2 * LANES:3 * LANES]
                s = jnp.dot(keys, qp, preferred_element_type=F32)
                s = jnp.concatenate([s[0:QBLK] + upper, s[QBLK:win_blocks * QBLK], s[win_blocks * QBLK:] + causal],
                                    axis=0)
                v_t = jnp.concatenate([vwt[first + e] for e in range(win_blocks + 1)], axis=1)
            else:
                s = jnp.dot(ksw_ref[0:(d + 1) * QBLK, 2 * LANES:3 * LANES], qp, preferred_element_type=F32)
                s = jnp.concatenate([s[0:d * QBLK], s[d * QBLK:] + causal], axis=0) if d else s + causal
                v_t = jnp.concatenate([vwt[e] for e in range(d + 1)], axis=1) if d else vwt[0]
            s_win.append(s)
            v_win.append(v_t)

        pcs = []
        for row0, s in zip(row0s, s_cmp):
            cvalid = (CMP_STRIDE * key + (CMP_LEN - 1) <= row0 + qry) & (key < N_CMP - 1)
            s = jnp.where(cvalid, s, NEG)
            m = jnp.max(s, axis=0, keepdims=True)
            p = jnp.where(cvalid, jnp.exp2(s - m), 0.0)
            l = jnp.sum(p, axis=0, keepdims=True)
            pcs.append(p / jnp.where(l > 0, l, 1.0))
        o_cmp = [jnp.dot(vct[...], pc.astype(BF16), preferred_element_type=F32) for pc in pcs]

        ovl = ovl_ref[...]
        imps = []
        for pc in pcs:
            psum = pc[:, 0:QBLK] + pc[:, QBLK:2 * QBLK] + pc[:, 2 * QBLK:3 * QBLK] + pc[:, 3 * QBLK:]
            p_hi = psum.astype(BF16)
            p_lo = (psum - p_hi.astype(F32)).astype(BF16)
            imps.append(jnp.dot(ovl, p_hi, preferred_element_type=F32) + jnp.dot(ovl, p_lo, preferred_element_type=F32))

        o_win = []
        for s, v_t in zip(s_win, v_win):
            m = jnp.max(s, axis=0, keepdims=True)
            p = jnp.exp2(s - m)
            l = jnp.sum(p, axis=0, keepdims=True)
            o_win.append(jnp.dot(v_t, p.astype(BF16), preferred_element_type=F32) / l)

        q_sel = []
        for row0, imp, qp in zip(row0s, imps, q_pads):
            t1 = row0 + qry1
            blk_t = t1 // SEL_LEN
            forced = (blk == 0) | (blk == blk_t) | (blk == blk_t - 1)
            v = jnp.where(blk * SEL_LEN <= t1, jnp.where(forced, BIG, imp), -BIG)
            selm = jnp.zeros((N_SELBLK, QBLK), F32)
            for _ in range(N_SEL):
                mx = jnp.max(v, axis=0, keepdims=True)
                first_hit = jnp.min(jnp.where(v == mx, blk, N_SELBLK), axis=0, keepdims=True)
                hit = blk == first_hit
                selm = jnp.where(hit, 1.0, selm)
                v = jnp.where(hit, LOWEST, v)
            notsel = jnp.concatenate([selm - 1.0, jnp.zeros((LANES - N_SELBLK, QBLK), F32)], axis=0).astype(BF16)
            q_sel.append(jnp.concatenate([qp, jnp.concatenate([notsel] * NSA_REP, axis=1)], axis=0))

        def sel_keys(t, n_rows):
            rows_t = pl.ds(pl.multiple_of(t * grp_keys, grp_keys), n_rows)
            return jnp.concatenate([ksw_ref[rows_t, 0:LANES], exp_ref[t, 0:n_rows, :]], axis=1)

        def sel_values(t, n_blk):
            return jnp.concatenate([vst[t * SEL_GROUP + e] for e in range(n_blk)], axis=1) if n_blk > 1 else vst[t * SEL_GROUP]

        wide = SEL_GROUP * nq
        st = (jnp.full((1, wide), LOWEST, F32), jnp.zeros((1, wide), F32), jnp.zeros((HEAD_DIM, wide), F32))
        if full_window:
            q_all = jnp.concatenate(q_sel, axis=1)

            def sel_body(t, st):
                s = jnp.dot(sel_keys(t, grp_keys), q_all, preferred_element_type=F32)
                return _softmax_step(st, s, sel_values(t, SEL_GROUP))

            st = lax.fori_loop(0, w, sel_body, st)
        o_sel = []
        for d, qs in enumerate(q_sel):
            s = jnp.dot(sel_keys(w, (d + 1) * QBLK), qs, preferred_element_type=F32)
            s = jnp.concatenate([s[0:d * QBLK], s[d * QBLK:] + causal], axis=0) if d else s + causal
            lanes = slice(d * nq, (d + 1) * nq)
            m, l, acc = _softmax_step((st[0][:, lanes], st[1][:, lanes], st[2][:, lanes]), s, sel_values(w, d + 1))
            o_sel.append(acc / l)

        for n, (i, row0) in enumerate(zip(blocks, row0s)):
            gti = gtt[i]

            def gate(br):
                return jnp.concatenate([gti[r * 3 + br:r * 3 + br + 1, :] for r in range(NSA_REP)], axis=1)

            out_t = gate(0) * o_cmp[n] + gate(1) * o_sel[n] + gate(2) * o_win[n]
            for pair in range(NSA_REP // 2):
                both = jnp.concatenate([out_t[:, (2 * pair) * QBLK:(2 * pair + 1) * QBLK],
                                        out_t[:, (2 * pair + 1) * QBLK:(2 * pair + 2) * QBLK]], axis=0)
                o_ref[pl.ds(row0, QBLK), pair * LANES:(pair + 1) * LANES] = both.T.astype(o_ref.dtype)

    def tail_groups(w, carry):
        qgroup(w, True)
        return carry

    qgroup(0, False)
    lax.fori_loop(1, N_QBLK // SEL_GROUP, tail_groups, 0)


def _nsa(bq, kvflat, ksw, gates, wk1, wk2d, pek, wv1, wv2t, pev, cc, cs1, cs2, ovl_t, expand_t):
    t = bq.shape[0]
    nb = t // SEQ
    const2 = lambda b, g: (0, 0)
    return pl.pallas_call(
        _nsa_kernel,
        grid=(nb, N_KV_NSA),
        in_specs=[
            pl.BlockSpec((SEQ, 2 * LANES), lambda b, g: (b, g)),
            pl.BlockSpec((None, None, N_CMP, CMP_STRIDE * HEAD_DIM), lambda b, g: (b, g, 0, 0)),
            pl.BlockSpec((None, None, N_CMP, CMP_STRIDE * HEAD_DIM), lambda b, g: (b, N_KV_NSA + g, 0, 0)),
            pl.BlockSpec((SEQ, 4 * LANES), lambda b, g: (b, 0)),
            pl.BlockSpec((SEQ, LANES), lambda b, g: (b, 0)),
            pl.BlockSpec((CMP_STRIDE * HEAD_DIM, 2 * CMP_HIDDEN), const2),
            pl.BlockSpec((CMP_HIDDEN, LANES), const2),
            pl.BlockSpec((32, CMP_STRIDE * HEAD_DIM), const2),
            pl.BlockSpec((CMP_STRIDE * HEAD_DIM, 2 * CMP_HIDDEN), const2),
            pl.BlockSpec((HEAD_DIM, CMP_HIDDEN), const2),
            pl.BlockSpec((32, CMP_STRIDE * HEAD_DIM), const2),
            pl.BlockSpec((N_CMP, LANES), const2),
            pl.BlockSpec((N_CMP, LANES), const2),
            pl.BlockSpec((N_CMP, LANES), const2),
            pl.BlockSpec((N_SELBLK, LANES), const2),
            pl.BlockSpec((N_QBLK // SEL_GROUP, SEL_GROUP * QBLK, LANES), lambda b, g: (0, 0, 0)),
        ],
        out_specs=pl.BlockSpec((SEQ, 2 * LANES), lambda b, g: (b, g)),
        out_shape=jax.ShapeDtypeStruct((t, 512), BF16),
        scratch_shapes=[pltpu.VMEM((N_QBLK, 2 * LANES, QBLK), BF16),
                        pltpu.VMEM((N_QBLK, HEAD_DIM, QBLK), BF16),
                        pltpu.VMEM((N_QBLK, HEAD_DIM, QBLK), BF16),
                        pltpu.VMEM((N_QBLK, 16, QBLK), F32),
                        pltpu.VMEM((N_CMP, LANES), BF16),
                        pltpu.VMEM((HEAD_DIM, N_CMP), BF16)],
        compiler_params=pltpu.CompilerParams(dimension_semantics=("arbitrary", "arbitrary"),
                                             vmem_limit_bytes=VMEM_LIMIT),
        name="nsa",
    )(bq, kvflat, kvflat, ksw, gates, wk1, wk2d, pek, wv1, wv2t, pev, cc, cs1, cs2, ovl_t, expand_t)


def _pack_bf16_pairs(lo, hi):
    lo_b = lax.bitcast_convert_type(lo.astype(BF16).astype(F32), jnp.uint32)
    hi_b = lax.bitcast_convert_type(hi.astype(BF16).astype(F32), jnp.uint32)
    return (hi_b & jnp.uint32(0xFFFF0000)) | (lo_b >> 16)


def _unpack_bf16_pairs(p):
    lo = lax.bitcast_convert_type(p << 16, F32)
    hi = lax.bitcast_convert_type(p & jnp.uint32(0xFFFF0000), F32)
    return lo, hi


def _post_kernel(x_ref, ma_ref, mb_ref, wo_ref, g2_ref, wr_ref, br_ref, tri_ref,
                 x1_ref, ha_ref, hb_ref, e1_ref, e2_ref, r1_ref, r2_ref, w_ref, cnt_ref):
    x1 = (x_ref[...] + jnp.dot(ma_ref[...], wo_ref[0:512, :], preferred_element_type=F32)
          + jnp.dot(mb_ref[...], wo_ref[512:1024, :], preferred_element_type=F32))
    x1_ref[...] = x1
    ms = jnp.mean(x1 * x1, axis=-1, keepdims=True)
    h2 = x1 * lax.rsqrt(ms + EPS) * g2_ref[...]
    ha_ref[...] = _pack_bf16_pairs(h2[:, 0:PACK_W], h2[:, 2 * PACK_W:3 * PACK_W])
    hb_ref[...] = _pack_bf16_pairs(h2[:, PACK_W:2 * PACK_W], h2[:, 3 * PACK_W:])

    h_hi = h2.astype(BF16)
    h_lo = (h2 - h_hi.astype(F32)).astype(BF16)
    wr = wr_ref[...]
    w_hi = wr.astype(BF16)
    w_lo = (wr - w_hi.astype(F32)).astype(BF16)
    lg = (lax.dot_general(w_hi, h_hi, _NT, preferred_element_type=F32)
          + lax.dot_general(w_hi, h_lo, _NT, preferred_element_type=F32)
          + lax.dot_general(w_lo, h_hi, _NT, preferred_element_type=F32)) + br_ref[:, 0:1]
    gl = lg[0:N_GROUPS, :]
    sub = _row_iota(gl.shape)
    gmax = jnp.max(gl, axis=0, keepdims=True)
    p_g = 1.0 / jnp.sum(jnp.exp(gl - gmax), axis=0, keepdims=True)
    g_star = jnp.min(jnp.where(gl == gmax, sub, N_GROUPS), axis=0, keepdims=True)
    el = jnp.zeros_like(gl)
    for gi in range(N_GROUPS):
        lo = N_GROUPS + gi * EXPERTS_PER_GROUP
        el = el + jnp.where(g_star == gi, lg[lo:lo + EXPERTS_PER_GROUP, :], 0.0)
    emax = jnp.max(el, axis=0, keepdims=True)
    ee = jnp.exp(el - emax)
    pe = ee / jnp.sum(ee, axis=0, keepdims=True)
    v1 = jnp.max(pe, axis=0, keepdims=True)
    i1 = jnp.min(jnp.where(pe == v1, sub, EXPERTS_PER_GROUP), axis=0, keepdims=True)
    pe2 = jnp.where(sub == i1, -1.0, pe)
    v2 = jnp.max(pe2, axis=0, keepdims=True)
    i2 = jnp.min(jnp.where(pe2 == v2, sub, EXPERTS_PER_GROUP), axis=0, keepdims=True)
    tot = v1 + v2
    e1 = g_star * EXPERTS_PER_GROUP + i1
    e2 = g_star * EXPERTS_PER_GROUP + i2
    e1_ref[0] = e1
    e2_ref[0] = e2

    wslab = jnp.concatenate([v1 / tot * p_g, v2 / tot * p_g, jnp.zeros((LANES - 2, MOE_TM), F32)], axis=0)
    for j in range(MOE_TM // LANES):
        w_ref[j * LANES:(j + 1) * LANES, :] = wslab[:, j * LANES:(j + 1) * LANES].T

    n_exp = N_GROUPS * EXPERTS_PER_GROUP
    sub_e = _row_iota((n_exp, MOE_TM))
    hit1 = sub_e == e1
    hit2 = sub_e == e2
    assigned = jnp.where(hit1, 1.0, 0.0) + jnp.where(hit2, 1.0, 0.0)
    before = jnp.dot(assigned.astype(BF16), tri_ref[...], preferred_element_type=F32)
    r1_ref[0] = jnp.sum(jnp.where(hit1, before, 0.0), axis=0, keepdims=True).astype(jnp.int32)
    r2_ref[0] = jnp.sum(jnp.where(hit2, before, 0.0), axis=0, keepdims=True).astype(jnp.int32)
    cnt = jnp.sum(assigned, axis=1, keepdims=True).astype(jnp.int32)
    cnt_ref[0] = jnp.broadcast_to(cnt, (n_exp, LANES))


def _post(x2, mix_a, mix_b, w_o, g2, wr_t, br, tri, part):
    t = x2.shape[0] // MOE_PARTS
    nt = t // MOE_TM
    first = part * nt
    n_exp = N_GROUPS * EXPERTS_PER_GROUP
    row = lambda i: (i, 0)
    const = lambda i: (0, 0)
    tok = lambda i: (i, 0, 0)
    src = lambda i: (first + i, 0)
    tok_spec = pl.BlockSpec((1, 1, MOE_TM), tok)
    tok_shape = jax.ShapeDtypeStruct((nt, 1, MOE_TM), jnp.int32)
    return pl.pallas_call(
        _post_kernel,
        grid=(nt,),
        in_specs=[pl.BlockSpec((MOE_TM, D_MODEL), src), pl.BlockSpec((MOE_TM, 512), src),
                  pl.BlockSpec((MOE_TM, 512), src), pl.BlockSpec((D_MODEL, D_MODEL), const),
                  pl.BlockSpec((1, D_MODEL), const), pl.BlockSpec((ROUTER_ROWS, D_MODEL), const),
                  pl.BlockSpec((ROUTER_ROWS, LANES), const), pl.BlockSpec((MOE_TM, MOE_TM), const)],
        out_specs=[pl.BlockSpec((MOE_TM, D_MODEL), row),
                   pl.BlockSpec((MOE_TM, PACK_W), row), pl.BlockSpec((MOE_TM, PACK_W), row),
                   tok_spec, tok_spec, tok_spec, tok_spec,
                   pl.BlockSpec((MOE_TM, LANES), row),
                   pl.BlockSpec((1, n_exp, LANES), tok)],
        out_shape=[jax.ShapeDtypeStruct((t, D_MODEL), F32),
                   jax.ShapeDtypeStruct((t, PACK_W), jnp.uint32), jax.ShapeDtypeStruct((t, PACK_W), jnp.uint32),
                   tok_shape, tok_shape, tok_shape, tok_shape,
                   jax.ShapeDtypeStruct((t, LANES), F32),
                   jax.ShapeDtypeStruct((nt, n_exp, LANES), jnp.int32)],
        compiler_params=pltpu.CompilerParams(dimension_semantics=("arbitrary",),
                                             vmem_limit_bytes=VMEM_LIMIT),
        name="post",
    )(x2, mix_a, mix_b, w_o, g2, wr_t, br, tri)


def _sc_mesh():
    return plsc.VectorSubcoreMesh(core_axis_name="core", subcore_axis_name="subcore")


def _sc_scatter_rows(x, pos1, pos2, n_out):
    r, c = x.shape

    @pl.kernel(out_type=jax.ShapeDtypeStruct((n_out, c), x.dtype), mesh=_sc_mesh(), scratch_types=[])
    def scatter_kernel(x_hbm, p1_hbm, p2_hbm, o_hbm):
        def body(x_vmem, p1_vmem, p2_vmem):
            pltpu.sync_copy(x_vmem, o_hbm.at[p1_vmem.at[0]])
            pltpu.sync_copy(x_vmem, o_hbm.at[p2_vmem.at[0]])

        idx = pl.BlockSpec((1, SC_WINDOW), lambda i: (0, i))
        pltpu.emit_pipeline(body, grid=(r // SC_WINDOW,),
                            in_specs=[pl.BlockSpec((SC_WINDOW, c), lambda i: (i, 0)), idx, idx], out_specs=[],
                            core_axis_name=("core", "subcore"),
                            dimension_semantics=(pltpu.PARALLEL,))(x_hbm, p1_hbm, p2_hbm)

    return scatter_kernel(x, pos1.reshape(1, r), pos2.reshape(1, r))


def _sc_gather_rows(y, idx):
    n = idx.shape[0]
    c = y.shape[1]

    @pl.kernel(out_type=jax.ShapeDtypeStruct((n, c), y.dtype), mesh=_sc_mesh(), scratch_types=[])
    def gather_kernel(y_hbm, i_hbm, o_hbm):
        def body(i_vmem, o_vmem):
            pltpu.sync_copy(y_hbm.at[i_vmem.at[0]], o_vmem)

        pltpu.emit_pipeline(body, grid=(n // SC_WINDOW,),
                            in_specs=[pl.BlockSpec((1, SC_WINDOW), lambda i: (0, i))],
                            out_specs=[pl.BlockSpec((SC_WINDOW, c), lambda i: (i, 0))],
                            core_axis_name=("core", "subcore"),
                            dimension_semantics=(pltpu.PARALLEL,))(i_hbm, o_hbm)

    return gather_kernel(y, idx.reshape(1, n))


def _ffn_kernel(ce_ref, cv_ref, xa_ref, xb_ref, wg_ref, wu_ref, wd_ref, ya_ref, yb_ref, wg_bf, wu_bf, wd_bf):
    c = pl.program_id(0)
    prev = ce_ref[jnp.maximum(c - 1, 0)]

    @pl.when((c == 0) | (ce_ref[c] != prev))
    def _():
        wg_bf[...] = wg_ref[0].astype(BF16)
        wu_bf[...] = wu_ref[0].astype(BF16)
        wd_bf[...] = wd_ref[0].astype(BF16)

    @pl.when(cv_ref[c] == 0)
    def _():
        ya_ref[...] = jnp.zeros_like(ya_ref)
        yb_ref[...] = jnp.zeros_like(yb_ref)

    @pl.when(cv_ref[c] != 0)
    def _():
        a_lo, a_hi = _unpack_bf16_pairs(xa_ref[...])
        b_lo, b_hi = _unpack_bf16_pairs(xb_ref[...])
        xs = jnp.concatenate([a_lo, b_lo, a_hi, b_hi], axis=1).astype(BF16)
        gate = jnp.dot(xs, wg_bf[...], preferred_element_type=F32)
        up = jnp.dot(xs, wu_bf[...], preferred_element_type=F32)
        he = (gate * jax.nn.sigmoid(gate) * up).astype(BF16)
        y = jnp.dot(he, wd_bf[...], preferred_element_type=F32)
        ya_ref[...] = _pack_bf16_pairs(y[:, 0:PACK_W], y[:, 2 * PACK_W:3 * PACK_W])
        yb_ref[...] = _pack_bf16_pairs(y[:, PACK_W:2 * PACK_W], y[:, 3 * PACK_W:])


def _ffn(chunk_expert, chunk_valid, xsa, xsb, wg, wu, wd):
    n_pad = xsa.shape[0]
    rows = pl.BlockSpec((FFN_ROWS, PACK_W), lambda c, ce, cv: (c, 0))
    grid_spec = pltpu.PrefetchScalarGridSpec(
        num_scalar_prefetch=2,
        grid=(n_pad // FFN_ROWS,),
        in_specs=[rows, rows,
                  pl.BlockSpec((1, D_MODEL, D_FF_EXPERT), lambda c, ce, cv: (ce[c], 0, 0)),
                  pl.BlockSpec((1, D_MODEL, D_FF_EXPERT), lambda c, ce, cv: (ce[c], 0, 0)),
                  pl.BlockSpec((1, D_FF_EXPERT, D_MODEL), lambda c, ce, cv: (ce[c], 0, 0))],
        out_specs=[rows, rows],
        scratch_shapes=[pltpu.VMEM((D_MODEL, D_FF_EXPERT), BF16), pltpu.VMEM((D_MODEL, D_FF_EXPERT), BF16),
                        pltpu.VMEM((D_FF_EXPERT, D_MODEL), BF16)],
    )
    out = jax.ShapeDtypeStruct((n_pad, PACK_W), jnp.uint32)
    return pl.pallas_call(
        _ffn_kernel,
        grid_spec=grid_spec,
        out_shape=[out, out],
        compiler_params=pltpu.CompilerParams(dimension_semantics=("arbitrary",),
                                             vmem_limit_bytes=VMEM_LIMIT),
        name="ffn",
    )(chunk_expert, chunk_valid, xsa, xsb, wg, wu, wd)


def _final_kernel(x1_ref, y1a_ref, y1b_ref, y2a_ref, y2b_ref, w_ref, gf_ref, o_ref):
    w1 = w_ref[:, 0:1]
    w2 = w_ref[:, 1:2]
    a1_lo, a1_hi = _unpack_bf16_pairs(y1a_ref[...])
    b1_lo, b1_hi = _unpack_bf16_pairs(y1b_ref[...])
    a2_lo, a2_hi = _unpack_bf16_pairs(y2a_ref[...])
    b2_lo, b2_hi = _unpack_bf16_pairs(y2b_ref[...])
    y = jnp.concatenate([w1 * a1_lo + w2 * a2_lo, w1 * b1_lo + w2 * b2_lo,
                         w1 * a1_hi + w2 * a2_hi, w1 * b1_hi + w2 * b2_hi], axis=1)
    xo = x1_ref[...] + y
    ms = jnp.mean(xo * xo, axis=-1, keepdims=True)
    o_ref[...] = xo * lax.rsqrt(ms + EPS) * gf_ref[...]


def _final(x1, yga, ygb, wtok, gf, out_prev, part):
    t = x1.shape[0]
    first = part * (t // FINAL_TM)
    row = lambda i: (i, 0)
    one = pl.BlockSpec((None, FINAL_TM, PACK_W), lambda i: (0, i, 0))
    two = pl.BlockSpec((None, FINAL_TM, PACK_W), lambda i: (1, i, 0))
    in_specs = [pl.BlockSpec((FINAL_TM, D_MODEL), row), one, one, two, two,
                pl.BlockSpec((FINAL_TM, LANES), row), pl.BlockSpec((1, D_MODEL), lambda i: (0, 0))]
    args = [x1, yga, ygb, yga, ygb, wtok, gf]
    kern = _final_kernel
    aliases = {}
    if out_prev is not None:
        in_specs.append(pl.BlockSpec(memory_space=pl.ANY))
        args.append(out_prev)
        aliases = {len(args) - 1: 0}
        kern = lambda *refs: _final_kernel(*refs[:7], refs[8])
    return pl.pallas_call(
        kern,
        grid=(t // FINAL_TM,),
        in_specs=in_specs,
        out_specs=pl.BlockSpec((FINAL_TM, D_MODEL), lambda i: (first + i, 0)),
        out_shape=jax.ShapeDtypeStruct((t * MOE_PARTS, D_MODEL), F32),
        input_output_aliases=aliases,
        compiler_params=pltpu.CompilerParams(dimension_semantics=("arbitrary",),
                                             vmem_limit_bytes=VMEM_LIMIT),
        name="final",
    )(*args)


def _moe_plan(cnt, e1, e2, r1, r2):
    nt, n_exp = cnt.shape
    tot = jnp.sum(cnt, axis=0)
    padded = (tot + FFN_ROWS - 1) // FFN_ROWS * FFN_ROWS
    seg_end = jnp.cumsum(padded)
    seg_start = seg_end - padded
    off = seg_start[None, :] + jnp.cumsum(cnt, axis=0) - cnt
    experts = jnp.arange(n_exp, dtype=jnp.int32)

    def lookup(e):
        return jnp.sum(jnp.where(e[:, :, None] == experts, off[:, None, :], 0), axis=2)

    pos1 = (lookup(e1) + r1).reshape(-1).astype(jnp.int32)
    pos2 = (lookup(e2) + r2).reshape(-1).astype(jnp.int32)
    n_chunks = (nt * MOE_TM * 2 + n_exp * FFN_ROWS) // FFN_ROWS
    cstart = jnp.arange(n_chunks, dtype=jnp.int32) * FFN_ROWS
    ce = jnp.sum((seg_end[None, :] <= cstart[:, None]).astype(jnp.int32), axis=1)
    cec = jnp.minimum(ce, n_exp - 1)
    seg_used = seg_start + tot
    valid = jnp.any((ce[:, None] == experts[None, :]) & (cstart[:, None] < seg_used[None, :]), axis=1)
    return pos1, pos2, cec.astype(jnp.int32), valid.astype(jnp.int32)


def _rope_tables(pos):
    half = ROT_DIM // 2
    inv_freq = ROPE_THETA ** (-jnp.arange(0, ROT_DIM, 2, dtype=F32) / ROT_DIM)
    ang = pos.astype(F32)[:, None] * inv_freq[None, :]
    cos, sin = jnp.cos(ang), jnp.sin(ang)
    l64 = np.arange(LANES) % HEAD_DIM
    f = l64 % half
    first = jnp.asarray(l64 < half)[None, :]
    second = jnp.asarray((l64 >= half) & (l64 < ROT_DIM))[None, :]
    c = jnp.where(first | second, cos[:, f], 1.0)
    s1 = jnp.where(first, -sin[:, f], 0.0)
    s2 = jnp.where(second, sin[:, f], 0.0)
    return c.astype(F32), s1.astype(F32), s2.astype(F32)


def _overlap_table_t():
    cs = np.arange(N_CMP)[None, :] * CMP_STRIDE
    js = np.arange(N_SELBLK)[:, None] * SEL_LEN
    ov = np.clip(np.minimum(cs + CMP_LEN, js + SEL_LEN) - np.maximum(cs, js), 0, None) / CMP_LEN
    ov[:, N_CMP - 1] = 0.0
    return ov.astype(np.float32)


def _expand_table_t():
    out = np.zeros((SEQ, LANES), np.float32)
    out[np.arange(SEQ), np.arange(SEQ) // SEL_LEN] = -NEG
    return out.reshape(N_QBLK // SEL_GROUP, SEL_GROUP * QBLK, LANES)


def kernel(x, norm1_g, w_in, pe_kc, w_kc1, w_kc2, pe_vc, w_vc1, w_vc2, w_o, norm2_g, w_rg, b_rg, w_re, b_re,
           w_gate, w_up, w_down, norm_f_g):
    nb, s, d = x.shape
    assert (s, d) == (SEQ, D_MODEL) and norm1_g.shape[0] == 1
    t = nb * s
    x2 = x.reshape(t, d)

    w_in_p = jnp.pad(w_in[0], ((0, 0), (0, D_IN_PAD - w_in.shape[2]))).astype(BF16)
    rc, rs1, rs2 = _rope_tables(jnp.arange(SEQ))
    cc, cs1, cs2 = _rope_tables(jnp.arange(N_CMP) * CMP_STRIDE + CMP_LEN - 1)
    half_flat = CMP_STRIDE * HEAD_DIM

    def cmp_w1(w1):
        return jnp.concatenate([w1[:half_flat], w1[half_flat:]], axis=1).astype(BF16)

    def cmp_pe(pe):
        rows = pe.reshape(2, half_flat)
        return jnp.concatenate([jnp.broadcast_to(rows[0:1], (16, half_flat)),
                                jnp.broadcast_to(rows[1:2], (16, half_flat))], axis=0).astype(BF16)

    wk2d = jnp.concatenate([w_kc2[0], w_kc2[0]], axis=1).astype(BF16)
    wv2t = w_vc2[0].T.astype(BF16)
    ovl_t = jnp.asarray(_overlap_table_t(), BF16)
    expand_t = jnp.asarray(_expand_table_t(), BF16)

    (aq, ak, av, aq4, ak4, av4, aq16, ak16, av16,
     bq, kcv, ksw, gates) = _proj(x2, norm1_g, w_in_p, rc, rs1, rs2)
    mix_a = _dilated((aq, ak, av), (aq4, ak4, av4), (aq16, ak16, av16))

    kvflat = kcv.reshape(nb, N_CMP, CMP_STRIDE, 2 * N_KV_NSA, HEAD_DIM).transpose(0, 3, 1, 2, 4)
    kvflat = kvflat.reshape(nb, 2 * N_KV_NSA, N_CMP, half_flat)
    mix_b = _nsa(bq, kvflat, ksw, gates, cmp_w1(w_kc1[0]), wk2d, cmp_pe(pe_kc[0]),
                 cmp_w1(w_vc1[0]), wv2t, cmp_pe(pe_vc[0]), cc, cs1, cs2, ovl_t, expand_t)

    n_router = N_GROUPS + N_GROUPS * EXPERTS_PER_GROUP
    wr_t = jnp.pad(jnp.concatenate([w_rg[0].T, w_re[0].reshape(d, -1).T], axis=0),
                   ((0, ROUTER_ROWS - n_router), (0, 0)))
    br = jnp.broadcast_to(jnp.pad(jnp.concatenate([b_rg[0], b_re[0].reshape(-1)]),
                                  (0, ROUTER_ROWS - n_router))[:, None], (ROUTER_ROWS, LANES))
    tri = jnp.asarray(np.triu(np.ones((MOE_TM, MOE_TM), np.float32), 1), BF16)
    tp = t // MOE_PARTS
    n_pad = 2 * tp + N_GROUPS * EXPERTS_PER_GROUP * FFN_ROWS
    w_o_bf = w_o[0].astype(BF16)
    gf = norm_f_g.reshape(1, d)
    routed = []
    for part in range(MOE_PARTS):
        x1, hpa, hpb, e1, e2, r1, r2, wtok, cnt = _post(x2, mix_a, mix_b, w_o_bf, norm2_g, wr_t, br, tri, part)
        pos1, pos2, chunk_expert, chunk_valid = _moe_plan(cnt[:, :, 0], e1[:, 0], e2[:, 0], r1[:, 0], r2[:, 0])
        xsa = _sc_scatter_rows(hpa, pos1, pos2, n_pad)
        xsb = _sc_scatter_rows(hpb, pos1, pos2, n_pad)
        routed.append((x1, wtok, pos1, pos2, chunk_expert, chunk_valid, xsa, xsb))
    gathered = []
    for x1, wtok, pos1, pos2, chunk_expert, chunk_valid, xsa, xsb in routed:
        ysa, ysb = _ffn(chunk_expert, chunk_valid, xsa, xsb, w_gate[0], w_up[0], w_down[0])
        pos12 = jnp.concatenate([pos1, pos2])
        gathered.append((x1, wtok, _sc_gather_rows(ysa, pos12), _sc_gather_rows(ysb, pos12)))
    out = None
    for part, (x1, wtok, yga, ygb) in enumerate(gathered):
        out = _final(x1, yga.reshape(2, tp, PACK_W), ygb.reshape(2, tp, PACK_W), wtok, gf, out, part)
    return out.reshape(nb, s, d)
```

```python
import numpy as np
import jax
import jax.numpy as jnp
from jax import lax
from jax.experimental import pallas as pl
from jax.experimental.pallas import tpu as pltpu
from jax.experimental.pallas import tpu_sc as plsc

F32 = jnp.float32
BF16 = jnp.bfloat16

D_MODEL = 1024
SEQ = 2048
HEAD_DIM = 64
N_HEADS_DIL = 8
DIL_PATTERNS = ((128, 1), (512, 4), (2048, 16))
N_HEADS_NSA = 8
N_KV_NSA = 2
NSA_REP = N_HEADS_NSA // N_KV_NSA
CMP_STRIDE = 16
CMP_LEN = 32
CMP_HIDDEN = 256
SEL_LEN = 64
N_SEL = 8
WIN = 512
ROPE_THETA = 500000.0
ROT_DIM = HEAD_DIM // 4
N_GROUPS = 8
EXPERTS_PER_GROUP = 8
D_FF_EXPERT = 256
EPS = 1e-6
NEG = -1e30
BIG = 1e9
LOWEST = -3.0e38
LOG2_E = 1.4426950408889634

LANES = 128
QBLK = 128
N_QBLK = SEQ // QBLK
N_CMP = SEQ // CMP_STRIDE
N_SELBLK = SEQ // SEL_LEN
D_IN_PAD = 23 * LANES
PROJ_TM = 512
SEL_GROUP = 4
DIL_UNROLL = 8
MOE_TM = 1024
MOE_PARTS = 1
ROUTER_ROWS = 80
FFN_ROWS = 512
FINAL_TM = 512
PACK_W = D_MODEL // 4
SC_WINDOW = 128
VMEM_LIMIT = 56 * 1024 * 1024

_NT = (((1,), (1,)), ((), ()))


def _lane_iota(shape):
    return lax.broadcasted_iota(jnp.int32, shape, 1)


def _row_iota(shape):
    return lax.broadcasted_iota(jnp.int32, shape, 0)


def _rope_lanes(y, c, s1, s2):
    return y * c + pltpu.roll(y, LANES - ROT_DIM // 2, axis=1) * s1 + pltpu.roll(y, ROT_DIM // 2, axis=1) * s2


def _proj_kernel(x_ref, g_ref, w_ref, c_ref, s1_ref, s2_ref,
                 aq_ref, ak_ref, av_ref, aq4_ref, ak4_ref, av4_ref, aq16_ref, ak16_ref, av16_ref,
                 bq_ref, kcv_ref, ksw_ref, gate_ref, nat_f32, d4_f32):
    x = x_ref[...]
    ms = jnp.mean(x * x, axis=-1, keepdims=True)
    h = (x * lax.rsqrt(ms + EPS) * g_ref[...]).astype(BF16)
    c = c_ref[...]
    s1 = s1_ref[...]
    s2 = s2_ref[...]

    def seg(lo, width):
        return jnp.dot(h, w_ref[:, lo:lo + width], preferred_element_type=F32)

    def store(dst, col, y, rope, scale, dilated=None):
        for j in range(y.shape[1] // LANES):
            yj = y[:, j * LANES:(j + 1) * LANES]
            if rope:
                yj = _rope_lanes(yj, c, s1, s2)
            if scale != 1.0:
                yj = yj * scale
            cols = slice(col + j * LANES, col + (j + 1) * LANES)
            dst[:, cols] = yj.astype(dst.dtype)
            if dilated is not None:
                d4_ref, d16_ref = dilated
                quarter = PROJ_TM // 4
                nat_f32[...] = yj
                for r in range(4):
                    part = nat_f32[pl.ds(r, quarter, stride=4), :]
                    d4_ref[r, :, cols] = part.astype(d4_ref.dtype)
                    d4_f32[r * quarter:(r + 1) * quarter, :] = part
                for r in range(4):
                    for a_ in range(4):
                        part = d4_f32[pl.ds(r * quarter + a_, quarter // 4, stride=4), :]
                        d16_ref[4 * a_ + r, :, cols] = part.astype(d16_ref.dtype)

    qscale = HEAD_DIM ** -0.5 * LOG2_E
    store(aq_ref, 0, seg(0, 512), True, qscale, (aq4_ref, aq16_ref))
    store(ak_ref, 0, seg(512, 512), True, 1.0, (ak4_ref, ak16_ref))
    store(av_ref, 0, seg(1024, 512), False, 1.0, (av4_ref, av16_ref))
    store(bq_ref, 0, seg(1536, 512), True, qscale)
    store(kcv_ref, 0, seg(2048, 256), False, 1.0)
    store(ksw_ref, 0, seg(2304, 128), True, 1.0)
    store(ksw_ref, 128, seg(2432, 128), False, 1.0)
    store(ksw_ref, 256, seg(2560, 128), True, 1.0)
    store(ksw_ref, 384, seg(2688, 128), False, 1.0)
    gate_ref[...] = jax.nn.sigmoid(seg(2816, 128))


def _proj(x2, g1, w_in_p, rc, rs1, rs2):
    t = x2.shape[0]
    nb = t // SEQ
    nblk_s = SEQ // PROJ_TM
    row = lambda i: (i, 0)
    pos = lambda i: (i % nblk_s, 0)
    const = lambda i: (0, 0)
    perm = lambda i: (i // nblk_s, 0, i % nblk_s, 0)
    wide = jax.ShapeDtypeStruct((t, 512), BF16)
    wide_spec = pl.BlockSpec((PROJ_TM, 512), row)
    d4 = jax.ShapeDtypeStruct((nb, 4, SEQ // 4, 512), BF16)
    d4_spec = pl.BlockSpec((None, 4, PROJ_TM // 4, 512), perm)
    d16 = jax.ShapeDtypeStruct((nb, 16, SEQ // 16, 512), BF16)
    d16_spec = pl.BlockSpec((None, 16, PROJ_TM // 16, 512), perm)
    outs = [wide] * 3 + [d4] * 3 + [d16] * 3 + [wide, jax.ShapeDtypeStruct((t, 256), BF16), wide,
                                               jax.ShapeDtypeStruct((t, LANES), F32)]
    out_specs = [wide_spec] * 3 + [d4_spec] * 3 + [d16_spec] * 3 + [
        wide_spec, pl.BlockSpec((PROJ_TM, 256), row), wide_spec, pl.BlockSpec((PROJ_TM, LANES), row)]
    return pl.pallas_call(
        _proj_kernel,
        grid=(t // PROJ_TM,),
        in_specs=[pl.BlockSpec((PROJ_TM, D_MODEL), row), pl.BlockSpec((1, D_MODEL), const),
                  pl.BlockSpec((D_MODEL, D_IN_PAD), const),
                  pl.BlockSpec((PROJ_TM, LANES), pos), pl.BlockSpec((PROJ_TM, LANES), pos),
                  pl.BlockSpec((PROJ_TM, LANES), pos)],
        out_specs=out_specs,
        out_shape=outs,
        scratch_shapes=[pltpu.VMEM((PROJ_TM, LANES), F32)] * 2,
        compiler_params=pltpu.CompilerParams(dimension_semantics=("arbitrary",),
                                             vmem_limit_bytes=VMEM_LIMIT),
        name="proj",
    )(x2, g1, w_in_p, rc, rs1, rs2)


def _dilated_kernel(q1_ref, k1_ref, v1_ref, q4_ref, k4_ref, v4_ref, q16_ref, k16_ref, v16_ref, o_ref,
                    qt, vt, op, lp, on0, on1, on2, ln0, ln1, ln2):
    c2 = _row_iota((2 * QBLK, 2 * QBLK))
    a2 = _lane_iota((2 * QBLK, 2 * QBLK)) % QBLK
    band = jnp.where((c2 >= a2) & (c2 <= a2 + QBLK), 0.0, NEG)
    band_noprev = jnp.where(c2 < QBLK, NEG, band)
    c1 = _row_iota((QBLK, 2 * QBLK))
    a1 = _lane_iota((QBLK, 2 * QBLK)) % QBLK
    band_own = jnp.where(c1 <= a1, 0.0, NEG)
    top_sq = _row_iota((LANES, QBLK)) < HEAD_DIM

    inputs = ((q1_ref, k1_ref, v1_ref), (q4_ref, k4_ref, v4_ref), (q16_ref, k16_ref, v16_ref))
    for pi, (window, dil) in enumerate(DIL_PATTERNS):
        seg_len = SEQ // dil
        nseg_blk = seg_len // QBLK
        q_ref, k_ref, v_ref = inputs[pi]

        def block(ref, j, d=None, nseg_blk=nseg_blk, dil=dil):
            if dil == 1:
                return ref[pl.ds(pl.multiple_of(j * QBLK, QBLK), QBLK), :]
            if d is None:
                return ref[j // nseg_blk, (j % nseg_blk) * QBLK:(j % nseg_blk + 1) * QBLK, :]
            within = d % nseg_blk
            return ref[j // nseg_blk, within * QBLK:(within + 1) * QBLK, :]

        for j in range(N_QBLK):
            qt[j] = block(q_ref, j).T if dil > 1 else q_ref[j * QBLK:(j + 1) * QBLK, :].T
            vt[j] = block(v_ref, j).T if dil > 1 else v_ref[j * QBLK:(j + 1) * QBLK, :].T

        def body(u, carry, nseg_blk=nseg_blk, k_ref=k_ref, block=block):
            js = [u * DIL_UNROLL + d for d in range(DIL_UNROLL)]
            prevs, scores = [], []
            for d, j in enumerate(js):
                qtb = qt[j]
                zero = jnp.zeros_like(qtb)
                q_both = jnp.concatenate([jnp.where(top_sq, qtb, zero), jnp.where(top_sq, zero, qtb)], axis=1)
                if nseg_blk == 1:
                    keys, bias, jp = block(k_ref, j, d), band_own, j
                else:
                    if nseg_blk <= DIL_UNROLL:
                        first = d % nseg_blk == 0
                        jp = j if first else j - 1
                        dp = d if first else d - 1
                        bias = band_noprev if first else band
                    else:
                        jp = jnp.maximum(j - 1, 0) if d == 0 else j - 1
                        dp = d - 1
                        bias = jnp.where(u == 0, band_noprev, band) if d == 0 else band
                    keys = jnp.concatenate([block(k_ref, jp, dp), block(k_ref, j, d)], axis=0)
                prevs.append(jp)
                scores.append(jnp.dot(keys, q_both, preferred_element_type=F32) + bias)
            stats = []
            for s in scores:
                m = jnp.max(s, axis=0, keepdims=True)
                p = jnp.exp2(s - m)
                stats.append((m, jnp.sum(p, axis=0, keepdims=True), p.astype(BF16)))
            outs = []
            for j, jp, (m, l, p) in zip(js, prevs, stats):
                v_span = vt[j] if nseg_blk == 1 else jnp.concatenate([vt[jp], vt[j]], axis=1)
                outs.append(jnp.dot(v_span, p, preferred_element_type=F32))
            for j, (m, l, p), res in zip(js, stats, outs):
                row0 = pl.multiple_of(j * QBLK, QBLK)
                lse = m + jnp.log2(l)
                o_t = jnp.where(top_sq, res[:, 0:QBLK] / l[:, 0:QBLK], res[:, QBLK:] / l[:, QBLK:])
                lse_t = jnp.where(top_sq, jnp.broadcast_to(lse[:, 0:QBLK], (LANES, QBLK)),
                                  jnp.broadcast_to(lse[:, QBLK:], (LANES, QBLK)))
                op[pl.ds(row0, QBLK), :] = o_t.T
                lp[pl.ds(row0, QBLK), :] = lse_t.T
            return carry

        lax.fori_loop(0, N_QBLK // DIL_UNROLL, body, 0)

        for r in range(dil):
            dst = pl.ds(r, seg_len, stride=dil) if dil > 1 else pl.ds(0, SEQ)
            src = pl.ds(r * seg_len, seg_len)
            (on0, on1, on2)[pi][dst, :] = op[src, :]
            (ln0, ln1, ln2)[pi][dst, :] = lp[src, :]

    l0, l1, l2 = ln0[...], ln1[...], ln2[...]
    mx = jnp.maximum(jnp.maximum(l0, l1), l2)
    e0, e1, e2 = jnp.exp2(l0 - mx), jnp.exp2(l1 - mx), jnp.exp2(l2 - mx)
    den = e0 + e1 + e2
    out = (e0 / den) * on0[...] + (e1 / den) * on1[...] + (e2 / den) * on2[...]
    o_ref[...] = out.astype(o_ref.dtype)


def _dilated(a1, a4, a16):
    t = a1[0].shape[0]
    nb = t // SEQ
    spec1 = pl.BlockSpec((SEQ, LANES), lambda b, hp: (b, hp))
    spec4 = pl.BlockSpec((None, 4, SEQ // 4, LANES), lambda b, hp: (b, 0, 0, hp))
    spec16 = pl.BlockSpec((None, 16, SEQ // 16, LANES), lambda b, hp: (b, 0, 0, hp))
    return pl.pallas_call(
        _dilated_kernel,
        grid=(nb, N_HEADS_DIL // 2),
        in_specs=[spec1] * 3 + [spec4] * 3 + [spec16] * 3,
        out_specs=spec1,
        out_shape=jax.ShapeDtypeStruct((t, 512), BF16),
        scratch_shapes=[pltpu.VMEM((N_QBLK, LANES, QBLK), BF16), pltpu.VMEM((N_QBLK, LANES, QBLK), BF16)]
        + [pltpu.VMEM((SEQ, LANES), F32)] * 8,
        compiler_params=pltpu.CompilerParams(dimension_semantics=("arbitrary", "arbitrary"),
                                             vmem_limit_bytes=VMEM_LIMIT),
        name="dilated",
    )(*a1, *a4, *a16)


def _gelu_tanh(x):
    return 0.5 * x * (1.0 + jnp.tanh(np.sqrt(2.0 / np.pi).astype(np.float32) * (x + 0.044715 * (x * x * x))))


def _softmax_step(state, s, v_t):
    m, l, acc = state
    mn = jnp.maximum(m, jnp.max(s, axis=0, keepdims=True))
    al = jnp.exp2(m - mn)
    p = jnp.exp2(s - mn)
    l = al * l + jnp.sum(p, axis=0, keepdims=True)
    acc = al * acc + jnp.dot(v_t, p.astype(BF16), preferred_element_type=F32)
    return mn, l, acc


def _nsa_kernel(q_ref, kflat_ref, vflat_ref, ksw_ref, gate_ref,
                wk1_ref, wk2d_ref, pek_ref, wv1_ref, wv2t_ref, pev_ref,
                cc_ref, cs1_ref, cs2_ref, ovl_ref, exp_ref,
                o_ref,
                qt, vst, vwt, gtt, kc2, vct):
    g = pl.program_id(1)
    nq = NSA_REP * QBLK

    for i in range(N_QBLK):
        rows = pl.ds(i * QBLK, QBLK)
        qt[i, 0:LANES, :] = q_ref[rows, 0:LANES].T
        qt[i, LANES:2 * LANES, :] = q_ref[rows, LANES:2 * LANES].T
        vs_t = ksw_ref[rows, LANES:2 * LANES].T
        vst[i] = jnp.where(g == 0, vs_t[0:HEAD_DIM], vs_t[HEAD_DIM:])
        vw_t = ksw_ref[rows, 3 * LANES:4 * LANES].T
        vwt[i] = jnp.where(g == 0, vw_t[0:HEAD_DIM], vw_t[HEAD_DIM:])
        gt = gate_ref[rows, :]
        gt = jnp.where(g == 0, gt, pltpu.roll(gt, LANES - 3 * NSA_REP, axis=1))
        gtt[i] = gt.T[0:16, :]

    def hidden(flat_ref, w1_ref, pe_ref):
        ab = jnp.dot(flat_ref[...], w1_ref[...], preferred_element_type=F32)
        pb = (jnp.dot(pe_ref[0:16, :], w1_ref[:, 0:CMP_HIDDEN], preferred_element_type=F32)
              + jnp.dot(pe_ref[16:32, :], w1_ref[:, CMP_HIDDEN:], preferred_element_type=F32))
        hid = ab[:, 0:CMP_HIDDEN] + pltpu.roll(ab[:, CMP_HIDDEN:], N_CMP - 1, axis=0) + pb[0:1, :]
        return _gelu_tanh(hid).astype(BF16)

    hk = hidden(kflat_ref, wk1_ref, pek_ref)
    kc2[...] = _rope_lanes(jnp.dot(hk, wk2d_ref[...], preferred_element_type=F32),
                           cc_ref[...], cs1_ref[...], cs2_ref[...]).astype(BF16)
    hv = hidden(vflat_ref, wv1_ref, pev_ref)
    vct[...] = lax.dot_general(wv2t_ref[...], hv, _NT, preferred_element_type=F32).astype(BF16)

    key = _row_iota((QBLK, nq))
    qry = _lane_iota((QBLK, nq)) % QBLK
    causal = jnp.where(key <= qry, 0.0, NEG)
    upper = jnp.where(key > qry, 0.0, NEG)
    blk = _row_iota((N_SELBLK, QBLK))
    qry1 = _lane_iota((N_SELBLK, QBLK))
    grp_keys = SEL_GROUP * QBLK
    win_blocks = WIN // QBLK

    def qgroup(w, full_window):
        blocks = [SEL_GROUP * w + d for d in range(SEL_GROUP)]
        row0s = [pl.multiple_of(i * QBLK, QBLK) for i in blocks]

        q_pads = []
        for i in blocks:
            q_t = jnp.concatenate([qt[i, r * HEAD_DIM:(r + 1) * HEAD_DIM, :] for r in range(NSA_REP)], axis=1)
            zero = jnp.zeros_like(q_t)
            q_pads.append(jnp.where(g == 0, jnp.concatenate([q_t, zero], axis=0),
                                    jnp.concatenate([zero, q_t], axis=0)))
        s_cmp = [jnp.dot(kc2[...], qp, preferred_element_type=F32) for qp in q_pads]
        s_win, v_win = [], []
        for d, (i, qp) in enumerate(zip(blocks, q_pads)):
            if full_window:
                first = i - win_blocks
                keys = ksw_ref[pl.ds(pl.multiple_of(first * QBLK, QBLK), (win_blocks + 1) * QBLK), 2 * LANES:3 * LANES]
                s = jnp.dot(keys, qp, preferred_element_type=F32)
                s = jnp.concatenate([s[0:QBLK] + upper, s[QBLK:win_blocks * QBLK], s[win_blocks * QBLK:] + causal],
                                    axis=0)
                v_t = jnp.concatenate([vwt[first + e] for e in range(win_blocks + 1)], axis=1)
            else:
                s = jnp.dot(ksw_ref[0:(d + 1) * QBLK, 2 * LANES:3 * LANES], qp, preferred_element_type=F32)
                s = jnp.concatenate([s[0:d * QBLK], s[d * QBLK:] + causal], axis=0) if d else s + causal
                v_t = jnp.concatenate([vwt[e] for e in range(d + 1)], axis=1) if d else vwt[0]
            s_win.append(s)
            v_win.append(v_t)

        pcs = []
        for row0, s in zip(row0s, s_cmp):
            cvalid = (CMP_STRIDE * key + (CMP_LEN - 1) <= row0 + qry) & (key < N_CMP - 1)
            s = jnp.where(cvalid, s, NEG)
            m = jnp.max(s, axis=0, keepdims=True)
            p = jnp.where(cvalid, jnp.exp2(s - m), 0.0)
            l = jnp.sum(p, axis=0, keepdims=True)
            pcs.append(p / jnp.where(l > 0, l, 1.0))
        o_cmp = [jnp.dot(vct[...], pc.astype(BF16), preferred_element_type=F32) for pc in pcs]

        ovl = ovl_ref[...]
        imps = []
        for pc in pcs:
            psum = pc[:, 0:QBLK] + pc[:, QBLK:2 * QBLK] + pc[:, 2 * QBLK:3 * QBLK] + pc[:, 3 * QBLK:]
            p_hi = psum.astype(BF16)
            p_lo = (psum - p_hi.astype(F32)).astype(BF16)
            imps.append(jnp.dot(ovl, p_hi, preferred_element_type=F32) + jnp.dot(ovl, p_lo, preferred_element_type=F32))

        o_win = []
        for s, v_t in zip(s_win, v_win):
            m = jnp.max(s, axis=0, keepdims=True)
            p = jnp.exp2(s - m)
            l = jnp.sum(p, axis=0, keepdims=True)
            o_win.append(jnp.dot(v_t, p.astype(BF16), preferred_element_type=F32) / l)

        q_sel = []
        for row0, imp, qp in zip(row0s, imps, q_pads):
            t1 = row0 + qry1
            blk_t = t1 // SEL_LEN
            forced = (blk == 0) | (blk == blk_t) | (blk == blk_t - 1)
            v = jnp.where(blk * SEL_LEN <= t1, jnp.where(forced, BIG, imp), -BIG)
            selm = jnp.zeros((N_SELBLK, QBLK), F32)
            for _ in range(N_SEL):
                mx = jnp.max(v, axis=0, keepdims=True)
                first_hit = jnp.min(jnp.where(v == mx, blk, N_SELBLK), axis=0, keepdims=True)
                hit = blk == first_hit
                selm = jnp.where(hit, 1.0, selm)
                v = jnp.where(hit, LOWEST, v)
            notsel = jnp.concatenate([selm - 1.0, jnp.zeros((LANES - N_SELBLK, QBLK), F32)], axis=0).astype(BF16)
            q_sel.append(jnp.concatenate([qp, jnp.concatenate([notsel] * NSA_REP, axis=1)], axis=0))

        def sel_keys(t, n_rows):
            rows_t = pl.ds(pl.multiple_of(t * grp_keys, grp_keys), n_rows)
            return jnp.concatenate([ksw_ref[rows_t, 0:LANES], exp_ref[t, 0:n_rows, :]], axis=1)

        def sel_values(t, n_blk):
            return jnp.concatenate([vst[t * SEL_GROUP + e] for e in range(n_blk)], axis=1) if n_blk > 1 else vst[t * SEL_GROUP]

        wide = SEL_GROUP * nq
        st = (jnp.full((1, wide), LOWEST, F32), jnp.zeros((1, wide), F32), jnp.zeros((HEAD_DIM, wide), F32))
        if full_window:
            q_all = jnp.concatenate(q_sel, axis=1)

            def sel_body(t, st):
                s = jnp.dot(sel_keys(t, grp_keys), q_all, preferred_element_type=F32)
                return _softmax_step(st, s, sel_values(t, SEL_GROUP))

            st = lax.fori_loop(0, w, sel_body, st)
        o_sel = []
        for d, qs in enumerate(q_sel):
            s = jnp.dot(sel_keys(w, (d + 1) * QBLK), qs, preferred_element_type=F32)
            s = jnp.concatenate([s[0:d * QBLK], s[d * QBLK:] + causal], axis=0) if d else s + causal
            lanes = slice(d * nq, (d + 1) * nq)
            m, l, acc = _softmax_step((st[0][:, lanes], st[1][:, lanes], st[2][:, lanes]), s, sel_values(w, d + 1))
            o_sel.append(acc / l)

        for n, (i, row0) in enumerate(zip(blocks, row0s)):
            gti = gtt[i]

            def gate(br):
                return jnp.concatenate([gti[r * 3 + br:r * 3 + br + 1, :] for r in range(NSA_REP)], axis=1)

            out_t = gate(0) * o_cmp[n] + gate(1) * o_sel[n] + gate(2) * o_win[n]
            for pair in range(NSA_REP // 2):
                both = jnp.concatenate([out_t[:, (2 * pair) * QBLK:(2 * pair + 1) * QBLK],
                                        out_t[:, (2 * pair + 1) * QBLK:(2 * pair + 2) * QBLK]], axis=0)
                o_ref[pl.ds(row0, QBLK), pair * LANES:(pair + 1) * LANES] = both.T.astype(o_ref.dtype)

    def tail_groups(w, carry):
        qgroup(w, True)
        return carry

    qgroup(0, False)
    lax.fori_loop(1, N_QBLK // SEL_GROUP, tail_groups, 0)


def _nsa(bq, kvflat, ksw, gates, wk1, wk2d, pek, wv1, wv2t, pev, cc, cs1, cs2, ovl_t, expand_t):
    t = bq.shape[0]
    nb = t // SEQ
    const2 = lambda b, g: (0, 0)
    return pl.pallas_call(
        _nsa_kernel,
        grid=(nb, N_KV_NSA),
        in_specs=[
            pl.BlockSpec((SEQ, 2 * LANES), lambda b, g: (b, g)),
            pl.BlockSpec((None, None, N_CMP, CMP_STRIDE * HEAD_DIM), lambda b, g: (b, g, 0, 0)),
            pl.BlockSpec((None, None, N_CMP, CMP_STRIDE * HEAD_DIM), lambda b, g: (b, N_KV_NSA + g, 0, 0)),
            pl.BlockSpec((SEQ, 4 * LANES), lambda b, g: (b, 0)),
            pl.BlockSpec((SEQ, LANES), lambda b, g: (b, 0)),
            pl.BlockSpec((CMP_STRIDE * HEAD_DIM, 2 * CMP_HIDDEN), const2),
            pl.BlockSpec((CMP_HIDDEN, LANES), const2),
            pl.BlockSpec((32, CMP_STRIDE * HEAD_DIM), const2),
            pl.BlockSpec((CMP_STRIDE * HEAD_DIM, 2 * CMP_HIDDEN), const2),
            pl.BlockSpec((HEAD_DIM, CMP_HIDDEN), const2),
            pl.BlockSpec((32, CMP_STRIDE * HEAD_DIM), const2),
            pl.BlockSpec((N_CMP, LANES), const2),
            pl.BlockSpec((N_CMP, LANES), const2),
            pl.BlockSpec((N_CMP, LANES), const2),
            pl.BlockSpec((N_SELBLK, LANES), const2),
            pl.BlockSpec((N_QBLK // SEL_GROUP, SEL_GROUP * QBLK, LANES), lambda b, g: (0, 0, 0)),
        ],
        out_specs=pl.BlockSpec((SEQ, 2 * LANES), lambda b, g: (b, g)),
        out_shape=jax.ShapeDtypeStruct((t, 512), BF16),
        scratch_shapes=[pltpu.VMEM((N_QBLK, 2 * LANES, QBLK), BF16),
                        pltpu.VMEM((N_QBLK, HEAD_DIM, QBLK), BF16),
                        pltpu.VMEM((N_QBLK, HEAD_DIM, QBLK), BF16),
                        pltpu.VMEM((N_QBLK, 16, QBLK), F32),
                        pltpu.VMEM((N_CMP, LANES), BF16),
                        pltpu.VMEM((HEAD_DIM, N_CMP), BF16)],
        compiler_params=pltpu.CompilerParams(dimension_semantics=("arbitrary", "arbitrary"),
                                             vmem_limit_bytes=VMEM_LIMIT),
        name="nsa",
    )(bq, kvflat, kvflat, ksw, gates, wk1, wk2d, pek, wv1, wv2t, pev, cc, cs1, cs2, ovl_t, expand_t)


def _pack_bf16_pairs(lo, hi):
    lo_b = lax.bitcast_convert_type(lo.astype(BF16).astype(F32), jnp.uint32)
    hi_b = lax.bitcast_convert_type(hi.astype(BF16).astype(F32), jnp.uint32)
    return (hi_b & jnp.uint32(0xFFFF0000)) | (lo_b >> 16)


def _unpack_bf16_pairs(p):
    lo = lax.bitcast_convert_type(p << 16, F32)
    hi = lax.bitcast_convert_type(p & jnp.uint32(0xFFFF0000), F32)
    return lo, hi


def _post_kernel(x_ref, ma_ref, mb_ref, wo_ref, g2_ref, wr_ref, br_ref, tri_ref,
                 x1_ref, ha_ref, hb_ref, e1_ref, e2_ref, r1_ref, r2_ref, w_ref, cnt_ref):
    x1 = (x_ref[...] + jnp.dot(ma_ref[...], wo_ref[0:512, :], preferred_element_type=F32)
          + jnp.dot(mb_ref[...], wo_ref[512:1024, :], preferred_element_type=F32))
    x1_ref[...] = x1
    ms = jnp.mean(x1 * x1, axis=-1, keepdims=True)
    h2 = x1 * lax.rsqrt(ms + EPS) * g2_ref[...]
    ha_ref[...] = _pack_bf16_pairs(h2[:, 0:PACK_W], h2[:, 2 * PACK_W:3 * PACK_W])
    hb_ref[...] = _pack_bf16_pairs(h2[:, PACK_W:2 * PACK_W], h2[:, 3 * PACK_W:])

    h_hi = h2.astype(BF16)
    h_lo = (h2 - h_hi.astype(F32)).astype(BF16)
    wr = wr_ref[...]
    w_hi = wr.astype(BF16)
    w_lo = (wr - w_hi.astype(F32)).astype(BF16)
    lg = (lax.dot_general(w_hi, h_hi, _NT, preferred_element_type=F32)
          + lax.dot_general(w_hi, h_lo, _NT, preferred_element_type=F32)
          + lax.dot_general(w_lo, h_hi, _NT, preferred_element_type=F32)) + br_ref[:, 0:1]
    gl = lg[0:N_GROUPS, :]
    sub = _row_iota(gl.shape)
    gmax = jnp.max(gl, axis=0, keepdims=True)
    p_g = 1.0 / jnp.sum(jnp.exp(gl - gmax), axis=0, keepdims=True)
    g_star = jnp.min(jnp.where(gl == gmax, sub, N_GROUPS), axis=0, keepdims=True)
    el = jnp.zeros_like(gl)
    for gi in range(N_GROUPS):
        lo = N_GROUPS + gi * EXPERTS_PER_GROUP
        el = el + jnp.where(g_star == gi, lg[lo:lo + EXPERTS_PER_GROUP, :], 0.0)
    emax = jnp.max(el, axis=0, keepdims=True)
    ee = jnp.exp(el - emax)
    pe = ee / jnp.sum(ee, axis=0, keepdims=True)
    v1 = jnp.max(pe, axis=0, keepdims=True)
    i1 = jnp.min(jnp.where(pe == v1, sub, EXPERTS_PER_GROUP), axis=0, keepdims=True)
    pe2 = jnp.where(sub == i1, -1.0, pe)
    v2 = jnp.max(pe2, axis=0, keepdims=True)
    i2 = jnp.min(jnp.where(pe2 == v2, sub, EXPERTS_PER_GROUP), axis=0, keepdims=True)
    tot = v1 + v2
    e1 = g_star * EXPERTS_PER_GROUP + i1
    e2 = g_star * EXPERTS_PER_GROUP + i2
    e1_ref[0] = e1
    e2_ref[0] = e2

    wslab = jnp.concatenate([v1 / tot * p_g, v2 / tot * p_g, jnp.zeros((LANES - 2, MOE_TM), F32)], axis=0)
    for j in range(MOE_TM // LANES):
        w_ref[j * LANES:(j + 1) * LANES, :] = wslab[:, j * LANES:(j + 1) * LANES].T

    n_exp = N_GROUPS * EXPERTS_PER_GROUP
    sub_e = _row_iota((n_exp, MOE_TM))
    hit1 = sub_e == e1
    hit2 = sub_e == e2
    assigned = jnp.where(hit1, 1.0, 0.0) + jnp.where(hit2, 1.0, 0.0)
    before = jnp.dot(assigned.astype(BF16), tri_ref[...], preferred_element_type=F32)
    r1_ref[0] = jnp.sum(jnp.where(hit1, before, 0.0), axis=0, keepdims=True).astype(jnp.int32)
    r2_ref[0] = jnp.sum(jnp.where(hit2, before, 0.0), axis=0, keepdims=True).astype(jnp.int32)
    cnt = jnp.sum(assigned, axis=1, keepdims=True).astype(jnp.int32)
    cnt_ref[0] = jnp.broadcast_to(cnt, (n_exp, LANES))


def _post(x2, mix_a, mix_b, w_o, g2, wr_t, br, tri, part):
    t = x2.shape[0] // MOE_PARTS
    nt = t // MOE_TM
    first = part * nt
    n_exp = N_GROUPS * EXPERTS_PER_GROUP
    row = lambda i: (i, 0)
    const = lambda i: (0, 0)
    tok = lambda i: (i, 0, 0)
    src = lambda i: (first + i, 0)
    tok_spec = pl.BlockSpec((1, 1, MOE_TM), tok)
    tok_shape = jax.ShapeDtypeStruct((nt, 1, MOE_TM), jnp.int32)
    return pl.pallas_call(
        _post_kernel,
        grid=(nt,),
        in_specs=[pl.BlockSpec((MOE_TM, D_MODEL), src), pl.BlockSpec((MOE_TM, 512), src),
                  pl.BlockSpec((MOE_TM, 512), src), pl.BlockSpec((D_MODEL, D_MODEL), const),
                  pl.BlockSpec((1, D_MODEL), const), pl.BlockSpec((ROUTER_ROWS, D_MODEL), const),
                  pl.BlockSpec((ROUTER_ROWS, LANES), const), pl.BlockSpec((MOE_TM, MOE_TM), const)],
        out_specs=[pl.BlockSpec((MOE_TM, D_MODEL), row),
                   pl.BlockSpec((MOE_TM, PACK_W), row), pl.BlockSpec((MOE_TM, PACK_W), row),
                   tok_spec, tok_spec, tok_spec, tok_spec,
                   pl.BlockSpec((MOE_TM, LANES), row),
                   pl.BlockSpec((1, n_exp, LANES), tok)],
        out_shape=[jax.ShapeDtypeStruct((t, D_MODEL), F32),
                   jax.ShapeDtypeStruct((t, PACK_W), jnp.uint32), jax.ShapeDtypeStruct((t, PACK_W), jnp.uint32),
                   tok_shape, tok_shape, tok_shape, tok_shape,
                   jax.ShapeDtypeStruct((t, LANES), F32),
                   jax.ShapeDtypeStruct((nt, n_exp, LANES), jnp.int32)],
        compiler_params=pltpu.CompilerParams(dimension_semantics=("arbitrary",),
                                             vmem_limit_bytes=VMEM_LIMIT),
        name="post",
    )(x2, mix_a, mix_b, w_o, g2, wr_t, br, tri)


def _sc_mesh():
    return plsc.VectorSubcoreMesh(core_axis_name="core", subcore_axis_name="subcore")


def _sc_scatter_rows(x, pos1, pos2, n_out):
    r, c = x.shape

    @pl.kernel(out_type=jax.ShapeDtypeStruct((n_out, c), x.dtype), mesh=_sc_mesh(), scratch_types=[])
    def scatter_kernel(x_hbm, p1_hbm, p2_hbm, o_hbm):
        def body(x_vmem, p1_vmem, p2_vmem):
            pltpu.sync_copy(x_vmem, o_hbm.at[p1_vmem.at[0]])
            pltpu.sync_copy(x_vmem, o_hbm.at[p2_vmem.at[0]])

        idx = pl.BlockSpec((1, SC_WINDOW), lambda i: (0, i))
        pltpu.emit_pipeline(body, grid=(r // SC_WINDOW,),
                            in_specs=[pl.BlockSpec((SC_WINDOW, c), lambda i: (i, 0)), idx, idx], out_specs=[],
                            core_axis_name=("core", "subcore"),
                            dimension_semantics=(pltpu.PARALLEL,))(x_hbm, p1_hbm, p2_hbm)

    return scatter_kernel(x, pos1.reshape(1, r), pos2.reshape(1, r))


def _sc_gather_rows(y, idx):
    n = idx.shape[0]
    c = y.shape[1]

    @pl.kernel(out_type=jax.ShapeDtypeStruct((n, c), y.dtype), mesh=_sc_mesh(), scratch_types=[])
    def gather_kernel(y_hbm, i_hbm, o_hbm):
        def body(i_vmem, o_vmem):
            pltpu.sync_copy(y_hbm.at[i_vmem.at[0]], o_vmem)

        pltpu.emit_pipeline(body, grid=(n // SC_WINDOW,),
                            in_specs=[pl.BlockSpec((1, SC_WINDOW), lambda i: (0, i))],
                            out_specs=[pl.BlockSpec((SC_WINDOW, c), lambda i: (i, 0))],
                            core_axis_name=("core", "subcore"),
                            dimension_semantics=(pltpu.PARALLEL,))(i_hbm, o_hbm)

    return gather_kernel(y, idx.reshape(1, n))


def _ffn_kernel(ce_ref, cv_ref, xa_ref, xb_ref, wg_ref, wu_ref, wd_ref, ya_ref, yb_ref, wg_bf, wu_bf, wd_bf):
    c = pl.program_id(0)
    prev = ce_ref[jnp.maximum(c - 1, 0)]

    @pl.when((c == 0) | (ce_ref[c] != prev))
    def _():
        wg_bf[...] = wg_ref[0].astype(BF16)
        wu_bf[...] = wu_ref[0].astype(BF16)
        wd_bf[...] = wd_ref[0].astype(BF16)

    @pl.when(cv_ref[c] == 0)
    def _():
        ya_ref[...] = jnp.zeros_like(ya_ref)
        yb_ref[...] = jnp.zeros_like(yb_ref)

    @pl.when(cv_ref[c] != 0)
    def _():
        a_lo, a_hi = _unpack_bf16_pairs(xa_ref[...])
        b_lo, b_hi = _unpack_bf16_pairs(xb_ref[...])
        xs = jnp.concatenate([a_lo, b_lo, a_hi, b_hi], axis=1).astype(BF16)
        gate = jnp.dot(xs, wg_bf[...], preferred_element_type=F32)
        up = jnp.dot(xs, wu_bf[...], preferred_element_type=F32)
        he = (gate * jax.nn.sigmoid(gate) * up).astype(BF16)
        y = jnp.dot(he, wd_bf[...], preferred_element_type=F32)
        ya_ref[...] = _pack_bf16_pairs(y[:, 0:PACK_W], y[:, 2 * PACK_W:3 * PACK_W])
        yb_ref[...] = _pack_bf16_pairs(y[:, PACK_W:2 * PACK_W], y[:, 3 * PACK_W:])


def _ffn(chunk_expert, chunk_valid, xsa, xsb, wg, wu, wd):
    n_pad = xsa.shape[0]
    rows = pl.BlockSpec((FFN_ROWS, PACK_W), lambda c, ce, cv: (c, 0))
    grid_spec = pltpu.PrefetchScalarGridSpec(
        num_scalar_prefetch=2,
        grid=(n_pad // FFN_ROWS,),
        in_specs=[rows, rows,
                  pl.BlockSpec((1, D_MODEL, D_FF_EXPERT), lambda c, ce, cv: (ce[c], 0, 0)),
                  pl.BlockSpec((1, D_MODEL, D_FF_EXPERT), lambda c, ce, cv: (ce[c], 0, 0)),
                  pl.BlockSpec((1, D_FF_EXPERT, D_MODEL), lambda c, ce, cv: (ce[c], 0, 0))],
        out_specs=[rows, rows],
        scratch_shapes=[pltpu.VMEM((D_MODEL, D_FF_EXPERT), BF16), pltpu.VMEM((D_MODEL, D_FF_EXPERT), BF16),
                        pltpu.VMEM((D_FF_EXPERT, D_MODEL), BF16)],
    )
    out = jax.ShapeDtypeStruct((n_pad, PACK_W), jnp.uint32)
    return pl.pallas_call(
        _ffn_kernel,
        grid_spec=grid_spec,
        out_shape=[out, out],
        compiler_params=pltpu.CompilerParams(dimension_semantics=("arbitrary",),
                                             vmem_limit_bytes=VMEM_LIMIT),
        name="ffn",
    )(chunk_expert, chunk_valid, xsa, xsb, wg, wu, wd)


def _final_kernel(x1_ref, y1a_ref, y1b_ref, y2a_ref, y2b_ref, w_ref, gf_ref, o_ref):
    w1 = w_ref[:, 0:1]
    w2 = w_ref[:, 1:2]
    a1_lo, a1_hi = _unpack_bf16_pairs(y1a_ref[...])
    b1_lo, b1_hi = _unpack_bf16_pairs(y1b_ref[...])
    a2_lo, a2_hi = _unpack_bf16_pairs(y2a_ref[...])
    b2_lo, b2_hi = _unpack_bf16_pairs(y2b_ref[...])
    y = jnp.concatenate([w1 * a1_lo + w2 * a2_lo, w1 * b1_lo + w2 * b2_lo,
                         w1 * a1_hi + w2 * a2_hi, w1 * b1_hi + w2 * b2_hi], axis=1)
    xo = x1_ref[...] + y
    ms = jnp.mean(xo * xo, axis=-1, keepdims=True)
    o_ref[...] = xo * lax.rsqrt(ms + EPS) * gf_ref[...]


def _final(x1, yga, ygb, wtok, gf, out_prev, part):
    t = x1.shape[0]
    first = part * (t // FINAL_TM)
    row = lambda i: (i, 0)
    one = pl.BlockSpec((None, FINAL_TM, PACK_W), lambda i: (0, i, 0))
    two = pl.BlockSpec((None, FINAL_TM, PACK_W), lambda i: (1, i, 0))
    in_specs = [pl.BlockSpec((FINAL_TM, D_MODEL), row), one, one, two, two,
                pl.BlockSpec((FINAL_TM, LANES), row), pl.BlockSpec((1, D_MODEL), lambda i: (0, 0))]
    args = [x1, yga, ygb, yga, ygb, wtok, gf]
    kern = _final_kernel
    aliases = {}
    if out_prev is not None:
        in_specs.append(pl.BlockSpec(memory_space=pl.ANY))
        args.append(out_prev)
        aliases = {len(args) - 1: 0}
        kern = lambda *refs: _final_kernel(*refs[:7], refs[8])
    return pl.pallas_call(
        kern,
        grid=(t // FINAL_TM,),
        in_specs=in_specs,
        out_specs=pl.BlockSpec((FINAL_TM, D_MODEL), lambda i: (first + i, 0)),
        out_shape=jax.ShapeDtypeStruct((t * MOE_PARTS, D_MODEL), F32),
        input_output_aliases=aliases,
        compiler_params=pltpu.CompilerParams(dimension_semantics=("arbitrary",),
                                             vmem_limit_bytes=VMEM_LIMIT),
        name="final",
    )(*args)


def _moe_plan(cnt, e1, e2, r1, r2):
    nt, n_exp = cnt.shape
    tot = jnp.sum(cnt, axis=0)
    padded = (tot + FFN_ROWS - 1) // FFN_ROWS * FFN_ROWS
    seg_end = jnp.cumsum(padded)
    seg_start = seg_end - padded
    off = seg_start[None, :] + jnp.cumsum(cnt, axis=0) - cnt
    experts = jnp.arange(n_exp, dtype=jnp.int32)

    def lookup(e):
        return jnp.sum(jnp.where(e[:, :, None] == experts, off[:, None, :], 0), axis=2)

    pos1 = (lookup(e1) + r1).reshape(-1).astype(jnp.int32)
    pos2 = (lookup(e2) + r2).reshape(-1).astype(jnp.int32)
    n_chunks = (nt * MOE_TM * 2 + n_exp * FFN_ROWS) // FFN_ROWS
    cstart = jnp.arange(n_chunks, dtype=jnp.int32) * FFN_ROWS
    ce = jnp.sum((seg_end[None, :] <= cstart[:, None]).astype(jnp.int32), axis=1)
    cec = jnp.minimum(ce, n_exp - 1)
    seg_used = seg_start + tot
    valid = jnp.any((ce[:, None] == experts[None, :]) & (cstart[:, None] < seg_used[None, :]), axis=1)
    return pos1, pos2, cec.astype(jnp.int32), valid.astype(jnp.int32)


def _rope_tables(pos):
    half = ROT_DIM // 2
    inv_freq = ROPE_THETA ** (-jnp.arange(0, ROT_DIM, 2, dtype=F32) / ROT_DIM)
    ang = pos.astype(F32)[:, None] * inv_freq[None, :]
    cos, sin = jnp.cos(ang), jnp.sin(ang)
    l64 = np.arange(LANES) % HEAD_DIM
    f = l64 % half
    first = jnp.asarray(l64 < half)[None, :]
    second = jnp.asarray((l64 >= half) & (l64 < ROT_DIM))[None, :]
    c = jnp.where(first | second, cos[:, f], 1.0)
    s1 = jnp.where(first, -sin[:, f], 0.0)
    s2 = jnp.where(second, sin[:, f], 0.0)
    return c.astype(F32), s1.astype(F32), s2.astype(F32)


def _overlap_table_t():
    cs = np.arange(N_CMP)[None, :] * CMP_STRIDE
    js = np.arange(N_SELBLK)[:, None] * SEL_LEN
    ov = np.clip(np.minimum(cs + CMP_LEN, js + SEL_LEN) - np.maximum(cs, js), 0, None) / CMP_LEN
    ov[:, N_CMP - 1] = 0.0
    return ov.astype(np.float32)


def _expand_table_t():
    out = np.zeros((SEQ, LANES), np.float32)
    out[np.arange(SEQ), np.arange(SEQ) // SEL_LEN] = -NEG
    return out.reshape(N_QBLK // SEL_GROUP, SEL_GROUP * QBLK, LANES)


def kernel(x, norm1_g, w_in, pe_kc, w_kc1, w_kc2, pe_vc, w_vc1, w_vc2, w_o, norm2_g, w_rg, b_rg, w_re, b_re,
           w_gate, w_up, w_down, norm_f_g):
    nb, s, d = x.shape
    assert (s, d) == (SEQ, D_MODEL) and norm1_g.shape[0] == 1
    t = nb * s
    x2 = x.reshape(t, d)

    w_in_p = jnp.pad(w_in[0], ((0, 0), (0, D_IN_PAD - w_in.shape[2]))).astype(BF16)
    rc, rs1, rs2 = _rope_tables(jnp.arange(SEQ))
    cc, cs1, cs2 = _rope_tables(jnp.arange(N_CMP) * CMP_STRIDE + CMP_LEN - 1)
    half_flat = CMP_STRIDE * HEAD_DIM

    def cmp_w1(w1):
        return jnp.concatenate([w1[:half_flat], w1[half_flat:]], axis=1).astype(BF16)

    def cmp_pe(pe):
        rows = pe.reshape(2, half_flat)
        return jnp.concatenate([jnp.broadcast_to(rows[0:1], (16, half_flat)),
                                jnp.broadcast_to(rows[1:2], (16, half_flat))], axis=0).astype(BF16)

    wk2d = jnp.concatenate([w_kc2[0], w_kc2[0]], axis=1).astype(BF16)
    wv2t = w_vc2[0].T.astype(BF16)
    ovl_t = jnp.asarray(_overlap_table_t(), BF16)
    expand_t = jnp.asarray(_expand_table_t(), BF16)

    (aq, ak, av, aq4, ak4, av4, aq16, ak16, av16,
     bq, kcv, ksw, gates) = _proj(x2, norm1_g, w_in_p, rc, rs1, rs2)
    mix_a = _dilated((aq, ak, av), (aq4, ak4, av4), (aq16, ak16, av16))

    kvflat = kcv.reshape(nb, N_CMP, CMP_STRIDE, 2 * N_KV_NSA, HEAD_DIM).transpose(0, 3, 1, 2, 4)
    kvflat = kvflat.reshape(nb, 2 * N_KV_NSA, N_CMP, half_flat)
    mix_b = _nsa(bq, kvflat, ksw, gates, cmp_w1(w_kc1[0]), wk2d, cmp_pe(pe_kc[0]),
                 cmp_w1(w_vc1[0]), wv2t, cmp_pe(pe_vc[0]), cc, cs1, cs2, ovl_t, expand_t)

    n_router = N_GROUPS + N_GROUPS * EXPERTS_PER_GROUP
    wr_t = jnp.pad(jnp.concatenate([w_rg[0].T, w_re[0].reshape(d, -1).T], axis=0),
                   ((0, ROUTER_ROWS - n_router), (0, 0)))
    br = jnp.broadcast_to(jnp.pad(jnp.concatenate([b_rg[0], b_re[0].reshape(-1)]),
                                  (0, ROUTER_ROWS - n_router))[:, None], (ROUTER_ROWS, LANES))
    tri = jnp.asarray(np.triu(np.ones((MOE_TM, MOE_TM), np.float32), 1), BF16)
    tp = t // MOE_PARTS
    n_pad = 2 * tp + N_GROUPS * EXPERTS_PER_GROUP * FFN_ROWS
    w_o_bf = w_o[0].astype(BF16)
    gf = norm_f_g.reshape(1, d)
    routed = []
    for part in range(MOE_PARTS):
        x1, hpa, hpb, e1, e2, r1, r2, wtok, cnt = _post(x2, mix_a, mix_b, w_o_bf, norm2_g, wr_t, br, tri, part)
        pos1, pos2, chunk_expert, chunk_valid = _moe_plan(cnt[:, :, 0], e1[:, 0], e2[:, 0], r1[:, 0], r2[:, 0])
        xsa = _sc_scatter_rows(hpa, pos1, pos2, n_pad)
        xsb = _sc_scatter_rows(hpb, pos1, pos2, n_pad)
        routed.append((x1, wtok, pos1, pos2, chunk_expert, chunk_valid, xsa, xsb))
    gathered = []
    for x1, wtok, pos1, pos2, chunk_expert, chunk_valid, xsa, xsb in routed:
        ysa, ysb = _ffn(chunk_expert, chunk_valid, xsa, xsb, w_gate[0], w_up[0], w_down[0])
        pos12 = jnp.concatenate([pos1, pos2])
        gathered.append((x1, wtok, _sc_gather_rows(ysa, pos12), _sc_gather_rows(ysb, pos12)))
    out = None
    for part, (x1, wtok, yga, ygb) in enumerate(gathered):
        out = _final(x1, yga.reshape(2, tp, PACK_W), ygb.reshape(2, tp, PACK_W), wtok, gf, out, part)
    return out.reshape(nb, s, d)
```

```python
import numpy as np
import jax
import jax.numpy as jnp
from jax import lax
from jax.experimental import pallas as pl
from jax.experimental.pallas import tpu as pltpu
from jax.experimental.pallas import tpu_sc as plsc

F32 = jnp.float32
BF16 = jnp.bfloat16

D_MODEL = 1024
SEQ = 2048
HEAD_DIM = 64
N_HEADS_DIL = 8
DIL_PATTERNS = ((128, 1), (512, 4), (2048, 16))
N_HEADS_NSA = 8
N_KV_NSA = 2
NSA_REP = N_HEADS_NSA // N_KV_NSA
CMP_STRIDE = 16
CMP_LEN = 32
CMP_HIDDEN = 256
SEL_LEN = 64
N_SEL = 8
WIN = 512
ROPE_THETA = 500000.0
ROT_DIM = HEAD_DIM // 4
N_GROUPS = 8
EXPERTS_PER_GROUP = 8
D_FF_EXPERT = 256
EPS = 1e-6
NEG = -1e30
BIG = 1e9
LOWEST = -3.0e38
LOG2_E = 1.4426950408889634

LANES = 128
QBLK = 128
N_QBLK = SEQ // QBLK
N_CMP = SEQ // CMP_STRIDE
N_SELBLK = SEQ // SEL_LEN
D_IN_PAD = 23 * LANES
PROJ_TM = 512
SEL_GROUP = 4
DIL_UNROLL = 16
MOE_TM = 1024
MOE_PARTS = 1
ROUTER_ROWS = 80
FFN_ROWS = 512
FINAL_TM = 512
PACK_W = D_MODEL // 4
SC_WINDOW = 128
VMEM_LIMIT = 56 * 1024 * 1024

_NT = (((1,), (1,)), ((), ()))


def _lane_iota(shape):
    return lax.broadcasted_iota(jnp.int32, shape, 1)


def _row_iota(shape):
    return lax.broadcasted_iota(jnp.int32, shape, 0)


def _rope_lanes(y, c, s1, s2):
    return y * c + pltpu.roll(y, LANES - ROT_DIM // 2, axis=1) * s1 + pltpu.roll(y, ROT_DIM // 2, axis=1) * s2


def _proj_kernel(x_ref, g_ref, w_ref, c_ref, s1_ref, s2_ref,
                 aq_ref, ak_ref, av_ref, aq4_ref, ak4_ref, av4_ref, aq16_ref, ak16_ref, av16_ref,
                 bq_ref, kcv_ref, ksw_ref, gate_ref, nat_f32, d4_f32):
    x = x_ref[...]
    ms = jnp.mean(x * x, axis=-1, keepdims=True)
    h = (x * lax.rsqrt(ms + EPS) * g_ref[...]).astype(BF16)
    c = c_ref[...]
    s1 = s1_ref[...]
    s2 = s2_ref[...]

    def seg(lo, width):
        return jnp.dot(h, w_ref[:, lo:lo + width], preferred_element_type=F32)

    def store(dst, col, y, rope, scale, dilated=None):
        for j in range(y.shape[1] // LANES):
            yj = y[:, j * LANES:(j + 1) * LANES]
            if rope:
                yj = _rope_lanes(yj, c, s1, s2)
            if scale != 1.0:
                yj = yj * scale
            cols = slice(col + j * LANES, col + (j + 1) * LANES)
            dst[:, cols] = yj.astype(dst.dtype)
            if dilated is not None:
                d4_ref, d16_ref = dilated
                quarter = PROJ_TM // 4
                nat_f32[...] = yj
                for r in range(4):
                    part = nat_f32[pl.ds(r, quarter, stride=4), :]
                    d4_ref[r, :, cols] = part.astype(d4_ref.dtype)
                    d4_f32[r * quarter:(r + 1) * quarter, :] = part
                for r in range(4):
                    for a_ in range(4):
                        part = d4_f32[pl.ds(r * quarter + a_, quarter // 4, stride=4), :]
                        d16_ref[4 * a_ + r, :, cols] = part.astype(d16_ref.dtype)

    qscale = HEAD_DIM ** -0.5 * LOG2_E
    store(aq_ref, 0, seg(0, 512), True, qscale, (aq4_ref, aq16_ref))
    store(ak_ref, 0, seg(512, 512), True, 1.0, (ak4_ref, ak16_ref))
    store(av_ref, 0, seg(1024, 512), False, 1.0, (av4_ref, av16_ref))
    store(bq_ref, 0, seg(1536, 512), True, qscale)
    store(kcv_ref, 0, seg(2048, 256), False, 1.0)
    store(ksw_ref, 0, seg(2304, 128), True, 1.0)
    store(ksw_ref, 128, seg(2432, 128), False, 1.0)
    store(ksw_ref, 256, seg(2560, 128), True, 1.0)
    store(ksw_ref, 384, seg(2688, 128), False, 1.0)
    gate_ref[...] = jax.nn.sigmoid(seg(2816, 128))


def _proj(x2, g1, w_in_p, rc, rs1, rs2):
    t = x2.shape[0]
    nb = t // SEQ
    nblk_s = SEQ // PROJ_TM
    row = lambda i: (i, 0)
    pos = lambda i: (i % nblk_s, 0)
    const = lambda i: (0, 0)
    perm = lambda i: (i // nblk_s, 0, i % nblk_s, 0)
    wide = jax.ShapeDtypeStruct((t, 512), BF16)
    wide_spec = pl.BlockSpec((PROJ_TM, 512), row)
    d4 = jax.ShapeDtypeStruct((nb, 4, SEQ // 4, 512), BF16)
    d4_spec = pl.BlockSpec((None, 4, PROJ_TM // 4, 512), perm)
    d16 = jax.ShapeDtypeStruct((nb, 16, SEQ // 16, 512), BF16)
    d16_spec = pl.BlockSpec((None, 16, PROJ_TM // 16, 512), perm)
    outs = [wide] * 3 + [d4] * 3 + [d16] * 3 + [wide, jax.ShapeDtypeStruct((t, 256), BF16), wide,
                                               jax.ShapeDtypeStruct((t, LANES), F32)]
    out_specs = [wide_spec] * 3 + [d4_spec] * 3 + [d16_spec] * 3 + [
        wide_spec, pl.BlockSpec((PROJ_TM, 256), row), wide_spec, pl.BlockSpec((PROJ_TM, LANES), row)]
    return pl.pallas_call(
        _proj_kernel,
        grid=(t // PROJ_TM,),
        in_specs=[pl.BlockSpec((PROJ_TM, D_MODEL), row), pl.BlockSpec((1, D_MODEL), const),
                  pl.BlockSpec((D_MODEL, D_IN_PAD), const),
                  pl.BlockSpec((PROJ_TM, LANES), pos), pl.BlockSpec((PROJ_TM, LANES), pos),
                  pl.BlockSpec((PROJ_TM, LANES), pos)],
        out_specs=out_specs,
        out_shape=outs,
        scratch_shapes=[pltpu.VMEM((PROJ_TM, LANES), F32)] * 2,
        compiler_params=pltpu.CompilerParams(dimension_semantics=("arbitrary",),
                                             vmem_limit_bytes=VMEM_LIMIT),
        name="proj",
    )(x2, g1, w_in_p, rc, rs1, rs2)


def _dilated_kernel(q1_ref, k1_ref, v1_ref, q4_ref, k4_ref, v4_ref, q16_ref, k16_ref, v16_ref, o_ref,
                    qt, vt, op, lp, on0, on1, on2, ln0, ln1, ln2):
    c2 = _row_iota((2 * QBLK, 2 * QBLK))
    a2 = _lane_iota((2 * QBLK, 2 * QBLK)) % QBLK
    band = jnp.where((c2 >= a2) & (c2 <= a2 + QBLK), 0.0, NEG)
    band_noprev = jnp.where(c2 < QBLK, NEG, band)
    c1 = _row_iota((QBLK, 2 * QBLK))
    a1 = _lane_iota((QBLK, 2 * QBLK)) % QBLK
    band_own = jnp.where(c1 <= a1, 0.0, NEG)
    top_sq = _row_iota((LANES, QBLK)) < HEAD_DIM

    inputs = ((q1_ref, k1_ref, v1_ref), (q4_ref, k4_ref, v4_ref), (q16_ref, k16_ref, v16_ref))
    for pi, (window, dil) in enumerate(DIL_PATTERNS):
        seg_len = SEQ // dil
        nseg_blk = seg_len // QBLK
        q_ref, k_ref, v_ref = inputs[pi]

        def block(ref, j, d=None, nseg_blk=nseg_blk, dil=dil):
            if dil == 1:
                return ref[pl.ds(pl.multiple_of(j * QBLK, QBLK), QBLK), :]
            if d is None:
                return ref[j // nseg_blk, (j % nseg_blk) * QBLK:(j % nseg_blk + 1) * QBLK, :]
            within = d % nseg_blk
            return ref[j // nseg_blk, within * QBLK:(within + 1) * QBLK, :]

        for j in range(N_QBLK):
            qt[j] = block(q_ref, j).T if dil > 1 else q_ref[j * QBLK:(j + 1) * QBLK, :].T
            vt[j] = block(v_ref, j).T if dil > 1 else v_ref[j * QBLK:(j + 1) * QBLK, :].T

        def body(u, carry, nseg_blk=nseg_blk, k_ref=k_ref, block=block):
            js = [u * DIL_UNROLL + d for d in range(DIL_UNROLL)]
            prevs, scores = [], []
            for d, j in enumerate(js):
                qtb = qt[j]
                zero = jnp.zeros_like(qtb)
                q_both = jnp.concatenate([jnp.where(top_sq, qtb, zero), jnp.where(top_sq, zero, qtb)], axis=1)
                if nseg_blk == 1:
                    keys, bias, jp = block(k_ref, j, d), band_own, j
                else:
                    if nseg_blk <= DIL_UNROLL:
                        first = d % nseg_blk == 0
                        jp = j if first else j - 1
                        dp = d if first else d - 1
                        bias = band_noprev if first else band
                    else:
                        jp = jnp.maximum(j - 1, 0) if d == 0 else j - 1
                        dp = d - 1
                        bias = jnp.where(u == 0, band_noprev, band) if d == 0 else band
                    keys = jnp.concatenate([block(k_ref, jp, dp), block(k_ref, j, d)], axis=0)
                prevs.append(jp)
                scores.append(jnp.dot(keys, q_both, preferred_element_type=F32) + bias)
            stats = []
            for s in scores:
                m = jnp.max(s, axis=0, keepdims=True)
                p = jnp.exp2(s - m)
                stats.append((m, jnp.sum(p, axis=0, keepdims=True), p.astype(BF16)))
            outs = []
            for j, jp, (m, l, p) in zip(js, prevs, stats):
                v_span = vt[j] if nseg_blk == 1 else jnp.concatenate([vt[jp], vt[j]], axis=1)
                outs.append(jnp.dot(v_span, p, preferred_element_type=F32))
            for j, (m, l, p), res in zip(js, stats, outs):
                row0 = pl.multiple_of(j * QBLK, QBLK)
                lse = m + jnp.log2(l)
                o_t = jnp.where(top_sq, res[:, 0:QBLK] / l[:, 0:QBLK], res[:, QBLK:] / l[:, QBLK:])
                lse_t = jnp.where(top_sq, jnp.broadcast_to(lse[:, 0:QBLK], (LANES, QBLK)),
                                  jnp.broadcast_to(lse[:, QBLK:], (LANES, QBLK)))
                op[pl.ds(row0, QBLK), :] = o_t.T
                lp[pl.ds(row0, QBLK), :] = lse_t.T
            return carry

        lax.fori_loop(0, N_QBLK // DIL_UNROLL, body, 0)

        for r in range(dil):
            dst = pl.ds(r, seg_len, stride=dil) if dil > 1 else pl.ds(0, SEQ)
            src = pl.ds(r * seg_len, seg_len)
            (on0, on1, on2)[pi][dst, :] = op[src, :]
            (ln0, ln1, ln2)[pi][dst, :] = lp[src, :]

    l0, l1, l2 = ln0[...], ln1[...], ln2[...]
    mx = jnp.maximum(jnp.maximum(l0, l1), l2)
    e0, e1, e2 = jnp.exp2(l0 - mx), jnp.exp2(l1 - mx), jnp.exp2(l2 - mx)
    den = e0 + e1 + e2
    out = (e0 / den) * on0[...] + (e1 / den) * on1[...] + (e2 / den) * on2[...]
    o_ref[...] = out.astype(o_ref.dtype)


def _dilated(a1, a4, a16):
    t = a1[0].shape[0]
    nb = t // SEQ
    spec1 = pl.BlockSpec((SEQ, LANES), lambda b, hp: (b, hp))
    spec4 = pl.BlockSpec((None, 4, SEQ // 4, LANES), lambda b, hp: (b, 0, 0, hp))
    spec16 = pl.BlockSpec((None, 16, SEQ // 16, LANES), lambda b, hp: (b, 0, 0, hp))
    return pl.pallas_call(
        _dilated_kernel,
        grid=(nb, N_HEADS_DIL // 2),
        in_specs=[spec1] * 3 + [spec4] * 3 + [spec16] * 3,
        out_specs=spec1,
        out_shape=jax.ShapeDtypeStruct((t, 512), BF16),
        scratch_shapes=[pltpu.VMEM((N_QBLK, LANES, QBLK), BF16), pltpu.VMEM((N_QBLK, LANES, QBLK), BF16)]
        + [pltpu.VMEM((SEQ, LANES), F32)] * 8,
        compiler_params=pltpu.CompilerParams(dimension_semantics=("arbitrary", "arbitrary"),
                                             vmem_limit_bytes=VMEM_LIMIT),
        name="dilated",
    )(*a1, *a4, *a16)


def _gelu_tanh(x):
    return 0.5 * x * (1.0 + jnp.tanh(np.sqrt(2.0 / np.pi).astype(np.float32) * (x + 0.044715 * (x * x * x))))


def _softmax_step(state, s, v_t):
    m, l, acc = state
    mn = jnp.maximum(m, jnp.max(s, axis=0, keepdims=True))
    al = jnp.exp2(m - mn)
    p = jnp.exp2(s - mn)
    l = al * l + jnp.sum(p, axis=0, keepdims=True)
    acc = al * acc + jnp.dot(v_t, p.astype(BF16), preferred_element_type=F32)
    return mn, l, acc


def _nsa_kernel(q_ref, kflat_ref, vflat_ref, ksw_ref, gate_ref,
                wk1_ref, wk2d_ref, pek_ref, wv1_ref, wv2t_ref, pev_ref,
                cc_ref, cs1_ref, cs2_ref, ovl_ref, exp_ref,
                o_ref,
                qt, vst, vwt, gtt, kc2, vct):
    g = pl.program_id(1)
    nq = NSA_REP * QBLK

    for i in range(N_QBLK):
        rows = pl.ds(i * QBLK, QBLK)
        qt[i, 0:LANES, :] = q_ref[rows, 0:LANES].T
        qt[i, LANES:2 * LANES, :] = q_ref[rows, LANES:2 * LANES].T
        vs_t = ksw_ref[rows, LANES:2 * LANES].T
        vst[i] = jnp.where(g == 0, vs_t[0:HEAD_DIM], vs_t[HEAD_DIM:])
        vw_t = ksw_ref[rows, 3 * LANES:4 * LANES].T
        vwt[i] = jnp.where(g == 0, vw_t[0:HEAD_DIM], vw_t[HEAD_DIM:])
        gt = gate_ref[rows, :]
        gt = jnp.where(g == 0, gt, pltpu.roll(gt, LANES - 3 * NSA_REP, axis=1))
        gtt[i] = gt.T[0:16, :]

    def hidden(flat_ref, w1_ref, pe_ref):
        ab = jnp.dot(flat_ref[...], w1_ref[...], preferred_element_type=F32)
        pb = (jnp.dot(pe_ref[0:16, :], w1_ref[:, 0:CMP_HIDDEN], preferred_element_type=F32)
              + jnp.dot(pe_ref[16:32, :], w1_ref[:, CMP_HIDDEN:], preferred_element_type=F32))
        hid = ab[:, 0:CMP_HIDDEN] + pltpu.roll(ab[:, CMP_HIDDEN:], N_CMP - 1, axis=0) + pb[0:1, :]
        return _gelu_tanh(hid).astype(BF16)

    hk = hidden(kflat_ref, wk1_ref, pek_ref)
    kc2[...] = _rope_lanes(jnp.dot(hk, wk2d_ref[...], preferred_element_type=F32),
                           cc_ref[...], cs1_ref[...], cs2_ref[...]).astype(BF16)
    hv = hidden(vflat_ref, wv1_ref, pev_ref)
    vct[...] = lax.dot_general(wv2t_ref[...], hv, _NT, preferred_element_type=F32).astype(BF16)

    key = _row_iota((QBLK, nq))
    qry = _lane_iota((QBLK, nq)) % QBLK
    causal = jnp.where(key <= qry, 0.0, NEG)
    upper = jnp.where(key > qry, 0.0, NEG)
    blk = _row_iota((N_SELBLK, QBLK))
    qry1 = _lane_iota((N_SELBLK, QBLK))
    grp_keys = SEL_GROUP * QBLK
    win_blocks = WIN // QBLK

    def qgroup(w, full_window):
        blocks = [SEL_GROUP * w + d for d in range(SEL_GROUP)]
        row0s = [pl.multiple_of(i * QBLK, QBLK) for i in blocks]

        q_pads = []
        for i in blocks:
            q_t = jnp.concatenate([qt[i, r * HEAD_DIM:(r + 1) * HEAD_DIM, :] for r in range(NSA_REP)], axis=1)
            zero = jnp.zeros_like(q_t)
            q_pads.append(jnp.where(g == 0, jnp.concatenate([q_t, zero], axis=0),
                                    jnp.concatenate([zero, q_t], axis=0)))
        s_cmp = [jnp.dot(kc2[...], qp, preferred_element_type=F32) for qp in q_pads]
        s_win, v_win = [], []
        for d, (i, qp) in enumerate(zip(blocks, q_pads)):
            if full_window:
                first = i - win_blocks
                keys = ksw_ref[pl.ds(pl.multiple_of(first * QBLK, QBLK), (win_blocks + 1) * QBLK), 2 * LANES:3 * LANES]
                s = jnp.dot(keys, qp, preferred_element_type=F32)
                s = jnp.concatenate([s[0:QBLK] + upper, s[QBLK:win_blocks * QBLK], s[win_blocks * QBLK:] + causal],
                                    axis=0)
                v_t = jnp.concatenate([vwt[first + e] for e in range(win_blocks + 1)], axis=1)
            else:
                s = jnp.dot(ksw_ref[0:(d + 1) * QBLK, 2 * LANES:3 * LANES], qp, preferred_element_type=F32)
                s = jnp.concatenate([s[0:d * QBLK], s[d * QBLK:] + causal], axis=0) if d else s + causal
                v_t = jnp.concatenate([vwt[e] for e in range(d + 1)], axis=1) if d else vwt[0]
            s_win.append(s)
            v_win.append(v_t)

        pcs = []
        for row0, s in zip(row0s, s_cmp):
            cvalid = (CMP_STRIDE * key + (CMP_LEN - 1) <= row0 + qry) & (key < N_CMP - 1)
            s = jnp.where(cvalid, s, NEG)
            m = jnp.max(s, axis=0, keepdims=True)
            p = jnp.where(cvalid, jnp.exp2(s - m), 0.0)
            l = jnp.sum(p, axis=0, keepdims=True)
            pcs.append(p / jnp.where(l > 0, l, 1.0))
        o_cmp = [jnp.dot(vct[...], pc.astype(BF16), preferred_element_type=F32) for pc in pcs]

        ovl = ovl_ref[...]
        imps = []
        for pc in pcs:
            psum = pc[:, 0:QBLK] + pc[:, QBLK:2 * QBLK] + pc[:, 2 * QBLK:3 * QBLK] + pc[:, 3 * QBLK:]
            p_hi = psum.astype(BF16)
            p_lo = (psum - p_hi.astype(F32)).astype(BF16)
            imps.append(jnp.dot(ovl, p_hi, preferred_element_type=F32) + jnp.dot(ovl, p_lo, preferred_element_type=F32))

        o_win = []
        for s, v_t in zip(s_win, v_win):
            m = jnp.max(s, axis=0, keepdims=True)
            p = jnp.exp2(s - m)
            l = jnp.sum(p, axis=0, keepdims=True)
            o_win.append(jnp.dot(v_t, p.astype(BF16), preferred_element_type=F32) / l)

        q_sel = []
        for row0, imp, qp in zip(row0s, imps, q_pads):
            t1 = row0 + qry1
            blk_t = t1 // SEL_LEN
            forced = (blk == 0) | (blk == blk_t) | (blk == blk_t - 1)
            v = jnp.where(blk * SEL_LEN <= t1, jnp.where(forced, BIG, imp), -BIG)
            selm = jnp.zeros((N_SELBLK, QBLK), F32)
            for _ in range(N_SEL):
                mx = jnp.max(v, axis=0, keepdims=True)
                first_hit = jnp.min(jnp.where(v == mx, blk, N_SELBLK), axis=0, keepdims=True)
                hit = blk == first_hit
                selm = jnp.where(hit, 1.0, selm)
                v = jnp.where(hit, LOWEST, v)
            notsel = jnp.concatenate([selm - 1.0, jnp.zeros((LANES - N_SELBLK, QBLK), F32)], axis=0).astype(BF16)
            q_sel.append(jnp.concatenate([qp, jnp.concatenate([notsel] * NSA_REP, axis=1)], axis=0))

        def sel_keys(t, n_rows):
            rows_t = pl.ds(pl.multiple_of(t * grp_keys, grp_keys), n_rows)
            return jnp.concatenate([ksw_ref[rows_t, 0:LANES], exp_ref[t, 0:n_rows, :]], axis=1)

        def sel_values(t, n_blk):
            return jnp.concatenate([vst[t * SEL_GROUP + e] for e in range(n_blk)], axis=1) if n_blk > 1 else vst[t * SEL_GROUP]

        wide = SEL_GROUP * nq
        st = (jnp.full((1, wide), LOWEST, F32), jnp.zeros((1, wide), F32), jnp.zeros((HEAD_DIM, wide), F32))
        if full_window:
            q_all = jnp.concatenate(q_sel, axis=1)

            def sel_body(t, st):
                s = jnp.dot(sel_keys(t, grp_keys), q_all, preferred_element_type=F32)
                return _softmax_step(st, s, sel_values(t, SEL_GROUP))

            for t_ in range(w):
                st = sel_body(t_, st)
        o_sel = []
        for d, qs in enumerate(q_sel):
            s = jnp.dot(sel_keys(w, (d + 1) * QBLK), qs, preferred_element_type=F32)
            s = jnp.concatenate([s[0:d * QBLK], s[d * QBLK:] + causal], axis=0) if d else s + causal
            lanes = slice(d * nq, (d + 1) * nq)
            m, l, acc = _softmax_step((st[0][:, lanes], st[1][:, lanes], st[2][:, lanes]), s, sel_values(w, d + 1))
            o_sel.append(acc / l)

        for n, (i, row0) in enumerate(zip(blocks, row0s)):
            gti = gtt[i]

            def gate(br):
                return jnp.concatenate([gti[r * 3 + br:r * 3 + br + 1, :] for r in range(NSA_REP)], axis=1)

            out_t = gate(0) * o_cmp[n] + gate(1) * o_sel[n] + gate(2) * o_win[n]
            for pair in range(NSA_REP // 2):
                both = jnp.concatenate([out_t[:, (2 * pair) * QBLK:(2 * pair + 1) * QBLK],
                                        out_t[:, (2 * pair + 1) * QBLK:(2 * pair + 2) * QBLK]], axis=0)
                o_ref[pl.ds(row0, QBLK), pair * LANES:(pair + 1) * LANES] = both.T.astype(o_ref.dtype)

    qgroup(0, False)
    for w in range(1, N_QBLK // SEL_GROUP):
        qgroup(w, True)


def _nsa(bq, kvflat, ksw, gates, wk1, wk2d, pek, wv1, wv2t, pev, cc, cs1, cs2, ovl_t, expand_t):
    t = bq.shape[0]
    nb = t // SEQ
    const2 = lambda b, g: (0, 0)
    return pl.pallas_call(
        _nsa_kernel,
        grid=(nb, N_KV_NSA),
        in_specs=[
            pl.BlockSpec((SEQ, 2 * LANES), lambda b, g: (b, g)),
            pl.BlockSpec((None, None, N_CMP, CMP_STRIDE * HEAD_DIM), lambda b, g: (b, g, 0, 0)),
            pl.BlockSpec((None, None, N_CMP, CMP_STRIDE * HEAD_DIM), lambda b, g: (b, N_KV_NSA + g, 0, 0)),
            pl.BlockSpec((SEQ, 4 * LANES), lambda b, g: (b, 0)),
            pl.BlockSpec((SEQ, LANES), lambda b, g: (b, 0)),
            pl.BlockSpec((CMP_STRIDE * HEAD_DIM, 2 * CMP_HIDDEN), const2),
            pl.BlockSpec((CMP_HIDDEN, LANES), const2),
            pl.BlockSpec((32, CMP_STRIDE * HEAD_DIM), const2),
            pl.BlockSpec((CMP_STRIDE * HEAD_DIM, 2 * CMP_HIDDEN), const2),
            pl.BlockSpec((HEAD_DIM, CMP_HIDDEN), const2),
            pl.BlockSpec((32, CMP_STRIDE * HEAD_DIM), const2),
            pl.BlockSpec((N_CMP, LANES), const2),
            pl.BlockSpec((N_CMP, LANES), const2),
            pl.BlockSpec((N_CMP, LANES), const2),
            pl.BlockSpec((N_SELBLK, LANES), const2),
            pl.BlockSpec((N_QBLK // SEL_GROUP, SEL_GROUP * QBLK, LANES), lambda b, g: (0, 0, 0)),
        ],
        out_specs=pl.BlockSpec((SEQ, 2 * LANES), lambda b, g: (b, g)),
        out_shape=jax.ShapeDtypeStruct((t, 512), BF16),
        scratch_shapes=[pltpu.VMEM((N_QBLK, 2 * LANES, QBLK), BF16),
                        pltpu.VMEM((N_QBLK, HEAD_DIM, QBLK), BF16),
                        pltpu.VMEM((N_QBLK, HEAD_DIM, QBLK), BF16),
                        pltpu.VMEM((N_QBLK, 16, QBLK), F32),
                        pltpu.VMEM((N_CMP, LANES), BF16),
                        pltpu.VMEM((HEAD_DIM, N_CMP), BF16)],
        compiler_params=pltpu.CompilerParams(dimension_semantics=("arbitrary", "arbitrary"),
                                             vmem_limit_bytes=VMEM_LIMIT),
        name="nsa",
    )(bq, kvflat, kvflat, ksw, gates, wk1, wk2d, pek, wv1, wv2t, pev, cc, cs1, cs2, ovl_t, expand_t)


def _pack_bf16_pairs(lo, hi):
    lo_b = lax.bitcast_convert_type(lo.astype(BF16).astype(F32), jnp.uint32)
    hi_b = lax.bitcast_convert_type(hi.astype(BF16).astype(F32), jnp.uint32)
    return (hi_b & jnp.uint32(0xFFFF0000)) | (lo_b >> 16)


def _unpack_bf16_pairs(p):
    lo = lax.bitcast_convert_type(p << 16, F32)
    hi = lax.bitcast_convert_type(p & jnp.uint32(0xFFFF0000), F32)
    return lo, hi


def _post_kernel(x_ref, ma_ref, mb_ref, wo_ref, g2_ref, wr_ref, br_ref, tri_ref,
                 x1_ref, ha_ref, hb_ref, e1_ref, e2_ref, r1_ref, r2_ref, w_ref, cnt_ref):
    x1 = (x_ref[...] + jnp.dot(ma_ref[...], wo_ref[0:512, :], preferred_element_type=F32)
          + jnp.dot(mb_ref[...], wo_ref[512:1024, :], preferred_element_type=F32))
    x1_ref[...] = x1
    ms = jnp.mean(x1 * x1, axis=-1, keepdims=True)
    h2 = x1 * lax.rsqrt(ms + EPS) * g2_ref[...]
    ha_ref[...] = _pack_bf16_pairs(h2[:, 0:PACK_W], h2[:, 2 * PACK_W:3 * PACK_W])
    hb_ref[...] = _pack_bf16_pairs(h2[:, PACK_W:2 * PACK_W], h2[:, 3 * PACK_W:])

    h_hi = h2.astype(BF16)
    h_lo = (h2 - h_hi.astype(F32)).astype(BF16)
    wr = wr_ref[...]
    w_hi = wr.astype(BF16)
    w_lo = (wr - w_hi.astype(F32)).astype(BF16)
    lg = (lax.dot_general(w_hi, h_hi, _NT, preferred_element_type=F32)
          + lax.dot_general(w_hi, h_lo, _NT, preferred_element_type=F32)
          + lax.dot_general(w_lo, h_hi, _NT, preferred_element_type=F32)) + br_ref[:, 0:1]
    gl = lg[0:N_GROUPS, :]
    sub = _row_iota(gl.shape)
    gmax = jnp.max(gl, axis=0, keepdims=True)
    p_g = 1.0 / jnp.sum(jnp.exp(gl - gmax), axis=0, keepdims=True)
    g_star = jnp.min(jnp.where(gl == gmax, sub, N_GROUPS), axis=0, keepdims=True)
    el = jnp.zeros_like(gl)
    for gi in range(N_GROUPS):
        lo = N_GROUPS + gi * EXPERTS_PER_GROUP
        el = el + jnp.where(g_star == gi, lg[lo:lo + EXPERTS_PER_GROUP, :], 0.0)
    emax = jnp.max(el, axis=0, keepdims=True)
    ee = jnp.exp(el - emax)
    pe = ee / jnp.sum(ee, axis=0, keepdims=True)
    v1 = jnp.max(pe, axis=0, keepdims=True)
    i1 = jnp.min(jnp.where(pe == v1, sub, EXPERTS_PER_GROUP), axis=0, keepdims=True)
    pe2 = jnp.where(sub == i1, -1.0, pe)
    v2 = jnp.max(pe2, axis=0, keepdims=True)
    i2 = jnp.min(jnp.where(pe2 == v2, sub, EXPERTS_PER_GROUP), axis=0, keepdims=True)
    tot = v1 + v2
    e1 = g_star * EXPERTS_PER_GROUP + i1
    e2 = g_star * EXPERTS_PER_GROUP + i2
    e1_ref[0] = e1
    e2_ref[0] = e2

    wslab = jnp.concatenate([v1 / tot * p_g, v2 / tot * p_g, jnp.zeros((LANES - 2, MOE_TM), F32)], axis=0)
    for j in range(MOE_TM // LANES):
        w_ref[j * LANES:(j + 1) * LANES, :] = wslab[:, j * LANES:(j + 1) * LANES].T

    n_exp = N_GROUPS * EXPERTS_PER_GROUP
    sub_e = _row_iota((n_exp, MOE_TM))
    hit1 = sub_e == e1
    hit2 = sub_e == e2
    assigned = jnp.where(hit1, 1.0, 0.0) + jnp.where(hit2, 1.0, 0.0)
    before = jnp.dot(assigned.astype(BF16), tri_ref[...], preferred_element_type=F32)
    r1_ref[0] = jnp.sum(jnp.where(hit1, before, 0.0), axis=0, keepdims=True).astype(jnp.int32)
    r2_ref[0] = jnp.sum(jnp.where(hit2, before, 0.0), axis=0, keepdims=True).astype(jnp.int32)
    cnt = jnp.sum(assigned, axis=1, keepdims=True).astype(jnp.int32)
    cnt_ref[0] = jnp.broadcast_to(cnt, (n_exp, LANES))


def _post(x2, mix_a, mix_b, w_o, g2, wr_t, br, tri, part):
    t = x2.shape[0] // MOE_PARTS
    nt = t // MOE_TM
    first = part * nt
    n_exp = N_GROUPS * EXPERTS_PER_GROUP
    row = lambda i: (i, 0)
    const = lambda i: (0, 0)
    tok = lambda i: (i, 0, 0)
    src = lambda i: (first + i, 0)
    tok_spec = pl.BlockSpec((1, 1, MOE_TM), tok)
    tok_shape = jax.ShapeDtypeStruct((nt, 1, MOE_TM), jnp.int32)
    return pl.pallas_call(
        _post_kernel,
        grid=(nt,),
        in_specs=[pl.BlockSpec((MOE_TM, D_MODEL), src), pl.BlockSpec((MOE_TM, 512), src),
                  pl.BlockSpec((MOE_TM, 512), src), pl.BlockSpec((D_MODEL, D_MODEL), const),
                  pl.BlockSpec((1, D_MODEL), const), pl.BlockSpec((ROUTER_ROWS, D_MODEL), const),
                  pl.BlockSpec((ROUTER_ROWS, LANES), const), pl.BlockSpec((MOE_TM, MOE_TM), const)],
        out_specs=[pl.BlockSpec((MOE_TM, D_MODEL), row),
                   pl.BlockSpec((MOE_TM, PACK_W), row), pl.BlockSpec((MOE_TM, PACK_W), row),
                   tok_spec, tok_spec, tok_spec, tok_spec,
                   pl.BlockSpec((MOE_TM, LANES), row),
                   pl.BlockSpec((1, n_exp, LANES), tok)],
        out_shape=[jax.ShapeDtypeStruct((t, D_MODEL), F32),
                   jax.ShapeDtypeStruct((t, PACK_W), jnp.uint32), jax.ShapeDtypeStruct((t, PACK_W), jnp.uint32),
                   tok_shape, tok_shape, tok_shape, tok_shape,
                   jax.ShapeDtypeStruct((t, LANES), F32),
                   jax.ShapeDtypeStruct((nt, n_exp, LANES), jnp.int32)],
        compiler_params=pltpu.CompilerParams(dimension_semantics=("arbitrary",),
                                             vmem_limit_bytes=VMEM_LIMIT),
        name="post",
    )(x2, mix_a, mix_b, w_o, g2, wr_t, br, tri)


def _sc_mesh():
    return plsc.VectorSubcoreMesh(core_axis_name="core", subcore_axis_name="subcore")


def _sc_scatter_rows(x, pos1, pos2, n_out):
    r, c = x.shape

    @pl.kernel(out_type=jax.ShapeDtypeStruct((n_out, c), x.dtype), mesh=_sc_mesh(), scratch_types=[])
    def scatter_kernel(x_hbm, p1_hbm, p2_hbm, o_hbm):
        def body(x_vmem, p1_vmem, p2_vmem):
            pltpu.sync_copy(x_vmem, o_hbm.at[p1_vmem.at[0]])
            pltpu.sync_copy(x_vmem, o_hbm.at[p2_vmem.at[0]])

        idx = pl.BlockSpec((1, SC_WINDOW), lambda i: (0, i))
        pltpu.emit_pipeline(body, grid=(r // SC_WINDOW,),
                            in_specs=[pl.BlockSpec((SC_WINDOW, c), lambda i: (i, 0)), idx, idx], out_specs=[],
                            core_axis_name=("core", "subcore"),
                            dimension_semantics=(pltpu.PARALLEL,))(x_hbm, p1_hbm, p2_hbm)

    return scatter_kernel(x, pos1.reshape(1, r), pos2.reshape(1, r))


def _sc_gather_rows(y, idx):
    n = idx.shape[0]
    c = y.shape[1]

    @pl.kernel(out_type=jax.ShapeDtypeStruct((n, c), y.dtype), mesh=_sc_mesh(), scratch_types=[])
    def gather_kernel(y_hbm, i_hbm, o_hbm):
        def body(i_vmem, o_vmem):
            pltpu.sync_copy(y_hbm.at[i_vmem.at[0]], o_vmem)

        pltpu.emit_pipeline(body, grid=(n // SC_WINDOW,),
                            in_specs=[pl.BlockSpec((1, SC_WINDOW), lambda i: (0, i))],
                            out_specs=[pl.BlockSpec((SC_WINDOW, c), lambda i: (i, 0))],
                            core_axis_name=("core", "subcore"),
                            dimension_semantics=(pltpu.PARALLEL,))(i_hbm, o_hbm)

    return gather_kernel(y, idx.reshape(1, n))


def _ffn_kernel(ce_ref, cv_ref, xa_ref, xb_ref, wg_ref, wu_ref, wd_ref, ya_ref, yb_ref, wg_bf, wu_bf, wd_bf):
    c = pl.program_id(0)
    prev = ce_ref[jnp.maximum(c - 1, 0)]

    @pl.when((c == 0) | (ce_ref[c] != prev))
    def _():
        wg_bf[...] = wg_ref[0].astype(BF16)
        wu_bf[...] = wu_ref[0].astype(BF16)
        wd_bf[...] = wd_ref[0].astype(BF16)

    @pl.when(cv_ref[c] == 0)
    def _():
        ya_ref[...] = jnp.zeros_like(ya_ref)
        yb_ref[...] = jnp.zeros_like(yb_ref)

    @pl.when(cv_ref[c] != 0)
    def _():
        a_lo, a_hi = _unpack_bf16_pairs(xa_ref[...])
        b_lo, b_hi = _unpack_bf16_pairs(xb_ref[...])
        xs = jnp.concatenate([a_lo, b_lo, a_hi, b_hi], axis=1).astype(BF16)
        gate = jnp.dot(xs, wg_bf[...], preferred_element_type=F32)
        up = jnp.dot(xs, wu_bf[...], preferred_element_type=F32)
        he = (gate * jax.nn.sigmoid(gate) * up).astype(BF16)
        y = jnp.dot(he, wd_bf[...], preferred_element_type=F32)
        ya_ref[...] = _pack_bf16_pairs(y[:, 0:PACK_W], y[:, 2 * PACK_W:3 * PACK_W])
        yb_ref[...] = _pack_bf16_pairs(y[:, PACK_W:2 * PACK_W], y[:, 3 * PACK_W:])


def _ffn(chunk_expert, chunk_valid, xsa, xsb, wg, wu, wd):
    n_pad = xsa.shape[0]
    rows = pl.BlockSpec((FFN_ROWS, PACK_W), lambda c, ce, cv: (c, 0))
    grid_spec = pltpu.PrefetchScalarGridSpec(
        num_scalar_prefetch=2,
        grid=(n_pad // FFN_ROWS,),
        in_specs=[rows, rows,
                  pl.BlockSpec((1, D_MODEL, D_FF_EXPERT), lambda c, ce, cv: (ce[c], 0, 0)),
                  pl.BlockSpec((1, D_MODEL, D_FF_EXPERT), lambda c, ce, cv: (ce[c], 0, 0)),
                  pl.BlockSpec((1, D_FF_EXPERT, D_MODEL), lambda c, ce, cv: (ce[c], 0, 0))],
        out_specs=[rows, rows],
        scratch_shapes=[pltpu.VMEM((D_MODEL, D_FF_EXPERT), BF16), pltpu.VMEM((D_MODEL, D_FF_EXPERT), BF16),
                        pltpu.VMEM((D_FF_EXPERT, D_MODEL), BF16)],
    )
    out = jax.ShapeDtypeStruct((n_pad, PACK_W), jnp.uint32)
    return pl.pallas_call(
        _ffn_kernel,
        grid_spec=grid_spec,
        out_shape=[out, out],
        compiler_params=pltpu.CompilerParams(dimension_semantics=("arbitrary",),
                                             vmem_limit_bytes=VMEM_LIMIT),
        name="ffn",
    )(chunk_expert, chunk_valid, xsa, xsb, wg, wu, wd)


def _final_kernel(x1_ref, y1a_ref, y1b_ref, y2a_ref, y2b_ref, w_ref, gf_ref, o_ref):
    w1 = w_ref[:, 0:1]
    w2 = w_ref[:, 1:2]
    a1_lo, a1_hi = _unpack_bf16_pairs(y1a_ref[...])
    b1_lo, b1_hi = _unpack_bf16_pairs(y1b_ref[...])
    a2_lo, a2_hi = _unpack_bf16_pairs(y2a_ref[...])
    b2_lo, b2_hi = _unpack_bf16_pairs(y2b_ref[...])
    y = jnp.concatenate([w1 * a1_lo + w2 * a2_lo, w1 * b1_lo + w2 * b2_lo,
                         w1 * a1_hi + w2 * a2_hi, w1 * b1_hi + w2 * b2_hi], axis=1)
    xo = x1_ref[...] + y
    ms = jnp.mean(xo * xo, axis=-1, keepdims=True)
    o_ref[...] = xo * lax.rsqrt(ms + EPS) * gf_ref[...]


def _final(x1, yga, ygb, wtok, gf, out_prev, part):
    t = x1.shape[0]
    first = part * (t // FINAL_TM)
    row = lambda i: (i, 0)
    one = pl.BlockSpec((None, FINAL_TM, PACK_W), lambda i: (0, i, 0))
    two = pl.BlockSpec((None, FINAL_TM, PACK_W), lambda i: (1, i, 0))
    in_specs = [pl.BlockSpec((FINAL_TM, D_MODEL), row), one, one, two, two,
                pl.BlockSpec((FINAL_TM, LANES), row), pl.BlockSpec((1, D_MODEL), lambda i: (0, 0))]
    args = [x1, yga, ygb, yga, ygb, wtok, gf]
    kern = _final_kernel
    aliases = {}
    if out_prev is not None:
        in_specs.append(pl.BlockSpec(memory_space=pl.ANY))
        args.append(out_prev)
        aliases = {len(args) - 1: 0}
        kern = lambda *refs: _final_kernel(*refs[:7], refs[8])
    return pl.pallas_call(
        kern,
        grid=(t // FINAL_TM,),
        in_specs=in_specs,
        out_specs=pl.BlockSpec((FINAL_TM, D_MODEL), lambda i: (first + i, 0)),
        out_shape=jax.ShapeDtypeStruct((t * MOE_PARTS, D_MODEL), F32),
        input_output_aliases=aliases,
        compiler_params=pltpu.CompilerParams(dimension_semantics=("arbitrary",),
                                             vmem_limit_bytes=VMEM_LIMIT),
        name="final",
    )(*args)


def _moe_plan(cnt, e1, e2, r1, r2):
    nt, n_exp = cnt.shape
    tot = jnp.sum(cnt, axis=0)
    padded = (tot + FFN_ROWS - 1) // FFN_ROWS * FFN_ROWS
    seg_end = jnp.cumsum(padded)
    seg_start = seg_end - padded
    off = seg_start[None, :] + jnp.cumsum(cnt, axis=0) - cnt
    experts = jnp.arange(n_exp, dtype=jnp.int32)

    def lookup(e):
        return jnp.sum(jnp.where(e[:, :, None] == experts, off[:, None, :], 0), axis=2)

    pos1 = (lookup(e1) + r1).reshape(-1).astype(jnp.int32)
    pos2 = (lookup(e2) + r2).reshape(-1).astype(jnp.int32)
    n_chunks = (nt * MOE_TM * 2 + n_exp * FFN_ROWS) // FFN_ROWS
    cstart = jnp.arange(n_chunks, dtype=jnp.int32) * FFN_ROWS
    ce = jnp.sum((seg_end[None, :] <= cstart[:, None]).astype(jnp.int32), axis=1)
    cec = jnp.minimum(ce, n_exp - 1)
    seg_used = seg_start + tot
    valid = jnp.any((ce[:, None] == experts[None, :]) & (cstart[:, None] < seg_used[None, :]), axis=1)
    return pos1, pos2, cec.astype(jnp.int32), valid.astype(jnp.int32)


def _rope_tables(pos):
    half = ROT_DIM // 2
    inv_freq = ROPE_THETA ** (-jnp.arange(0, ROT_DIM, 2, dtype=F32) / ROT_DIM)
    ang = pos.astype(F32)[:, None] * inv_freq[None, :]
    cos, sin = jnp.cos(ang), jnp.sin(ang)
    l64 = np.arange(LANES) % HEAD_DIM
    f = l64 % half
    first = jnp.asarray(l64 < half)[None, :]
    second = jnp.asarray((l64 >= half) & (l64 < ROT_DIM))[None, :]
    c = jnp.where(first | second, cos[:, f], 1.0)
    s1 = jnp.where(first, -sin[:, f], 0.0)
    s2 = jnp.where(second, sin[:, f], 0.0)
    return c.astype(F32), s1.astype(F32), s2.astype(F32)


def _overlap_table_t():
    cs = np.arange(N_CMP)[None, :] * CMP_STRIDE
    js = np.arange(N_SELBLK)[:, None] * SEL_LEN
    ov = np.clip(np.minimum(cs + CMP_LEN, js + SEL_LEN) - np.maximum(cs, js), 0, None) / CMP_LEN
    ov[:, N_CMP - 1] = 0.0
    return ov.astype(np.float32)


def _expand_table_t():
    out = np.zeros((SEQ, LANES), np.float32)
    out[np.arange(SEQ), np.arange(SEQ) // SEL_LEN] = -NEG
    return out.reshape(N_QBLK // SEL_GROUP, SEL_GROUP * QBLK, LANES)


def kernel(x, norm1_g, w_in, pe_kc, w_kc1, w_kc2, pe_vc, w_vc1, w_vc2, w_o, norm2_g, w_rg, b_rg, w_re, b_re,
           w_gate, w_up, w_down, norm_f_g):
    nb, s, d = x.shape
    assert (s, d) == (SEQ, D_MODEL) and norm1_g.shape[0] == 1
    t = nb * s
    x2 = x.reshape(t, d)

    w_in_p = jnp.pad(w_in[0], ((0, 0), (0, D_IN_PAD - w_in.shape[2]))).astype(BF16)
    rc, rs1, rs2 = _rope_tables(jnp.arange(SEQ))
    cc, cs1, cs2 = _rope_tables(jnp.arange(N_CMP) * CMP_STRIDE + CMP_LEN - 1)
    half_flat = CMP_STRIDE * HEAD_DIM

    def cmp_w1(w1):
        return jnp.concatenate([w1[:half_flat], w1[half_flat:]], axis=1).astype(BF16)

    def cmp_pe(pe):
        rows = pe.reshape(2, half_flat)
        return jnp.concatenate([jnp.broadcast_to(rows[0:1], (16, half_flat)),
                                jnp.broadcast_to(rows[1:2], (16, half_flat))], axis=0).astype(BF16)

    wk2d = jnp.concatenate([w_kc2[0], w_kc2[0]], axis=1).astype(BF16)
    wv2t = w_vc2[0].T.astype(BF16)
    ovl_t = jnp.asarray(_overlap_table_t(), BF16)
    expand_t = jnp.asarray(_expand_table_t(), BF16)

    (aq, ak, av, aq4, ak4, av4, aq16, ak16, av16,
     bq, kcv, ksw, gates) = _proj(x2, norm1_g, w_in_p, rc, rs1, rs2)
    mix_a = _dilated((aq, ak, av), (aq4, ak4, av4), (aq16, ak16, av16))

    kvflat = kcv.reshape(nb, N_CMP, CMP_STRIDE, 2 * N_KV_NSA, HEAD_DIM).transpose(0, 3, 1, 2, 4)
    kvflat = kvflat.reshape(nb, 2 * N_KV_NSA, N_CMP, half_flat)
    mix_b = _nsa(bq, kvflat, ksw, gates, cmp_w1(w_kc1[0]), wk2d, cmp_pe(pe_kc[0]),
                 cmp_w1(w_vc1[0]), wv2t, cmp_pe(pe_vc[0]), cc, cs1, cs2, ovl_t, expand_t)

    n_router = N_GROUPS + N_GROUPS * EXPERTS_PER_GROUP
    wr_t = jnp.pad(jnp.concatenate([w_rg[0].T, w_re[0].reshape(d, -1).T], axis=0),
                   ((0, ROUTER_ROWS - n_router), (0, 0)))
    br = jnp.broadcast_to(jnp.pad(jnp.concatenate([b_rg[0], b_re[0].reshape(-1)]),
                                  (0, ROUTER_ROWS - n_router))[:, None], (ROUTER_ROWS, LANES))
    tri = jnp.asarray(np.triu(np.ones((MOE_TM, MOE_TM), np.float32), 1), BF16)
    tp = t // MOE_PARTS
    n_pad = 2 * tp + N_GROUPS * EXPERTS_PER_GROUP * FFN_ROWS
    w_o_bf = w_o[0].astype(BF16)
    gf = norm_f_g.reshape(1, d)
    routed = []
    for part in range(MOE_PARTS):
        x1, hpa, hpb, e1, e2, r1, r2, wtok, cnt = _post(x2, mix_a, mix_b, w_o_bf, norm2_g, wr_t, br, tri, part)
        pos1, pos2, chunk_expert, chunk_valid = _moe_plan(cnt[:, :, 0], e1[:, 0], e2[:, 0], r1[:, 0], r2[:, 0])
        xsa = _sc_scatter_rows(hpa, pos1, pos2, n_pad)
        xsb = _sc_scatter_rows(hpb, pos1, pos2, n_pad)
        routed.append((x1, wtok, pos1, pos2, chunk_expert, chunk_valid, xsa, xsb))
    gathered = []
    for x1, wtok, pos1, pos2, chunk_expert, chunk_valid, xsa, xsb in routed:
        ysa, ysb = _ffn(chunk_expert, chunk_valid, xsa, xsb, w_gate[0], w_up[0], w_down[0])
        pos12 = jnp.concatenate([pos1, pos2])
        gathered.append((x1, wtok, _sc_gather_rows(ysa, pos12), _sc_gather_rows(ysb, pos12)))
    out = None
    for part, (x1, wtok, yga, ygb) in enumerate(gathered):
        out = _final(x1, yga.reshape(2, tp, PACK_W), ygb.reshape(2, tp, PACK_W), wtok, gf, out, part)
    return out.reshape(nb, s, d)
```

```python
import numpy as np
import jax
import jax.numpy as jnp
from jax import lax
from jax.experimental import pallas as pl
from jax.experimental.pallas import tpu as pltpu
from jax.experimental.pallas import tpu_sc as plsc

F32 = jnp.float32
BF16 = jnp.bfloat16

D_MODEL = 1024
SEQ = 2048
HEAD_DIM = 64
N_HEADS_DIL = 8
DIL_PATTERNS = ((128, 1), (512, 4), (2048, 16))
N_HEADS_NSA = 8
N_KV_NSA = 2
NSA_REP = N_HEADS_NSA // N_KV_NSA
CMP_STRIDE = 16
CMP_LEN = 32
CMP_HIDDEN = 256
SEL_LEN = 64
N_SEL = 8
WIN = 512
ROPE_THETA = 500000.0
ROT_DIM = HEAD_DIM // 4
N_GROUPS = 8
EXPERTS_PER_GROUP = 8
D_FF_EXPERT = 256
EPS = 1e-6
NEG = -1e30
BIG = 1e9
LOWEST = -3.0e38
LOG2_E = 1.4426950408889634

LANES = 128
QBLK = 128
N_QBLK = SEQ // QBLK
N_CMP = SEQ // CMP_STRIDE
N_SELBLK = SEQ // SEL_LEN
D_IN_PAD = 23 * LANES
PROJ_TM = 1024
SEL_GROUP = 4
DIL_UNROLL = 16
MOE_TM = 1024
MOE_PARTS = 1
ROUTER_ROWS = 80
FFN_ROWS = 512
FINAL_TM = 512
PACK_W = D_MODEL // 4
ONES_ROWS = 16
SC_WINDOW = 128
VMEM_LIMIT = 56 * 1024 * 1024

_NT = (((1,), (1,)), ((), ()))


def _lane_iota(shape):
    return lax.broadcasted_iota(jnp.int32, shape, 1)


def _row_iota(shape):
    return lax.broadcasted_iota(jnp.int32, shape, 0)


def _rope_lanes(y, c, s1, s2):
    return y * c + pltpu.roll(y, LANES - ROT_DIM // 2, axis=1) * s1 + pltpu.roll(y, ROT_DIM // 2, axis=1) * s2


def _proj_kernel(x_ref, g_ref, w_ref, c_ref, s1_ref, s2_ref,
                 aq_ref, ak_ref, av_ref, aq4_ref, ak4_ref, av4_ref, aq16_ref, ak16_ref, av16_ref,
                 bq_ref, kcv_ref, ksw_ref, gate_ref, nat_f32, d4_f32):
    x = x_ref[...]
    ms = jnp.mean(x * x, axis=-1, keepdims=True)
    h = (x * lax.rsqrt(ms + EPS) * g_ref[...]).astype(BF16)
    c = c_ref[...]
    s1 = s1_ref[...]
    s2 = s2_ref[...]

    def seg(lo, width):
        return jnp.dot(h, w_ref[:, lo:lo + width], preferred_element_type=F32)

    def store(dst, col, y, rope, scale, dilated=None):
        for j in range(y.shape[1] // LANES):
            yj = y[:, j * LANES:(j + 1) * LANES]
            if rope:
                yj = _rope_lanes(yj, c, s1, s2)
            if scale != 1.0:
                yj = yj * scale
            cols = slice(col + j * LANES, col + (j + 1) * LANES)
            dst[:, cols] = yj.astype(dst.dtype)
            if dilated is not None:
                d4_ref, d16_ref = dilated
                quarter = PROJ_TM // 4
                nat_f32[...] = yj
                for r in range(4):
                    part = nat_f32[pl.ds(r, quarter, stride=4), :]
                    d4_ref[r, :, cols] = part.astype(d4_ref.dtype)
                    d4_f32[r * quarter:(r + 1) * quarter, :] = part
                for r in range(4):
                    for a_ in range(4):
                        part = d4_f32[pl.ds(r * quarter + a_, quarter // 4, stride=4), :]
                        d16_ref[4 * a_ + r, :, cols] = part.astype(d16_ref.dtype)

    qscale = HEAD_DIM ** -0.5 * LOG2_E
    store(aq_ref, 0, seg(0, 512), True, qscale, (aq4_ref, aq16_ref))
    store(ak_ref, 0, seg(512, 512), True, 1.0, (ak4_ref, ak16_ref))
    store(av_ref, 0, seg(1024, 512), False, 1.0, (av4_ref, av16_ref))
    store(bq_ref, 0, seg(1536, 512), True, qscale)
    store(kcv_ref, 0, seg(2048, 256), False, 1.0)
    store(ksw_ref, 0, seg(2304, 128), True, 1.0)
    store(ksw_ref, 128, seg(2432, 128), False, 1.0)
    store(ksw_ref, 256, seg(2560, 128), True, 1.0)
    store(ksw_ref, 384, seg(2688, 128), False, 1.0)
    gate_ref[...] = jax.nn.sigmoid(seg(2816, 128))


def _proj(x2, g1, w_in_p, rc, rs1, rs2):
    t = x2.shape[0]
    nb = t // SEQ
    nblk_s = SEQ // PROJ_TM
    row = lambda i: (i, 0)
    pos = lambda i: (i % nblk_s, 0)
    const = lambda i: (0, 0)
    perm = lambda i: (i // nblk_s, 0, i % nblk_s, 0)
    wide = jax.ShapeDtypeStruct((t, 512), BF16)
    wide_spec = pl.BlockSpec((PROJ_TM, 512), row)
    d4 = jax.ShapeDtypeStruct((nb, 4, SEQ // 4, 512), BF16)
    d4_spec = pl.BlockSpec((None, 4, PROJ_TM // 4, 512), perm)
    d16 = jax.ShapeDtypeStruct((nb, 16, SEQ // 16, 512), BF16)
    d16_spec = pl.BlockSpec((None, 16, PROJ_TM // 16, 512), perm)
    outs = [wide] * 3 + [d4] * 3 + [d16] * 3 + [wide, jax.ShapeDtypeStruct((t, 256), BF16), wide,
                                               jax.ShapeDtypeStruct((t, LANES), F32)]
    out_specs = [wide_spec] * 3 + [d4_spec] * 3 + [d16_spec] * 3 + [
        wide_spec, pl.BlockSpec((PROJ_TM, 256), row), wide_spec, pl.BlockSpec((PROJ_TM, LANES), row)]
    return pl.pallas_call(
        _proj_kernel,
        grid=(t // PROJ_TM,),
        in_specs=[pl.BlockSpec((PROJ_TM, D_MODEL), row), pl.BlockSpec((1, D_MODEL), const),
                  pl.BlockSpec((D_MODEL, D_IN_PAD), const),
                  pl.BlockSpec((PROJ_TM, LANES), pos), pl.BlockSpec((PROJ_TM, LANES), pos),
                  pl.BlockSpec((PROJ_TM, LANES), pos)],
        out_specs=out_specs,
        out_shape=outs,
        scratch_shapes=[pltpu.VMEM((PROJ_TM, LANES), F32)] * 2,
        compiler_params=pltpu.CompilerParams(dimension_semantics=("arbitrary",),
                                             vmem_limit_bytes=VMEM_LIMIT),
        name="proj",
    )(x2, g1, w_in_p, rc, rs1, rs2)


def _dilated_kernel(q1_ref, k1_ref, v1_ref, q4_ref, k4_ref, v4_ref, q16_ref, k16_ref, v16_ref, o_ref,
                    qt, vt, op, lp, on0, on1, on2, ln0, ln1, ln2):
    c2 = _row_iota((2 * QBLK, 2 * QBLK))
    a2 = _lane_iota((2 * QBLK, 2 * QBLK)) % QBLK
    band = jnp.where((c2 >= a2) & (c2 <= a2 + QBLK), 0.0, NEG)
    band_noprev = jnp.where(c2 < QBLK, NEG, band)
    c1 = _row_iota((QBLK, 2 * QBLK))
    a1 = _lane_iota((QBLK, 2 * QBLK)) % QBLK
    band_own = jnp.where(c1 <= a1, 0.0, NEG)
    top_sq = _row_iota((LANES, QBLK)) < HEAD_DIM

    for j in range(N_QBLK):
        vt[j, LANES:, :] = jnp.ones((ONES_ROWS, QBLK), BF16)

    inputs = ((q1_ref, k1_ref, v1_ref), (q4_ref, k4_ref, v4_ref), (q16_ref, k16_ref, v16_ref))
    for pi, (window, dil) in enumerate(DIL_PATTERNS):
        seg_len = SEQ // dil
        nseg_blk = seg_len // QBLK
        q_ref, k_ref, v_ref = inputs[pi]

        def block(ref, j, d=None, nseg_blk=nseg_blk, dil=dil):
            if dil == 1:
                return ref[pl.ds(pl.multiple_of(j * QBLK, QBLK), QBLK), :]
            if d is None:
                return ref[j // nseg_blk, (j % nseg_blk) * QBLK:(j % nseg_blk + 1) * QBLK, :]
            within = d % nseg_blk
            return ref[j // nseg_blk, within * QBLK:(within + 1) * QBLK, :]

        for j in range(N_QBLK):
            qt[j] = block(q_ref, j).T if dil > 1 else q_ref[j * QBLK:(j + 1) * QBLK, :].T
            vt[j, 0:LANES, :] = block(v_ref, j).T if dil > 1 else v_ref[j * QBLK:(j + 1) * QBLK, :].T

        def body(u, carry, nseg_blk=nseg_blk, k_ref=k_ref, block=block):
            js = [u * DIL_UNROLL + d for d in range(DIL_UNROLL)]
            prevs, scores = [], []
            for d, j in enumerate(js):
                qtb = qt[j]
                zero = jnp.zeros_like(qtb)
                q_both = jnp.concatenate([jnp.where(top_sq, qtb, zero), jnp.where(top_sq, zero, qtb)], axis=1)
                if nseg_blk == 1:
                    keys, bias, jp = block(k_ref, j, d), band_own, j
                else:
                    if nseg_blk <= DIL_UNROLL:
                        first = d % nseg_blk == 0
                        jp = j if first else j - 1
                        dp = d if first else d - 1
                        bias = band_noprev if first else band
                    else:
                        jp = jnp.maximum(j - 1, 0) if d == 0 else j - 1
                        dp = d - 1
                        bias = jnp.where(u == 0, band_noprev, band) if d == 0 else band
                    keys = jnp.concatenate([block(k_ref, jp, dp), block(k_ref, j, d)], axis=0)
                prevs.append(jp)
                scores.append(jnp.dot(keys, q_both, preferred_element_type=F32) + bias)
            stats = []
            for s in scores:
                m = jnp.max(s, axis=0, keepdims=True)
                stats.append((m, jnp.exp2((s - m).astype(BF16))))
            outs = []
            for j, jp, (m, p) in zip(js, prevs, stats):
                v_span = vt[j] if nseg_blk == 1 else jnp.concatenate([vt[jp], vt[j]], axis=1)
                outs.append(jnp.dot(v_span, p, preferred_element_type=F32))
            for j, (m, p), res in zip(js, stats, outs):
                row0 = pl.multiple_of(j * QBLK, QBLK)
                l = res[LANES:LANES + 1, :]
                lse = m + jnp.log2(l)
                o_t = jnp.where(top_sq, res[0:LANES, 0:QBLK] / l[:, 0:QBLK], res[0:LANES, QBLK:] / l[:, QBLK:])
                lse_t = jnp.where(top_sq, jnp.broadcast_to(lse[:, 0:QBLK], (LANES, QBLK)),
                                  jnp.broadcast_to(lse[:, QBLK:], (LANES, QBLK)))
                op[pl.ds(row0, QBLK), :] = o_t.T
                lp[pl.ds(row0, QBLK), :] = lse_t.T
            return carry

        lax.fori_loop(0, N_QBLK // DIL_UNROLL, body, 0)

        for r in range(dil):
            dst = pl.ds(r, seg_len, stride=dil) if dil > 1 else pl.ds(0, SEQ)
            src = pl.ds(r * seg_len, seg_len)
            (on0, on1, on2)[pi][dst, :] = op[src, :]
            (ln0, ln1, ln2)[pi][dst, :] = lp[src, :]

    l0, l1, l2 = ln0[...], ln1[...], ln2[...]
    mx = jnp.maximum(jnp.maximum(l0, l1), l2)
    e0, e1, e2 = jnp.exp2(l0 - mx), jnp.exp2(l1 - mx), jnp.exp2(l2 - mx)
    den = e0 + e1 + e2
    out = (e0 / den) * on0[...] + (e1 / den) * on1[...] + (e2 / den) * on2[...]
    o_ref[...] = out.astype(o_ref.dtype)


def _dilated(a1, a4, a16):
    t = a1[0].shape[0]
    nb = t // SEQ
    spec1 = pl.BlockSpec((SEQ, LANES), lambda b, hp: (b, hp))
    spec4 = pl.BlockSpec((None, 4, SEQ // 4, LANES), lambda b, hp: (b, 0, 0, hp))
    spec16 = pl.BlockSpec((None, 16, SEQ // 16, LANES), lambda b, hp: (b, 0, 0, hp))
    return pl.pallas_call(
        _dilated_kernel,
        grid=(nb, N_HEADS_DIL // 2),
        in_specs=[spec1] * 3 + [spec4] * 3 + [spec16] * 3,
        out_specs=spec1,
        out_shape=jax.ShapeDtypeStruct((t, 512), BF16),
        scratch_shapes=[pltpu.VMEM((N_QBLK, LANES, QBLK), BF16), pltpu.VMEM((N_QBLK, LANES + ONES_ROWS, QBLK), BF16)]
        + [pltpu.VMEM((SEQ, LANES), F32)] * 8,
        compiler_params=pltpu.CompilerParams(dimension_semantics=("arbitrary", "arbitrary"),
                                             vmem_limit_bytes=VMEM_LIMIT),
        name="dilated",
    )(*a1, *a4, *a16)


def _gelu_tanh(x):
    return 0.5 * x * (1.0 + jnp.tanh(np.sqrt(2.0 / np.pi).astype(np.float32) * (x + 0.044715 * (x * x * x))))


def _softmax_step(state, s, v_t):
    m, l, acc = state
    mn = jnp.maximum(m, jnp.max(s, axis=0, keepdims=True))
    al = jnp.exp2(m - mn)
    p = jnp.exp2(s - mn)
    l = al * l + jnp.sum(p, axis=0, keepdims=True)
    acc = al * acc + jnp.dot(v_t, p.astype(BF16), preferred_element_type=F32)
    return mn, l, acc


def _nsa_kernel(q_ref, kflat_ref, vflat_ref, ksw_ref, gate_ref,
                wk1_ref, wk2d_ref, pek_ref, wv1_ref, wv2t_ref, pev_ref,
                cc_ref, cs1_ref, cs2_ref, ovl_ref, exp_ref,
                o_ref,
                qt, vst, vwt, gtt, kc2, vct):
    g = pl.program_id(1)
    nq = NSA_REP * QBLK

    for i in range(N_QBLK):
        rows = pl.ds(i * QBLK, QBLK)
        qt[i, 0:LANES, :] = q_ref[rows, 0:LANES].T
        qt[i, LANES:2 * LANES, :] = q_ref[rows, LANES:2 * LANES].T
        vs_t = ksw_ref[rows, LANES:2 * LANES].T
        vst[i] = jnp.where(g == 0, vs_t[0:HEAD_DIM], vs_t[HEAD_DIM:])
        vw_t = ksw_ref[rows, 3 * LANES:4 * LANES].T
        vwt[i] = jnp.where(g == 0, vw_t[0:HEAD_DIM], vw_t[HEAD_DIM:])
        gt = gate_ref[rows, :]
        gt = jnp.where(g == 0, gt, pltpu.roll(gt, LANES - 3 * NSA_REP, axis=1))
        gtt[i] = gt.T[0:16, :]

    def hidden(flat_ref, w1_ref, pe_ref):
        ab = jnp.dot(flat_ref[...], w1_ref[...], preferred_element_type=F32)
        pb = (jnp.dot(pe_ref[0:16, :], w1_ref[:, 0:CMP_HIDDEN], preferred_element_type=F32)
              + jnp.dot(pe_ref[16:32, :], w1_ref[:, CMP_HIDDEN:], preferred_element_type=F32))
        hid = ab[:, 0:CMP_HIDDEN] + pltpu.roll(ab[:, CMP_HIDDEN:], N_CMP - 1, axis=0) + pb[0:1, :]
        return _gelu_tanh(hid).astype(BF16)

    hk = hidden(kflat_ref, wk1_ref, pek_ref)
    kc2[...] = _rope_lanes(jnp.dot(hk, wk2d_ref[...], preferred_element_type=F32),
                           cc_ref[...], cs1_ref[...], cs2_ref[...]).astype(BF16)
    hv = hidden(vflat_ref, wv1_ref, pev_ref)
    vct[...] = lax.dot_general(wv2t_ref[...], hv, _NT, preferred_element_type=F32).astype(BF16)

    key = _row_iota((QBLK, nq))
    qry = _lane_iota((QBLK, nq)) % QBLK
    causal = jnp.where(key <= qry, 0.0, NEG)
    upper = jnp.where(key > qry, 0.0, NEG)
    blk = _row_iota((N_SELBLK, QBLK))
    qry1 = _lane_iota((N_SELBLK, QBLK))
    grp_keys = SEL_GROUP * QBLK
    win_blocks = WIN // QBLK

    def qgroup(w, full_window):
        blocks = [SEL_GROUP * w + d for d in range(SEL_GROUP)]
        row0s = [pl.multiple_of(i * QBLK, QBLK) for i in blocks]

        q_pads = []
        for i in blocks:
            q_t = jnp.concatenate([qt[i, r * HEAD_DIM:(r + 1) * HEAD_DIM, :] for r in range(NSA_REP)], axis=1)
            zero = jnp.zeros_like(q_t)
            q_pads.append(jnp.where(g == 0, jnp.concatenate([q_t, zero], axis=0),
                                    jnp.concatenate([zero, q_t], axis=0)))
        s_cmp = [jnp.dot(kc2[...], qp, preferred_element_type=F32) for qp in q_pads]
        s_win, v_win = [], []
        for d, (i, qp) in enumerate(zip(blocks, q_pads)):
            if full_window:
                first = i - win_blocks
                keys = ksw_ref[pl.ds(pl.multiple_of(first * QBLK, QBLK), (win_blocks + 1) * QBLK), 2 * LANES:3 * LANES]
                s = jnp.dot(keys, qp, preferred_element_type=F32)
                s = jnp.concatenate([s[0:QBLK] + upper, s[QBLK:win_blocks * QBLK], s[win_blocks * QBLK:] + causal],
                                    axis=0)
                v_t = jnp.concatenate([vwt[first + e] for e in range(win_blocks + 1)], axis=1)
            else:
                s = jnp.dot(ksw_ref[0:(d + 1) * QBLK, 2 * LANES:3 * LANES], qp, preferred_element_type=F32)
                s = jnp.concatenate([s[0:d * QBLK], s[d * QBLK:] + causal], axis=0) if d else s + causal
                v_t = jnp.concatenate([vwt[e] for e in range(d + 1)], axis=1) if d else vwt[0]
            s_win.append(s)
            v_win.append(v_t)

        pcs = []
        for row0, s in zip(row0s, s_cmp):
            cvalid = (CMP_STRIDE * key + (CMP_LEN - 1) <= row0 + qry) & (key < N_CMP - 1)
            s = jnp.where(cvalid, s, NEG)
            m = jnp.max(s, axis=0, keepdims=True)
            p = jnp.where(cvalid, jnp.exp2(s - m), 0.0)
            l = jnp.sum(p, axis=0, keepdims=True)
            pcs.append(p / jnp.where(l > 0, l, 1.0))
        o_cmp = [jnp.dot(vct[...], pc.astype(BF16), preferred_element_type=F32) for pc in pcs]

        ovl = ovl_ref[...]
        imps = []
        for pc in pcs:
            psum = pc[:, 0:QBLK] + pc[:, QBLK:2 * QBLK] + pc[:, 2 * QBLK:3 * QBLK] + pc[:, 3 * QBLK:]
            p_hi = psum.astype(BF16)
            p_lo = (psum - p_hi.astype(F32)).astype(BF16)
            imps.append(jnp.dot(ovl, p_hi, preferred_element_type=F32) + jnp.dot(ovl, p_lo, preferred_element_type=F32))

        o_win = []
        for s, v_t in zip(s_win, v_win):
            m = jnp.max(s, axis=0, keepdims=True)
            p = jnp.exp2(s - m)
            l = jnp.sum(p, axis=0, keepdims=True)
            o_win.append(jnp.dot(v_t, p.astype(BF16), preferred_element_type=F32) / l)

        q_sel = []
        for row0, imp, qp in zip(row0s, imps, q_pads):
            t1 = row0 + qry1
            blk_t = t1 // SEL_LEN
            forced = (blk == 0) | (blk == blk_t) | (blk == blk_t - 1)
            v = jnp.where(blk * SEL_LEN <= t1, jnp.where(forced, BIG, imp), -BIG)
            selm = jnp.zeros((N_SELBLK, QBLK), F32)
            for _ in range(N_SEL):
                mx = jnp.max(v, axis=0, keepdims=True)
                first_hit = jnp.min(jnp.where(v == mx, blk, N_SELBLK), axis=0, keepdims=True)
                hit = blk == first_hit
                selm = jnp.where(hit, 1.0, selm)
                v = jnp.where(hit, LOWEST, v)
            notsel = jnp.concatenate([selm - 1.0, jnp.zeros((LANES - N_SELBLK, QBLK), F32)], axis=0).astype(BF16)
            q_sel.append(jnp.concatenate([qp, jnp.concatenate([notsel] * NSA_REP, axis=1)], axis=0))

        def sel_keys(t, n_rows):
            rows_t = pl.ds(pl.multiple_of(t * grp_keys, grp_keys), n_rows)
            return jnp.concatenate([ksw_ref[rows_t, 0:LANES], exp_ref[t, 0:n_rows, :]], axis=1)

        def sel_values(t, n_blk):
            return jnp.concatenate([vst[t * SEL_GROUP + e] for e in range(n_blk)], axis=1) if n_blk > 1 else vst[t * SEL_GROUP]

        wide = SEL_GROUP * nq
        st = (jnp.full((1, wide), LOWEST, F32), jnp.zeros((1, wide), F32), jnp.zeros((HEAD_DIM, wide), F32))
        if full_window:
            q_all = jnp.concatenate(q_sel, axis=1)

            def sel_body(t, st):
                s = jnp.dot(sel_keys(t, grp_keys), q_all, preferred_element_type=F32)
                return _softmax_step(st, s, sel_values(t, SEL_GROUP))

            for t_ in range(w):
                st = sel_body(t_, st)
        o_sel = []
        for d, qs in enumerate(q_sel):
            s = jnp.dot(sel_keys(w, (d + 1) * QBLK), qs, preferred_element_type=F32)
            s = jnp.concatenate([s[0:d * QBLK], s[d * QBLK:] + causal], axis=0) if d else s + causal
            lanes = slice(d * nq, (d + 1) * nq)
            m, l, acc = _softmax_step((st[0][:, lanes], st[1][:, lanes], st[2][:, lanes]), s, sel_values(w, d + 1))
            o_sel.append(acc / l)

        for n, (i, row0) in enumerate(zip(blocks, row0s)):
            gti = gtt[i]

            def gate(br):
                return jnp.concatenate([gti[r * 3 + br:r * 3 + br + 1, :] for r in range(NSA_REP)], axis=1)

            out_t = gate(0) * o_cmp[n] + gate(1) * o_sel[n] + gate(2) * o_win[n]
            for pair in range(NSA_REP // 2):
                both = jnp.concatenate([out_t[:, (2 * pair) * QBLK:(2 * pair + 1) * QBLK],
                                        out_t[:, (2 * pair + 1) * QBLK:(2 * pair + 2) * QBLK]], axis=0)
                o_ref[pl.ds(row0, QBLK), pair * LANES:(pair + 1) * LANES] = both.T.astype(o_ref.dtype)

    qgroup(0, False)
    for w in range(1, N_QBLK // SEL_GROUP):
        qgroup(w, True)


def _nsa(bq, kvflat, ksw, gates, wk1, wk2d, pek, wv1, wv2t, pev, cc, cs1, cs2, ovl_t, expand_t):
    t = bq.shape[0]
    nb = t // SEQ
    const2 = lambda b, g: (0, 0)
    return pl.pallas_call(
        _nsa_kernel,
        grid=(nb, N_KV_NSA),
        in_specs=[
            pl.BlockSpec((SEQ, 2 * LANES), lambda b, g: (b, g)),
            pl.BlockSpec((None, None, N_CMP, CMP_STRIDE * HEAD_DIM), lambda b, g: (b, g, 0, 0)),
            pl.BlockSpec((None, None, N_CMP, CMP_STRIDE * HEAD_DIM), lambda b, g: (b, N_KV_NSA + g, 0, 0)),
            pl.BlockSpec((SEQ, 4 * LANES), lambda b, g: (b, 0)),
            pl.BlockSpec((SEQ, LANES), lambda b, g: (b, 0)),
            pl.BlockSpec((CMP_STRIDE * HEAD_DIM, 2 * CMP_HIDDEN), const2),
            pl.BlockSpec((CMP_HIDDEN, LANES), const2),
            pl.BlockSpec((32, CMP_STRIDE * HEAD_DIM), const2),
            pl.BlockSpec((CMP_STRIDE * HEAD_DIM, 2 * CMP_HIDDEN), const2),
            pl.BlockSpec((HEAD_DIM, CMP_HIDDEN), const2),
            pl.BlockSpec((32, CMP_STRIDE * HEAD_DIM), const2),
            pl.BlockSpec((N_CMP, LANES), const2),
            pl.BlockSpec((N_CMP, LANES), const2),
            pl.BlockSpec((N_CMP, LANES), const2),
            pl.BlockSpec((N_SELBLK, LANES), const2),
            pl.BlockSpec((N_QBLK // SEL_GROUP, SEL_GROUP * QBLK, LANES), lambda b, g: (0, 0, 0)),
        ],
        out_specs=pl.BlockSpec((SEQ, 2 * LANES), lambda b, g: (b, g)),
        out_shape=jax.ShapeDtypeStruct((t, 512), BF16),
        scratch_shapes=[pltpu.VMEM((N_QBLK, 2 * LANES, QBLK), BF16),
                        pltpu.VMEM((N_QBLK, HEAD_DIM, QBLK), BF16),
                        pltpu.VMEM((N_QBLK, HEAD_DIM, QBLK), BF16),
                        pltpu.VMEM((N_QBLK, 16, QBLK), F32),
                        pltpu.VMEM((N_CMP, LANES), BF16),
                        pltpu.VMEM((HEAD_DIM, N_CMP), BF16)],
        compiler_params=pltpu.CompilerParams(dimension_semantics=("arbitrary", "arbitrary"),
                                             vmem_limit_bytes=VMEM_LIMIT),
        name="nsa",
    )(bq, kvflat, kvflat, ksw, gates, wk1, wk2d, pek, wv1, wv2t, pev, cc, cs1, cs2, ovl_t, expand_t)


def _pack_bf16_pairs(lo, hi):
    lo_b = lax.bitcast_convert_type(lo.astype(BF16).astype(F32), jnp.uint32)
    hi_b = lax.bitcast_convert_type(hi.astype(BF16).astype(F32), jnp.uint32)
    return (hi_b & jnp.uint32(0xFFFF0000)) | (lo_b >> 16)


def _unpack_bf16_pairs(p):
    lo = lax.bitcast_convert_type(p << 16, F32)
    hi = lax.bitcast_convert_type(p & jnp.uint32(0xFFFF0000), F32)
    return lo, hi


def _post_kernel(x_ref, ma_ref, mb_ref, wo_ref, g2_ref, wr_ref, br_ref, tri_ref,
                 x1_ref, ha_ref, hb_ref, e1_ref, e2_ref, r1_ref, r2_ref, w_ref, cnt_ref):
    x1 = (x_ref[...] + jnp.dot(ma_ref[...], wo_ref[0:512, :], preferred_element_type=F32)
          + jnp.dot(mb_ref[...], wo_ref[512:1024, :], preferred_element_type=F32))
    x1_ref[...] = x1
    ms = jnp.mean(x1 * x1, axis=-1, keepdims=True)
    h2 = x1 * lax.rsqrt(ms + EPS) * g2_ref[...]
    ha_ref[...] = _pack_bf16_pairs(h2[:, 0:PACK_W], h2[:, 2 * PACK_W:3 * PACK_W])
    hb_ref[...] = _pack_bf16_pairs(h2[:, PACK_W:2 * PACK_W], h2[:, 3 * PACK_W:])

    h_hi = h2.astype(BF16)
    h_lo = (h2 - h_hi.astype(F32)).astype(BF16)
    wr = wr_ref[...]
    w_hi = wr.astype(BF16)
    w_lo = (wr - w_hi.astype(F32)).astype(BF16)
    lg = (lax.dot_general(w_hi, h_hi, _NT, preferred_element_type=F32)
          + lax.dot_general(w_hi, h_lo, _NT, preferred_element_type=F32)
          + lax.dot_general(w_lo, h_hi, _NT, preferred_element_type=F32)) + br_ref[:, 0:1]
    gl = lg[0:N_GROUPS, :]
    sub = _row_iota(gl.shape)
    gmax = jnp.max(gl, axis=0, keepdims=True)
    p_g = 1.0 / jnp.sum(jnp.exp(gl - gmax), axis=0, keepdims=True)
    g_star = jnp.min(jnp.where(gl == gmax, sub, N_GROUPS), axis=0, keepdims=True)
    el = jnp.zeros_like(gl)
    for gi in range(N_GROUPS):
        lo = N_GROUPS + gi * EXPERTS_PER_GROUP
        el = el + jnp.where(g_star == gi, lg[lo:lo + EXPERTS_PER_GROUP, :], 0.0)
    emax = jnp.max(el, axis=0, keepdims=True)
    ee = jnp.exp(el - emax)
    pe = ee / jnp.sum(ee, axis=0, keepdims=True)
    v1 = jnp.max(pe, axis=0, keepdims=True)
    i1 = jnp.min(jnp.where(pe == v1, sub, EXPERTS_PER_GROUP), axis=0, keepdims=True)
    pe2 = jnp.where(sub == i1, -1.0, pe)
    v2 = jnp.max(pe2, axis=0, keepdims=True)
    i2 = jnp.min(jnp.where(pe2 == v2, sub, EXPERTS_PER_GROUP), axis=0, keepdims=True)
    tot = v1 + v2
    e1 = g_star * EXPERTS_PER_GROUP + i1
    e2 = g_star * EXPERTS_PER_GROUP + i2
    e1_ref[0] = e1
    e2_ref[0] = e2

    wslab = jnp.concatenate([v1 / tot * p_g, v2 / tot * p_g, jnp.zeros((LANES - 2, MOE_TM), F32)], axis=0)
    for j in range(MOE_TM // LANES):
        w_ref[j * LANES:(j + 1) * LANES, :] = wslab[:, j * LANES:(j + 1) * LANES].T

    n_exp = N_GROUPS * EXPERTS_PER_GROUP
    sub_e = _row_iota((n_exp, MOE_TM))
    hit1 = sub_e == e1
    hit2 = sub_e == e2
    assigned = jnp.where(hit1, 1.0, 0.0) + jnp.where(hit2, 1.0, 0.0)
    before = jnp.dot(assigned.astype(BF16), tri_ref[...], preferred_element_type=F32)
    r1_ref[0] = jnp.sum(jnp.where(hit1, before, 0.0), axis=0, keepdims=True).astype(jnp.int32)
    r2_ref[0] = jnp.sum(jnp.where(hit2, before, 0.0), axis=0, keepdims=True).astype(jnp.int32)
    cnt = jnp.sum(assigned, axis=1, keepdims=True).astype(jnp.int32)
    cnt_ref[0] = jnp.broadcast_to(cnt, (n_exp, LANES))


def _post(x2, mix_a, mix_b, w_o, g2, wr_t, br, tri, part):
    t = x2.shape[0] // MOE_PARTS
    nt = t // MOE_TM
    first = part * nt
    n_exp = N_GROUPS * EXPERTS_PER_GROUP
    row = lambda i: (i, 0)
    const = lambda i: (0, 0)
    tok = lambda i: (i, 0, 0)
    src = lambda i: (first + i, 0)
    tok_spec = pl.BlockSpec((1, 1, MOE_TM), tok)
    tok_shape = jax.ShapeDtypeStruct((nt, 1, MOE_TM), jnp.int32)
    return pl.pallas_call(
        _post_kernel,
        grid=(nt,),
        in_specs=[pl.BlockSpec((MOE_TM, D_MODEL), src), pl.BlockSpec((MOE_TM, 512), src),
                  pl.BlockSpec((MOE_TM, 512), src), pl.BlockSpec((D_MODEL, D_MODEL), const),
                  pl.BlockSpec((1, D_MODEL), const), pl.BlockSpec((ROUTER_ROWS, D_MODEL), const),
                  pl.BlockSpec((ROUTER_ROWS, LANES), const), pl.BlockSpec((MOE_TM, MOE_TM), const)],
        out_specs=[pl.BlockSpec((MOE_TM, D_MODEL), row),
                   pl.BlockSpec((MOE_TM, PACK_W), row), pl.BlockSpec((MOE_TM, PACK_W), row),
                   tok_spec, tok_spec, tok_spec, tok_spec,
                   pl.BlockSpec((MOE_TM, LANES), row),
                   pl.BlockSpec((1, n_exp, LANES), tok)],
        out_shape=[jax.ShapeDtypeStruct((t, D_MODEL), F32),
                   jax.ShapeDtypeStruct((t, PACK_W), jnp.uint32), jax.ShapeDtypeStruct((t, PACK_W), jnp.uint32),
                   tok_shape, tok_shape, tok_shape, tok_shape,
                   jax.ShapeDtypeStruct((t, LANES), F32),
                   jax.ShapeDtypeStruct((nt, n_exp, LANES), jnp.int32)],
        compiler_params=pltpu.CompilerParams(dimension_semantics=("arbitrary",),
                                             vmem_limit_bytes=VMEM_LIMIT),
        name="post",
    )(x2, mix_a, mix_b, w_o, g2, wr_t, br, tri)


def _sc_mesh():
    return plsc.VectorSubcoreMesh(core_axis_name="core", subcore_axis_name="subcore")


def _sc_scatter_rows(x, pos1, pos2, n_out):
    r, c = x.shape

    @pl.kernel(out_type=jax.ShapeDtypeStruct((n_out, c), x.dtype), mesh=_sc_mesh(), scratch_types=[])
    def scatter_kernel(x_hbm, p1_hbm, p2_hbm, o_hbm):
        def body(x_vmem, p1_vmem, p2_vmem):
            pltpu.sync_copy(x_vmem, o_hbm.at[p1_vmem.at[0]])
            pltpu.sync_copy(x_vmem, o_hbm.at[p2_vmem.at[0]])

        idx = pl.BlockSpec((1, SC_WINDOW), lambda i: (0, i))
        pltpu.emit_pipeline(body, grid=(r // SC_WINDOW,),
                            in_specs=[pl.BlockSpec((SC_WINDOW, c), lambda i: (i, 0)), idx, idx], out_specs=[],
                            core_axis_name=("core", "subcore"),
                            dimension_semantics=(pltpu.PARALLEL,))(x_hbm, p1_hbm, p2_hbm)

    return scatter_kernel(x, pos1.reshape(1, r), pos2.reshape(1, r))


def _sc_gather_rows(y, idx):
    n = idx.shape[0]
    c = y.shape[1]

    @pl.kernel(out_type=jax.ShapeDtypeStruct((n, c), y.dtype), mesh=_sc_mesh(), scratch_types=[])
    def gather_kernel(y_hbm, i_hbm, o_hbm):
        def body(i_vmem, o_vmem):
            pltpu.sync_copy(y_hbm.at[i_vmem.at[0]], o_vmem)

        pltpu.emit_pipeline(body, grid=(n // SC_WINDOW,),
                            in_specs=[pl.BlockSpec((1, SC_WINDOW), lambda i: (0, i))],
                            out_specs=[pl.BlockSpec((SC_WINDOW, c), lambda i: (i, 0))],
                            core_axis_name=("core", "subcore"),
                            dimension_semantics=(pltpu.PARALLEL,))(i_hbm, o_hbm)

    return gather_kernel(y, idx.reshape(1, n))


def _ffn_kernel(ce_ref, cv_ref, xa_ref, xb_ref, wg_ref, wu_ref, wd_ref, ya_ref, yb_ref, wg_bf, wu_bf, wd_bf):
    c = pl.program_id(0)
    prev = ce_ref[jnp.maximum(c - 1, 0)]

    @pl.when((c == 0) | (ce_ref[c] != prev))
    def _():
        wg_bf[...] = wg_ref[0].astype(BF16)
        wu_bf[...] = wu_ref[0].astype(BF16)
        wd_bf[...] = wd_ref[0].astype(BF16)

    @pl.when(cv_ref[c] == 0)
    def _():
        ya_ref[...] = jnp.zeros_like(ya_ref)
        yb_ref[...] = jnp.zeros_like(yb_ref)

    @pl.when(cv_ref[c] != 0)
    def _():
        a_lo, a_hi = _unpack_bf16_pairs(xa_ref[...])
        b_lo, b_hi = _unpack_bf16_pairs(xb_ref[...])
        xs = jnp.concatenate([a_lo, b_lo, a_hi, b_hi], axis=1).astype(BF16)
        gate = jnp.dot(xs, wg_bf[...], preferred_element_type=F32)
        up = jnp.dot(xs, wu_bf[...], preferred_element_type=F32)
        he = (gate * jax.nn.sigmoid(gate) * up).astype(BF16)
        y = jnp.dot(he, wd_bf[...], preferred_element_type=F32)
        ya_ref[...] = _pack_bf16_pairs(y[:, 0:PACK_W], y[:, 2 * PACK_W:3 * PACK_W])
        yb_ref[...] = _pack_bf16_pairs(y[:, PACK_W:2 * PACK_W], y[:, 3 * PACK_W:])


def _ffn(chunk_expert, chunk_valid, xsa, xsb, wg, wu, wd):
    n_pad = xsa.shape[0]
    rows = pl.BlockSpec((FFN_ROWS, PACK_W), lambda c, ce, cv: (c, 0))
    grid_spec = pltpu.PrefetchScalarGridSpec(
        num_scalar_prefetch=2,
        grid=(n_pad // FFN_ROWS,),
        in_specs=[rows, rows,
                  pl.BlockSpec((1, D_MODEL, D_FF_EXPERT), lambda c, ce, cv: (ce[c], 0, 0)),
                  pl.BlockSpec((1, D_MODEL, D_FF_EXPERT), lambda c, ce, cv: (ce[c], 0, 0)),
                  pl.BlockSpec((1, D_FF_EXPERT, D_MODEL), lambda c, ce, cv: (ce[c], 0, 0))],
        out_specs=[rows, rows],
        scratch_shapes=[pltpu.VMEM((D_MODEL, D_FF_EXPERT), BF16), pltpu.VMEM((D_MODEL, D_FF_EXPERT), BF16),
                        pltpu.VMEM((D_FF_EXPERT, D_MODEL), BF16)],
    )
    out = jax.ShapeDtypeStruct((n_pad, PACK_W), jnp.uint32)
    return pl.pallas_call(
        _ffn_kernel,
        grid_spec=grid_spec,
        out_shape=[out, out],
        compiler_params=pltpu.CompilerParams(dimension_semantics=("arbitrary",),
                                             vmem_limit_bytes=VMEM_LIMIT),
        name="ffn",
    )(chunk_expert, chunk_valid, xsa, xsb, wg, wu, wd)


def _final_kernel(x1_ref, y1a_ref, y1b_ref, y2a_ref, y2b_ref, w_ref, gf_ref, o_ref):
    w1 = w_ref[:, 0:1]
    w2 = w_ref[:, 1:2]
    a1_lo, a1_hi = _unpack_bf16_pairs(y1a_ref[...])
    b1_lo, b1_hi = _unpack_bf16_pairs(y1b_ref[...])
    a2_lo, a2_hi = _unpack_bf16_pairs(y2a_ref[...])
    b2_lo, b2_hi = _unpack_bf16_pairs(y2b_ref[...])
    y = jnp.concatenate([w1 * a1_lo + w2 * a2_lo, w1 * b1_lo + w2 * b2_lo,
                         w1 * a1_hi + w2 * a2_hi, w1 * b1_hi + w2 * b2_hi], axis=1)
    xo = x1_ref[...] + y
    ms = jnp.mean(xo * xo, axis=-1, keepdims=True)
    o_ref[...] = xo * lax.rsqrt(ms + EPS) * gf_ref[...]


def _final(x1, yga, ygb, wtok, gf, out_prev, part):
    t = x1.shape[0]
    first = part * (t // FINAL_TM)
    row = lambda i: (i, 0)
    one = pl.BlockSpec((None, FINAL_TM, PACK_W), lambda i: (0, i, 0))
    two = pl.BlockSpec((None, FINAL_TM, PACK_W), lambda i: (1, i, 0))
    in_specs = [pl.BlockSpec((FINAL_TM, D_MODEL), row), one, one, two, two,
                pl.BlockSpec((FINAL_TM, LANES), row), pl.BlockSpec((1, D_MODEL), lambda i: (0, 0))]
    args = [x1, yga, ygb, yga, ygb, wtok, gf]
    kern = _final_kernel
    aliases = {}
    if out_prev is not None:
        in_specs.append(pl.BlockSpec(memory_space=pl.ANY))
        args.append(out_prev)
        aliases = {len(args) - 1: 0}
        kern = lambda *refs: _final_kernel(*refs[:7], refs[8])
    return pl.pallas_call(
        kern,
        grid=(t // FINAL_TM,),
        in_specs=in_specs,
        out_specs=pl.BlockSpec((FINAL_TM, D_MODEL), lambda i: (first + i, 0)),
        out_shape=jax.ShapeDtypeStruct((t * MOE_PARTS, D_MODEL), F32),
        input_output_aliases=aliases,
        compiler_params=pltpu.CompilerParams(dimension_semantics=("arbitrary",),
                                             vmem_limit_bytes=VMEM_LIMIT),
        name="final",
    )(*args)


def _moe_plan(cnt, e1, e2, r1, r2):
    nt, n_exp = cnt.shape
    tot = jnp.sum(cnt, axis=0)
    padded = (tot + FFN_ROWS - 1) // FFN_ROWS * FFN_ROWS
    seg_end = jnp.cumsum(padded)
    seg_start = seg_end - padded
    off = seg_start[None, :] + jnp.cumsum(cnt, axis=0) - cnt
    experts = jnp.arange(n_exp, dtype=jnp.int32)

    def lookup(e):
        return jnp.sum(jnp.where(e[:, :, None] == experts, off[:, None, :], 0), axis=2)

    pos1 = (lookup(e1) + r1).reshape(-1).astype(jnp.int32)
    pos2 = (lookup(e2) + r2).reshape(-1).astype(jnp.int32)
    n_chunks = (nt * MOE_TM * 2 + n_exp * FFN_ROWS) // FFN_ROWS
    cstart = jnp.arange(n_chunks, dtype=jnp.int32) * FFN_ROWS
    ce = jnp.sum((seg_end[None, :] <= cstart[:, None]).astype(jnp.int32), axis=1)
    cec = jnp.minimum(ce, n_exp - 1)
    seg_used = seg_start + tot
    valid = jnp.any((ce[:, None] == experts[None, :]) & (cstart[:, None] < seg_used[None, :]), axis=1)
    return pos1, pos2, cec.astype(jnp.int32), valid.astype(jnp.int32)


def _rope_tables(pos):
    half = ROT_DIM // 2
    inv_freq = ROPE_THETA ** (-jnp.arange(0, ROT_DIM, 2, dtype=F32) / ROT_DIM)
    ang = pos.astype(F32)[:, None] * inv_freq[None, :]
    cos, sin = jnp.cos(ang), jnp.sin(ang)
    l64 = np.arange(LANES) % HEAD_DIM
    f = l64 % half
    first = jnp.asarray(l64 < half)[None, :]
    second = jnp.asarray((l64 >= half) & (l64 < ROT_DIM))[None, :]
    c = jnp.where(first | second, cos[:, f], 1.0)
    s1 = jnp.where(first, -sin[:, f], 0.0)
    s2 = jnp.where(second, sin[:, f], 0.0)
    return c.astype(F32), s1.astype(F32), s2.astype(F32)


def _overlap_table_t():
    cs = np.arange(N_CMP)[None, :] * CMP_STRIDE
    js = np.arange(N_SELBLK)[:, None] * SEL_LEN
    ov = np.clip(np.minimum(cs + CMP_LEN, js + SEL_LEN) - np.maximum(cs, js), 0, None) / CMP_LEN
    ov[:, N_CMP - 1] = 0.0
    return ov.astype(np.float32)


def _expand_table_t():
    out = np.zeros((SEQ, LANES), np.float32)
    out[np.arange(SEQ), np.arange(SEQ) // SEL_LEN] = -NEG
    return out.reshape(N_QBLK // SEL_GROUP, SEL_GROUP * QBLK, LANES)


def kernel(x, norm1_g, w_in, pe_kc, w_kc1, w_kc2, pe_vc, w_vc1, w_vc2, w_o, norm2_g, w_rg, b_rg, w_re, b_re,
           w_gate, w_up, w_down, norm_f_g):
    nb, s, d = x.shape
    assert (s, d) == (SEQ, D_MODEL) and norm1_g.shape[0] == 1
    t = nb * s
    x2 = x.reshape(t, d)

    w_in_p = jnp.pad(w_in[0], ((0, 0), (0, D_IN_PAD - w_in.shape[2]))).astype(BF16)
    rc, rs1, rs2 = _rope_tables(jnp.arange(SEQ))
    cc, cs1, cs2 = _rope_tables(jnp.arange(N_CMP) * CMP_STRIDE + CMP_LEN - 1)
    half_flat = CMP_STRIDE * HEAD_DIM

    def cmp_w1(w1):
        return jnp.concatenate([w1[:half_flat], w1[half_flat:]], axis=1).astype(BF16)

    def cmp_pe(pe):
        rows = pe.reshape(2, half_flat)
        return jnp.concatenate([jnp.broadcast_to(rows[0:1], (16, half_flat)),
                                jnp.broadcast_to(rows[1:2], (16, half_flat))], axis=0).astype(BF16)

    wk2d = jnp.concatenate([w_kc2[0], w_kc2[0]], axis=1).astype(BF16)
    wv2t = w_vc2[0].T.astype(BF16)
    ovl_t = jnp.asarray(_overlap_table_t(), BF16)
    expand_t = jnp.asarray(_expand_table_t(), BF16)

    (aq, ak, av, aq4, ak4, av4, aq16, ak16, av16,
     bq, kcv, ksw, gates) = _proj(x2, norm1_g, w_in_p, rc, rs1, rs2)
    mix_a = _dilated((aq, ak, av), (aq4, ak4, av4), (aq16, ak16, av16))

    kvflat = kcv.reshape(nb, N_CMP, CMP_STRIDE, 2 * N_KV_NSA, HEAD_DIM).transpose(0, 3, 1, 2, 4)
    kvflat = kvflat.reshape(nb, 2 * N_KV_NSA, N_CMP, half_flat)
    mix_b = _nsa(bq, kvflat, ksw, gates, cmp_w1(w_kc1[0]), wk2d, cmp_pe(pe_kc[0]),
                 cmp_w1(w_vc1[0]), wv2t, cmp_pe(pe_vc[0]), cc, cs1, cs2, ovl_t, expand_t)

    n_router = N_GROUPS + N_GROUPS * EXPERTS_PER_GROUP
    wr_t = jnp.pad(jnp.concatenate([w_rg[0].T, w_re[0].reshape(d, -1).T], axis=0),
                   ((0, ROUTER_ROWS - n_router), (0, 0)))
    br = jnp.broadcast_to(jnp.pad(jnp.concatenate([b_rg[0], b_re[0].reshape(-1)]),
                                  (0, ROUTER_ROWS - n_router))[:, None], (ROUTER_ROWS, LANES))
    tri = jnp.asarray(np.triu(np.ones((MOE_TM, MOE_TM), np.float32), 1), BF16)
    tp = t // MOE_PARTS
    n_pad = 2 * tp + N_GROUPS * EXPERTS_PER_GROUP * FFN_ROWS
    w_o_bf = w_o[0].astype(BF16)
    gf = norm_f_g.reshape(1, d)
    routed = []
    for part in range(MOE_PARTS):
        x1, hpa, hpb, e1, e2, r1, r2, wtok, cnt = _post(x2, mix_a, mix_b, w_o_bf, norm2_g, wr_t, br, tri, part)
        pos1, pos2, chunk_expert, chunk_valid = _moe_plan(cnt[:, :, 0], e1[:, 0], e2[:, 0], r1[:, 0], r2[:, 0])
        xsa = _sc_scatter_rows(hpa, pos1, pos2, n_pad)
        xsb = _sc_scatter_rows(hpb, pos1, pos2, n_pad)
        routed.append((x1, wtok, pos1, pos2, chunk_expert, chunk_valid, xsa, xsb))
    gathered = []
    for x1, wtok, pos1, pos2, chunk_expert, chunk_valid, xsa, xsb in routed:
        ysa, ysb = _ffn(chunk_expert, chunk_valid, xsa, xsb, w_gate[0], w_up[0], w_down[0])
        pos12 = jnp.concatenate([pos1, pos2])
        gathered.append((x1, wtok, _sc_gather_rows(ysa, pos12), _sc_gather_rows(ysb, pos12)))
    out = None
    for part, (x1, wtok, yga, ygb) in enumerate(gathered):
        out = _final(x1, yga.reshape(2, tp, PACK_W), ygb.reshape(2, tp, PACK_W), wtok, gf, out, part)
    return out.reshape(nb, s, d)
```

```python
import numpy as np
import jax
import jax.numpy as jnp
from jax import lax
from jax.experimental import pallas as pl
from jax.experimental.pallas import tpu as pltpu
from jax.experimental.pallas import tpu_sc as plsc

F32 = jnp.float32
BF16 = jnp.bfloat16

D_MODEL = 1024
SEQ = 2048
HEAD_DIM = 64
N_HEADS_DIL = 8
DIL_PATTERNS = ((128, 1), (512, 4), (2048, 16))
N_HEADS_NSA = 8
N_KV_NSA = 2
NSA_REP = N_HEADS_NSA // N_KV_NSA
CMP_STRIDE = 16
CMP_LEN = 32
CMP_HIDDEN = 256
SEL_LEN = 64
N_SEL = 8
WIN = 512
ROPE_THETA = 500000.0
ROT_DIM = HEAD_DIM // 4
N_GROUPS = 8
EXPERTS_PER_GROUP = 8
D_FF_EXPERT = 256
EPS = 1e-6
NEG = -1e30
BIG = 1e9
LOWEST = -3.0e38
LOG2_E = 1.4426950408889634

A_COLS = N_HEADS_DIL * HEAD_DIM
B_COLS = N_HEADS_NSA * HEAD_DIM
KV_COLS = N_KV_NSA * HEAD_DIM
N_EXPERTS = N_GROUPS * EXPERTS_PER_GROUP
N_ROUTER = N_GROUPS + N_EXPERTS

LANES = 128
QBLK = 128
N_QBLK = SEQ // QBLK
N_CMP = SEQ // CMP_STRIDE
N_SELBLK = SEQ // SEL_LEN
D_IN_PAD = 23 * LANES
PROJ_TM = 1024
SEL_GROUP = 4
MOE_TM = 1024
MOE_PARTS = 1
ROUTER_ROWS = 80
FFN_ROWS = 512
FINAL_TM = 1024
PACK_W = D_MODEL // 4
BF16_SUBLANES = 16
ONES_ROWS = BF16_SUBLANES
PE_ROWS = BF16_SUBLANES
GATE_ROWS = 16
SC_WINDOW = 128
VMEM_LIMIT = 56 * 1024 * 1024

_NT = (((1,), (1,)), ((), ()))


def _lane_iota(shape):
    return lax.broadcasted_iota(jnp.int32, shape, 1)


def _row_iota(shape):
    return lax.broadcasted_iota(jnp.int32, shape, 0)


def _rope_lanes(y, c, s1, s2):
    return y * c + pltpu.roll(y, LANES - ROT_DIM // 2, axis=1) * s1 + pltpu.roll(y, ROT_DIM // 2, axis=1) * s2


def _proj_kernel(x_ref, g_ref, w_ref, c_ref, s1_ref, s2_ref,
                 aq_ref, ak_ref, av_ref, aq4_ref, ak4_ref, av4_ref, aq16_ref, ak16_ref, av16_ref,
                 bq_ref, kcv_ref, ksw_ref, gate_ref, nat_f32, d4_f32):
    x = x_ref[...]
    ms = jnp.mean(x * x, axis=-1, keepdims=True)
    h = (x * lax.rsqrt(ms + EPS) * g_ref[...]).astype(BF16)
    c = c_ref[...]
    s1 = s1_ref[...]
    s2 = s2_ref[...]

    def seg(lo, width):
        return jnp.dot(h, w_ref[:, lo:lo + width], preferred_element_type=F32)

    def store(dst, col, y, rope, scale, dilated=None):
        for j in range(y.shape[1] // LANES):
            yj = y[:, j * LANES:(j + 1) * LANES]
            if rope:
                yj = _rope_lanes(yj, c, s1, s2)
            if scale != 1.0:
                yj = yj * scale
            cols = slice(col + j * LANES, col + (j + 1) * LANES)
            dst[:, cols] = yj.astype(dst.dtype)
            if dilated is not None:
                d4_ref, d16_ref = dilated
                quarter = PROJ_TM // 4
                nat_f32[...] = yj
                for r in range(4):
                    part = nat_f32[pl.ds(r, quarter, stride=4), :]
                    d4_ref[r, :, cols] = part.astype(d4_ref.dtype)
                    d4_f32[r * quarter:(r + 1) * quarter, :] = part
                for r in range(4):
                    for a_ in range(4):
                        part = d4_f32[pl.ds(r * quarter + a_, quarter // 4, stride=4), :]
                        d16_ref[4 * a_ + r, :, cols] = part.astype(d16_ref.dtype)

    qscale = HEAD_DIM ** -0.5 * LOG2_E
    store(aq_ref, 0, seg(0, A_COLS), True, qscale, (aq4_ref, aq16_ref))
    store(ak_ref, 0, seg(A_COLS, A_COLS), True, 1.0, (ak4_ref, ak16_ref))
    store(av_ref, 0, seg(2 * A_COLS, A_COLS), False, 1.0, (av4_ref, av16_ref))
    nsa0 = 3 * A_COLS
    store(bq_ref, 0, seg(nsa0, B_COLS), True, qscale)
    store(kcv_ref, 0, seg(nsa0 + B_COLS, 2 * KV_COLS), False, 1.0)
    kv0 = nsa0 + B_COLS + 2 * KV_COLS
    for n, rope in enumerate((True, False, True, False)):
        store(ksw_ref, n * KV_COLS, seg(kv0 + n * KV_COLS, KV_COLS), rope, 1.0)
    gate_ref[...] = jax.nn.sigmoid(seg(kv0 + 4 * KV_COLS, LANES))


def _proj(x2, g1, w_in_p, rc, rs1, rs2):
    t = x2.shape[0]
    nb = t // SEQ
    nblk_s = SEQ // PROJ_TM
    row = lambda i: (i, 0)
    pos = lambda i: (i % nblk_s, 0)
    const = lambda i: (0, 0)
    perm = lambda i: (i // nblk_s, 0, i % nblk_s, 0)
    wide = jax.ShapeDtypeStruct((t, A_COLS), BF16)
    wide_spec = pl.BlockSpec((PROJ_TM, A_COLS), row)
    d4 = jax.ShapeDtypeStruct((nb, 4, SEQ // 4, A_COLS), BF16)
    d4_spec = pl.BlockSpec((None, 4, PROJ_TM // 4, A_COLS), perm)
    d16 = jax.ShapeDtypeStruct((nb, 16, SEQ // 16, A_COLS), BF16)
    d16_spec = pl.BlockSpec((None, 16, PROJ_TM // 16, A_COLS), perm)
    outs = [wide] * 3 + [d4] * 3 + [d16] * 3 + [wide, jax.ShapeDtypeStruct((t, 2 * KV_COLS), BF16), wide,
                                               jax.ShapeDtypeStruct((t, LANES), F32)]
    out_specs = [wide_spec] * 3 + [d4_spec] * 3 + [d16_spec] * 3 + [
        wide_spec, pl.BlockSpec((PROJ_TM, 2 * KV_COLS), row), wide_spec, pl.BlockSpec((PROJ_TM, LANES), row)]
    return pl.pallas_call(
        _proj_kernel,
        grid=(t // PROJ_TM,),
        in_specs=[pl.BlockSpec((PROJ_TM, D_MODEL), row), pl.BlockSpec((1, D_MODEL), const),
                  pl.BlockSpec((D_MODEL, D_IN_PAD), const),
                  pl.BlockSpec((PROJ_TM, LANES), pos), pl.BlockSpec((PROJ_TM, LANES), pos),
                  pl.BlockSpec((PROJ_TM, LANES), pos)],
        out_specs=out_specs,
        out_shape=outs,
        scratch_shapes=[pltpu.VMEM((PROJ_TM, LANES), F32)] * 2,
        compiler_params=pltpu.CompilerParams(dimension_semantics=("arbitrary",),
                                             vmem_limit_bytes=VMEM_LIMIT),
        name="proj",
    )(x2, g1, w_in_p, rc, rs1, rs2)


def _dilated_kernel(q1_ref, k1_ref, v1_ref, q4_ref, k4_ref, v4_ref, q16_ref, k16_ref, v16_ref, o_ref,
                    qt, vt, op, lp, on0, on1, on2, ln0, ln1, ln2):
    c2 = _row_iota((2 * QBLK, 2 * QBLK))
    a2 = _lane_iota((2 * QBLK, 2 * QBLK)) % QBLK
    band = jnp.where((c2 >= a2) & (c2 <= a2 + QBLK), 0.0, NEG)
    band_noprev = jnp.where(c2 < QBLK, NEG, band)
    c1 = _row_iota((QBLK, 2 * QBLK))
    a1 = _lane_iota((QBLK, 2 * QBLK)) % QBLK
    band_own = jnp.where(c1 <= a1, 0.0, NEG)
    top_sq = _row_iota((LANES, QBLK)) < HEAD_DIM

    for j in range(N_QBLK):
        vt[j, LANES:, :] = jnp.ones((ONES_ROWS, QBLK), BF16)

    inputs = ((q1_ref, k1_ref, v1_ref), (q4_ref, k4_ref, v4_ref), (q16_ref, k16_ref, v16_ref))
    for pi, (window, dil) in enumerate(DIL_PATTERNS):
        seg_len = SEQ // dil
        nseg_blk = seg_len // QBLK
        q_ref, k_ref, v_ref = inputs[pi]

        def block(ref, j, nseg_blk=nseg_blk, dil=dil):
            if dil == 1:
                return ref[j * QBLK:(j + 1) * QBLK, :]
            return ref[j // nseg_blk, (j % nseg_blk) * QBLK:(j % nseg_blk + 1) * QBLK, :]

        for j in range(N_QBLK):
            qt[j] = block(q_ref, j).T
            vt[j, 0:LANES, :] = block(v_ref, j).T

        prevs, scores = [], []
        for j in range(N_QBLK):
            qtb = qt[j]
            zero = jnp.zeros_like(qtb)
            q_both = jnp.concatenate([jnp.where(top_sq, qtb, zero), jnp.where(top_sq, zero, qtb)], axis=1)
            if nseg_blk == 1:
                keys, bias, jp = block(k_ref, j), band_own, j
            else:
                first = j % nseg_blk == 0
                jp = j if first else j - 1
                bias = band_noprev if first else band
                keys = jnp.concatenate([block(k_ref, jp), block(k_ref, j)], axis=0)
            prevs.append(jp)
            scores.append(jnp.dot(keys, q_both, preferred_element_type=F32) + bias)
        stats = []
        for s in scores:
            m = jnp.max(s, axis=0, keepdims=True)
            stats.append((m, jnp.exp2((s - m).astype(BF16))))
        outs = []
        for j, (jp, (m, p)) in enumerate(zip(prevs, stats)):
            v_span = vt[j] if nseg_blk == 1 else jnp.concatenate([vt[jp], vt[j]], axis=1)
            outs.append(jnp.dot(v_span, p, preferred_element_type=F32))
        for j, ((m, p), res) in enumerate(zip(stats, outs)):
            l = res[LANES:LANES + 1, :]
            lse = m + jnp.log2(l)
            o_t = jnp.where(top_sq, res[0:LANES, 0:QBLK] / l[:, 0:QBLK], res[0:LANES, QBLK:] / l[:, QBLK:])
            lse_t = jnp.where(top_sq, jnp.broadcast_to(lse[:, 0:QBLK], (LANES, QBLK)),
                              jnp.broadcast_to(lse[:, QBLK:], (LANES, QBLK)))
            op[j * QBLK:(j + 1) * QBLK, :] = o_t.T
            lp[j * QBLK:(j + 1) * QBLK, :] = lse_t.T

        for r in range(dil):
            dst = pl.ds(r, seg_len, stride=dil) if dil > 1 else pl.ds(0, SEQ)
            src = pl.ds(r * seg_len, seg_len)
            (on0, on1, on2)[pi][dst, :] = op[src, :]
            (ln0, ln1, ln2)[pi][dst, :] = lp[src, :]

    l0, l1, l2 = ln0[...], ln1[...], ln2[...]
    mx = jnp.maximum(jnp.maximum(l0, l1), l2)
    e0, e1, e2 = jnp.exp2(l0 - mx), jnp.exp2(l1 - mx), jnp.exp2(l2 - mx)
    den = e0 + e1 + e2
    out = (e0 / den) * on0[...] + (e1 / den) * on1[...] + (e2 / den) * on2[...]
    o_ref[...] = out.astype(o_ref.dtype)


def _dilated(a1, a4, a16):
    t = a1[0].shape[0]
    nb = t // SEQ
    spec1 = pl.BlockSpec((SEQ, LANES), lambda b, hp: (b, hp))
    spec4 = pl.BlockSpec((None, 4, SEQ // 4, LANES), lambda b, hp: (b, 0, 0, hp))
    spec16 = pl.BlockSpec((None, 16, SEQ // 16, LANES), lambda b, hp: (b, 0, 0, hp))
    return pl.pallas_call(
        _dilated_kernel,
        grid=(nb, N_HEADS_DIL // 2),
        in_specs=[spec1] * 3 + [spec4] * 3 + [spec16] * 3,
        out_specs=spec1,
        out_shape=jax.ShapeDtypeStruct((t, A_COLS), BF16),
        scratch_shapes=[pltpu.VMEM((N_QBLK, LANES, QBLK), BF16), pltpu.VMEM((N_QBLK, LANES + ONES_ROWS, QBLK), BF16)]
        + [pltpu.VMEM((SEQ, LANES), F32)] * 8,
        compiler_params=pltpu.CompilerParams(dimension_semantics=("arbitrary", "arbitrary"),
                                             vmem_limit_bytes=VMEM_LIMIT),
        name="dilated",
    )(*a1, *a4, *a16)


def _gelu_tanh(x):
    return 0.5 * x * (1.0 + jnp.tanh(np.sqrt(2.0 / np.pi).astype(np.float32) * (x + 0.044715 * (x * x * x))))


def _softmax_step(state, s, v_t):
    m, l, acc = state
    mn = jnp.maximum(m, jnp.max(s, axis=0, keepdims=True))
    al = jnp.exp2(m - mn)
    p = jnp.exp2(s - mn)
    l = al * l + jnp.sum(p, axis=0, keepdims=True)
    acc = al * acc + jnp.dot(v_t, p.astype(BF16), preferred_element_type=F32)
    return mn, l, acc


def _nsa_kernel(q_ref, kflat_ref, vflat_ref, ksw_ref, gate_ref,
                wk1_ref, wk2d_ref, pek_ref, wv1_ref, wv2t_ref, pev_ref,
                cc_ref, cs1_ref, cs2_ref, ovl_ref, exp_ref,
                o_ref,
                qt, vst, vwt, gtt, kc2, vct):
    g = pl.program_id(1)
    nq = NSA_REP * QBLK

    for i in range(N_QBLK):
        rows = pl.ds(i * QBLK, QBLK)
        qt[i, 0:LANES, :] = q_ref[rows, 0:LANES].T
        qt[i, LANES:2 * LANES, :] = q_ref[rows, LANES:2 * LANES].T
        vs_t = ksw_ref[rows, LANES:2 * LANES].T
        vst[i] = jnp.where(g == 0, vs_t[0:HEAD_DIM], vs_t[HEAD_DIM:])
        vw_t = ksw_ref[rows, 3 * LANES:4 * LANES].T
        vwt[i] = jnp.where(g == 0, vw_t[0:HEAD_DIM], vw_t[HEAD_DIM:])
        gt = gate_ref[rows, :]
        gt = jnp.where(g == 0, gt, pltpu.roll(gt, LANES - 3 * NSA_REP, axis=1))
        gtt[i] = gt.T[0:GATE_ROWS, :]

    def hidden(flat_ref, w1_ref, pe_ref):
        ab = jnp.dot(flat_ref[...], w1_ref[...], preferred_element_type=F32)
        pb = (jnp.dot(pe_ref[0:PE_ROWS, :], w1_ref[:, 0:CMP_HIDDEN], preferred_element_type=F32)
              + jnp.dot(pe_ref[PE_ROWS:, :], w1_ref[:, CMP_HIDDEN:], preferred_element_type=F32))
        hid = ab[:, 0:CMP_HIDDEN] + pltpu.roll(ab[:, CMP_HIDDEN:], N_CMP - 1, axis=0) + pb[0:1, :]
        return _gelu_tanh(hid).astype(BF16)

    hk = hidden(kflat_ref, wk1_ref, pek_ref)
    kc2[...] = _rope_lanes(jnp.dot(hk, wk2d_ref[...], preferred_element_type=F32),
                           cc_ref[...], cs1_ref[...], cs2_ref[...]).astype(BF16)
    hv = hidden(vflat_ref, wv1_ref, pev_ref)
    vct[...] = lax.dot_general(wv2t_ref[...], hv, _NT, preferred_element_type=F32).astype(BF16)

    key = _row_iota((QBLK, nq))
    qry = _lane_iota((QBLK, nq)) % QBLK
    causal = jnp.where(key <= qry, 0.0, NEG)
    upper = jnp.where(key > qry, 0.0, NEG)
    blk = _row_iota((N_SELBLK, QBLK))
    qry1 = _lane_iota((N_SELBLK, QBLK))
    grp_keys = SEL_GROUP * QBLK
    win_blocks = WIN // QBLK

    def qgroup(w, full_window):
        blocks = [SEL_GROUP * w + d for d in range(SEL_GROUP)]
        row0s = [pl.multiple_of(i * QBLK, QBLK) for i in blocks]

        q_pads = []
        for i in blocks:
            q_t = jnp.concatenate([qt[i, r * HEAD_DIM:(r + 1) * HEAD_DIM, :] for r in range(NSA_REP)], axis=1)
            zero = jnp.zeros_like(q_t)
            q_pads.append(jnp.where(g == 0, jnp.concatenate([q_t, zero], axis=0),
                                    jnp.concatenate([zero, q_t], axis=0)))
        s_cmp = [jnp.dot(kc2[...], qp, preferred_element_type=F32) for qp in q_pads]
        s_win, v_win = [], []
        for d, (i, qp) in enumerate(zip(blocks, q_pads)):
            if full_window:
                first = i - win_blocks
                keys = ksw_ref[pl.ds(pl.multiple_of(first * QBLK, QBLK), (win_blocks + 1) * QBLK), 2 * LANES:3 * LANES]
                s = jnp.dot(keys, qp, preferred_element_type=F32)
                s = jnp.concatenate([s[0:QBLK] + upper, s[QBLK:win_blocks * QBLK], s[win_blocks * QBLK:] + causal],
                                    axis=0)
                v_t = jnp.concatenate([vwt[first + e] for e in range(win_blocks + 1)], axis=1)
            else:
                s = jnp.dot(ksw_ref[0:(d + 1) * QBLK, 2 * LANES:3 * LANES], qp, preferred_element_type=F32)
                s = jnp.concatenate([s[0:d * QBLK], s[d * QBLK:] + causal], axis=0) if d else s + causal
                v_t = jnp.concatenate([vwt[e] for e in range(d + 1)], axis=1) if d else vwt[0]
            s_win.append(s)
            v_win.append(v_t)

        pcs = []
        for row0, s in zip(row0s, s_cmp):
            cvalid = (CMP_STRIDE * key + (CMP_LEN - 1) <= row0 + qry) & (key < N_CMP - 1)
            s = jnp.where(cvalid, s, NEG)
            m = jnp.max(s, axis=0, keepdims=True)
            p = jnp.where(cvalid, jnp.exp2(s - m), 0.0)
            l = jnp.sum(p, axis=0, keepdims=True)
            pcs.append(p / jnp.where(l > 0, l, 1.0))
        o_cmp = [jnp.dot(vct[...], pc.astype(BF16), preferred_element_type=F32) for pc in pcs]

        ovl = ovl_ref[...]
        imps = []
        for pc in pcs:
            psum = pc[:, 0:QBLK] + pc[:, QBLK:2 * QBLK] + pc[:, 2 * QBLK:3 * QBLK] + pc[:, 3 * QBLK:]
            p_hi = psum.astype(BF16)
            p_lo = (psum - p_hi.astype(F32)).astype(BF16)
            imps.append(jnp.dot(ovl, p_hi, preferred_element_type=F32) + jnp.dot(ovl, p_lo, preferred_element_type=F32))

        o_win = []
        for s, v_t in zip(s_win, v_win):
            m = jnp.max(s, axis=0, keepdims=True)
            p = jnp.exp2(s - m)
            l = jnp.sum(p, axis=0, keepdims=True)
            o_win.append(jnp.dot(v_t, p.astype(BF16), preferred_element_type=F32) / l)

        q_sel = []
        for row0, imp, qp in zip(row0s, imps, q_pads):
            t1 = row0 + qry1
            blk_t = t1 // SEL_LEN
            forced = (blk == 0) | (blk == blk_t) | (blk == blk_t - 1)
            v = jnp.where(blk * SEL_LEN <= t1, jnp.where(forced, BIG, imp), -BIG)
            selm = jnp.zeros((N_SELBLK, QBLK), F32)
            for _ in range(N_SEL):
                mx = jnp.max(v, axis=0, keepdims=True)
                first_hit = jnp.min(jnp.where(v == mx, blk, N_SELBLK), axis=0, keepdims=True)
                hit = blk == first_hit
                selm = jnp.where(hit, 1.0, selm)
                v = jnp.where(hit, LOWEST, v)
            notsel = jnp.concatenate([selm - 1.0, jnp.zeros((LANES - N_SELBLK, QBLK), F32)], axis=0).astype(BF16)
            q_sel.append(jnp.concatenate([qp, jnp.concatenate([notsel] * NSA_REP, axis=1)], axis=0))

        def sel_keys(t, n_rows):
            rows_t = pl.ds(pl.multiple_of(t * grp_keys, grp_keys), n_rows)
            return jnp.concatenate([ksw_ref[rows_t, 0:LANES], exp_ref[t, 0:n_rows, :]], axis=1)

        def sel_values(t, n_blk):
            return jnp.concatenate([vst[t * SEL_GROUP + e] for e in range(n_blk)], axis=1) if n_blk > 1 else vst[t * SEL_GROUP]

        wide = SEL_GROUP * nq
        st = (jnp.full((1, wide), LOWEST, F32), jnp.zeros((1, wide), F32), jnp.zeros((HEAD_DIM, wide), F32))
        if full_window:
            q_all = jnp.concatenate(q_sel, axis=1)

            def sel_body(t, st):
                s = jnp.dot(sel_keys(t, grp_keys), q_all, preferred_element_type=F32)
                return _softmax_step(st, s, sel_values(t, SEL_GROUP))

            for t_ in range(w):
                st = sel_body(t_, st)
        o_sel = []
        for d, qs in enumerate(q_sel):
            s = jnp.dot(sel_keys(w, (d + 1) * QBLK), qs, preferred_element_type=F32)
            s = jnp.concatenate([s[0:d * QBLK], s[d * QBLK:] + causal], axis=0) if d else s + causal
            lanes = slice(d * nq, (d + 1) * nq)
            m, l, acc = _softmax_step((st[0][:, lanes], st[1][:, lanes], st[2][:, lanes]), s, sel_values(w, d + 1))
            o_sel.append(acc / l)

        for n, (i, row0) in enumerate(zip(blocks, row0s)):
            gti = gtt[i]

            def gate(br):
                return jnp.concatenate([gti[r * 3 + br:r * 3 + br + 1, :] for r in range(NSA_REP)], axis=1)

            out_t = gate(0) * o_cmp[n] + gate(1) * o_sel[n] + gate(2) * o_win[n]
            for pair in range(NSA_REP // 2):
                both = jnp.concatenate([out_t[:, (2 * pair) * QBLK:(2 * pair + 1) * QBLK],
                                        out_t[:, (2 * pair + 1) * QBLK:(2 * pair + 2) * QBLK]], axis=0)
                o_ref[pl.ds(row0, QBLK), pair * LANES:(pair + 1) * LANES] = both.T.astype(o_ref.dtype)

    qgroup(0, False)
    for w in range(1, N_QBLK // SEL_GROUP):
        qgroup(w, True)


def _nsa(bq, kvflat, ksw, gates, wk1, wk2d, pek, wv1, wv2t, pev, cc, cs1, cs2, ovl_t, expand_t):
    t = bq.shape[0]
    nb = t // SEQ
    const2 = lambda b, g: (0, 0)
    return pl.pallas_call(
        _nsa_kernel,
        grid=(nb, N_KV_NSA),
        in_specs=[
            pl.BlockSpec((SEQ, 2 * LANES), lambda b, g: (b, g)),
            pl.BlockSpec((None, None, N_CMP, CMP_STRIDE * HEAD_DIM), lambda b, g: (b, g, 0, 0)),
            pl.BlockSpec((None, None, N_CMP, CMP_STRIDE * HEAD_DIM), lambda b, g: (b, N_KV_NSA + g, 0, 0)),
            pl.BlockSpec((SEQ, 4 * LANES), lambda b, g: (b, 0)),
            pl.BlockSpec((SEQ, LANES), lambda b, g: (b, 0)),
            pl.BlockSpec((CMP_STRIDE * HEAD_DIM, 2 * CMP_HIDDEN), const2),
            pl.BlockSpec((CMP_HIDDEN, LANES), const2),
            pl.BlockSpec((2 * PE_ROWS, CMP_STRIDE * HEAD_DIM), const2),
            pl.BlockSpec((CMP_STRIDE * HEAD_DIM, 2 * CMP_HIDDEN), const2),
            pl.BlockSpec((HEAD_DIM, CMP_HIDDEN), const2),
            pl.BlockSpec((2 * PE_ROWS, CMP_STRIDE * HEAD_DIM), const2),
            pl.BlockSpec((N_CMP, LANES), const2),
            pl.BlockSpec((N_CMP, LANES), const2),
            pl.BlockSpec((N_CMP, LANES), const2),
            pl.BlockSpec((N_SELBLK, LANES), const2),
            pl.BlockSpec((N_QBLK // SEL_GROUP, SEL_GROUP * QBLK, LANES), lambda b, g: (0, 0, 0)),
        ],
        out_specs=pl.BlockSpec((SEQ, 2 * LANES), lambda b, g: (b, g)),
        out_shape=jax.ShapeDtypeStruct((t, B_COLS), BF16),
        scratch_shapes=[pltpu.VMEM((N_QBLK, 2 * LANES, QBLK), BF16),
                        pltpu.VMEM((N_QBLK, HEAD_DIM, QBLK), BF16),
                        pltpu.VMEM((N_QBLK, HEAD_DIM, QBLK), BF16),
                        pltpu.VMEM((N_QBLK, GATE_ROWS, QBLK), F32),
                        pltpu.VMEM((N_CMP, LANES), BF16),
                        pltpu.VMEM((HEAD_DIM, N_CMP), BF16)],
        compiler_params=pltpu.CompilerParams(dimension_semantics=("arbitrary", "arbitrary"),
                                             vmem_limit_bytes=VMEM_LIMIT),
        name="nsa",
    )(bq, kvflat, kvflat, ksw, gates, wk1, wk2d, pek, wv1, wv2t, pev, cc, cs1, cs2, ovl_t, expand_t)


def _pack_bf16_pairs(lo, hi):
    lo_b = lax.bitcast_convert_type(lo.astype(BF16).astype(F32), jnp.uint32)
    hi_b = lax.bitcast_convert_type(hi.astype(BF16).astype(F32), jnp.uint32)
    return (hi_b & jnp.uint32(0xFFFF0000)) | (lo_b >> 16)


def _unpack_bf16_pairs(p):
    lo = lax.bitcast_convert_type(p << 16, F32)
    hi = lax.bitcast_convert_type(p & jnp.uint32(0xFFFF0000), F32)
    return lo, hi


def _post_kernel(x_ref, ma_ref, mb_ref, wo_ref, g2_ref, wr_ref, br_ref, tri_ref,
                 x1_ref, ha_ref, hb_ref, e1_ref, e2_ref, r1_ref, r2_ref, w_ref, cnt_ref):
    x1 = (x_ref[...] + jnp.dot(ma_ref[...], wo_ref[0:A_COLS, :], preferred_element_type=F32)
          + jnp.dot(mb_ref[...], wo_ref[A_COLS:, :], preferred_element_type=F32))
    x1_ref[...] = x1
    ms = jnp.mean(x1 * x1, axis=-1, keepdims=True)
    h2 = x1 * lax.rsqrt(ms + EPS) * g2_ref[...]
    ha_ref[...] = _pack_bf16_pairs(h2[:, 0:PACK_W], h2[:, 2 * PACK_W:3 * PACK_W])
    hb_ref[...] = _pack_bf16_pairs(h2[:, PACK_W:2 * PACK_W], h2[:, 3 * PACK_W:])

    h_hi = h2.astype(BF16)
    h_lo = (h2 - h_hi.astype(F32)).astype(BF16)
    wr = wr_ref[...]
    w_hi = wr.astype(BF16)
    w_lo = (wr - w_hi.astype(F32)).astype(BF16)
    lg = (lax.dot_general(w_hi, h_hi, _NT, preferred_element_type=F32)
          + lax.dot_general(w_hi, h_lo, _NT, preferred_element_type=F32)
          + lax.dot_general(w_lo, h_hi, _NT, preferred_element_type=F32)) + br_ref[:, 0:1]
    gl = lg[0:N_GROUPS, :]
    sub = _row_iota(gl.shape)
    gmax = jnp.max(gl, axis=0, keepdims=True)
    p_g = 1.0 / jnp.sum(jnp.exp(gl - gmax), axis=0, keepdims=True)
    g_star = jnp.min(jnp.where(gl == gmax, sub, N_GROUPS), axis=0, keepdims=True)
    el = jnp.zeros_like(gl)
    for gi in range(N_GROUPS):
        lo = N_GROUPS + gi * EXPERTS_PER_GROUP
        el = el + jnp.where(g_star == gi, lg[lo:lo + EXPERTS_PER_GROUP, :], 0.0)
    emax = jnp.max(el, axis=0, keepdims=True)
    ee = jnp.exp(el - emax)
    pe = ee / jnp.sum(ee, axis=0, keepdims=True)
    v1 = jnp.max(pe, axis=0, keepdims=True)
    i1 = jnp.min(jnp.where(pe == v1, sub, EXPERTS_PER_GROUP), axis=0, keepdims=True)
    pe2 = jnp.where(sub == i1, -1.0, pe)
    v2 = jnp.max(pe2, axis=0, keepdims=True)
    i2 = jnp.min(jnp.where(pe2 == v2, sub, EXPERTS_PER_GROUP), axis=0, keepdims=True)
    tot = v1 + v2
    e1 = g_star * EXPERTS_PER_GROUP + i1
    e2 = g_star * EXPERTS_PER_GROUP + i2
    e1_ref[0] = e1
    e2_ref[0] = e2

    wslab = jnp.concatenate([v1 / tot * p_g, v2 / tot * p_g, jnp.zeros((LANES - 2, MOE_TM), F32)], axis=0)
    for j in range(MOE_TM // LANES):
        w_ref[j * LANES:(j + 1) * LANES, :] = wslab[:, j * LANES:(j + 1) * LANES].T

    n_exp = N_EXPERTS
    sub_e = _row_iota((n_exp, MOE_TM))
    hit1 = sub_e == e1
    hit2 = sub_e == e2
    assigned = jnp.where(hit1, 1.0, 0.0) + jnp.where(hit2, 1.0, 0.0)
    before = jnp.dot(assigned.astype(BF16), tri_ref[...], preferred_element_type=F32)
    r1_ref[0] = jnp.sum(jnp.where(hit1, before, 0.0), axis=0, keepdims=True).astype(jnp.int32)
    r2_ref[0] = jnp.sum(jnp.where(hit2, before, 0.0), axis=0, keepdims=True).astype(jnp.int32)
    cnt = jnp.sum(assigned, axis=1, keepdims=True).astype(jnp.int32)
    cnt_ref[0] = jnp.broadcast_to(cnt, (n_exp, LANES))


def _post(x2, mix_a, mix_b, w_o, g2, wr_t, br, tri, part):
    t = x2.shape[0] // MOE_PARTS
    nt = t // MOE_TM
    first = part * nt
    n_exp = N_EXPERTS
    row = lambda i: (i, 0)
    const = lambda i: (0, 0)
    tok = lambda i: (i, 0, 0)
    src = lambda i: (first + i, 0)
    tok_spec = pl.BlockSpec((1, 1, MOE_TM), tok)
    tok_shape = jax.ShapeDtypeStruct((nt, 1, MOE_TM), jnp.int32)
    return pl.pallas_call(
        _post_kernel,
        grid=(nt,),
        in_specs=[pl.BlockSpec((MOE_TM, D_MODEL), src), pl.BlockSpec((MOE_TM, A_COLS), src),
                  pl.BlockSpec((MOE_TM, B_COLS), src), pl.BlockSpec((D_MODEL, D_MODEL), const),
                  pl.BlockSpec((1, D_MODEL), const), pl.BlockSpec((ROUTER_ROWS, D_MODEL), const),
                  pl.BlockSpec((ROUTER_ROWS, LANES), const), pl.BlockSpec((MOE_TM, MOE_TM), const)],
        out_specs=[pl.BlockSpec((MOE_TM, D_MODEL), row),
                   pl.BlockSpec((MOE_TM, PACK_W), row), pl.BlockSpec((MOE_TM, PACK_W), row),
                   tok_spec, tok_spec, tok_spec, tok_spec,
                   pl.BlockSpec((MOE_TM, LANES), row),
                   pl.BlockSpec((1, n_exp, LANES), tok)],
        out_shape=[jax.ShapeDtypeStruct((t, D_MODEL), F32),
                   jax.ShapeDtypeStruct((t, PACK_W), jnp.uint32), jax.ShapeDtypeStruct((t, PACK_W), jnp.uint32),
                   tok_shape, tok_shape, tok_shape, tok_shape,
                   jax.ShapeDtypeStruct((t, LANES), F32),
                   jax.ShapeDtypeStruct((nt, n_exp, LANES), jnp.int32)],
        compiler_params=pltpu.CompilerParams(dimension_semantics=("arbitrary",),
                                             vmem_limit_bytes=VMEM_LIMIT),
        name="post",
    )(x2, mix_a, mix_b, w_o, g2, wr_t, br, tri)


def _sc_mesh():
    return plsc.VectorSubcoreMesh(core_axis_name="core", subcore_axis_name="subcore")


def _sc_scatter_rows(x, pos1, pos2, n_out):
    r, c = x.shape

    @pl.kernel(out_type=jax.ShapeDtypeStruct((n_out, c), x.dtype), mesh=_sc_mesh(), scratch_types=[])
    def scatter_kernel(x_hbm, p1_hbm, p2_hbm, o_hbm):
        def body(x_vmem, p1_vmem, p2_vmem):
            pltpu.sync_copy(x_vmem, o_hbm.at[p1_vmem.at[0]])
            pltpu.sync_copy(x_vmem, o_hbm.at[p2_vmem.at[0]])

        idx = pl.BlockSpec((1, SC_WINDOW), lambda i: (0, i))
        pltpu.emit_pipeline(body, grid=(r // SC_WINDOW,),
                            in_specs=[pl.BlockSpec((SC_WINDOW, c), lambda i: (i, 0)), idx, idx], out_specs=[],
                            core_axis_name=("core", "subcore"),
                            dimension_semantics=(pltpu.PARALLEL,))(x_hbm, p1_hbm, p2_hbm)

    return scatter_kernel(x, pos1.reshape(1, r), pos2.reshape(1, r))


def _sc_gather_rows(y, idx):
    n = idx.shape[0]
    c = y.shape[1]

    @pl.kernel(out_type=jax.ShapeDtypeStruct((n, c), y.dtype), mesh=_sc_mesh(), scratch_types=[])
    def gather_kernel(y_hbm, i_hbm, o_hbm):
        def body(i_vmem, o_vmem):
            pltpu.sync_copy(y_hbm.at[i_vmem.at[0]], o_vmem)

        pltpu.emit_pipeline(body, grid=(n // SC_WINDOW,),
                            in_specs=[pl.BlockSpec((1, SC_WINDOW), lambda i: (0, i))],
                            out_specs=[pl.BlockSpec((SC_WINDOW, c), lambda i: (i, 0))],
                            core_axis_name=("core", "subcore"),
                            dimension_semantics=(pltpu.PARALLEL,))(i_hbm, o_hbm)

    return gather_kernel(y, idx.reshape(1, n))


def _ffn_kernel(ce_ref, cv_ref, xa_ref, xb_ref, wg_ref, wu_ref, wd_ref, ya_ref, yb_ref, wg_bf, wu_bf, wd_bf):
    c = pl.program_id(0)
    prev = ce_ref[jnp.maximum(c - 1, 0)]

    @pl.when((c == 0) | (ce_ref[c] != prev))
    def _():
        wg_bf[...] = wg_ref[0].astype(BF16)
        wu_bf[...] = wu_ref[0].astype(BF16)
        wd_bf[...] = wd_ref[0].astype(BF16)

    @pl.when(cv_ref[c] == 0)
    def _():
        ya_ref[...] = jnp.zeros_like(ya_ref)
        yb_ref[...] = jnp.zeros_like(yb_ref)

    @pl.when(cv_ref[c] != 0)
    def _():
        a_lo, a_hi = _unpack_bf16_pairs(xa_ref[...])
        b_lo, b_hi = _unpack_bf16_pairs(xb_ref[...])
        xs = jnp.concatenate([a_lo, b_lo, a_hi, b_hi], axis=1).astype(BF16)
        gate = jnp.dot(xs, wg_bf[...], preferred_element_type=F32)
        up = jnp.dot(xs, wu_bf[...], preferred_element_type=F32)
        he = (gate * jax.nn.sigmoid(gate) * up).astype(BF16)
        y = jnp.dot(he, wd_bf[...], preferred_element_type=F32)
        ya_ref[...] = _pack_bf16_pairs(y[:, 0:PACK_W], y[:, 2 * PACK_W:3 * PACK_W])
        yb_ref[...] = _pack_bf16_pairs(y[:, PACK_W:2 * PACK_W], y[:, 3 * PACK_W:])


def _ffn(chunk_expert, chunk_valid, xsa, xsb, wg, wu, wd):
    n_pad = xsa.shape[0]
    rows = pl.BlockSpec((FFN_ROWS, PACK_W), lambda c, ce, cv: (c, 0))
    grid_spec = pltpu.PrefetchScalarGridSpec(
        num_scalar_prefetch=2,
        grid=(n_pad // FFN_ROWS,),
        in_specs=[rows, rows,
                  pl.BlockSpec((1, D_MODEL, D_FF_EXPERT), lambda c, ce, cv: (ce[c], 0, 0)),
                  pl.BlockSpec((1, D_MODEL, D_FF_EXPERT), lambda c, ce, cv: (ce[c], 0, 0)),
                  pl.BlockSpec((1, D_FF_EXPERT, D_MODEL), lambda c, ce, cv: (ce[c], 0, 0))],
        out_specs=[rows, rows],
        scratch_shapes=[pltpu.VMEM((D_MODEL, D_FF_EXPERT), BF16), pltpu.VMEM((D_MODEL, D_FF_EXPERT), BF16),
                        pltpu.VMEM((D_FF_EXPERT, D_MODEL), BF16)],
    )
    out = jax.ShapeDtypeStruct((n_pad, PACK_W), jnp.uint32)
    return pl.pallas_call(
        _ffn_kernel,
        grid_spec=grid_spec,
        out_shape=[out, out],
        compiler_params=pltpu.CompilerParams(dimension_semantics=("arbitrary",),
                                             vmem_limit_bytes=VMEM_LIMIT),
        name="ffn",
    )(chunk_expert, chunk_valid, xsa, xsb, wg, wu, wd)


def _final_kernel(x1_ref, y1a_ref, y1b_ref, y2a_ref, y2b_ref, w_ref, gf_ref, o_ref):
    w1 = w_ref[:, 0:1]
    w2 = w_ref[:, 1:2]
    a1_lo, a1_hi = _unpack_bf16_pairs(y1a_ref[...])
    b1_lo, b1_hi = _unpack_bf16_pairs(y1b_ref[...])
    a2_lo, a2_hi = _unpack_bf16_pairs(y2a_ref[...])
    b2_lo, b2_hi = _unpack_bf16_pairs(y2b_ref[...])
    y = jnp.concatenate([w1 * a1_lo + w2 * a2_lo, w1 * b1_lo + w2 * b2_lo,
                         w1 * a1_hi + w2 * a2_hi, w1 * b1_hi + w2 * b2_hi], axis=1)
    xo = x1_ref[...] + y
    ms = jnp.mean(xo * xo, axis=-1, keepdims=True)
    o_ref[...] = xo * lax.rsqrt(ms + EPS) * gf_ref[...]


def _final(x1, yga, ygb, wtok, gf, out_prev, part):
    t = x1.shape[0]
    first = part * (t // FINAL_TM)
    row = lambda i: (i, 0)
    one = pl.BlockSpec((None, FINAL_TM, PACK_W), lambda i: (0, i, 0))
    two = pl.BlockSpec((None, FINAL_TM, PACK_W), lambda i: (1, i, 0))
    in_specs = [pl.BlockSpec((FINAL_TM, D_MODEL), row), one, one, two, two,
                pl.BlockSpec((FINAL_TM, LANES), row), pl.BlockSpec((1, D_MODEL), lambda i: (0, 0))]
    args = [x1, yga, ygb, yga, ygb, wtok, gf]
    kern = _final_kernel
    aliases = {}
    if out_prev is not None:
        in_specs.append(pl.BlockSpec(memory_space=pl.ANY))
        args.append(out_prev)
        aliases = {len(args) - 1: 0}
        kern = lambda *refs: _final_kernel(*refs[:7], refs[8])
    return pl.pallas_call(
        kern,
        grid=(t // FINAL_TM,),
        in_specs=in_specs,
        out_specs=pl.BlockSpec((FINAL_TM, D_MODEL), lambda i: (first + i, 0)),
        out_shape=jax.ShapeDtypeStruct((t * MOE_PARTS, D_MODEL), F32),
        input_output_aliases=aliases,
        compiler_params=pltpu.CompilerParams(dimension_semantics=("arbitrary",),
                                             vmem_limit_bytes=VMEM_LIMIT),
        name="final",
    )(*args)


def _moe_plan(cnt, e1, e2, r1, r2):
    nt, n_exp = cnt.shape
    tot = jnp.sum(cnt, axis=0)
    padded = (tot + FFN_ROWS - 1) // FFN_ROWS * FFN_ROWS
    seg_end = jnp.cumsum(padded)
    seg_start = seg_end - padded
    off = seg_start[None, :] + jnp.cumsum(cnt, axis=0) - cnt
    experts = jnp.arange(n_exp, dtype=jnp.int32)

    def lookup(e):
        return jnp.sum(jnp.where(e[:, :, None] == experts, off[:, None, :], 0), axis=2)

    pos1 = (lookup(e1) + r1).reshape(-1).astype(jnp.int32)
    pos2 = (lookup(e2) + r2).reshape(-1).astype(jnp.int32)
    n_chunks = (nt * MOE_TM * 2 + n_exp * FFN_ROWS) // FFN_ROWS
    cstart = jnp.arange(n_chunks, dtype=jnp.int32) * FFN_ROWS
    ce = jnp.sum((seg_end[None, :] <= cstart[:, None]).astype(jnp.int32), axis=1)
    cec = jnp.minimum(ce, n_exp - 1)
    seg_used = seg_start + tot
    valid = jnp.any((ce[:, None] == experts[None, :]) & (cstart[:, None] < seg_used[None, :]), axis=1)
    return pos1, pos2, cec.astype(jnp.int32), valid.astype(jnp.int32)


def _rope_tables(pos):
    half = ROT_DIM // 2
    inv_freq = ROPE_THETA ** (-jnp.arange(0, ROT_DIM, 2, dtype=F32) / ROT_DIM)
    ang = pos.astype(F32)[:, None] * inv_freq[None, :]
    cos, sin = jnp.cos(ang), jnp.sin(ang)
    l64 = np.arange(LANES) % HEAD_DIM
    f = l64 % half
    first = jnp.asarray(l64 < half)[None, :]
    second = jnp.asarray((l64 >= half) & (l64 < ROT_DIM))[None, :]
    c = jnp.where(first | second, cos[:, f], 1.0)
    s1 = jnp.where(first, -sin[:, f], 0.0)
    s2 = jnp.where(second, sin[:, f], 0.0)
    return c.astype(F32), s1.astype(F32), s2.astype(F32)


def _overlap_table_t():
    cs = np.arange(N_CMP)[None, :] * CMP_STRIDE
    js = np.arange(N_SELBLK)[:, None] * SEL_LEN
    ov = np.clip(np.minimum(cs + CMP_LEN, js + SEL_LEN) - np.maximum(cs, js), 0, None) / CMP_LEN
    ov[:, N_CMP - 1] = 0.0
    return ov.astype(np.float32)


def _expand_table_t():
    out = np.zeros((SEQ, LANES), np.float32)
    out[np.arange(SEQ), np.arange(SEQ) // SEL_LEN] = -NEG
    return out.reshape(N_QBLK // SEL_GROUP, SEL_GROUP * QBLK, LANES)


def kernel(x, norm1_g, w_in, pe_kc, w_kc1, w_kc2, pe_vc, w_vc1, w_vc2, w_o, norm2_g, w_rg, b_rg, w_re, b_re,
           w_gate, w_up, w_down, norm_f_g):
    nb, s, d = x.shape
    assert (s, d) == (SEQ, D_MODEL) and norm1_g.shape[0] == 1
    t = nb * s
    x2 = x.reshape(t, d)

    w_in_p = jnp.pad(w_in[0], ((0, 0), (0, D_IN_PAD - w_in.shape[2]))).astype(BF16)
    rc, rs1, rs2 = _rope_tables(jnp.arange(SEQ))
    cc, cs1, cs2 = _rope_tables(jnp.arange(N_CMP) * CMP_STRIDE + CMP_LEN - 1)
    half_flat = CMP_STRIDE * HEAD_DIM

    def cmp_w1(w1):
        return jnp.concatenate([w1[:half_flat], w1[half_flat:]], axis=1).astype(BF16)

    def cmp_pe(pe):
        rows = pe.reshape(2, half_flat)
        return jnp.concatenate([jnp.broadcast_to(rows[0:1], (PE_ROWS, half_flat)),
                                jnp.broadcast_to(rows[1:2], (PE_ROWS, half_flat))], axis=0).astype(BF16)

    wk2d = jnp.concatenate([w_kc2[0], w_kc2[0]], axis=1).astype(BF16)
    wv2t = w_vc2[0].T.astype(BF16)
    ovl_t = jnp.asarray(_overlap_table_t(), BF16)
    expand_t = jnp.asarray(_expand_table_t(), BF16)

    (aq, ak, av, aq4, ak4, av4, aq16, ak16, av16,
     bq, kcv, ksw, gates) = _proj(x2, norm1_g, w_in_p, rc, rs1, rs2)
    mix_a = _dilated((aq, ak, av), (aq4, ak4, av4), (aq16, ak16, av16))

    kvflat = kcv.reshape(nb, N_CMP, CMP_STRIDE, 2 * N_KV_NSA, HEAD_DIM).transpose(0, 3, 1, 2, 4)
    kvflat = kvflat.reshape(nb, 2 * N_KV_NSA, N_CMP, half_flat)
    mix_b = _nsa(bq, kvflat, ksw, gates, cmp_w1(w_kc1[0]), wk2d, cmp_pe(pe_kc[0]),
                 cmp_w1(w_vc1[0]), wv2t, cmp_pe(pe_vc[0]), cc, cs1, cs2, ovl_t, expand_t)

    wr_t = jnp.pad(jnp.concatenate([w_rg[0].T, w_re[0].reshape(d, -1).T], axis=0),
                   ((0, ROUTER_ROWS - N_ROUTER), (0, 0)))
    br = jnp.broadcast_to(jnp.pad(jnp.concatenate([b_rg[0], b_re[0].reshape(-1)]),
                                  (0, ROUTER_ROWS - N_ROUTER))[:, None], (ROUTER_ROWS, LANES))
    tri = jnp.asarray(np.triu(np.ones((MOE_TM, MOE_TM), np.float32), 1), BF16)
    tp = t // MOE_PARTS
    n_pad = 2 * tp + N_GROUPS * EXPERTS_PER_GROUP * FFN_ROWS
    w_o_bf = w_o[0].astype(BF16)
    gf = norm_f_g.reshape(1, d)
    routed = []
    for part in range(MOE_PARTS):
        x1, hpa, hpb, e1, e2, r1, r2, wtok, cnt = _post(x2, mix_a, mix_b, w_o_bf, norm2_g, wr_t, br, tri, part)
        pos1, pos2, chunk_expert, chunk_valid = _moe_plan(cnt[:, :, 0], e1[:, 0], e2[:, 0], r1[:, 0], r2[:, 0])
        xsa = _sc_scatter_rows(hpa, pos1, pos2, n_pad)
        xsb = _sc_scatter_rows(hpb, pos1, pos2, n_pad)
        routed.append((x1, wtok, pos1, pos2, chunk_expert, chunk_valid, xsa, xsb))
    gathered = []
    for x1, wtok, pos1, pos2, chunk_expert, chunk_valid, xsa, xsb in routed:
        ysa, ysb = _ffn(chunk_expert, chunk_valid, xsa, xsb, w_gate[0], w_up[0], w_down[0])
        pos12 = jnp.concatenate([pos1, pos2])
        gathered.append((x1, wtok, _sc_gather_rows(ysa, pos12), _sc_gather_rows(ysb, pos12)))
    out = None
    for part, (x1, wtok, yga, ygb) in enumerate(gathered):
        out = _final(x1, yga.reshape(2, tp, PACK_W), ygb.reshape(2, tp, PACK_W), wtok, gf, out, part)
    return out.reshape(nb, s, d)
```

```python
import numpy as np
import jax
import jax.numpy as jnp
from jax import lax
from jax.experimental import pallas as pl
from jax.experimental.pallas import tpu as pltpu
from jax.experimental.pallas import tpu_sc as plsc

F32 = jnp.float32
BF16 = jnp.bfloat16

D_MODEL = 1024
SEQ = 2048
HEAD_DIM = 64
N_HEADS_DIL = 8
DIL_PATTERNS = ((128, 1), (512, 4), (2048, 16))
N_HEADS_NSA = 8
N_KV_NSA = 2
NSA_REP = N_HEADS_NSA // N_KV_NSA
CMP_STRIDE = 16
CMP_LEN = 32
CMP_HIDDEN = 256
SEL_LEN = 64
N_SEL = 8
WIN = 512
ROPE_THETA = 500000.0
ROT_DIM = HEAD_DIM // 4
N_GROUPS = 8
EXPERTS_PER_GROUP = 8
D_FF_EXPERT = 256
EPS = 1e-6
NEG = -1e30
BIG = 1e9
LOWEST = -3.0e38
LOG2_E = 1.4426950408889634

A_COLS = N_HEADS_DIL * HEAD_DIM
B_COLS = N_HEADS_NSA * HEAD_DIM
KV_COLS = N_KV_NSA * HEAD_DIM
N_EXPERTS = N_GROUPS * EXPERTS_PER_GROUP
N_ROUTER = N_GROUPS + N_EXPERTS

LANES = 128
QBLK = 128
N_QBLK = SEQ // QBLK
N_CMP = SEQ // CMP_STRIDE
N_SELBLK = SEQ // SEL_LEN
D_IN_PAD = 23 * LANES
PROJ_TM = 1024
SEL_GROUP = 4
MOE_TM = 1024
MOE_PARTS = 1
ROUTER_ROWS = 80
FFN_ROWS = 512
FINAL_TM = 1024
PACK_W = D_MODEL // 4
BF16_SUBLANES = 16
ONES_ROWS = BF16_SUBLANES
PE_ROWS = BF16_SUBLANES
GATE_ROWS = 16
SC_WINDOW = 128
VMEM_LIMIT = 56 * 1024 * 1024

_NT = (((1,), (1,)), ((), ()))


def _lane_iota(shape):
    return lax.broadcasted_iota(jnp.int32, shape, 1)


def _row_iota(shape):
    return lax.broadcasted_iota(jnp.int32, shape, 0)


def _rope_lanes(y, c, s1, s2):
    return y * c + pltpu.roll(y, LANES - ROT_DIM // 2, axis=1) * s1 + pltpu.roll(y, ROT_DIM // 2, axis=1) * s2


def _proj_kernel(x_ref, g_ref, w_ref, c_ref, s1_ref, s2_ref,
                 aq_ref, ak_ref, av_ref, aq4_ref, ak4_ref, av4_ref, aq16_ref, ak16_ref, av16_ref,
                 bq_ref, kvflat_ref, ksw_ref, gate_ref, nat_f32, d4_f32):
    x = x_ref[...]
    ms = jnp.mean(x * x, axis=-1, keepdims=True)
    h = (x * lax.rsqrt(ms + EPS) * g_ref[...]).astype(BF16)
    c = c_ref[...]
    s1 = s1_ref[...]
    s2 = s2_ref[...]

    def seg(lo, width):
        return jnp.dot(h, w_ref[:, lo:lo + width], preferred_element_type=F32)

    def store(dst, col, y, rope, scale, dilated=None):
        for j in range(y.shape[1] // LANES):
            yj = y[:, j * LANES:(j + 1) * LANES]
            if rope:
                yj = _rope_lanes(yj, c, s1, s2)
            if scale != 1.0:
                yj = yj * scale
            cols = slice(col + j * LANES, col + (j + 1) * LANES)
            dst[:, cols] = yj.astype(dst.dtype)
            if dilated is not None:
                d4_ref, d16_ref = dilated
                quarter = PROJ_TM // 4
                nat_f32[...] = yj
                for r in range(4):
                    part = nat_f32[pl.ds(r, quarter, stride=4), :]
                    d4_ref[r, :, cols] = part.astype(d4_ref.dtype)
                    d4_f32[r * quarter:(r + 1) * quarter, :] = part
                for r in range(4):
                    for a_ in range(4):
                        part = d4_f32[pl.ds(r * quarter + a_, quarter // 4, stride=4), :]
                        d16_ref[4 * a_ + r, :, cols] = part.astype(d16_ref.dtype)

    qscale = HEAD_DIM ** -0.5 * LOG2_E
    store(aq_ref, 0, seg(0, A_COLS), True, qscale, (aq4_ref, aq16_ref))
    store(ak_ref, 0, seg(A_COLS, A_COLS), True, 1.0, (ak4_ref, ak16_ref))
    store(av_ref, 0, seg(2 * A_COLS, A_COLS), False, 1.0, (av4_ref, av16_ref))
    nsa0 = 3 * A_COLS
    store(bq_ref, 0, seg(nsa0, B_COLS), True, qscale)
    y = seg(nsa0 + B_COLS, 2 * KV_COLS)
    quarter = PROJ_TM // 4
    left_half = _lane_iota((quarter // 4, LANES)) < HEAD_DIM
    for kind in range(2):
        nat_f32[...] = y[:, kind * KV_COLS:(kind + 1) * KV_COLS]
        for r in range(4):
            d4_f32[r * quarter:(r + 1) * quarter, :] = nat_f32[pl.ds(r, quarter, stride=4), :]

        def chunk_pos(i):
            return d4_f32[pl.ds((i % 4) * quarter + i // 4, quarter // 4, stride=4), :]

        for pair in range(CMP_STRIDE // 2):
            even, odd = chunk_pos(2 * pair), chunk_pos(2 * pair + 1)
            cols = slice(pair * LANES, (pair + 1) * LANES)
            kvflat_ref[2 * kind, :, cols] = jnp.where(left_half, even, pltpu.roll(odd, HEAD_DIM, axis=1)).astype(BF16)
            kvflat_ref[2 * kind + 1, :, cols] = jnp.where(left_half, pltpu.roll(even, HEAD_DIM, axis=1), odd).astype(BF16)
    kv0 = nsa0 + B_COLS + 2 * KV_COLS
    for n, rope in enumerate((True, False, True, False)):
        store(ksw_ref, n * KV_COLS, seg(kv0 + n * KV_COLS, KV_COLS), rope, 1.0)
    gate_ref[...] = jax.nn.sigmoid(seg(kv0 + 4 * KV_COLS, LANES))


def _proj(x2, g1, w_in_p, rc, rs1, rs2):
    t = x2.shape[0]
    nb = t // SEQ
    nblk_s = SEQ // PROJ_TM
    row = lambda i: (i, 0)
    pos = lambda i: (i % nblk_s, 0)
    const = lambda i: (0, 0)
    perm = lambda i: (i // nblk_s, 0, i % nblk_s, 0)
    wide = jax.ShapeDtypeStruct((t, A_COLS), BF16)
    wide_spec = pl.BlockSpec((PROJ_TM, A_COLS), row)
    d4 = jax.ShapeDtypeStruct((nb, 4, SEQ // 4, A_COLS), BF16)
    d4_spec = pl.BlockSpec((None, 4, PROJ_TM // 4, A_COLS), perm)
    d16 = jax.ShapeDtypeStruct((nb, 16, SEQ // 16, A_COLS), BF16)
    d16_spec = pl.BlockSpec((None, 16, PROJ_TM // 16, A_COLS), perm)
    flat = jax.ShapeDtypeStruct((nb, 2 * N_KV_NSA, N_CMP, CMP_STRIDE * HEAD_DIM), BF16)
    flat_spec = pl.BlockSpec((None, 2 * N_KV_NSA, PROJ_TM // CMP_STRIDE, CMP_STRIDE * HEAD_DIM), perm)
    outs = [wide] * 3 + [d4] * 3 + [d16] * 3 + [wide, flat, wide,
                                               jax.ShapeDtypeStruct((t, LANES), F32)]
    out_specs = [wide_spec] * 3 + [d4_spec] * 3 + [d16_spec] * 3 + [
        wide_spec, flat_spec, wide_spec, pl.BlockSpec((PROJ_TM, LANES), row)]
    return pl.pallas_call(
        _proj_kernel,
        grid=(t // PROJ_TM,),
        in_specs=[pl.BlockSpec((PROJ_TM, D_MODEL), row), pl.BlockSpec((1, D_MODEL), const),
                  pl.BlockSpec((D_MODEL, D_IN_PAD), const),
                  pl.BlockSpec((PROJ_TM, LANES), pos), pl.BlockSpec((PROJ_TM, LANES), pos),
                  pl.BlockSpec((PROJ_TM, LANES), pos)],
        out_specs=out_specs,
        out_shape=outs,
        scratch_shapes=[pltpu.VMEM((PROJ_TM, LANES), F32)] * 2,
        compiler_params=pltpu.CompilerParams(dimension_semantics=("arbitrary",),
                                             vmem_limit_bytes=VMEM_LIMIT),
        name="proj",
    )(x2, g1, w_in_p, rc, rs1, rs2)


def _dilated_kernel(q1_ref, k1_ref, v1_ref, q4_ref, k4_ref, v4_ref, q16_ref, k16_ref, v16_ref, o_ref,
                    qt, vt, op, lp, on0, on1, on2, ln0, ln1, ln2):
    c2 = _row_iota((2 * QBLK, 2 * QBLK))
    a2 = _lane_iota((2 * QBLK, 2 * QBLK)) % QBLK
    band = jnp.where((c2 >= a2) & (c2 <= a2 + QBLK), 0.0, NEG)
    band_noprev = jnp.where(c2 < QBLK, NEG, band)
    c1 = _row_iota((QBLK, 2 * QBLK))
    a1 = _lane_iota((QBLK, 2 * QBLK)) % QBLK
    band_own = jnp.where(c1 <= a1, 0.0, NEG)
    top_sq = _row_iota((LANES, QBLK)) < HEAD_DIM

    for j in range(N_QBLK):
        vt[j, LANES:, :] = jnp.ones((ONES_ROWS, QBLK), BF16)

    inputs = ((q1_ref, k1_ref, v1_ref), (q4_ref, k4_ref, v4_ref), (q16_ref, k16_ref, v16_ref))
    for pi, (window, dil) in enumerate(DIL_PATTERNS):
        seg_len = SEQ // dil
        nseg_blk = seg_len // QBLK
        q_ref, k_ref, v_ref = inputs[pi]

        def block(ref, j, nseg_blk=nseg_blk, dil=dil):
            if dil == 1:
                return ref[j * QBLK:(j + 1) * QBLK, :]
            return ref[j // nseg_blk, (j % nseg_blk) * QBLK:(j % nseg_blk + 1) * QBLK, :]

        for j in range(N_QBLK):
            qt[j] = block(q_ref, j).T
            vt[j, 0:LANES, :] = block(v_ref, j).T

        prevs, scores = [], []
        for j in range(N_QBLK):
            qtb = qt[j]
            zero = jnp.zeros_like(qtb)
            q_both = jnp.concatenate([jnp.where(top_sq, qtb, zero), jnp.where(top_sq, zero, qtb)], axis=1)
            if nseg_blk == 1:
                keys, bias, jp = block(k_ref, j), band_own, j
            else:
                first = j % nseg_blk == 0
                jp = j if first else j - 1
                bias = band_noprev if first else band
                keys = jnp.concatenate([block(k_ref, jp), block(k_ref, j)], axis=0)
            prevs.append(jp)
            scores.append(jnp.dot(keys, q_both, preferred_element_type=F32) + bias)
        stats = []
        for s in scores:
            m = jnp.max(s, axis=0, keepdims=True)
            stats.append((m, jnp.exp2((s - m).astype(BF16))))
        outs = []
        for j, (jp, (m, p)) in enumerate(zip(prevs, stats)):
            v_span = vt[j] if nseg_blk == 1 else jnp.concatenate([vt[jp], vt[j]], axis=1)
            outs.append(jnp.dot(v_span, p, preferred_element_type=F32))
        for j, ((m, p), res) in enumerate(zip(stats, outs)):
            l = res[LANES:LANES + 1, :]
            lse = m + jnp.log2(l)
            o_t = jnp.where(top_sq, res[0:LANES, 0:QBLK] / l[:, 0:QBLK], res[0:LANES, QBLK:] / l[:, QBLK:])
            lse_t = jnp.where(top_sq, jnp.broadcast_to(lse[:, 0:QBLK], (LANES, QBLK)),
                              jnp.broadcast_to(lse[:, QBLK:], (LANES, QBLK)))
            op[j * QBLK:(j + 1) * QBLK, :] = o_t.T
            lp[j * QBLK:(j + 1) * QBLK, :] = lse_t.T

        for r in range(dil):
            dst = pl.ds(r, seg_len, stride=dil) if dil > 1 else pl.ds(0, SEQ)
            src = pl.ds(r * seg_len, seg_len)
            (on0, on1, on2)[pi][dst, :] = op[src, :]
            (ln0, ln1, ln2)[pi][dst, :] = lp[src, :]

    l0, l1, l2 = ln0[...], ln1[...], ln2[...]
    mx = jnp.maximum(jnp.maximum(l0, l1), l2)
    e0, e1, e2 = jnp.exp2(l0 - mx), jnp.exp2(l1 - mx), jnp.exp2(l2 - mx)
    den = e0 + e1 + e2
    out = (e0 / den) * on0[...] + (e1 / den) * on1[...] + (e2 / den) * on2[...]
    o_ref[...] = out.astype(o_ref.dtype)


def _dilated(a1, a4, a16):
    t = a1[0].shape[0]
    nb = t // SEQ
    spec1 = pl.BlockSpec((SEQ, LANES), lambda b, hp: (b, hp))
    spec4 = pl.BlockSpec((None, 4, SEQ // 4, LANES), lambda b, hp: (b, 0, 0, hp))
    spec16 = pl.BlockSpec((None, 16, SEQ // 16, LANES), lambda b, hp: (b, 0, 0, hp))
    return pl.pallas_call(
        _dilated_kernel,
        grid=(nb, N_HEADS_DIL // 2),
        in_specs=[spec1] * 3 + [spec4] * 3 + [spec16] * 3,
        out_specs=spec1,
        out_shape=jax.ShapeDtypeStruct((t, A_COLS), BF16),
        scratch_shapes=[pltpu.VMEM((N_QBLK, LANES, QBLK), BF16), pltpu.VMEM((N_QBLK, LANES + ONES_ROWS, QBLK), BF16)]
        + [pltpu.VMEM((SEQ, LANES), F32)] * 8,
        compiler_params=pltpu.CompilerParams(dimension_semantics=("arbitrary", "arbitrary"),
                                             vmem_limit_bytes=VMEM_LIMIT),
        name="dilated",
    )(*a1, *a4, *a16)


def _gelu_tanh(x):
    return 0.5 * x * (1.0 + jnp.tanh(np.sqrt(2.0 / np.pi).astype(np.float32) * (x + 0.044715 * (x * x * x))))


def _softmax_step(state, s, v_t):
    m, l, acc = state
    mn = jnp.maximum(m, jnp.max(s, axis=0, keepdims=True))
    al = jnp.exp2(m - mn)
    p = jnp.exp2(s - mn)
    l = al * l + jnp.sum(p, axis=0, keepdims=True)
    acc = al * acc + jnp.dot(v_t, p.astype(BF16), preferred_element_type=F32)
    return mn, l, acc


def _nsa_kernel(q_ref, kflat_ref, vflat_ref, ksw_ref, gate_ref,
                wk1_ref, wk2d_ref, pek_ref, wv1_ref, wv2t_ref, pev_ref,
                cc_ref, cs1_ref, cs2_ref, ovl_ref, exp_ref,
                o_ref,
                qt, vst, vwt, gtt, kc2, vct):
    g = pl.program_id(1)
    nq = NSA_REP * QBLK

    for i in range(N_QBLK):
        rows = pl.ds(i * QBLK, QBLK)
        qt[i, 0:LANES, :] = q_ref[rows, 0:LANES].T
        qt[i, LANES:2 * LANES, :] = q_ref[rows, LANES:2 * LANES].T
        vs_t = ksw_ref[rows, LANES:2 * LANES].T
        vst[i] = jnp.where(g == 0, vs_t[0:HEAD_DIM], vs_t[HEAD_DIM:])
        vw_t = ksw_ref[rows, 3 * LANES:4 * LANES].T
        vwt[i] = jnp.where(g == 0, vw_t[0:HEAD_DIM], vw_t[HEAD_DIM:])
        gt = gate_ref[rows, :]
        gt = jnp.where(g == 0, gt, pltpu.roll(gt, LANES - 3 * NSA_REP, axis=1))
        gtt[i] = gt.T[0:GATE_ROWS, :]

    def hidden(flat_ref, w1_ref, pe_ref):
        ab = jnp.dot(flat_ref[...], w1_ref[...], preferred_element_type=F32)
        pb = (jnp.dot(pe_ref[0:PE_ROWS, :], w1_ref[:, 0:CMP_HIDDEN], preferred_element_type=F32)
              + jnp.dot(pe_ref[PE_ROWS:, :], w1_ref[:, CMP_HIDDEN:], preferred_element_type=F32))
        hid = ab[:, 0:CMP_HIDDEN] + pltpu.roll(ab[:, CMP_HIDDEN:], N_CMP - 1, axis=0) + pb[0:1, :]
        return _gelu_tanh(hid).astype(BF16)

    hk = hidden(kflat_ref, wk1_ref, pek_ref)
    kc2[...] = _rope_lanes(jnp.dot(hk, wk2d_ref[...], preferred_element_type=F32),
                           cc_ref[...], cs1_ref[...], cs2_ref[...]).astype(BF16)
    hv = hidden(vflat_ref, wv1_ref, pev_ref)
    vct[...] = lax.dot_general(wv2t_ref[...], hv, _NT, preferred_element_type=F32).astype(BF16)

    key = _row_iota((QBLK, nq))
    qry = _lane_iota((QBLK, nq)) % QBLK
    causal = jnp.where(key <= qry, 0.0, NEG)
    upper = jnp.where(key > qry, 0.0, NEG)
    blk = _row_iota((N_SELBLK, QBLK))
    qry1 = _lane_iota((N_SELBLK, QBLK))
    grp_keys = SEL_GROUP * QBLK
    win_blocks = WIN // QBLK

    def qgroup(w, full_window):
        blocks = [SEL_GROUP * w + d for d in range(SEL_GROUP)]
        row0s = [pl.multiple_of(i * QBLK, QBLK) for i in blocks]

        q_pads = []
        for i in blocks:
            q_t = jnp.concatenate([qt[i, r * HEAD_DIM:(r + 1) * HEAD_DIM, :] for r in range(NSA_REP)], axis=1)
            zero = jnp.zeros_like(q_t)
            q_pads.append(jnp.where(g == 0, jnp.concatenate([q_t, zero], axis=0),
                                    jnp.concatenate([zero, q_t], axis=0)))
        s_cmp = [jnp.dot(kc2[...], qp, preferred_element_type=F32) for qp in q_pads]
        s_win, v_win = [], []
        for d, (i, qp) in enumerate(zip(blocks, q_pads)):
            if full_window:
                first = i - win_blocks
                keys = ksw_ref[pl.ds(pl.multiple_of(first * QBLK, QBLK), (win_blocks + 1) * QBLK), 2 * LANES:3 * LANES]
                s = jnp.dot(keys, qp, preferred_element_type=F32)
                s = jnp.concatenate([s[0:QBLK] + upper, s[QBLK:win_blocks * QBLK], s[win_blocks * QBLK:] + causal],
                                    axis=0)
                v_t = jnp.concatenate([vwt[first + e] for e in range(win_blocks + 1)], axis=1)
            else:
                s = jnp.dot(ksw_ref[0:(d + 1) * QBLK, 2 * LANES:3 * LANES], qp, preferred_element_type=F32)
                s = jnp.concatenate([s[0:d * QBLK], s[d * QBLK:] + causal], axis=0) if d else s + causal
                v_t = jnp.concatenate([vwt[e] for e in range(d + 1)], axis=1) if d else vwt[0]
            s_win.append(s)
            v_win.append(v_t)

        pcs = []
        for row0, s in zip(row0s, s_cmp):
            cvalid = (CMP_STRIDE * key + (CMP_LEN - 1) <= row0 + qry) & (key < N_CMP - 1)
            s = jnp.where(cvalid, s, NEG)
            m = jnp.max(s, axis=0, keepdims=True)
            p = jnp.where(cvalid, jnp.exp2(s - m), 0.0)
            l = jnp.sum(p, axis=0, keepdims=True)
            pcs.append(p / jnp.where(l > 0, l, 1.0))
        o_cmp = [jnp.dot(vct[...], pc.astype(BF16), preferred_element_type=F32) for pc in pcs]

        ovl = ovl_ref[...]
        imps = []
        for pc in pcs:
            psum = pc[:, 0:QBLK] + pc[:, QBLK:2 * QBLK] + pc[:, 2 * QBLK:3 * QBLK] + pc[:, 3 * QBLK:]
            p_hi = psum.astype(BF16)
            p_lo = (psum - p_hi.astype(F32)).astype(BF16)
            imps.append(jnp.dot(ovl, p_hi, preferred_element_type=F32) + jnp.dot(ovl, p_lo, preferred_element_type=F32))

        o_win = []
        for s, v_t in zip(s_win, v_win):
            m = jnp.max(s, axis=0, keepdims=True)
            p = jnp.exp2(s - m)
            l = jnp.sum(p, axis=0, keepdims=True)
            o_win.append(jnp.dot(v_t, p.astype(BF16), preferred_element_type=F32) / l)

        q_sel = []
        for row0, imp, qp in zip(row0s, imps, q_pads):
            t1 = row0 + qry1
            blk_t = t1 // SEL_LEN
            forced = (blk == 0) | (blk == blk_t) | (blk == blk_t - 1)
            v = jnp.where(blk * SEL_LEN <= t1, jnp.where(forced, BIG, imp), -BIG)
            selm = jnp.zeros((N_SELBLK, QBLK), F32)
            for _ in range(N_SEL):
                mx = jnp.max(v, axis=0, keepdims=True)
                first_hit = jnp.min(jnp.where(v == mx, blk, N_SELBLK), axis=0, keepdims=True)
                hit = blk == first_hit
                selm = jnp.where(hit, 1.0, selm)
                v = jnp.where(hit, LOWEST, v)
            notsel = jnp.concatenate([selm - 1.0, jnp.zeros((LANES - N_SELBLK, QBLK), F32)], axis=0).astype(BF16)
            q_sel.append(jnp.concatenate([qp, jnp.concatenate([notsel] * NSA_REP, axis=1)], axis=0))

        def sel_keys(t, n_rows):
            rows_t = pl.ds(pl.multiple_of(t * grp_keys, grp_keys), n_rows)
            return jnp.concatenate([ksw_ref[rows_t, 0:LANES], exp_ref[t, 0:n_rows, :]], axis=1)

        def sel_values(t, n_blk):
            return jnp.concatenate([vst[t * SEL_GROUP + e] for e in range(n_blk)], axis=1) if n_blk > 1 else vst[t * SEL_GROUP]

        wide = SEL_GROUP * nq
        st = (jnp.full((1, wide), LOWEST, F32), jnp.zeros((1, wide), F32), jnp.zeros((HEAD_DIM, wide), F32))
        if full_window:
            q_all = jnp.concatenate(q_sel, axis=1)

            def sel_body(t, st):
                s = jnp.dot(sel_keys(t, grp_keys), q_all, preferred_element_type=F32)
                return _softmax_step(st, s, sel_values(t, SEL_GROUP))

            for t_ in range(w):
                st = sel_body(t_, st)
        o_sel = []
        for d, qs in enumerate(q_sel):
            s = jnp.dot(sel_keys(w, (d + 1) * QBLK), qs, preferred_element_type=F32)
            s = jnp.concatenate([s[0:d * QBLK], s[d * QBLK:] + causal], axis=0) if d else s + causal
            lanes = slice(d * nq, (d + 1) * nq)
            m, l, acc = _softmax_step((st[0][:, lanes], st[1][:, lanes], st[2][:, lanes]), s, sel_values(w, d + 1))
            o_sel.append(acc / l)

        for n, (i, row0) in enumerate(zip(blocks, row0s)):
            gti = gtt[i]

            def gate(br):
                return jnp.concatenate([gti[r * 3 + br:r * 3 + br + 1, :] for r in range(NSA_REP)], axis=1)

            out_t = gate(0) * o_cmp[n] + gate(1) * o_sel[n] + gate(2) * o_win[n]
            for pair in range(NSA_REP // 2):
                both = jnp.concatenate([out_t[:, (2 * pair) * QBLK:(2 * pair + 1) * QBLK],
                                        out_t[:, (2 * pair + 1) * QBLK:(2 * pair + 2) * QBLK]], axis=0)
                o_ref[pl.ds(row0, QBLK), pair * LANES:(pair + 1) * LANES] = both.T.astype(o_ref.dtype)

    qgroup(0, False)
    for w in range(1, N_QBLK // SEL_GROUP):
        qgroup(w, True)


def _nsa(bq, kvflat, ksw, gates, wk1, wk2d, pek, wv1, wv2t, pev, cc, cs1, cs2, ovl_t, expand_t):
    t = bq.shape[0]
    nb = t // SEQ
    const2 = lambda b, g: (0, 0)
    return pl.pallas_call(
        _nsa_kernel,
        grid=(nb, N_KV_NSA),
        in_specs=[
            pl.BlockSpec((SEQ, 2 * LANES), lambda b, g: (b, g)),
            pl.BlockSpec((None, None, N_CMP, CMP_STRIDE * HEAD_DIM), lambda b, g: (b, g, 0, 0)),
            pl.BlockSpec((None, None, N_CMP, CMP_STRIDE * HEAD_DIM), lambda b, g: (b, N_KV_NSA + g, 0, 0)),
            pl.BlockSpec((SEQ, 4 * LANES), lambda b, g: (b, 0)),
            pl.BlockSpec((SEQ, LANES), lambda b, g: (b, 0)),
            pl.BlockSpec((CMP_STRIDE * HEAD_DIM, 2 * CMP_HIDDEN), const2),
            pl.BlockSpec((CMP_HIDDEN, LANES), const2),
            pl.BlockSpec((2 * PE_ROWS, CMP_STRIDE * HEAD_DIM), const2),
            pl.BlockSpec((CMP_STRIDE * HEAD_DIM, 2 * CMP_HIDDEN), const2),
            pl.BlockSpec((HEAD_DIM, CMP_HIDDEN), const2),
            pl.BlockSpec((2 * PE_ROWS, CMP_STRIDE * HEAD_DIM), const2),
            pl.BlockSpec((N_CMP, LANES), const2),
            pl.BlockSpec((N_CMP, LANES), const2),
            pl.BlockSpec((N_CMP, LANES), const2),
            pl.BlockSpec((N_SELBLK, LANES), const2),
            pl.BlockSpec((N_QBLK // SEL_GROUP, SEL_GROUP * QBLK, LANES), lambda b, g: (0, 0, 0)),
        ],
        out_specs=pl.BlockSpec((SEQ, 2 * LANES), lambda b, g: (b, g)),
        out_shape=jax.ShapeDtypeStruct((t, B_COLS), BF16),
        scratch_shapes=[pltpu.VMEM((N_QBLK, 2 * LANES, QBLK), BF16),
                        pltpu.VMEM((N_QBLK, HEAD_DIM, QBLK), BF16),
                        pltpu.VMEM((N_QBLK, HEAD_DIM, QBLK), BF16),
                        pltpu.VMEM((N_QBLK, GATE_ROWS, QBLK), F32),
                        pltpu.VMEM((N_CMP, LANES), BF16),
                        pltpu.VMEM((HEAD_DIM, N_CMP), BF16)],
        compiler_params=pltpu.CompilerParams(dimension_semantics=("arbitrary", "arbitrary"),
                                             vmem_limit_bytes=VMEM_LIMIT),
        name="nsa",
    )(bq, kvflat, kvflat, ksw, gates, wk1, wk2d, pek, wv1, wv2t, pev, cc, cs1, cs2, ovl_t, expand_t)


def _pack_bf16_pairs(lo, hi):
    lo_b = lax.bitcast_convert_type(lo.astype(BF16).astype(F32), jnp.uint32)
    hi_b = lax.bitcast_convert_type(hi.astype(BF16).astype(F32), jnp.uint32)
    return (hi_b & jnp.uint32(0xFFFF0000)) | (lo_b >> 16)


def _unpack_bf16_pairs(p):
    lo = lax.bitcast_convert_type(p << 16, F32)
    hi = lax.bitcast_convert_type(p & jnp.uint32(0xFFFF0000), F32)
    return lo, hi


def _post_kernel(x_ref, ma_ref, mb_ref, wo_ref, g2_ref, wr_ref, br_ref, tri_ref,
                 x1_ref, ha_ref, hb_ref, e1_ref, e2_ref, r1_ref, r2_ref, w_ref, cnt_ref):
    x1 = (x_ref[...] + jnp.dot(ma_ref[...], wo_ref[0:A_COLS, :], preferred_element_type=F32)
          + jnp.dot(mb_ref[...], wo_ref[A_COLS:, :], preferred_element_type=F32))
    x1_ref[...] = x1
    ms = jnp.mean(x1 * x1, axis=-1, keepdims=True)
    h2 = x1 * lax.rsqrt(ms + EPS) * g2_ref[...]
    ha_ref[...] = _pack_bf16_pairs(h2[:, 0:PACK_W], h2[:, 2 * PACK_W:3 * PACK_W])
    hb_ref[...] = _pack_bf16_pairs(h2[:, PACK_W:2 * PACK_W], h2[:, 3 * PACK_W:])

    h_hi = h2.astype(BF16)
    h_lo = (h2 - h_hi.astype(F32)).astype(BF16)
    wr = wr_ref[...]
    w_hi = wr.astype(BF16)
    w_lo = (wr - w_hi.astype(F32)).astype(BF16)
    lg = (lax.dot_general(w_hi, h_hi, _NT, preferred_element_type=F32)
          + lax.dot_general(w_hi, h_lo, _NT, preferred_element_type=F32)
          + lax.dot_general(w_lo, h_hi, _NT, preferred_element_type=F32)) + br_ref[:, 0:1]
    gl = lg[0:N_GROUPS, :]
    sub = _row_iota(gl.shape)
    gmax = jnp.max(gl, axis=0, keepdims=True)
    p_g = 1.0 / jnp.sum(jnp.exp(gl - gmax), axis=0, keepdims=True)
    g_star = jnp.min(jnp.where(gl == gmax, sub, N_GROUPS), axis=0, keepdims=True)
    el = jnp.zeros_like(gl)
    for gi in range(N_GROUPS):
        lo = N_GROUPS + gi * EXPERTS_PER_GROUP
        el = el + jnp.where(g_star == gi, lg[lo:lo + EXPERTS_PER_GROUP, :], 0.0)
    emax = jnp.max(el, axis=0, keepdims=True)
    ee = jnp.exp(el - emax)
    pe = ee / jnp.sum(ee, axis=0, keepdims=True)
    v1 = jnp.max(pe, axis=0, keepdims=True)
    i1 = jnp.min(jnp.where(pe == v1, sub, EXPERTS_PER_GROUP), axis=0, keepdims=True)
    pe2 = jnp.where(sub == i1, -1.0, pe)
    v2 = jnp.max(pe2, axis=0, keepdims=True)
    i2 = jnp.min(jnp.where(pe2 == v2, sub, EXPERTS_PER_GROUP), axis=0, keepdims=True)
    tot = v1 + v2
    e1 = g_star * EXPERTS_PER_GROUP + i1
    e2 = g_star * EXPERTS_PER_GROUP + i2
    e1_ref[0] = e1
    e2_ref[0] = e2

    wslab = jnp.concatenate([v1 / tot * p_g, v2 / tot * p_g, jnp.zeros((LANES - 2, MOE_TM), F32)], axis=0)
    for j in range(MOE_TM // LANES):
        w_ref[j * LANES:(j + 1) * LANES, :] = wslab[:, j * LANES:(j + 1) * LANES].T

    n_exp = N_EXPERTS
    sub_e = _row_iota((n_exp, MOE_TM))
    hit1 = sub_e == e1
    hit2 = sub_e == e2
    assigned = jnp.where(hit1, 1.0, 0.0) + jnp.where(hit2, 1.0, 0.0)
    before = jnp.dot(assigned.astype(BF16), tri_ref[...], preferred_element_type=F32)
    r1_ref[0] = jnp.sum(jnp.where(hit1, before, 0.0), axis=0, keepdims=True).astype(jnp.int32)
    r2_ref[0] = jnp.sum(jnp.where(hit2, before, 0.0), axis=0, keepdims=True).astype(jnp.int32)
    cnt = jnp.sum(assigned, axis=1, keepdims=True).astype(jnp.int32)
    cnt_ref[0] = jnp.broadcast_to(cnt, (n_exp, LANES))


def _post(x2, mix_a, mix_b, w_o, g2, wr_t, br, tri, part):
    t = x2.shape[0] // MOE_PARTS
    nt = t // MOE_TM
    first = part * nt
    n_exp = N_EXPERTS
    row = lambda i: (i, 0)
    const = lambda i: (0, 0)
    tok = lambda i: (i, 0, 0)
    src = lambda i: (first + i, 0)
    tok_spec = pl.BlockSpec((1, 1, MOE_TM), tok)
    tok_shape = jax.ShapeDtypeStruct((nt, 1, MOE_TM), jnp.int32)
    return pl.pallas_call(
        _post_kernel,
        grid=(nt,),
        in_specs=[pl.BlockSpec((MOE_TM, D_MODEL), src), pl.BlockSpec((MOE_TM, A_COLS), src),
                  pl.BlockSpec((MOE_TM, B_COLS), src), pl.BlockSpec((D_MODEL, D_MODEL), const),
                  pl.BlockSpec((1, D_MODEL), const), pl.BlockSpec((ROUTER_ROWS, D_MODEL), const),
                  pl.BlockSpec((ROUTER_ROWS, LANES), const), pl.BlockSpec((MOE_TM, MOE_TM), const)],
        out_specs=[pl.BlockSpec((MOE_TM, D_MODEL), row),
                   pl.BlockSpec((MOE_TM, PACK_W), row), pl.BlockSpec((MOE_TM, PACK_W), row),
                   tok_spec, tok_spec, tok_spec, tok_spec,
                   pl.BlockSpec((MOE_TM, LANES), row),
                   pl.BlockSpec((1, n_exp, LANES), tok)],
        out_shape=[jax.ShapeDtypeStruct((t, D_MODEL), F32),
                   jax.ShapeDtypeStruct((t, PACK_W), jnp.uint32), jax.ShapeDtypeStruct((t, PACK_W), jnp.uint32),
                   tok_shape, tok_shape, tok_shape, tok_shape,
                   jax.ShapeDtypeStruct((t, LANES), F32),
                   jax.ShapeDtypeStruct((nt, n_exp, LANES), jnp.int32)],
        compiler_params=pltpu.CompilerParams(dimension_semantics=("arbitrary",),
                                             vmem_limit_bytes=VMEM_LIMIT),
        name="post",
    )(x2, mix_a, mix_b, w_o, g2, wr_t, br, tri)


def _sc_mesh():
    return plsc.VectorSubcoreMesh(core_axis_name="core", subcore_axis_name="subcore")


def _sc_scatter_rows(x, pos1, pos2, n_out):
    r, c = x.shape

    @pl.kernel(out_type=jax.ShapeDtypeStruct((n_out, c), x.dtype), mesh=_sc_mesh(), scratch_types=[])
    def scatter_kernel(x_hbm, p1_hbm, p2_hbm, o_hbm):
        def body(x_vmem, p1_vmem, p2_vmem):
            pltpu.sync_copy(x_vmem, o_hbm.at[p1_vmem.at[0]])
            pltpu.sync_copy(x_vmem, o_hbm.at[p2_vmem.at[0]])

        idx = pl.BlockSpec((1, SC_WINDOW), lambda i: (0, i))
        pltpu.emit_pipeline(body, grid=(r // SC_WINDOW,),
                            in_specs=[pl.BlockSpec((SC_WINDOW, c), lambda i: (i, 0)), idx, idx], out_specs=[],
                            core_axis_name=("core", "subcore"),
                            dimension_semantics=(pltpu.PARALLEL,))(x_hbm, p1_hbm, p2_hbm)

    return scatter_kernel(x, pos1.reshape(1, r), pos2.reshape(1, r))


def _sc_gather_rows(y, idx):
    n = idx.shape[0]
    c = y.shape[1]

    @pl.kernel(out_type=jax.ShapeDtypeStruct((n, c), y.dtype), mesh=_sc_mesh(), scratch_types=[])
    def gather_kernel(y_hbm, i_hbm, o_hbm):
        def body(i_vmem, o_vmem):
            pltpu.sync_copy(y_hbm.at[i_vmem.at[0]], o_vmem)

        pltpu.emit_pipeline(body, grid=(n // SC_WINDOW,),
                            in_specs=[pl.BlockSpec((1, SC_WINDOW), lambda i: (0, i))],
                            out_specs=[pl.BlockSpec((SC_WINDOW, c), lambda i: (i, 0))],
                            core_axis_name=("core", "subcore"),
                            dimension_semantics=(pltpu.PARALLEL,))(i_hbm, o_hbm)

    return gather_kernel(y, idx.reshape(1, n))


def _ffn_kernel(ce_ref, cv_ref, xa_ref, xb_ref, wg_ref, wu_ref, wd_ref, ya_ref, yb_ref, wg_bf, wu_bf, wd_bf):
    c = pl.program_id(0)
    prev = ce_ref[jnp.maximum(c - 1, 0)]

    @pl.when((c == 0) | (ce_ref[c] != prev))
    def _():
        wg_bf[...] = wg_ref[0].astype(BF16)
        wu_bf[...] = wu_ref[0].astype(BF16)
        wd_bf[...] = wd_ref[0].astype(BF16)

    @pl.when(cv_ref[c] == 0)
    def _():
        ya_ref[...] = jnp.zeros_like(ya_ref)
        yb_ref[...] = jnp.zeros_like(yb_ref)

    @pl.when(cv_ref[c] != 0)
    def _():
        a_lo, a_hi = _unpack_bf16_pairs(xa_ref[...])
        b_lo, b_hi = _unpack_bf16_pairs(xb_ref[...])
        xs = jnp.concatenate([a_lo, b_lo, a_hi, b_hi], axis=1).astype(BF16)
        gate = jnp.dot(xs, wg_bf[...], preferred_element_type=F32)
        up = jnp.dot(xs, wu_bf[...], preferred_element_type=F32)
        he = (gate * jax.nn.sigmoid(gate) * up).astype(BF16)
        y = jnp.dot(he, wd_bf[...], preferred_element_type=F32)
        ya_ref[...] = _pack_bf16_pairs(y[:, 0:PACK_W], y[:, 2 * PACK_W:3 * PACK_W])
        yb_ref[...] = _pack_bf16_pairs(y[:, PACK_W:2 * PACK_W], y[:, 3 * PACK_W:])


def _ffn(chunk_expert, chunk_valid, xsa, xsb, wg, wu, wd):
    n_pad = xsa.shape[0]
    rows = pl.BlockSpec((FFN_ROWS, PACK_W), lambda c, ce, cv: (c, 0))
    grid_spec = pltpu.PrefetchScalarGridSpec(
        num_scalar_prefetch=2,
        grid=(n_pad // FFN_ROWS,),
        in_specs=[rows, rows,
                  pl.BlockSpec((1, D_MODEL, D_FF_EXPERT), lambda c, ce, cv: (ce[c], 0, 0)),
                  pl.BlockSpec((1, D_MODEL, D_FF_EXPERT), lambda c, ce, cv: (ce[c], 0, 0)),
                  pl.BlockSpec((1, D_FF_EXPERT, D_MODEL), lambda c, ce, cv: (ce[c], 0, 0))],
        out_specs=[rows, rows],
        scratch_shapes=[pltpu.VMEM((D_MODEL, D_FF_EXPERT), BF16), pltpu.VMEM((D_MODEL, D_FF_EXPERT), BF16),
                        pltpu.VMEM((D_FF_EXPERT, D_MODEL), BF16)],
    )
    out = jax.ShapeDtypeStruct((n_pad, PACK_W), jnp.uint32)
    return pl.pallas_call(
        _ffn_kernel,
        grid_spec=grid_spec,
        out_shape=[out, out],
        compiler_params=pltpu.CompilerParams(dimension_semantics=("arbitrary",),
                                             vmem_limit_bytes=VMEM_LIMIT),
        name="ffn",
    )(chunk_expert, chunk_valid, xsa, xsb, wg, wu, wd)


def _final_kernel(x1_ref, y1a_ref, y1b_ref, y2a_ref, y2b_ref, w_ref, gf_ref, o_ref):
    w1 = w_ref[:, 0:1]
    w2 = w_ref[:, 1:2]
    a1_lo, a1_hi = _unpack_bf16_pairs(y1a_ref[...])
    b1_lo, b1_hi = _unpack_bf16_pairs(y1b_ref[...])
    a2_lo, a2_hi = _unpack_bf16_pairs(y2a_ref[...])
    b2_lo, b2_hi = _unpack_bf16_pairs(y2b_ref[...])
    y = jnp.concatenate([w1 * a1_lo + w2 * a2_lo, w1 * b1_lo + w2 * b2_lo,
                         w1 * a1_hi + w2 * a2_hi, w1 * b1_hi + w2 * b2_hi], axis=1)
    xo = x1_ref[...] + y
    ms = jnp.mean(xo * xo, axis=-1, keepdims=True)
    o_ref[...] = xo * lax.rsqrt(ms + EPS) * gf_ref[...]


def _final(x1, yga, ygb, wtok, gf, out_prev, part):
    t = x1.shape[0]
    first = part * (t // FINAL_TM)
    row = lambda i: (i, 0)
    one = pl.BlockSpec((None, FINAL_TM, PACK_W), lambda i: (0, i, 0))
    two = pl.BlockSpec((None, FINAL_TM, PACK_W), lambda i: (1, i, 0))
    in_specs = [pl.BlockSpec((FINAL_TM, D_MODEL), row), one, one, two, two,
                pl.BlockSpec((FINAL_TM, LANES), row), pl.BlockSpec((1, D_MODEL), lambda i: (0, 0))]
    args = [x1, yga, ygb, yga, ygb, wtok, gf]
    kern = _final_kernel
    aliases = {}
    if out_prev is not None:
        in_specs.append(pl.BlockSpec(memory_space=pl.ANY))
        args.append(out_prev)
        aliases = {len(args) - 1: 0}
        kern = lambda *refs: _final_kernel(*refs[:7], refs[8])
    return pl.pallas_call(
        kern,
        grid=(t // FINAL_TM,),
        in_specs=in_specs,
        out_specs=pl.BlockSpec((FINAL_TM, D_MODEL), lambda i: (first + i, 0)),
        out_shape=jax.ShapeDtypeStruct((t * MOE_PARTS, D_MODEL), F32),
        input_output_aliases=aliases,
        compiler_params=pltpu.CompilerParams(dimension_semantics=("arbitrary",),
                                             vmem_limit_bytes=VMEM_LIMIT),
        name="final",
    )(*args)


def _moe_plan(cnt, e1, e2, r1, r2):
    nt, n_exp = cnt.shape
    tot = jnp.sum(cnt, axis=0)
    padded = (tot + FFN_ROWS - 1) // FFN_ROWS * FFN_ROWS
    seg_end = jnp.cumsum(padded)
    seg_start = seg_end - padded
    off = seg_start[None, :] + jnp.cumsum(cnt, axis=0) - cnt
    experts = jnp.arange(n_exp, dtype=jnp.int32)

    def lookup(e):
        return jnp.sum(jnp.where(e[:, :, None] == experts, off[:, None, :], 0), axis=2)

    pos1 = (lookup(e1) + r1).reshape(-1).astype(jnp.int32)
    pos2 = (lookup(e2) + r2).reshape(-1).astype(jnp.int32)
    n_chunks = (nt * MOE_TM * 2 + n_exp * FFN_ROWS) // FFN_ROWS
    cstart = jnp.arange(n_chunks, dtype=jnp.int32) * FFN_ROWS
    ce = jnp.sum((seg_end[None, :] <= cstart[:, None]).astype(jnp.int32), axis=1)
    cec = jnp.minimum(ce, n_exp - 1)
    seg_used = seg_start + tot
    valid = jnp.any((ce[:, None] == experts[None, :]) & (cstart[:, None] < seg_used[None, :]), axis=1)
    return pos1, pos2, cec.astype(jnp.int32), valid.astype(jnp.int32)


def _rope_tables(pos):
    half = ROT_DIM // 2
    inv_freq = ROPE_THETA ** (-jnp.arange(0, ROT_DIM, 2, dtype=F32) / ROT_DIM)
    ang = pos.astype(F32)[:, None] * inv_freq[None, :]
    cos, sin = jnp.cos(ang), jnp.sin(ang)
    l64 = np.arange(LANES) % HEAD_DIM
    f = l64 % half
    first = jnp.asarray(l64 < half)[None, :]
    second = jnp.asarray((l64 >= half) & (l64 < ROT_DIM))[None, :]
    c = jnp.where(first | second, cos[:, f], 1.0)
    s1 = jnp.where(first, -sin[:, f], 0.0)
    s2 = jnp.where(second, sin[:, f], 0.0)
    return c.astype(F32), s1.astype(F32), s2.astype(F32)


def _overlap_table_t():
    cs = np.arange(N_CMP)[None, :] * CMP_STRIDE
    js = np.arange(N_SELBLK)[:, None] * SEL_LEN
    ov = np.clip(np.minimum(cs + CMP_LEN, js + SEL_LEN) - np.maximum(cs, js), 0, None) / CMP_LEN
    ov[:, N_CMP - 1] = 0.0
    return ov.astype(np.float32)


def _expand_table_t():
    out = np.zeros((SEQ, LANES), np.float32)
    out[np.arange(SEQ), np.arange(SEQ) // SEL_LEN] = -NEG
    return out.reshape(N_QBLK // SEL_GROUP, SEL_GROUP * QBLK, LANES)


def kernel(x, norm1_g, w_in, pe_kc, w_kc1, w_kc2, pe_vc, w_vc1, w_vc2, w_o, norm2_g, w_rg, b_rg, w_re, b_re,
           w_gate, w_up, w_down, norm_f_g):
    nb, s, d = x.shape
    assert (s, d) == (SEQ, D_MODEL) and norm1_g.shape[0] == 1
    t = nb * s
    x2 = x.reshape(t, d)

    w_in_p = jnp.pad(w_in[0], ((0, 0), (0, D_IN_PAD - w_in.shape[2]))).astype(BF16)
    rc, rs1, rs2 = _rope_tables(jnp.arange(SEQ))
    cc, cs1, cs2 = _rope_tables(jnp.arange(N_CMP) * CMP_STRIDE + CMP_LEN - 1)
    half_flat = CMP_STRIDE * HEAD_DIM

    def cmp_w1(w1):
        return jnp.concatenate([w1[:half_flat], w1[half_flat:]], axis=1).astype(BF16)

    def cmp_pe(pe):
        rows = pe.reshape(2, half_flat)
        return jnp.concatenate([jnp.broadcast_to(rows[0:1], (PE_ROWS, half_flat)),
                                jnp.broadcast_to(rows[1:2], (PE_ROWS, half_flat))], axis=0).astype(BF16)

    wk2d = jnp.concatenate([w_kc2[0], w_kc2[0]], axis=1).astype(BF16)
    wv2t = w_vc2[0].T.astype(BF16)
    ovl_t = jnp.asarray(_overlap_table_t(), BF16)
    expand_t = jnp.asarray(_expand_table_t(), BF16)

    (aq, ak, av, aq4, ak4, av4, aq16, ak16, av16,
     bq, kvflat, ksw, gates) = _proj(x2, norm1_g, w_in_p, rc, rs1, rs2)
    mix_a = _dilated((aq, ak, av), (aq4, ak4, av4), (aq16, ak16, av16))

    mix_b = _nsa(bq, kvflat, ksw, gates, cmp_w1(w_kc1[0]), wk2d, cmp_pe(pe_kc[0]),
                 cmp_w1(w_vc1[0]), wv2t, cmp_pe(pe_vc[0]), cc, cs1, cs2, ovl_t, expand_t)

    wr_t = jnp.pad(jnp.concatenate([w_rg[0].T, w_re[0].reshape(d, -1).T], axis=0),
                   ((0, ROUTER_ROWS - N_ROUTER), (0, 0)))
    br = jnp.broadcast_to(jnp.pad(jnp.concatenate([b_rg[0], b_re[0].reshape(-1)]),
                                  (0, ROUTER_ROWS - N_ROUTER))[:, None], (ROUTER_ROWS, LANES))
    tri = jnp.asarray(np.triu(np.ones((MOE_TM, MOE_TM), np.float32), 1), BF16)
    tp = t // MOE_PARTS
    n_pad = 2 * tp + N_GROUPS * EXPERTS_PER_GROUP * FFN_ROWS
    w_o_bf = w_o[0].astype(BF16)
    gf = norm_f_g.reshape(1, d)
    routed = []
    for part in range(MOE_PARTS):
        x1, hpa, hpb, e1, e2, r1, r2, wtok, cnt = _post(x2, mix_a, mix_b, w_o_bf, norm2_g, wr_t, br, tri, part)
        pos1, pos2, chunk_expert, chunk_valid = _moe_plan(cnt[:, :, 0], e1[:, 0], e2[:, 0], r1[:, 0], r2[:, 0])
        xsa = _sc_scatter_rows(hpa, pos1, pos2, n_pad)
        xsb = _sc_scatter_rows(hpb, pos1, pos2, n_pad)
        routed.append((x1, wtok, pos1, pos2, chunk_expert, chunk_valid, xsa, xsb))
    gathered = []
    for x1, wtok, pos1, pos2, chunk_expert, chunk_valid, xsa, xsb in routed:
        ysa, ysb = _ffn(chunk_expert, chunk_valid, xsa, xsb, w_gate[0], w_up[0], w_down[0])
        pos12 = jnp.concatenate([pos1, pos2])
        gathered.append((x1, wtok, _sc_gather_rows(ysa, pos12), _sc_gather_rows(ysb, pos12)))
    out = None
    for part, (x1, wtok, yga, ygb) in enumerate(gathered):
        out = _final(x1, yga.reshape(2, tp, PACK_W), ygb.reshape(2, tp, PACK_W), wtok, gf, out, part)
    return out.reshape(nb, s, d)
```

```python
import numpy as np
import jax
import jax.numpy as jnp
from jax import lax
from jax.experimental import pallas as pl
from jax.experimental.pallas import tpu as pltpu
from jax.experimental.pallas import tpu_sc as plsc

F32 = jnp.float32
BF16 = jnp.bfloat16

D_MODEL = 1024
SEQ = 2048
HEAD_DIM = 64
N_HEADS_DIL = 8
DIL_PATTERNS = ((128, 1), (512, 4), (2048, 16))
N_HEADS_NSA = 8
N_KV_NSA = 2
NSA_REP = N_HEADS_NSA // N_KV_NSA
CMP_STRIDE = 16
CMP_LEN = 32
CMP_HIDDEN = 256
SEL_LEN = 64
N_SEL = 8
WIN = 512
ROPE_THETA = 500000.0
ROT_DIM = HEAD_DIM // 4
N_GROUPS = 8
EXPERTS_PER_GROUP = 8
D_FF_EXPERT = 256
EPS = 1e-6
NEG = -1e30
BIG = 1e9
LOWEST = -3.0e38
LOG2_E = 1.4426950408889634

A_COLS = N_HEADS_DIL * HEAD_DIM
B_COLS = N_HEADS_NSA * HEAD_DIM
KV_COLS = N_KV_NSA * HEAD_DIM
N_EXPERTS = N_GROUPS * EXPERTS_PER_GROUP
N_ROUTER = N_GROUPS + N_EXPERTS

LANES = 128
QBLK = 128
N_QBLK = SEQ // QBLK
N_CMP = SEQ // CMP_STRIDE
N_SELBLK = SEQ // SEL_LEN
D_IN_PAD = 23 * LANES
PROJ_TM = 1024
SEL_GROUP = 4
MOE_TM = 1024
FINAL_PARTS = 2
ROUTER_ROWS = 80
FFN_ROWS = 512
FINAL_TM = 1024
PACK_W = D_MODEL // 4
BF16_SUBLANES = 16
ONES_ROWS = BF16_SUBLANES
PE_ROWS = BF16_SUBLANES
GATE_ROWS = 16
SC_WINDOW = 128
VMEM_LIMIT = 56 * 1024 * 1024

_NT = (((1,), (1,)), ((), ()))


def _lane_iota(shape):
    return lax.broadcasted_iota(jnp.int32, shape, 1)


def _row_iota(shape):
    return lax.broadcasted_iota(jnp.int32, shape, 0)


def _rope_lanes(y, c, s1, s2):
    return y * c + pltpu.roll(y, LANES - ROT_DIM // 2, axis=1) * s1 + pltpu.roll(y, ROT_DIM // 2, axis=1) * s2


def _proj_kernel(x_ref, g_ref, w_ref, c_ref, s1_ref, s2_ref,
                 aq_ref, ak_ref, av_ref, aq4_ref, ak4_ref, av4_ref, aq16_ref, ak16_ref, av16_ref,
                 bq_ref, kvflat_ref, ksw_ref, gate_ref, nat_f32, d4_f32):
    x = x_ref[...]
    ms = jnp.mean(x * x, axis=-1, keepdims=True)
    h = (x * lax.rsqrt(ms + EPS) * g_ref[...]).astype(BF16)
    c = c_ref[...]
    s1 = s1_ref[...]
    s2 = s2_ref[...]

    def seg(lo, width):
        return jnp.dot(h, w_ref[:, lo:lo + width], preferred_element_type=F32)

    def store(dst, col, y, rope, scale, dilated=None):
        for j in range(y.shape[1] // LANES):
            yj = y[:, j * LANES:(j + 1) * LANES]
            if rope:
                yj = _rope_lanes(yj, c, s1, s2)
            if scale != 1.0:
                yj = yj * scale
            cols = slice(col + j * LANES, col + (j + 1) * LANES)
            dst[:, cols] = yj.astype(dst.dtype)
            if dilated is not None:
                d4_ref, d16_ref = dilated
                quarter = PROJ_TM // 4
                nat_f32[...] = yj
                for r in range(4):
                    part = nat_f32[pl.ds(r, quarter, stride=4), :]
                    d4_ref[r, :, cols] = part.astype(d4_ref.dtype)
                    d4_f32[r * quarter:(r + 1) * quarter, :] = part
                for r in range(4):
                    for a_ in range(4):
                        part = d4_f32[pl.ds(r * quarter + a_, quarter // 4, stride=4), :]
                        d16_ref[4 * a_ + r, :, cols] = part.astype(d16_ref.dtype)

    qscale = HEAD_DIM ** -0.5 * LOG2_E
    store(aq_ref, 0, seg(0, A_COLS), True, qscale, (aq4_ref, aq16_ref))
    store(ak_ref, 0, seg(A_COLS, A_COLS), True, 1.0, (ak4_ref, ak16_ref))
    store(av_ref, 0, seg(2 * A_COLS, A_COLS), False, 1.0, (av4_ref, av16_ref))
    nsa0 = 3 * A_COLS
    store(bq_ref, 0, seg(nsa0, B_COLS), True, qscale)
    y = seg(nsa0 + B_COLS, 2 * KV_COLS)
    quarter = PROJ_TM // 4
    left_half = _lane_iota((quarter // 4, LANES)) < HEAD_DIM
    for kind in range(2):
        nat_f32[...] = y[:, kind * KV_COLS:(kind + 1) * KV_COLS]
        for r in range(4):
            d4_f32[r * quarter:(r + 1) * quarter, :] = nat_f32[pl.ds(r, quarter, stride=4), :]

        def chunk_pos(i):
            return d4_f32[pl.ds((i % 4) * quarter + i // 4, quarter // 4, stride=4), :]

        for pair in range(CMP_STRIDE // 2):
            even, odd = chunk_pos(2 * pair), chunk_pos(2 * pair + 1)
            cols = slice(pair * LANES, (pair + 1) * LANES)
            kvflat_ref[2 * kind, :, cols] = jnp.where(left_half, even, pltpu.roll(odd, HEAD_DIM, axis=1)).astype(BF16)
            kvflat_ref[2 * kind + 1, :, cols] = jnp.where(left_half, pltpu.roll(even, HEAD_DIM, axis=1), odd).astype(BF16)
    kv0 = nsa0 + B_COLS + 2 * KV_COLS
    for n, rope in enumerate((True, False, True, False)):
        store(ksw_ref, n * KV_COLS, seg(kv0 + n * KV_COLS, KV_COLS), rope, 1.0)
    gate_ref[...] = jax.nn.sigmoid(seg(kv0 + 4 * KV_COLS, LANES))


def _proj(x2, g1, w_in_p, rc, rs1, rs2):
    t = x2.shape[0]
    nb = t // SEQ
    nblk_s = SEQ // PROJ_TM
    row = lambda i: (i, 0)
    pos = lambda i: (i % nblk_s, 0)
    const = lambda i: (0, 0)
    perm = lambda i: (i // nblk_s, 0, i % nblk_s, 0)
    wide = jax.ShapeDtypeStruct((t, A_COLS), BF16)
    wide_spec = pl.BlockSpec((PROJ_TM, A_COLS), row)
    d4 = jax.ShapeDtypeStruct((nb, 4, SEQ // 4, A_COLS), BF16)
    d4_spec = pl.BlockSpec((None, 4, PROJ_TM // 4, A_COLS), perm)
    d16 = jax.ShapeDtypeStruct((nb, 16, SEQ // 16, A_COLS), BF16)
    d16_spec = pl.BlockSpec((None, 16, PROJ_TM // 16, A_COLS), perm)
    flat = jax.ShapeDtypeStruct((nb, 2 * N_KV_NSA, N_CMP, CMP_STRIDE * HEAD_DIM), BF16)
    flat_spec = pl.BlockSpec((None, 2 * N_KV_NSA, PROJ_TM // CMP_STRIDE, CMP_STRIDE * HEAD_DIM), perm)
    outs = [wide] * 3 + [d4] * 3 + [d16] * 3 + [wide, flat, wide,
                                               jax.ShapeDtypeStruct((t, LANES), F32)]
    out_specs = [wide_spec] * 3 + [d4_spec] * 3 + [d16_spec] * 3 + [
        wide_spec, flat_spec, wide_spec, pl.BlockSpec((PROJ_TM, LANES), row)]
    return pl.pallas_call(
        _proj_kernel,
        grid=(t // PROJ_TM,),
        in_specs=[pl.BlockSpec((PROJ_TM, D_MODEL), row), pl.BlockSpec((1, D_MODEL), const),
                  pl.BlockSpec((D_MODEL, D_IN_PAD), const),
                  pl.BlockSpec((PROJ_TM, LANES), pos), pl.BlockSpec((PROJ_TM, LANES), pos),
                  pl.BlockSpec((PROJ_TM, LANES), pos)],
        out_specs=out_specs,
        out_shape=outs,
        scratch_shapes=[pltpu.VMEM((PROJ_TM, LANES), F32)] * 2,
        compiler_params=pltpu.CompilerParams(dimension_semantics=("arbitrary",),
                                             vmem_limit_bytes=VMEM_LIMIT),
        name="proj",
    )(x2, g1, w_in_p, rc, rs1, rs2)


def _dilated_kernel(q1_ref, k1_ref, v1_ref, q4_ref, k4_ref, v4_ref, q16_ref, k16_ref, v16_ref, o_ref,
                    qt, vt, op, lp, on0, on1, on2, ln0, ln1, ln2):
    c2 = _row_iota((2 * QBLK, 2 * QBLK))
    a2 = _lane_iota((2 * QBLK, 2 * QBLK)) % QBLK
    band = jnp.where((c2 >= a2) & (c2 <= a2 + QBLK), 0.0, NEG)
    band_noprev = jnp.where(c2 < QBLK, NEG, band)
    c1 = _row_iota((QBLK, 2 * QBLK))
    a1 = _lane_iota((QBLK, 2 * QBLK)) % QBLK
    band_own = jnp.where(c1 <= a1, 0.0, NEG)
    top_sq = _row_iota((LANES, QBLK)) < HEAD_DIM

    for j in range(N_QBLK):
        vt[j, LANES:, :] = jnp.ones((ONES_ROWS, QBLK), BF16)

    inputs = ((q1_ref, k1_ref, v1_ref), (q4_ref, k4_ref, v4_ref), (q16_ref, k16_ref, v16_ref))
    for pi, (window, dil) in enumerate(DIL_PATTERNS):
        seg_len = SEQ // dil
        nseg_blk = seg_len // QBLK
        q_ref, k_ref, v_ref = inputs[pi]

        def block(ref, j, nseg_blk=nseg_blk, dil=dil):
            if dil == 1:
                return ref[j * QBLK:(j + 1) * QBLK, :]
            return ref[j // nseg_blk, (j % nseg_blk) * QBLK:(j % nseg_blk + 1) * QBLK, :]

        for j in range(N_QBLK):
            qt[j] = block(q_ref, j).T
            vt[j, 0:LANES, :] = block(v_ref, j).T

        prevs, scores = [], []
        for j in range(N_QBLK):
            qtb = qt[j]
            zero = jnp.zeros_like(qtb)
            q_both = jnp.concatenate([jnp.where(top_sq, qtb, zero), jnp.where(top_sq, zero, qtb)], axis=1)
            if nseg_blk == 1:
                keys, bias, jp = block(k_ref, j), band_own, j
            else:
                first = j % nseg_blk == 0
                jp = j if first else j - 1
                bias = band_noprev if first else band
                keys = jnp.concatenate([block(k_ref, jp), block(k_ref, j)], axis=0)
            prevs.append(jp)
            scores.append(jnp.dot(keys, q_both, preferred_element_type=F32) + bias)
        stats = []
        for s in scores:
            m = jnp.max(s, axis=0, keepdims=True)
            stats.append((m, jnp.exp2((s - m).astype(BF16))))
        outs = []
        for j, (jp, (m, p)) in enumerate(zip(prevs, stats)):
            v_span = vt[j] if nseg_blk == 1 else jnp.concatenate([vt[jp], vt[j]], axis=1)
            outs.append(jnp.dot(v_span, p, preferred_element_type=F32))
        for j, ((m, p), res) in enumerate(zip(stats, outs)):
            l = res[LANES:LANES + 1, :]
            lse = m + jnp.log2(l)
            o_t = jnp.where(top_sq, res[0:LANES, 0:QBLK] / l[:, 0:QBLK], res[0:LANES, QBLK:] / l[:, QBLK:])
            lse_t = jnp.where(top_sq, jnp.broadcast_to(lse[:, 0:QBLK], (LANES, QBLK)),
                              jnp.broadcast_to(lse[:, QBLK:], (LANES, QBLK)))
            op[j * QBLK:(j + 1) * QBLK, :] = o_t.T
            lp[j * QBLK:(j + 1) * QBLK, :] = lse_t.T

        for r in range(dil):
            dst = pl.ds(r, seg_len, stride=dil) if dil > 1 else pl.ds(0, SEQ)
            src = pl.ds(r * seg_len, seg_len)
            (on0, on1, on2)[pi][dst, :] = op[src, :]
            (ln0, ln1, ln2)[pi][dst, :] = lp[src, :]

    l0, l1, l2 = ln0[...], ln1[...], ln2[...]
    mx = jnp.maximum(jnp.maximum(l0, l1), l2)
    e0, e1, e2 = jnp.exp2(l0 - mx), jnp.exp2(l1 - mx), jnp.exp2(l2 - mx)
    den = e0 + e1 + e2
    out = (e0 / den) * on0[...] + (e1 / den) * on1[...] + (e2 / den) * on2[...]
    o_ref[...] = out.astype(o_ref.dtype)


def _dilated(a1, a4, a16):
    t = a1[0].shape[0]
    nb = t // SEQ
    spec1 = pl.BlockSpec((SEQ, LANES), lambda b, hp: (b, hp))
    spec4 = pl.BlockSpec((None, 4, SEQ // 4, LANES), lambda b, hp: (b, 0, 0, hp))
    spec16 = pl.BlockSpec((None, 16, SEQ // 16, LANES), lambda b, hp: (b, 0, 0, hp))
    return pl.pallas_call(
        _dilated_kernel,
        grid=(nb, N_HEADS_DIL // 2),
        in_specs=[spec1] * 3 + [spec4] * 3 + [spec16] * 3,
        out_specs=spec1,
        out_shape=jax.ShapeDtypeStruct((t, A_COLS), BF16),
        scratch_shapes=[pltpu.VMEM((N_QBLK, LANES, QBLK), BF16), pltpu.VMEM((N_QBLK, LANES + ONES_ROWS, QBLK), BF16)]
        + [pltpu.VMEM((SEQ, LANES), F32)] * 8,
        compiler_params=pltpu.CompilerParams(dimension_semantics=("arbitrary", "arbitrary"),
                                             vmem_limit_bytes=VMEM_LIMIT),
        name="dilated",
    )(*a1, *a4, *a16)


def _gelu_tanh(x):
    return 0.5 * x * (1.0 + jnp.tanh(np.sqrt(2.0 / np.pi).astype(np.float32) * (x + 0.044715 * (x * x * x))))


def _softmax_step(state, s, v_t):
    m, l, acc = state
    mn = jnp.maximum(m, jnp.max(s, axis=0, keepdims=True))
    al = jnp.exp2(m - mn)
    p = jnp.exp2(s - mn)
    l = al * l + jnp.sum(p, axis=0, keepdims=True)
    acc = al * acc + jnp.dot(v_t, p.astype(BF16), preferred_element_type=F32)
    return mn, l, acc


def _nsa_kernel(q_ref, kflat_ref, vflat_ref, ksw_ref, gate_ref,
                wk1_ref, wk2d_ref, pek_ref, wv1_ref, wv2t_ref, pev_ref,
                cc_ref, cs1_ref, cs2_ref, ovl_ref, exp_ref,
                o_ref,
                qt, vst, vwt, gtt, kc2, vct):
    g = pl.program_id(1)
    nq = NSA_REP * QBLK

    for i in range(N_QBLK):
        rows = pl.ds(i * QBLK, QBLK)
        qt[i, 0:LANES, :] = q_ref[rows, 0:LANES].T
        qt[i, LANES:2 * LANES, :] = q_ref[rows, LANES:2 * LANES].T
        vs_t = ksw_ref[rows, LANES:2 * LANES].T
        vst[i] = jnp.where(g == 0, vs_t[0:HEAD_DIM], vs_t[HEAD_DIM:])
        vw_t = ksw_ref[rows, 3 * LANES:4 * LANES].T
        vwt[i] = jnp.where(g == 0, vw_t[0:HEAD_DIM], vw_t[HEAD_DIM:])
        gt = gate_ref[rows, :]
        gt = jnp.where(g == 0, gt, pltpu.roll(gt, LANES - 3 * NSA_REP, axis=1))
        gtt[i] = gt.T[0:GATE_ROWS, :]

    def hidden(flat_ref, w1_ref, pe_ref):
        ab = jnp.dot(flat_ref[...], w1_ref[...], preferred_element_type=F32)
        pb = (jnp.dot(pe_ref[0:PE_ROWS, :], w1_ref[:, 0:CMP_HIDDEN], preferred_element_type=F32)
              + jnp.dot(pe_ref[PE_ROWS:, :], w1_ref[:, CMP_HIDDEN:], preferred_element_type=F32))
        hid = ab[:, 0:CMP_HIDDEN] + pltpu.roll(ab[:, CMP_HIDDEN:], N_CMP - 1, axis=0) + pb[0:1, :]
        return _gelu_tanh(hid).astype(BF16)

    hk = hidden(kflat_ref, wk1_ref, pek_ref)
    kc2[...] = _rope_lanes(jnp.dot(hk, wk2d_ref[...], preferred_element_type=F32),
                           cc_ref[...], cs1_ref[...], cs2_ref[...]).astype(BF16)
    hv = hidden(vflat_ref, wv1_ref, pev_ref)
    vct[...] = lax.dot_general(wv2t_ref[...], hv, _NT, preferred_element_type=F32).astype(BF16)

    key = _row_iota((QBLK, nq))
    qry = _lane_iota((QBLK, nq)) % QBLK
    causal = jnp.where(key <= qry, 0.0, NEG)
    upper = jnp.where(key > qry, 0.0, NEG)
    blk = _row_iota((N_SELBLK, QBLK))
    qry1 = _lane_iota((N_SELBLK, QBLK))
    grp_keys = SEL_GROUP * QBLK
    win_blocks = WIN // QBLK

    def qgroup(w, full_window):
        blocks = [SEL_GROUP * w + d for d in range(SEL_GROUP)]
        row0s = [pl.multiple_of(i * QBLK, QBLK) for i in blocks]

        q_pads = []
        for i in blocks:
            q_t = jnp.concatenate([qt[i, r * HEAD_DIM:(r + 1) * HEAD_DIM, :] for r in range(NSA_REP)], axis=1)
            zero = jnp.zeros_like(q_t)
            q_pads.append(jnp.where(g == 0, jnp.concatenate([q_t, zero], axis=0),
                                    jnp.concatenate([zero, q_t], axis=0)))
        s_cmp = [jnp.dot(kc2[...], qp, preferred_element_type=F32) for qp in q_pads]
        s_win, v_win = [], []
        for d, (i, qp) in enumerate(zip(blocks, q_pads)):
            if full_window:
                first = i - win_blocks
                keys = ksw_ref[pl.ds(pl.multiple_of(first * QBLK, QBLK), (win_blocks + 1) * QBLK), 2 * LANES:3 * LANES]
                s = jnp.dot(keys, qp, preferred_element_type=F32)
                s = jnp.concatenate([s[0:QBLK] + upper, s[QBLK:win_blocks * QBLK], s[win_blocks * QBLK:] + causal],
                                    axis=0)
                v_t = jnp.concatenate([vwt[first + e] for e in range(win_blocks + 1)], axis=1)
            else:
                s = jnp.dot(ksw_ref[0:(d + 1) * QBLK, 2 * LANES:3 * LANES], qp, preferred_element_type=F32)
                s = jnp.concatenate([s[0:d * QBLK], s[d * QBLK:] + causal], axis=0) if d else s + causal
                v_t = jnp.concatenate([vwt[e] for e in range(d + 1)], axis=1) if d else vwt[0]
            s_win.append(s)
            v_win.append(v_t)

        pcs = []
        for row0, s in zip(row0s, s_cmp):
            cvalid = (CMP_STRIDE * key + (CMP_LEN - 1) <= row0 + qry) & (key < N_CMP - 1)
            s = jnp.where(cvalid, s, NEG)
            m = jnp.max(s, axis=0, keepdims=True)
            p = jnp.where(cvalid, jnp.exp2(s - m), 0.0)
            l = jnp.sum(p, axis=0, keepdims=True)
            pcs.append(p / jnp.where(l > 0, l, 1.0))
        o_cmp = [jnp.dot(vct[...], pc.astype(BF16), preferred_element_type=F32) for pc in pcs]

        ovl = ovl_ref[...]
        imps = []
        for pc in pcs:
            psum = pc[:, 0:QBLK] + pc[:, QBLK:2 * QBLK] + pc[:, 2 * QBLK:3 * QBLK] + pc[:, 3 * QBLK:]
            p_hi = psum.astype(BF16)
            p_lo = (psum - p_hi.astype(F32)).astype(BF16)
            imps.append(jnp.dot(ovl, p_hi, preferred_element_type=F32) + jnp.dot(ovl, p_lo, preferred_element_type=F32))

        o_win = []
        for s, v_t in zip(s_win, v_win):
            m = jnp.max(s, axis=0, keepdims=True)
            p = jnp.exp2(s - m)
            l = jnp.sum(p, axis=0, keepdims=True)
            o_win.append(jnp.dot(v_t, p.astype(BF16), preferred_element_type=F32) / l)

        q_sel = []
        for row0, imp, qp in zip(row0s, imps, q_pads):
            t1 = row0 + qry1
            blk_t = t1 // SEL_LEN
            forced = (blk == 0) | (blk == blk_t) | (blk == blk_t - 1)
            v = jnp.where(blk * SEL_LEN <= t1, jnp.where(forced, BIG, imp), -BIG)
            selm = jnp.zeros((N_SELBLK, QBLK), F32)
            for _ in range(N_SEL):
                mx = jnp.max(v, axis=0, keepdims=True)
                first_hit = jnp.min(jnp.where(v == mx, blk, N_SELBLK), axis=0, keepdims=True)
                hit = blk == first_hit
                selm = jnp.where(hit, 1.0, selm)
                v = jnp.where(hit, LOWEST, v)
            notsel = jnp.concatenate([selm - 1.0, jnp.zeros((LANES - N_SELBLK, QBLK), F32)], axis=0).astype(BF16)
            q_sel.append(jnp.concatenate([qp, jnp.concatenate([notsel] * NSA_REP, axis=1)], axis=0))

        def sel_keys(t, n_rows):
            rows_t = pl.ds(pl.multiple_of(t * grp_keys, grp_keys), n_rows)
            return jnp.concatenate([ksw_ref[rows_t, 0:LANES], exp_ref[t, 0:n_rows, :]], axis=1)

        def sel_values(t, n_blk):
            return jnp.concatenate([vst[t * SEL_GROUP + e] for e in range(n_blk)], axis=1) if n_blk > 1 else vst[t * SEL_GROUP]

        wide = SEL_GROUP * nq
        st = (jnp.full((1, wide), LOWEST, F32), jnp.zeros((1, wide), F32), jnp.zeros((HEAD_DIM, wide), F32))
        if full_window:
            q_all = jnp.concatenate(q_sel, axis=1)

            def sel_body(t, st):
                s = jnp.dot(sel_keys(t, grp_keys), q_all, preferred_element_type=F32)
                return _softmax_step(st, s, sel_values(t, SEL_GROUP))

            for t_ in range(w):
                st = sel_body(t_, st)
        o_sel = []
        for d, qs in enumerate(q_sel):
            s = jnp.dot(sel_keys(w, (d + 1) * QBLK), qs, preferred_element_type=F32)
            s = jnp.concatenate([s[0:d * QBLK], s[d * QBLK:] + causal], axis=0) if d else s + causal
            lanes = slice(d * nq, (d + 1) * nq)
            m, l, acc = _softmax_step((st[0][:, lanes], st[1][:, lanes], st[2][:, lanes]), s, sel_values(w, d + 1))
            o_sel.append(acc / l)

        for n, (i, row0) in enumerate(zip(blocks, row0s)):
            gti = gtt[i]

            def gate(br):
                return jnp.concatenate([gti[r * 3 + br:r * 3 + br + 1, :] for r in range(NSA_REP)], axis=1)

            out_t = gate(0) * o_cmp[n] + gate(1) * o_sel[n] + gate(2) * o_win[n]
            for pair in range(NSA_REP // 2):
                both = jnp.concatenate([out_t[:, (2 * pair) * QBLK:(2 * pair + 1) * QBLK],
                                        out_t[:, (2 * pair + 1) * QBLK:(2 * pair + 2) * QBLK]], axis=0)
                o_ref[pl.ds(row0, QBLK), pair * LANES:(pair + 1) * LANES] = both.T.astype(o_ref.dtype)

    qgroup(0, False)
    for w in range(1, N_QBLK // SEL_GROUP):
        qgroup(w, True)


def _nsa(bq, kvflat, ksw, gates, wk1, wk2d, pek, wv1, wv2t, pev, cc, cs1, cs2, ovl_t, expand_t):
    t = bq.shape[0]
    nb = t // SEQ
    const2 = lambda b, g: (0, 0)
    return pl.pallas_call(
        _nsa_kernel,
        grid=(nb, N_KV_NSA),
        in_specs=[
            pl.BlockSpec((SEQ, 2 * LANES), lambda b, g: (b, g)),
            pl.BlockSpec((None, None, N_CMP, CMP_STRIDE * HEAD_DIM), lambda b, g: (b, g, 0, 0)),
            pl.BlockSpec((None, None, N_CMP, CMP_STRIDE * HEAD_DIM), lambda b, g: (b, N_KV_NSA + g, 0, 0)),
            pl.BlockSpec((SEQ, 4 * LANES), lambda b, g: (b, 0)),
            pl.BlockSpec((SEQ, LANES), lambda b, g: (b, 0)),
            pl.BlockSpec((CMP_STRIDE * HEAD_DIM, 2 * CMP_HIDDEN), const2),
            pl.BlockSpec((CMP_HIDDEN, LANES), const2),
            pl.BlockSpec((2 * PE_ROWS, CMP_STRIDE * HEAD_DIM), const2),
            pl.BlockSpec((CMP_STRIDE * HEAD_DIM, 2 * CMP_HIDDEN), const2),
            pl.BlockSpec((HEAD_DIM, CMP_HIDDEN), const2),
            pl.BlockSpec((2 * PE_ROWS, CMP_STRIDE * HEAD_DIM), const2),
            pl.BlockSpec((N_CMP, LANES), const2),
            pl.BlockSpec((N_CMP, LANES), const2),
            pl.BlockSpec((N_CMP, LANES), const2),
            pl.BlockSpec((N_SELBLK, LANES), const2),
            pl.BlockSpec((N_QBLK // SEL_GROUP, SEL_GROUP * QBLK, LANES), lambda b, g: (0, 0, 0)),
        ],
        out_specs=pl.BlockSpec((SEQ, 2 * LANES), lambda b, g: (b, g)),
        out_shape=jax.ShapeDtypeStruct((t, B_COLS), BF16),
        scratch_shapes=[pltpu.VMEM((N_QBLK, 2 * LANES, QBLK), BF16),
                        pltpu.VMEM((N_QBLK, HEAD_DIM, QBLK), BF16),
                        pltpu.VMEM((N_QBLK, HEAD_DIM, QBLK), BF16),
                        pltpu.VMEM((N_QBLK, GATE_ROWS, QBLK), F32),
                        pltpu.VMEM((N_CMP, LANES), BF16),
                        pltpu.VMEM((HEAD_DIM, N_CMP), BF16)],
        compiler_params=pltpu.CompilerParams(dimension_semantics=("arbitrary", "arbitrary"),
                                             vmem_limit_bytes=VMEM_LIMIT),
        name="nsa",
    )(bq, kvflat, kvflat, ksw, gates, wk1, wk2d, pek, wv1, wv2t, pev, cc, cs1, cs2, ovl_t, expand_t)


def _pack_bf16_pairs(lo, hi):
    lo_b = lax.bitcast_convert_type(lo.astype(BF16).astype(F32), jnp.uint32)
    hi_b = lax.bitcast_convert_type(hi.astype(BF16).astype(F32), jnp.uint32)
    return (hi_b & jnp.uint32(0xFFFF0000)) | (lo_b >> 16)


def _unpack_bf16_pairs(p):
    lo = lax.bitcast_convert_type(p << 16, F32)
    hi = lax.bitcast_convert_type(p & jnp.uint32(0xFFFF0000), F32)
    return lo, hi


def _post_kernel(x_ref, ma_ref, mb_ref, wo_ref, g2_ref, wr_ref, br_ref, tri_ref,
                 x1_ref, ha_ref, hb_ref, e1_ref, e2_ref, r1_ref, r2_ref, w_ref, cnt_ref):
    x1 = (x_ref[...] + jnp.dot(ma_ref[...], wo_ref[0:A_COLS, :], preferred_element_type=F32)
          + jnp.dot(mb_ref[...], wo_ref[A_COLS:, :], preferred_element_type=F32))
    x1_ref[...] = x1
    ms = jnp.mean(x1 * x1, axis=-1, keepdims=True)
    h2 = x1 * lax.rsqrt(ms + EPS) * g2_ref[...]
    ha_ref[...] = _pack_bf16_pairs(h2[:, 0:PACK_W], h2[:, 2 * PACK_W:3 * PACK_W])
    hb_ref[...] = _pack_bf16_pairs(h2[:, PACK_W:2 * PACK_W], h2[:, 3 * PACK_W:])

    h_hi = h2.astype(BF16)
    h_lo = (h2 - h_hi.astype(F32)).astype(BF16)
    wr = wr_ref[...]
    w_hi = wr.astype(BF16)
    w_lo = (wr - w_hi.astype(F32)).astype(BF16)
    lg = (lax.dot_general(w_hi, h_hi, _NT, preferred_element_type=F32)
          + lax.dot_general(w_hi, h_lo, _NT, preferred_element_type=F32)
          + lax.dot_general(w_lo, h_hi, _NT, preferred_element_type=F32)) + br_ref[:, 0:1]
    gl = lg[0:N_GROUPS, :]
    sub = _row_iota(gl.shape)
    gmax = jnp.max(gl, axis=0, keepdims=True)
    p_g = 1.0 / jnp.sum(jnp.exp(gl - gmax), axis=0, keepdims=True)
    g_star = jnp.min(jnp.where(gl == gmax, sub, N_GROUPS), axis=0, keepdims=True)
    el = jnp.zeros_like(gl)
    for gi in range(N_GROUPS):
        lo = N_GROUPS + gi * EXPERTS_PER_GROUP
        el = el + jnp.where(g_star == gi, lg[lo:lo + EXPERTS_PER_GROUP, :], 0.0)
    emax = jnp.max(el, axis=0, keepdims=True)
    ee = jnp.exp(el - emax)
    pe = ee / jnp.sum(ee, axis=0, keepdims=True)
    v1 = jnp.max(pe, axis=0, keepdims=True)
    i1 = jnp.min(jnp.where(pe == v1, sub, EXPERTS_PER_GROUP), axis=0, keepdims=True)
    pe2 = jnp.where(sub == i1, -1.0, pe)
    v2 = jnp.max(pe2, axis=0, keepdims=True)
    i2 = jnp.min(jnp.where(pe2 == v2, sub, EXPERTS_PER_GROUP), axis=0, keepdims=True)
    tot = v1 + v2
    e1 = g_star * EXPERTS_PER_GROUP + i1
    e2 = g_star * EXPERTS_PER_GROUP + i2
    e1_ref[0] = e1
    e2_ref[0] = e2

    wslab = jnp.concatenate([v1 / tot * p_g, v2 / tot * p_g, jnp.zeros((LANES - 2, MOE_TM), F32)], axis=0)
    for j in range(MOE_TM // LANES):
        w_ref[j * LANES:(j + 1) * LANES, :] = wslab[:, j * LANES:(j + 1) * LANES].T

    n_exp = N_EXPERTS
    sub_e = _row_iota((n_exp, MOE_TM))
    hit1 = sub_e == e1
    hit2 = sub_e == e2
    assigned = jnp.where(hit1, 1.0, 0.0) + jnp.where(hit2, 1.0, 0.0)
    before = jnp.dot(assigned.astype(BF16), tri_ref[...], preferred_element_type=F32)
    r1_ref[0] = jnp.sum(jnp.where(hit1, before, 0.0), axis=0, keepdims=True).astype(jnp.int32)
    r2_ref[0] = jnp.sum(jnp.where(hit2, before, 0.0), axis=0, keepdims=True).astype(jnp.int32)
    cnt = jnp.sum(assigned, axis=1, keepdims=True).astype(jnp.int32)
    cnt_ref[0] = jnp.broadcast_to(cnt, (n_exp, LANES))


def _post(x2, mix_a, mix_b, w_o, g2, wr_t, br, tri):
    t = x2.shape[0]
    nt = t // MOE_TM
    n_exp = N_EXPERTS
    row = lambda i: (i, 0)
    const = lambda i: (0, 0)
    tok = lambda i: (i, 0, 0)
    tok_spec = pl.BlockSpec((1, 1, MOE_TM), tok)
    tok_shape = jax.ShapeDtypeStruct((nt, 1, MOE_TM), jnp.int32)
    return pl.pallas_call(
        _post_kernel,
        grid=(nt,),
        in_specs=[pl.BlockSpec((MOE_TM, D_MODEL), row), pl.BlockSpec((MOE_TM, A_COLS), row),
                  pl.BlockSpec((MOE_TM, B_COLS), row), pl.BlockSpec((D_MODEL, D_MODEL), const),
                  pl.BlockSpec((1, D_MODEL), const), pl.BlockSpec((ROUTER_ROWS, D_MODEL), const),
                  pl.BlockSpec((ROUTER_ROWS, LANES), const), pl.BlockSpec((MOE_TM, MOE_TM), const)],
        out_specs=[pl.BlockSpec((MOE_TM, D_MODEL), row),
                   pl.BlockSpec((MOE_TM, PACK_W), row), pl.BlockSpec((MOE_TM, PACK_W), row),
                   tok_spec, tok_spec, tok_spec, tok_spec,
                   pl.BlockSpec((MOE_TM, LANES), row),
                   pl.BlockSpec((1, n_exp, LANES), tok)],
        out_shape=[jax.ShapeDtypeStruct((t, D_MODEL), F32),
                   jax.ShapeDtypeStruct((t, PACK_W), jnp.uint32), jax.ShapeDtypeStruct((t, PACK_W), jnp.uint32),
                   tok_shape, tok_shape, tok_shape, tok_shape,
                   jax.ShapeDtypeStruct((t, LANES), F32),
                   jax.ShapeDtypeStruct((nt, n_exp, LANES), jnp.int32)],
        compiler_params=pltpu.CompilerParams(dimension_semantics=("arbitrary",),
                                             vmem_limit_bytes=VMEM_LIMIT),
        name="post",
    )(x2, mix_a, mix_b, w_o, g2, wr_t, br, tri)


def _sc_mesh():
    return plsc.VectorSubcoreMesh(core_axis_name="core", subcore_axis_name="subcore")


def _sc_scatter_rows(x, pos1, pos2, n_out):
    r, c = x.shape

    @pl.kernel(out_type=jax.ShapeDtypeStruct((n_out, c), x.dtype), mesh=_sc_mesh(), scratch_types=[])
    def scatter_kernel(x_hbm, p1_hbm, p2_hbm, o_hbm):
        def body(x_vmem, p1_vmem, p2_vmem):
            pltpu.sync_copy(x_vmem, o_hbm.at[p1_vmem.at[0]])
            pltpu.sync_copy(x_vmem, o_hbm.at[p2_vmem.at[0]])

        idx = pl.BlockSpec((1, SC_WINDOW), lambda i: (0, i))
        pltpu.emit_pipeline(body, grid=(r // SC_WINDOW,),
                            in_specs=[pl.BlockSpec((SC_WINDOW, c), lambda i: (i, 0)), idx, idx], out_specs=[],
                            core_axis_name=("core", "subcore"),
                            dimension_semantics=(pltpu.PARALLEL,))(x_hbm, p1_hbm, p2_hbm)

    return scatter_kernel(x, pos1.reshape(1, r), pos2.reshape(1, r))


def _sc_gather_rows(y, idx):
    n = idx.shape[0]
    c = y.shape[1]

    @pl.kernel(out_type=jax.ShapeDtypeStruct((n, c), y.dtype), mesh=_sc_mesh(), scratch_types=[])
    def gather_kernel(y_hbm, i_hbm, o_hbm):
        def body(i_vmem, o_vmem):
            pltpu.sync_copy(y_hbm.at[i_vmem.at[0]], o_vmem)

        pltpu.emit_pipeline(body, grid=(n // SC_WINDOW,),
                            in_specs=[pl.BlockSpec((1, SC_WINDOW), lambda i: (0, i))],
                            out_specs=[pl.BlockSpec((SC_WINDOW, c), lambda i: (i, 0))],
                            core_axis_name=("core", "subcore"),
                            dimension_semantics=(pltpu.PARALLEL,))(i_hbm, o_hbm)

    return gather_kernel(y, idx.reshape(1, n))


def _ffn_kernel(ce_ref, cv_ref, xa_ref, xb_ref, wg_ref, wu_ref, wd_ref, ya_ref, yb_ref, wg_bf, wu_bf, wd_bf):
    c = pl.program_id(0)
    prev = ce_ref[jnp.maximum(c - 1, 0)]

    @pl.when((c == 0) | (ce_ref[c] != prev))
    def _():
        wg_bf[...] = wg_ref[0].astype(BF16)
        wu_bf[...] = wu_ref[0].astype(BF16)
        wd_bf[...] = wd_ref[0].astype(BF16)

    @pl.when(cv_ref[c] == 0)
    def _():
        ya_ref[...] = jnp.zeros_like(ya_ref)
        yb_ref[...] = jnp.zeros_like(yb_ref)

    @pl.when(cv_ref[c] != 0)
    def _():
        a_lo, a_hi = _unpack_bf16_pairs(xa_ref[...])
        b_lo, b_hi = _unpack_bf16_pairs(xb_ref[...])
        xs = jnp.concatenate([a_lo, b_lo, a_hi, b_hi], axis=1).astype(BF16)
        gate = jnp.dot(xs, wg_bf[...], preferred_element_type=F32)
        up = jnp.dot(xs, wu_bf[...], preferred_element_type=F32)
        he = (gate * jax.nn.sigmoid(gate) * up).astype(BF16)
        y = jnp.dot(he, wd_bf[...], preferred_element_type=F32)
        ya_ref[...] = _pack_bf16_pairs(y[:, 0:PACK_W], y[:, 2 * PACK_W:3 * PACK_W])
        yb_ref[...] = _pack_bf16_pairs(y[:, PACK_W:2 * PACK_W], y[:, 3 * PACK_W:])


def _ffn(chunk_expert, chunk_valid, xsa, xsb, wg, wu, wd):
    n_pad = xsa.shape[0]
    rows = pl.BlockSpec((FFN_ROWS, PACK_W), lambda c, ce, cv: (c, 0))
    grid_spec = pltpu.PrefetchScalarGridSpec(
        num_scalar_prefetch=2,
        grid=(n_pad // FFN_ROWS,),
        in_specs=[rows, rows,
                  pl.BlockSpec((1, D_MODEL, D_FF_EXPERT), lambda c, ce, cv: (ce[c], 0, 0)),
                  pl.BlockSpec((1, D_MODEL, D_FF_EXPERT), lambda c, ce, cv: (ce[c], 0, 0)),
                  pl.BlockSpec((1, D_FF_EXPERT, D_MODEL), lambda c, ce, cv: (ce[c], 0, 0))],
        out_specs=[rows, rows],
        scratch_shapes=[pltpu.VMEM((D_MODEL, D_FF_EXPERT), BF16), pltpu.VMEM((D_MODEL, D_FF_EXPERT), BF16),
                        pltpu.VMEM((D_FF_EXPERT, D_MODEL), BF16)],
    )
    out = jax.ShapeDtypeStruct((n_pad, PACK_W), jnp.uint32)
    return pl.pallas_call(
        _ffn_kernel,
        grid_spec=grid_spec,
        out_shape=[out, out],
        compiler_params=pltpu.CompilerParams(dimension_semantics=("arbitrary",),
                                             vmem_limit_bytes=VMEM_LIMIT),
        name="ffn",
    )(chunk_expert, chunk_valid, xsa, xsb, wg, wu, wd)


def _final_kernel(x1_ref, y1a_ref, y1b_ref, y2a_ref, y2b_ref, w_ref, gf_ref, o_ref):
    w1 = w_ref[:, 0:1]
    w2 = w_ref[:, 1:2]
    a1_lo, a1_hi = _unpack_bf16_pairs(y1a_ref[...])
    b1_lo, b1_hi = _unpack_bf16_pairs(y1b_ref[...])
    a2_lo, a2_hi = _unpack_bf16_pairs(y2a_ref[...])
    b2_lo, b2_hi = _unpack_bf16_pairs(y2b_ref[...])
    y = jnp.concatenate([w1 * a1_lo + w2 * a2_lo, w1 * b1_lo + w2 * b2_lo,
                         w1 * a1_hi + w2 * a2_hi, w1 * b1_hi + w2 * b2_hi], axis=1)
    xo = x1_ref[...] + y
    ms = jnp.mean(xo * xo, axis=-1, keepdims=True)
    o_ref[...] = xo * lax.rsqrt(ms + EPS) * gf_ref[...]


def _final(x1, yga, ygb, wtok, gf, out_prev, part):
    t = yga.shape[1]
    first = part * (t // FINAL_TM)
    row = lambda i: (first + i, 0)
    one = pl.BlockSpec((None, FINAL_TM, PACK_W), lambda i: (0, i, 0))
    two = pl.BlockSpec((None, FINAL_TM, PACK_W), lambda i: (1, i, 0))
    in_specs = [pl.BlockSpec((FINAL_TM, D_MODEL), row), one, one, two, two,
                pl.BlockSpec((FINAL_TM, LANES), row), pl.BlockSpec((1, D_MODEL), lambda i: (0, 0))]
    args = [x1, yga, ygb, yga, ygb, wtok, gf]
    kern = _final_kernel
    aliases = {}
    if out_prev is not None:
        in_specs.append(pl.BlockSpec(memory_space=pl.ANY))
        args.append(out_prev)
        aliases = {len(args) - 1: 0}
        kern = lambda *refs: _final_kernel(*refs[:7], refs[8])
    return pl.pallas_call(
        kern,
        grid=(t // FINAL_TM,),
        in_specs=in_specs,
        out_specs=pl.BlockSpec((FINAL_TM, D_MODEL), row),
        out_shape=jax.ShapeDtypeStruct(x1.shape, F32),
        input_output_aliases=aliases,
        compiler_params=pltpu.CompilerParams(dimension_semantics=("arbitrary",),
                                             vmem_limit_bytes=VMEM_LIMIT),
        name="final",
    )(*args)


def _moe_plan(cnt, e1, e2, r1, r2):
    nt, n_exp = cnt.shape
    tot = jnp.sum(cnt, axis=0)
    padded = (tot + FFN_ROWS - 1) // FFN_ROWS * FFN_ROWS
    seg_end = jnp.cumsum(padded)
    seg_start = seg_end - padded
    off = seg_start[None, :] + jnp.cumsum(cnt, axis=0) - cnt
    experts = jnp.arange(n_exp, dtype=jnp.int32)

    def lookup(e):
        return jnp.sum(jnp.where(e[:, :, None] == experts, off[:, None, :], 0), axis=2)

    pos1 = (lookup(e1) + r1).reshape(-1).astype(jnp.int32)
    pos2 = (lookup(e2) + r2).reshape(-1).astype(jnp.int32)
    n_chunks = (nt * MOE_TM * 2 + n_exp * FFN_ROWS) // FFN_ROWS
    cstart = jnp.arange(n_chunks, dtype=jnp.int32) * FFN_ROWS
    ce = jnp.sum((seg_end[None, :] <= cstart[:, None]).astype(jnp.int32), axis=1)
    cec = jnp.minimum(ce, n_exp - 1)
    seg_used = seg_start + tot
    valid = jnp.any((ce[:, None] == experts[None, :]) & (cstart[:, None] < seg_used[None, :]), axis=1)
    return pos1, pos2, cec.astype(jnp.int32), valid.astype(jnp.int32)


def _rope_tables(pos):
    half = ROT_DIM // 2
    inv_freq = ROPE_THETA ** (-jnp.arange(0, ROT_DIM, 2, dtype=F32) / ROT_DIM)
    ang = pos.astype(F32)[:, None] * inv_freq[None, :]
    cos, sin = jnp.cos(ang), jnp.sin(ang)
    l64 = np.arange(LANES) % HEAD_DIM
    f = l64 % half
    first = jnp.asarray(l64 < half)[None, :]
    second = jnp.asarray((l64 >= half) & (l64 < ROT_DIM))[None, :]
    c = jnp.where(first | second, cos[:, f], 1.0)
    s1 = jnp.where(first, -sin[:, f], 0.0)
    s2 = jnp.where(second, sin[:, f], 0.0)
    return c.astype(F32), s1.astype(F32), s2.astype(F32)


def _overlap_table_t():
    cs = np.arange(N_CMP)[None, :] * CMP_STRIDE
    js = np.arange(N_SELBLK)[:, None] * SEL_LEN
    ov = np.clip(np.minimum(cs + CMP_LEN, js + SEL_LEN) - np.maximum(cs, js), 0, None) / CMP_LEN
    ov[:, N_CMP - 1] = 0.0
    return ov.astype(np.float32)


def _expand_table_t():
    out = np.zeros((SEQ, LANES), np.float32)
    out[np.arange(SEQ), np.arange(SEQ) // SEL_LEN] = -NEG
    return out.reshape(N_QBLK // SEL_GROUP, SEL_GROUP * QBLK, LANES)


def kernel(x, norm1_g, w_in, pe_kc, w_kc1, w_kc2, pe_vc, w_vc1, w_vc2, w_o, norm2_g, w_rg, b_rg, w_re, b_re,
           w_gate, w_up, w_down, norm_f_g):
    nb, s, d = x.shape
    assert (s, d) == (SEQ, D_MODEL) and norm1_g.shape[0] == 1
    t = nb * s
    x2 = x.reshape(t, d)

    w_in_p = jnp.pad(w_in[0], ((0, 0), (0, D_IN_PAD - w_in.shape[2]))).astype(BF16)
    rc, rs1, rs2 = _rope_tables(jnp.arange(SEQ))
    cc, cs1, cs2 = _rope_tables(jnp.arange(N_CMP) * CMP_STRIDE + CMP_LEN - 1)
    half_flat = CMP_STRIDE * HEAD_DIM

    def cmp_w1(w1):
        return jnp.concatenate([w1[:half_flat], w1[half_flat:]], axis=1).astype(BF16)

    def cmp_pe(pe):
        rows = pe.reshape(2, half_flat)
        return jnp.concatenate([jnp.broadcast_to(rows[0:1], (PE_ROWS, half_flat)),
                                jnp.broadcast_to(rows[1:2], (PE_ROWS, half_flat))], axis=0).astype(BF16)

    wk2d = jnp.concatenate([w_kc2[0], w_kc2[0]], axis=1).astype(BF16)
    wv2t = w_vc2[0].T.astype(BF16)
    ovl_t = jnp.asarray(_overlap_table_t(), BF16)
    expand_t = jnp.asarray(_expand_table_t(), BF16)

    (aq, ak, av, aq4, ak4, av4, aq16, ak16, av16,
     bq, kvflat, ksw, gates) = _proj(x2, norm1_g, w_in_p, rc, rs1, rs2)
    mix_a = _dilated((aq, ak, av), (aq4, ak4, av4), (aq16, ak16, av16))

    mix_b = _nsa(bq, kvflat, ksw, gates, cmp_w1(w_kc1[0]), wk2d, cmp_pe(pe_kc[0]),
                 cmp_w1(w_vc1[0]), wv2t, cmp_pe(pe_vc[0]), cc, cs1, cs2, ovl_t, expand_t)

    wr_t = jnp.pad(jnp.concatenate([w_rg[0].T, w_re[0].reshape(d, -1).T], axis=0),
                   ((0, ROUTER_ROWS - N_ROUTER), (0, 0)))
    br = jnp.broadcast_to(jnp.pad(jnp.concatenate([b_rg[0], b_re[0].reshape(-1)]),
                                  (0, ROUTER_ROWS - N_ROUTER))[:, None], (ROUTER_ROWS, LANES))
    tri = jnp.asarray(np.triu(np.ones((MOE_TM, MOE_TM), np.float32), 1), BF16)
    x1, hpa, hpb, e1, e2, r1, r2, wtok, cnt = _post(x2, mix_a, mix_b, w_o[0].astype(BF16), norm2_g, wr_t, br, tri)
    pos1, pos2, chunk_expert, chunk_valid = _moe_plan(cnt[:, :, 0], e1[:, 0], e2[:, 0], r1[:, 0], r2[:, 0])
    n_pad = 2 * t + N_EXPERTS * FFN_ROWS
    xsa = _sc_scatter_rows(hpa, pos1, pos2, n_pad)
    xsb = _sc_scatter_rows(hpb, pos1, pos2, n_pad)
    ysa, ysb = _ffn(chunk_expert, chunk_valid, xsa, xsb, w_gate[0], w_up[0], w_down[0])

    tp = t // FINAL_PARTS
    gf = norm_f_g.reshape(1, d)
    out = None
    for part in range(FINAL_PARTS):
        rng = slice(part * tp, (part + 1) * tp)
        idx = jnp.concatenate([pos1[rng], pos2[rng]])
        yga = _sc_gather_rows(ysa, idx).reshape(2, tp, PACK_W)
        ygb = _sc_gather_rows(ysb, idx).reshape(2, tp, PACK_W)
        out = _final(x1, yga, ygb, wtok, gf, out, part)
    return out.reshape(nb, s, d)
```

```python
import numpy as np
import jax
import jax.numpy as jnp
from jax import lax
from jax.experimental import pallas as pl
from jax.experimental.pallas import tpu as pltpu
from jax.experimental.pallas import tpu_sc as plsc

F32 = jnp.float32
BF16 = jnp.bfloat16

D_MODEL = 1024
SEQ = 2048
HEAD_DIM = 64
N_HEADS_DIL = 8
DIL_PATTERNS = ((128, 1), (512, 4), (2048, 16))
N_HEADS_NSA = 8
N_KV_NSA = 2
NSA_REP = N_HEADS_NSA // N_KV_NSA
CMP_STRIDE = 16
CMP_LEN = 32
CMP_HIDDEN = 256
SEL_LEN = 64
N_SEL = 8
WIN = 512
ROPE_THETA = 500000.0
ROT_DIM = HEAD_DIM // 4
N_GROUPS = 8
EXPERTS_PER_GROUP = 8
D_FF_EXPERT = 256
EPS = 1e-6
NEG = -1e30
BIG = 1e9
LOWEST = -3.0e38
LOG2_E = 1.4426950408889634

A_COLS = N_HEADS_DIL * HEAD_DIM
B_COLS = N_HEADS_NSA * HEAD_DIM
KV_COLS = N_KV_NSA * HEAD_DIM
N_EXPERTS = N_GROUPS * EXPERTS_PER_GROUP
N_ROUTER = N_GROUPS + N_EXPERTS

LANES = 128
QBLK = 128
N_QBLK = SEQ // QBLK
N_CMP = SEQ // CMP_STRIDE
N_SELBLK = SEQ // SEL_LEN
D_IN_PAD = 23 * LANES
PROJ_TM = 1024
SEL_GROUP = 4
MOE_TM = 1024
MOE_PARTS = 1
POST_SPLIT = 2
ROUTER_ROWS = 80
FFN_ROWS = 512
FINAL_TM = 1024
PACK_W = D_MODEL // 4
BF16_SUBLANES = 16
ONES_ROWS = BF16_SUBLANES
PE_ROWS = BF16_SUBLANES
GATE_ROWS = 16
SC_WINDOW = 128
VMEM_LIMIT = 56 * 1024 * 1024

_NT = (((1,), (1,)), ((), ()))


def _lane_iota(shape):
    return lax.broadcasted_iota(jnp.int32, shape, 1)


def _row_iota(shape):
    return lax.broadcasted_iota(jnp.int32, shape, 0)


def _rope_lanes(y, c, s1, s2):
    return y * c + pltpu.roll(y, LANES - ROT_DIM // 2, axis=1) * s1 + pltpu.roll(y, ROT_DIM // 2, axis=1) * s2


def _proj_kernel(x_ref, g_ref, w_ref, c_ref, s1_ref, s2_ref,
                 aq_ref, ak_ref, av_ref, aq4_ref, ak4_ref, av4_ref, aq16_ref, ak16_ref, av16_ref,
                 bq_ref, kvflat_ref, ksw_ref, gate_ref, nat_f32, d4_f32):
    x = x_ref[...]
    ms = jnp.mean(x * x, axis=-1, keepdims=True)
    h = (x * lax.rsqrt(ms + EPS) * g_ref[...]).astype(BF16)
    c = c_ref[...]
    s1 = s1_ref[...]
    s2 = s2_ref[...]

    def seg(lo, width):
        return jnp.dot(h, w_ref[:, lo:lo + width], preferred_element_type=F32)

    def store(dst, col, y, rope, scale, dilated=None):
        for j in range(y.shape[1] // LANES):
            yj = y[:, j * LANES:(j + 1) * LANES]
            if rope:
                yj = _rope_lanes(yj, c, s1, s2)
            if scale != 1.0:
                yj = yj * scale
            cols = slice(col + j * LANES, col + (j + 1) * LANES)
            dst[:, cols] = yj.astype(dst.dtype)
            if dilated is not None:
                d4_ref, d16_ref = dilated
                quarter = PROJ_TM // 4
                nat_f32[...] = yj
                for r in range(4):
                    part = nat_f32[pl.ds(r, quarter, stride=4), :]
                    d4_ref[r, :, cols] = part.astype(d4_ref.dtype)
                    d4_f32[r * quarter:(r + 1) * quarter, :] = part
                for r in range(4):
                    for a_ in range(4):
                        part = d4_f32[pl.ds(r * quarter + a_, quarter // 4, stride=4), :]
                        d16_ref[4 * a_ + r, :, cols] = part.astype(d16_ref.dtype)

    qscale = HEAD_DIM ** -0.5 * LOG2_E
    store(aq_ref, 0, seg(0, A_COLS), True, qscale, (aq4_ref, aq16_ref))
    store(ak_ref, 0, seg(A_COLS, A_COLS), True, 1.0, (ak4_ref, ak16_ref))
    store(av_ref, 0, seg(2 * A_COLS, A_COLS), False, 1.0, (av4_ref, av16_ref))
    nsa0 = 3 * A_COLS
    store(bq_ref, 0, seg(nsa0, B_COLS), True, qscale)
    y = seg(nsa0 + B_COLS, 2 * KV_COLS)
    quarter = PROJ_TM // 4
    left_half = _lane_iota((quarter // 4, LANES)) < HEAD_DIM
    for kind in range(2):
        nat_f32[...] = y[:, kind * KV_COLS:(kind + 1) * KV_COLS]
        for r in range(4):
            d4_f32[r * quarter:(r + 1) * quarter, :] = nat_f32[pl.ds(r, quarter, stride=4), :]

        def chunk_pos(i):
            return d4_f32[pl.ds((i % 4) * quarter + i // 4, quarter // 4, stride=4), :]

        for pair in range(CMP_STRIDE // 2):
            even, odd = chunk_pos(2 * pair), chunk_pos(2 * pair + 1)
            cols = slice(pair * LANES, (pair + 1) * LANES)
            kvflat_ref[2 * kind, :, cols] = jnp.where(left_half, even, pltpu.roll(odd, HEAD_DIM, axis=1)).astype(BF16)
            kvflat_ref[2 * kind + 1, :, cols] = jnp.where(left_half, pltpu.roll(even, HEAD_DIM, axis=1), odd).astype(BF16)
    kv0 = nsa0 + B_COLS + 2 * KV_COLS
    for n, rope in enumerate((True, False, True, False)):
        store(ksw_ref, n * KV_COLS, seg(kv0 + n * KV_COLS, KV_COLS), rope, 1.0)
    gate_ref[...] = jax.nn.sigmoid(seg(kv0 + 4 * KV_COLS, LANES))


def _proj(x2, g1, w_in_p, rc, rs1, rs2):
    t = x2.shape[0]
    nb = t // SEQ
    nblk_s = SEQ // PROJ_TM
    row = lambda i: (i, 0)
    pos = lambda i: (i % nblk_s, 0)
    const = lambda i: (0, 0)
    perm = lambda i: (i // nblk_s, 0, i % nblk_s, 0)
    wide = jax.ShapeDtypeStruct((t, A_COLS), BF16)
    wide_spec = pl.BlockSpec((PROJ_TM, A_COLS), row)
    d4 = jax.ShapeDtypeStruct((nb, 4, SEQ // 4, A_COLS), BF16)
    d4_spec = pl.BlockSpec((None, 4, PROJ_TM // 4, A_COLS), perm)
    d16 = jax.ShapeDtypeStruct((nb, 16, SEQ // 16, A_COLS), BF16)
    d16_spec = pl.BlockSpec((None, 16, PROJ_TM // 16, A_COLS), perm)
    flat = jax.ShapeDtypeStruct((nb, 2 * N_KV_NSA, N_CMP, CMP_STRIDE * HEAD_DIM), BF16)
    flat_spec = pl.BlockSpec((None, 2 * N_KV_NSA, PROJ_TM // CMP_STRIDE, CMP_STRIDE * HEAD_DIM), perm)
    outs = [wide] * 3 + [d4] * 3 + [d16] * 3 + [wide, flat, wide,
                                               jax.ShapeDtypeStruct((t, LANES), F32)]
    out_specs = [wide_spec] * 3 + [d4_spec] * 3 + [d16_spec] * 3 + [
        wide_spec, flat_spec, wide_spec, pl.BlockSpec((PROJ_TM, LANES), row)]
    return pl.pallas_call(
        _proj_kernel,
        grid=(t // PROJ_TM,),
        in_specs=[pl.BlockSpec((PROJ_TM, D_MODEL), row), pl.BlockSpec((1, D_MODEL), const),
                  pl.BlockSpec((D_MODEL, D_IN_PAD), const),
                  pl.BlockSpec((PROJ_TM, LANES), pos), pl.BlockSpec((PROJ_TM, LANES), pos),
                  pl.BlockSpec((PROJ_TM, LANES), pos)],
        out_specs=out_specs,
        out_shape=outs,
        scratch_shapes=[pltpu.VMEM((PROJ_TM, LANES), F32)] * 2,
        compiler_params=pltpu.CompilerParams(dimension_semantics=("arbitrary",),
                                             vmem_limit_bytes=VMEM_LIMIT),
        name="proj",
    )(x2, g1, w_in_p, rc, rs1, rs2)


def _dilated_kernel(q1_ref, k1_ref, v1_ref, q4_ref, k4_ref, v4_ref, q16_ref, k16_ref, v16_ref, o_ref,
                    qt, vt, op, lp, on0, on1, on2, ln0, ln1, ln2):
    c2 = _row_iota((2 * QBLK, 2 * QBLK))
    a2 = _lane_iota((2 * QBLK, 2 * QBLK)) % QBLK
    band = jnp.where((c2 >= a2) & (c2 <= a2 + QBLK), 0.0, NEG)
    band_noprev = jnp.where(c2 < QBLK, NEG, band)
    c1 = _row_iota((QBLK, 2 * QBLK))
    a1 = _lane_iota((QBLK, 2 * QBLK)) % QBLK
    band_own = jnp.where(c1 <= a1, 0.0, NEG)
    top_sq = _row_iota((LANES, QBLK)) < HEAD_DIM

    for j in range(N_QBLK):
        vt[j, LANES:, :] = jnp.ones((ONES_ROWS, QBLK), BF16)

    inputs = ((q1_ref, k1_ref, v1_ref), (q4_ref, k4_ref, v4_ref), (q16_ref, k16_ref, v16_ref))
    for pi, (window, dil) in enumerate(DIL_PATTERNS):
        seg_len = SEQ // dil
        nseg_blk = seg_len // QBLK
        q_ref, k_ref, v_ref = inputs[pi]

        def block(ref, j, nseg_blk=nseg_blk, dil=dil):
            if dil == 1:
                return ref[j * QBLK:(j + 1) * QBLK, :]
            return ref[j // nseg_blk, (j % nseg_blk) * QBLK:(j % nseg_blk + 1) * QBLK, :]

        for j in range(N_QBLK):
            qt[j] = block(q_ref, j).T
            vt[j, 0:LANES, :] = block(v_ref, j).T

        prevs, scores = [], []
        for j in range(N_QBLK):
            qtb = qt[j]
            zero = jnp.zeros_like(qtb)
            q_both = jnp.concatenate([jnp.where(top_sq, qtb, zero), jnp.where(top_sq, zero, qtb)], axis=1)
            if nseg_blk == 1:
                keys, bias, jp = block(k_ref, j), band_own, j
            else:
                first = j % nseg_blk == 0
                jp = j if first else j - 1
                bias = band_noprev if first else band
                keys = jnp.concatenate([block(k_ref, jp), block(k_ref, j)], axis=0)
            prevs.append(jp)
            scores.append(jnp.dot(keys, q_both, preferred_element_type=F32) + bias)
        stats = []
        for s in scores:
            m = jnp.max(s, axis=0, keepdims=True)
            stats.append((m, jnp.exp2((s - m).astype(BF16))))
        outs = []
        for j, (jp, (m, p)) in enumerate(zip(prevs, stats)):
            v_span = vt[j] if nseg_blk == 1 else jnp.concatenate([vt[jp], vt[j]], axis=1)
            outs.append(jnp.dot(v_span, p, preferred_element_type=F32))
        for j, ((m, p), res) in enumerate(zip(stats, outs)):
            l = res[LANES:LANES + 1, :]
            lse = m + jnp.log2(l)
            o_t = jnp.where(top_sq, res[0:LANES, 0:QBLK] / l[:, 0:QBLK], res[0:LANES, QBLK:] / l[:, QBLK:])
            lse_t = jnp.where(top_sq, jnp.broadcast_to(lse[:, 0:QBLK], (LANES, QBLK)),
                              jnp.broadcast_to(lse[:, QBLK:], (LANES, QBLK)))
            op[j * QBLK:(j + 1) * QBLK, :] = o_t.T
            lp[j * QBLK:(j + 1) * QBLK, :] = lse_t.T

        for r in range(dil):
            dst = pl.ds(r, seg_len, stride=dil) if dil > 1 else pl.ds(0, SEQ)
            src = pl.ds(r * seg_len, seg_len)
            (on0, on1, on2)[pi][dst, :] = op[src, :]
            (ln0, ln1, ln2)[pi][dst, :] = lp[src, :]

    l0, l1, l2 = ln0[...], ln1[...], ln2[...]
    mx = jnp.maximum(jnp.maximum(l0, l1), l2)
    e0, e1, e2 = jnp.exp2(l0 - mx), jnp.exp2(l1 - mx), jnp.exp2(l2 - mx)
    den = e0 + e1 + e2
    out = (e0 / den) * on0[...] + (e1 / den) * on1[...] + (e2 / den) * on2[...]
    o_ref[...] = out.astype(o_ref.dtype)


def _dilated(a1, a4, a16):
    t = a1[0].shape[0]
    nb = t // SEQ
    spec1 = pl.BlockSpec((SEQ, LANES), lambda b, hp: (b, hp))
    spec4 = pl.BlockSpec((None, 4, SEQ // 4, LANES), lambda b, hp: (b, 0, 0, hp))
    spec16 = pl.BlockSpec((None, 16, SEQ // 16, LANES), lambda b, hp: (b, 0, 0, hp))
    return pl.pallas_call(
        _dilated_kernel,
        grid=(nb, N_HEADS_DIL // 2),
        in_specs=[spec1] * 3 + [spec4] * 3 + [spec16] * 3,
        out_specs=spec1,
        out_shape=jax.ShapeDtypeStruct((t, A_COLS), BF16),
        scratch_shapes=[pltpu.VMEM((N_QBLK, LANES, QBLK), BF16), pltpu.VMEM((N_QBLK, LANES + ONES_ROWS, QBLK), BF16)]
        + [pltpu.VMEM((SEQ, LANES), F32)] * 8,
        compiler_params=pltpu.CompilerParams(dimension_semantics=("arbitrary", "arbitrary"),
                                             vmem_limit_bytes=VMEM_LIMIT),
        name="dilated",
    )(*a1, *a4, *a16)


def _gelu_tanh(x):
    return 0.5 * x * (1.0 + jnp.tanh(np.sqrt(2.0 / np.pi).astype(np.float32) * (x + 0.044715 * (x * x * x))))


def _softmax_step(state, s, v_t):
    m, l, acc = state
    mn = jnp.maximum(m, jnp.max(s, axis=0, keepdims=True))
    al = jnp.exp2(m - mn)
    p = jnp.exp2(s - mn)
    l = al * l + jnp.sum(p, axis=0, keepdims=True)
    acc = al * acc + jnp.dot(v_t, p.astype(BF16), preferred_element_type=F32)
    return mn, l, acc


def _nsa_kernel(q_ref, kflat_ref, vflat_ref, ksw_ref, gate_ref,
                wk1_ref, wk2d_ref, pek_ref, wv1_ref, wv2t_ref, pev_ref,
                cc_ref, cs1_ref, cs2_ref, ovl_ref, exp_ref,
                o_ref,
                qt, vst, vwt, gtt, kc2, vct):
    g = pl.program_id(1)
    nq = NSA_REP * QBLK

    for i in range(N_QBLK):
        rows = pl.ds(i * QBLK, QBLK)
        qt[i, 0:LANES, :] = q_ref[rows, 0:LANES].T
        qt[i, LANES:2 * LANES, :] = q_ref[rows, LANES:2 * LANES].T
        vs_t = ksw_ref[rows, LANES:2 * LANES].T
        vst[i] = jnp.where(g == 0, vs_t[0:HEAD_DIM], vs_t[HEAD_DIM:])
        vw_t = ksw_ref[rows, 3 * LANES:4 * LANES].T
        vwt[i] = jnp.where(g == 0, vw_t[0:HEAD_DIM], vw_t[HEAD_DIM:])
        gt = gate_ref[rows, :]
        gt = jnp.where(g == 0, gt, pltpu.roll(gt, LANES - 3 * NSA_REP, axis=1))
        gtt[i] = gt.T[0:GATE_ROWS, :]

    def hidden(flat_ref, w1_ref, pe_ref):
        ab = jnp.dot(flat_ref[...], w1_ref[...], preferred_element_type=F32)
        pb = (jnp.dot(pe_ref[0:PE_ROWS, :], w1_ref[:, 0:CMP_HIDDEN], preferred_element_type=F32)
              + jnp.dot(pe_ref[PE_ROWS:, :], w1_ref[:, CMP_HIDDEN:], preferred_element_type=F32))
        hid = ab[:, 0:CMP_HIDDEN] + pltpu.roll(ab[:, CMP_HIDDEN:], N_CMP - 1, axis=0) + pb[0:1, :]
        return _gelu_tanh(hid).astype(BF16)

    hk = hidden(kflat_ref, wk1_ref, pek_ref)
    kc2[...] = _rope_lanes(jnp.dot(hk, wk2d_ref[...], preferred_element_type=F32),
                           cc_ref[...], cs1_ref[...], cs2_ref[...]).astype(BF16)
    hv = hidden(vflat_ref, wv1_ref, pev_ref)
    vct[...] = lax.dot_general(wv2t_ref[...], hv, _NT, preferred_element_type=F32).astype(BF16)

    key = _row_iota((QBLK, nq))
    qry = _lane_iota((QBLK, nq)) % QBLK
    causal = jnp.where(key <= qry, 0.0, NEG)
    upper = jnp.where(key > qry, 0.0, NEG)
    blk = _row_iota((N_SELBLK, QBLK))
    qry1 = _lane_iota((N_SELBLK, QBLK))
    grp_keys = SEL_GROUP * QBLK
    win_blocks = WIN // QBLK

    def qgroup(w, full_window):
        blocks = [SEL_GROUP * w + d for d in range(SEL_GROUP)]
        row0s = [pl.multiple_of(i * QBLK, QBLK) for i in blocks]

        q_pads = []
        for i in blocks:
            q_t = jnp.concatenate([qt[i, r * HEAD_DIM:(r + 1) * HEAD_DIM, :] for r in range(NSA_REP)], axis=1)
            zero = jnp.zeros_like(q_t)
            q_pads.append(jnp.where(g == 0, jnp.concatenate([q_t, zero], axis=0),
                                    jnp.concatenate([zero, q_t], axis=0)))
        s_cmp = [jnp.dot(kc2[...], qp, preferred_element_type=F32) for qp in q_pads]
        s_win, v_win = [], []
        for d, (i, qp) in enumerate(zip(blocks, q_pads)):
            if full_window:
                first = i - win_blocks
                keys = ksw_ref[pl.ds(pl.multiple_of(first * QBLK, QBLK), (win_blocks + 1) * QBLK), 2 * LANES:3 * LANES]
                s = jnp.dot(keys, qp, preferred_element_type=F32)
                s = jnp.concatenate([s[0:QBLK] + upper, s[QBLK:win_blocks * QBLK], s[win_blocks * QBLK:] + causal],
                                    axis=0)
                v_t = jnp.concatenate([vwt[first + e] for e in range(win_blocks + 1)], axis=1)
            else:
                s = jnp.dot(ksw_ref[0:(d + 1) * QBLK, 2 * LANES:3 * LANES], qp, preferred_element_type=F32)
                s = jnp.concatenate([s[0:d * QBLK], s[d * QBLK:] + causal], axis=0) if d else s + causal
                v_t = jnp.concatenate([vwt[e] for e in range(d + 1)], axis=1) if d else vwt[0]
            s_win.append(s)
            v_win.append(v_t)

        pcs = []
        for row0, s in zip(row0s, s_cmp):
            cvalid = (CMP_STRIDE * key + (CMP_LEN - 1) <= row0 + qry) & (key < N_CMP - 1)
            s = jnp.where(cvalid, s, NEG)
            m = jnp.max(s, axis=0, keepdims=True)
            p = jnp.where(cvalid, jnp.exp2(s - m), 0.0)
            l = jnp.sum(p, axis=0, keepdims=True)
            pcs.append(p / jnp.where(l > 0, l, 1.0))
        o_cmp = [jnp.dot(vct[...], pc.astype(BF16), preferred_element_type=F32) for pc in pcs]

        ovl = ovl_ref[...]
        imps = []
        for pc in pcs:
            psum = pc[:, 0:QBLK] + pc[:, QBLK:2 * QBLK] + pc[:, 2 * QBLK:3 * QBLK] + pc[:, 3 * QBLK:]
            p_hi = psum.astype(BF16)
            p_lo = (psum - p_hi.astype(F32)).astype(BF16)
            imps.append(jnp.dot(ovl, p_hi, preferred_element_type=F32) + jnp.dot(ovl, p_lo, preferred_element_type=F32))

        o_win = []
        for s, v_t in zip(s_win, v_win):
            m = jnp.max(s, axis=0, keepdims=True)
            p = jnp.exp2(s - m)
            l = jnp.sum(p, axis=0, keepdims=True)
            o_win.append(jnp.dot(v_t, p.astype(BF16), preferred_element_type=F32) / l)

        q_sel = []
        for row0, imp, qp in zip(row0s, imps, q_pads):
            t1 = row0 + qry1
            blk_t = t1 // SEL_LEN
            forced = (blk == 0) | (blk == blk_t) | (blk == blk_t - 1)
            v = jnp.where(blk * SEL_LEN <= t1, jnp.where(forced, BIG, imp), -BIG)
            selm = jnp.zeros((N_SELBLK, QBLK), F32)
            for _ in range(N_SEL):
                mx = jnp.max(v, axis=0, keepdims=True)
                first_hit = jnp.min(jnp.where(v == mx, blk, N_SELBLK), axis=0, keepdims=True)
                hit = blk == first_hit
                selm = jnp.where(hit, 1.0, selm)
                v = jnp.where(hit, LOWEST, v)
            notsel = jnp.concatenate([selm - 1.0, jnp.zeros((LANES - N_SELBLK, QBLK), F32)], axis=0).astype(BF16)
            q_sel.append(jnp.concatenate([qp, jnp.concatenate([notsel] * NSA_REP, axis=1)], axis=0))

        def sel_keys(t, n_rows):
            rows_t = pl.ds(pl.multiple_of(t * grp_keys, grp_keys), n_rows)
            return jnp.concatenate([ksw_ref[rows_t, 0:LANES], exp_ref[t, 0:n_rows, :]], axis=1)

        def sel_values(t, n_blk):
            return jnp.concatenate([vst[t * SEL_GROUP + e] for e in range(n_blk)], axis=1) if n_blk > 1 else vst[t * SEL_GROUP]

        wide = SEL_GROUP * nq
        st = (jnp.full((1, wide), LOWEST, F32), jnp.zeros((1, wide), F32), jnp.zeros((HEAD_DIM, wide), F32))
        if full_window:
            q_all = jnp.concatenate(q_sel, axis=1)

            def sel_body(t, st):
                s = jnp.dot(sel_keys(t, grp_keys), q_all, preferred_element_type=F32)
                return _softmax_step(st, s, sel_values(t, SEL_GROUP))

            for t_ in range(w):
                st = sel_body(t_, st)
        o_sel = []
        for d, qs in enumerate(q_sel):
            s = jnp.dot(sel_keys(w, (d + 1) * QBLK), qs, preferred_element_type=F32)
            s = jnp.concatenate([s[0:d * QBLK], s[d * QBLK:] + causal], axis=0) if d else s + causal
            lanes = slice(d * nq, (d + 1) * nq)
            m, l, acc = _softmax_step((st[0][:, lanes], st[1][:, lanes], st[2][:, lanes]), s, sel_values(w, d + 1))
            o_sel.append(acc / l)

        for n, (i, row0) in enumerate(zip(blocks, row0s)):
            gti = gtt[i]

            def gate(br):
                return jnp.concatenate([gti[r * 3 + br:r * 3 + br + 1, :] for r in range(NSA_REP)], axis=1)

            out_t = gate(0) * o_cmp[n] + gate(1) * o_sel[n] + gate(2) * o_win[n]
            for pair in range(NSA_REP // 2):
                both = jnp.concatenate([out_t[:, (2 * pair) * QBLK:(2 * pair + 1) * QBLK],
                                        out_t[:, (2 * pair + 1) * QBLK:(2 * pair + 2) * QBLK]], axis=0)
                o_ref[pl.ds(row0, QBLK), pair * LANES:(pair + 1) * LANES] = both.T.astype(o_ref.dtype)

    qgroup(0, False)
    for w in range(1, N_QBLK // SEL_GROUP):
        qgroup(w, True)


def _nsa(bq, kvflat, ksw, gates, wk1, wk2d, pek, wv1, wv2t, pev, cc, cs1, cs2, ovl_t, expand_t):
    t = bq.shape[0]
    nb = t // SEQ
    const2 = lambda b, g: (0, 0)
    return pl.pallas_call(
        _nsa_kernel,
        grid=(nb, N_KV_NSA),
        in_specs=[
            pl.BlockSpec((SEQ, 2 * LANES), lambda b, g: (b, g)),
            pl.BlockSpec((None, None, N_CMP, CMP_STRIDE * HEAD_DIM), lambda b, g: (b, g, 0, 0)),
            pl.BlockSpec((None, None, N_CMP, CMP_STRIDE * HEAD_DIM), lambda b, g: (b, N_KV_NSA + g, 0, 0)),
            pl.BlockSpec((SEQ, 4 * LANES), lambda b, g: (b, 0)),
            pl.BlockSpec((SEQ, LANES), lambda b, g: (b, 0)),
            pl.BlockSpec((CMP_STRIDE * HEAD_DIM, 2 * CMP_HIDDEN), const2),
            pl.BlockSpec((CMP_HIDDEN, LANES), const2),
            pl.BlockSpec((2 * PE_ROWS, CMP_STRIDE * HEAD_DIM), const2),
            pl.BlockSpec((CMP_STRIDE * HEAD_DIM, 2 * CMP_HIDDEN), const2),
            pl.BlockSpec((HEAD_DIM, CMP_HIDDEN), const2),
            pl.BlockSpec((2 * PE_ROWS, CMP_STRIDE * HEAD_DIM), const2),
            pl.BlockSpec((N_CMP, LANES), const2),
            pl.BlockSpec((N_CMP, LANES), const2),
            pl.BlockSpec((N_CMP, LANES), const2),
            pl.BlockSpec((N_SELBLK, LANES), const2),
            pl.BlockSpec((N_QBLK // SEL_GROUP, SEL_GROUP * QBLK, LANES), lambda b, g: (0, 0, 0)),
        ],
        out_specs=pl.BlockSpec((SEQ, 2 * LANES), lambda b, g: (b, g)),
        out_shape=jax.ShapeDtypeStruct((t, B_COLS), BF16),
        scratch_shapes=[pltpu.VMEM((N_QBLK, 2 * LANES, QBLK), BF16),
                        pltpu.VMEM((N_QBLK, HEAD_DIM, QBLK), BF16),
                        pltpu.VMEM((N_QBLK, HEAD_DIM, QBLK), BF16),
                        pltpu.VMEM((N_QBLK, GATE_ROWS, QBLK), F32),
                        pltpu.VMEM((N_CMP, LANES), BF16),
                        pltpu.VMEM((HEAD_DIM, N_CMP), BF16)],
        compiler_params=pltpu.CompilerParams(dimension_semantics=("arbitrary", "arbitrary"),
                                             vmem_limit_bytes=VMEM_LIMIT),
        name="nsa",
    )(bq, kvflat, kvflat, ksw, gates, wk1, wk2d, pek, wv1, wv2t, pev, cc, cs1, cs2, ovl_t, expand_t)


def _pack_bf16_pairs(lo, hi):
    lo_b = lax.bitcast_convert_type(lo.astype(BF16).astype(F32), jnp.uint32)
    hi_b = lax.bitcast_convert_type(hi.astype(BF16).astype(F32), jnp.uint32)
    return (hi_b & jnp.uint32(0xFFFF0000)) | (lo_b >> 16)


def _unpack_bf16_pairs(p):
    lo = lax.bitcast_convert_type(p << 16, F32)
    hi = lax.bitcast_convert_type(p & jnp.uint32(0xFFFF0000), F32)
    return lo, hi


def _post_kernel(x_ref, ma_ref, mb_ref, wo_ref, g2_ref, wr_ref, br_ref, tri_ref,
                 x1_ref, ha_ref, hb_ref, e1_ref, e2_ref, r1_ref, r2_ref, w_ref, cnt_ref):
    sub_rows = MOE_TM // POST_SPLIT
    ranges = [slice(n * sub_rows, (n + 1) * sub_rows) for n in range(POST_SPLIT)]
    x1s = [x_ref[rows, :] + jnp.dot(ma_ref[rows, :], wo_ref[0:A_COLS, :], preferred_element_type=F32)
           + jnp.dot(mb_ref[rows, :], wo_ref[A_COLS:, :], preferred_element_type=F32) for rows in ranges]
    wr = wr_ref[...]
    w_hi = wr.astype(BF16)
    w_lo = (wr - w_hi.astype(F32)).astype(BF16)
    logits = []
    for rows, x1 in zip(ranges, x1s):
        x1_ref[rows, :] = x1
        ms = jnp.mean(x1 * x1, axis=-1, keepdims=True)
        h2 = x1 * lax.rsqrt(ms + EPS) * g2_ref[...]
        ha_ref[rows, :] = _pack_bf16_pairs(h2[:, 0:PACK_W], h2[:, 2 * PACK_W:3 * PACK_W])
        hb_ref[rows, :] = _pack_bf16_pairs(h2[:, PACK_W:2 * PACK_W], h2[:, 3 * PACK_W:])
        h_hi = h2.astype(BF16)
        h_lo = (h2 - h_hi.astype(F32)).astype(BF16)
        logits.append(lax.dot_general(w_hi, h_hi, _NT, preferred_element_type=F32)
                      + lax.dot_general(w_hi, h_lo, _NT, preferred_element_type=F32)
                      + lax.dot_general(w_lo, h_hi, _NT, preferred_element_type=F32))
    lg = jnp.concatenate(logits, axis=1) + br_ref[:, 0:1]
    gl = lg[0:N_GROUPS, :]
    sub = _row_iota(gl.shape)
    gmax = jnp.max(gl, axis=0, keepdims=True)
    p_g = 1.0 / jnp.sum(jnp.exp(gl - gmax), axis=0, keepdims=True)
    g_star = jnp.min(jnp.where(gl == gmax, sub, N_GROUPS), axis=0, keepdims=True)
    el = jnp.zeros_like(gl)
    for gi in range(N_GROUPS):
        lo = N_GROUPS + gi * EXPERTS_PER_GROUP
        el = el + jnp.where(g_star == gi, lg[lo:lo + EXPERTS_PER_GROUP, :], 0.0)
    emax = jnp.max(el, axis=0, keepdims=True)
    ee = jnp.exp(el - emax)
    pe = ee / jnp.sum(ee, axis=0, keepdims=True)
    v1 = jnp.max(pe, axis=0, keepdims=True)
    i1 = jnp.min(jnp.where(pe == v1, sub, EXPERTS_PER_GROUP), axis=0, keepdims=True)
    pe2 = jnp.where(sub == i1, -1.0, pe)
    v2 = jnp.max(pe2, axis=0, keepdims=True)
    i2 = jnp.min(jnp.where(pe2 == v2, sub, EXPERTS_PER_GROUP), axis=0, keepdims=True)
    tot = v1 + v2
    e1 = g_star * EXPERTS_PER_GROUP + i1
    e2 = g_star * EXPERTS_PER_GROUP + i2
    e1_ref[0] = e1
    e2_ref[0] = e2

    wslab = jnp.concatenate([v1 / tot * p_g, v2 / tot * p_g, jnp.zeros((LANES - 2, MOE_TM), F32)], axis=0)
    for j in range(MOE_TM // LANES):
        w_ref[j * LANES:(j + 1) * LANES, :] = wslab[:, j * LANES:(j + 1) * LANES].T

    n_exp = N_EXPERTS
    sub_e = _row_iota((n_exp, MOE_TM))
    hit1 = sub_e == e1
    hit2 = sub_e == e2
    assigned = jnp.where(hit1, 1.0, 0.0) + jnp.where(hit2, 1.0, 0.0)
    before = jnp.dot(assigned.astype(BF16), tri_ref[...], preferred_element_type=F32)
    r1_ref[0] = jnp.sum(jnp.where(hit1, before, 0.0), axis=0, keepdims=True).astype(jnp.int32)
    r2_ref[0] = jnp.sum(jnp.where(hit2, before, 0.0), axis=0, keepdims=True).astype(jnp.int32)
    cnt = jnp.sum(assigned, axis=1, keepdims=True).astype(jnp.int32)
    cnt_ref[0] = jnp.broadcast_to(cnt, (n_exp, LANES))


def _post(x2, mix_a, mix_b, w_o, g2, wr_t, br, tri, part):
    t = x2.shape[0] // MOE_PARTS
    nt = t // MOE_TM
    first = part * nt
    n_exp = N_EXPERTS
    row = lambda i: (i, 0)
    const = lambda i: (0, 0)
    tok = lambda i: (i, 0, 0)
    src = lambda i: (first + i, 0)
    tok_spec = pl.BlockSpec((1, 1, MOE_TM), tok)
    tok_shape = jax.ShapeDtypeStruct((nt, 1, MOE_TM), jnp.int32)
    return pl.pallas_call(
        _post_kernel,
        grid=(nt,),
        in_specs=[pl.BlockSpec((MOE_TM, D_MODEL), src), pl.BlockSpec((MOE_TM, A_COLS), src),
                  pl.BlockSpec((MOE_TM, B_COLS), src), pl.BlockSpec((D_MODEL, D_MODEL), const),
                  pl.BlockSpec((1, D_MODEL), const), pl.BlockSpec((ROUTER_ROWS, D_MODEL), const),
                  pl.BlockSpec((ROUTER_ROWS, LANES), const), pl.BlockSpec((MOE_TM, MOE_TM), const)],
        out_specs=[pl.BlockSpec((MOE_TM, D_MODEL), row),
                   pl.BlockSpec((MOE_TM, PACK_W), row), pl.BlockSpec((MOE_TM, PACK_W), row),
                   tok_spec, tok_spec, tok_spec, tok_spec,
                   pl.BlockSpec((MOE_TM, LANES), row),
                   pl.BlockSpec((1, n_exp, LANES), tok)],
        out_shape=[jax.ShapeDtypeStruct((t, D_MODEL), F32),
                   jax.ShapeDtypeStruct((t, PACK_W), jnp.uint32), jax.ShapeDtypeStruct((t, PACK_W), jnp.uint32),
                   tok_shape, tok_shape, tok_shape, tok_shape,
                   jax.ShapeDtypeStruct((t, LANES), F32),
                   jax.ShapeDtypeStruct((nt, n_exp, LANES), jnp.int32)],
        compiler_params=pltpu.CompilerParams(dimension_semantics=("arbitrary",),
                                             vmem_limit_bytes=VMEM_LIMIT),
        name="post",
    )(x2, mix_a, mix_b, w_o, g2, wr_t, br, tri)


def _sc_mesh():
    return plsc.VectorSubcoreMesh(core_axis_name="core", subcore_axis_name="subcore")


def _sc_scatter_rows(x, pos1, pos2, n_out):
    r, c = x.shape

    @pl.kernel(out_type=jax.ShapeDtypeStruct((n_out, c), x.dtype), mesh=_sc_mesh(), scratch_types=[])
    def scatter_kernel(x_hbm, p1_hbm, p2_hbm, o_hbm):
        def body(x_vmem, p1_vmem, p2_vmem):
            pltpu.sync_copy(x_vmem, o_hbm.at[p1_vmem.at[0]])
            pltpu.sync_copy(x_vmem, o_hbm.at[p2_vmem.at[0]])

        idx = pl.BlockSpec((1, SC_WINDOW), lambda i: (0, i))
        pltpu.emit_pipeline(body, grid=(r // SC_WINDOW,),
                            in_specs=[pl.BlockSpec((SC_WINDOW, c), lambda i: (i, 0)), idx, idx], out_specs=[],
                            core_axis_name=("core", "subcore"),
                            dimension_semantics=(pltpu.PARALLEL,))(x_hbm, p1_hbm, p2_hbm)

    return scatter_kernel(x, pos1.reshape(1, r), pos2.reshape(1, r))


def _sc_gather_rows(y, idx):
    n = idx.shape[0]
    c = y.shape[1]

    @pl.kernel(out_type=jax.ShapeDtypeStruct((n, c), y.dtype), mesh=_sc_mesh(), scratch_types=[])
    def gather_kernel(y_hbm, i_hbm, o_hbm):
        def body(i_vmem, o_vmem):
            pltpu.sync_copy(y_hbm.at[i_vmem.at[0]], o_vmem)

        pltpu.emit_pipeline(body, grid=(n // SC_WINDOW,),
                            in_specs=[pl.BlockSpec((1, SC_WINDOW), lambda i: (0, i))],
                            out_specs=[pl.BlockSpec((SC_WINDOW, c), lambda i: (i, 0))],
                            core_axis_name=("core", "subcore"),
                            dimension_semantics=(pltpu.PARALLEL,))(i_hbm, o_hbm)

    return gather_kernel(y, idx.reshape(1, n))


def _ffn_kernel(ce_ref, cv_ref, xa_ref, xb_ref, wg_ref, wu_ref, wd_ref, ya_ref, yb_ref, wg_bf, wu_bf, wd_bf):
    c = pl.program_id(0)
    prev = ce_ref[jnp.maximum(c - 1, 0)]

    @pl.when((c == 0) | (ce_ref[c] != prev))
    def _():
        wg_bf[...] = wg_ref[0].astype(BF16)
        wu_bf[...] = wu_ref[0].astype(BF16)
        wd_bf[...] = wd_ref[0].astype(BF16)

    @pl.when(cv_ref[c] == 0)
    def _():
        ya_ref[...] = jnp.zeros_like(ya_ref)
        yb_ref[...] = jnp.zeros_like(yb_ref)

    @pl.when(cv_ref[c] != 0)
    def _():
        a_lo, a_hi = _unpack_bf16_pairs(xa_ref[...])
        b_lo, b_hi = _unpack_bf16_pairs(xb_ref[...])
        xs = jnp.concatenate([a_lo, b_lo, a_hi, b_hi], axis=1).astype(BF16)
        gate = jnp.dot(xs, wg_bf[...], preferred_element_type=F32)
        up = jnp.dot(xs, wu_bf[...], preferred_element_type=F32)
        he = (gate * jax.nn.sigmoid(gate) * up).astype(BF16)
        y = jnp.dot(he, wd_bf[...], preferred_element_type=F32)
        ya_ref[...] = _pack_bf16_pairs(y[:, 0:PACK_W], y[:, 2 * PACK_W:3 * PACK_W])
        yb_ref[...] = _pack_bf16_pairs(y[:, PACK_W:2 * PACK_W], y[:, 3 * PACK_W:])


def _ffn(chunk_expert, chunk_valid, xsa, xsb, wg, wu, wd):
    n_pad = xsa.shape[0]
    rows = pl.BlockSpec((FFN_ROWS, PACK_W), lambda c, ce, cv: (c, 0))
    grid_spec = pltpu.PrefetchScalarGridSpec(
        num_scalar_prefetch=2,
        grid=(n_pad // FFN_ROWS,),
        in_specs=[rows, rows,
                  pl.BlockSpec((1, D_MODEL, D_FF_EXPERT), lambda c, ce, cv: (ce[c], 0, 0)),
                  pl.BlockSpec((1, D_MODEL, D_FF_EXPERT), lambda c, ce, cv: (ce[c], 0, 0)),
                  pl.BlockSpec((1, D_FF_EXPERT, D_MODEL), lambda c, ce, cv: (ce[c], 0, 0))],
        out_specs=[rows, rows],
        scratch_shapes=[pltpu.VMEM((D_MODEL, D_FF_EXPERT), BF16), pltpu.VMEM((D_MODEL, D_FF_EXPERT), BF16),
                        pltpu.VMEM((D_FF_EXPERT, D_MODEL), BF16)],
    )
    out = jax.ShapeDtypeStruct((n_pad, PACK_W), jnp.uint32)
    return pl.pallas_call(
        _ffn_kernel,
        grid_spec=grid_spec,
        out_shape=[out, out],
        compiler_params=pltpu.CompilerParams(dimension_semantics=("arbitrary",),
                                             vmem_limit_bytes=VMEM_LIMIT),
        name="ffn",
    )(chunk_expert, chunk_valid, xsa, xsb, wg, wu, wd)


def _final_kernel(x1_ref, y1a_ref, y1b_ref, y2a_ref, y2b_ref, w_ref, gf_ref, o_ref):
    w1 = w_ref[:, 0:1]
    w2 = w_ref[:, 1:2]
    a1_lo, a1_hi = _unpack_bf16_pairs(y1a_ref[...])
    b1_lo, b1_hi = _unpack_bf16_pairs(y1b_ref[...])
    a2_lo, a2_hi = _unpack_bf16_pairs(y2a_ref[...])
    b2_lo, b2_hi = _unpack_bf16_pairs(y2b_ref[...])
    y = jnp.concatenate([w1 * a1_lo + w2 * a2_lo, w1 * b1_lo + w2 * b2_lo,
                         w1 * a1_hi + w2 * a2_hi, w1 * b1_hi + w2 * b2_hi], axis=1)
    xo = x1_ref[...] + y
    ms = jnp.mean(xo * xo, axis=-1, keepdims=True)
    o_ref[...] = xo * lax.rsqrt(ms + EPS) * gf_ref[...]


def _final(x1, yga, ygb, wtok, gf, out_prev, part):
    t = x1.shape[0]
    first = part * (t // FINAL_TM)
    row = lambda i: (i, 0)
    one = pl.BlockSpec((None, FINAL_TM, PACK_W), lambda i: (0, i, 0))
    two = pl.BlockSpec((None, FINAL_TM, PACK_W), lambda i: (1, i, 0))
    in_specs = [pl.BlockSpec((FINAL_TM, D_MODEL), row), one, one, two, two,
                pl.BlockSpec((FINAL_TM, LANES), row), pl.BlockSpec((1, D_MODEL), lambda i: (0, 0))]
    args = [x1, yga, ygb, yga, ygb, wtok, gf]
    kern = _final_kernel
    aliases = {}
    if out_prev is not None:
        in_specs.append(pl.BlockSpec(memory_space=pl.ANY))
        args.append(out_prev)
        aliases = {len(args) - 1: 0}
        kern = lambda *refs: _final_kernel(*refs[:7], refs[8])
    return pl.pallas_call(
        kern,
        grid=(t // FINAL_TM,),
        in_specs=in_specs,
        out_specs=pl.BlockSpec((FINAL_TM, D_MODEL), lambda i: (first + i, 0)),
        out_shape=jax.ShapeDtypeStruct((t * MOE_PARTS, D_MODEL), F32),
        input_output_aliases=aliases,
        compiler_params=pltpu.CompilerParams(dimension_semantics=("arbitrary",),
                                             vmem_limit_bytes=VMEM_LIMIT),
        name="final",
    )(*args)


def _moe_plan(cnt, e1, e2, r1, r2):
    nt, n_exp = cnt.shape
    tot = jnp.sum(cnt, axis=0)
    padded = (tot + FFN_ROWS - 1) // FFN_ROWS * FFN_ROWS
    seg_end = jnp.cumsum(padded)
    seg_start = seg_end - padded
    off = seg_start[None, :] + jnp.cumsum(cnt, axis=0) - cnt
    experts = jnp.arange(n_exp, dtype=jnp.int32)

    def lookup(e):
        return jnp.sum(jnp.where(e[:, :, None] == experts, off[:, None, :], 0), axis=2)

    pos1 = (lookup(e1) + r1).reshape(-1).astype(jnp.int32)
    pos2 = (lookup(e2) + r2).reshape(-1).astype(jnp.int32)
    n_chunks = (nt * MOE_TM * 2 + n_exp * FFN_ROWS) // FFN_ROWS
    cstart = jnp.arange(n_chunks, dtype=jnp.int32) * FFN_ROWS
    ce = jnp.sum((seg_end[None, :] <= cstart[:, None]).astype(jnp.int32), axis=1)
    cec = jnp.minimum(ce, n_exp - 1)
    seg_used = seg_start + tot
    valid = jnp.any((ce[:, None] == experts[None, :]) & (cstart[:, None] < seg_used[None, :]), axis=1)
    return pos1, pos2, cec.astype(jnp.int32), valid.astype(jnp.int32)


def _rope_tables(pos):
    half = ROT_DIM // 2
    inv_freq = ROPE_THETA ** (-jnp.arange(0, ROT_DIM, 2, dtype=F32) / ROT_DIM)
    ang = pos.astype(F32)[:, None] * inv_freq[None, :]
    cos, sin = jnp.cos(ang), jnp.sin(ang)
    l64 = np.arange(LANES) % HEAD_DIM
    f = l64 % half
    first = jnp.asarray(l64 < half)[None, :]
    second = jnp.asarray((l64 >= half) & (l64 < ROT_DIM))[None, :]
    c = jnp.where(first | second, cos[:, f], 1.0)
    s1 = jnp.where(first, -sin[:, f], 0.0)
    s2 = jnp.where(second, sin[:, f], 0.0)
    return c.astype(F32), s1.astype(F32), s2.astype(F32)


def _overlap_table_t():
    cs = np.arange(N_CMP)[None, :] * CMP_STRIDE
    js = np.arange(N_SELBLK)[:, None] * SEL_LEN
    ov = np.clip(np.minimum(cs + CMP_LEN, js + SEL_LEN) - np.maximum(cs, js), 0, None) / CMP_LEN
    ov[:, N_CMP - 1] = 0.0
    return ov.astype(np.float32)


def _expand_table_t():
    out = np.zeros((SEQ, LANES), np.float32)
    out[np.arange(SEQ), np.arange(SEQ) // SEL_LEN] = -NEG
    return out.reshape(N_QBLK // SEL_GROUP, SEL_GROUP * QBLK, LANES)


def kernel(x, norm1_g, w_in, pe_kc, w_kc1, w_kc2, pe_vc, w_vc1, w_vc2, w_o, norm2_g, w_rg, b_rg, w_re, b_re,
           w_gate, w_up, w_down, norm_f_g):
    nb, s, d = x.shape
    assert (s, d) == (SEQ, D_MODEL) and norm1_g.shape[0] == 1
    t = nb * s
    x2 = x.reshape(t, d)

    w_in_p = jnp.pad(w_in[0], ((0, 0), (0, D_IN_PAD - w_in.shape[2]))).astype(BF16)
    rc, rs1, rs2 = _rope_tables(jnp.arange(SEQ))
    cc, cs1, cs2 = _rope_tables(jnp.arange(N_CMP) * CMP_STRIDE + CMP_LEN - 1)
    half_flat = CMP_STRIDE * HEAD_DIM

    def cmp_w1(w1):
        return jnp.concatenate([w1[:half_flat], w1[half_flat:]], axis=1).astype(BF16)

    def cmp_pe(pe):
        rows = pe.reshape(2, half_flat)
        return jnp.concatenate([jnp.broadcast_to(rows[0:1], (PE_ROWS, half_flat)),
                                jnp.broadcast_to(rows[1:2], (PE_ROWS, half_flat))], axis=0).astype(BF16)

    wk2d = jnp.concatenate([w_kc2[0], w_kc2[0]], axis=1).astype(BF16)
    wv2t = w_vc2[0].T.astype(BF16)
    ovl_t = jnp.asarray(_overlap_table_t(), BF16)
    expand_t = jnp.asarray(_expand_table_t(), BF16)

    (aq, ak, av, aq4, ak4, av4, aq16, ak16, av16,
     bq, kvflat, ksw, gates) = _proj(x2, norm1_g, w_in_p, rc, rs1, rs2)
    mix_a = _dilated((aq, ak, av), (aq4, ak4, av4), (aq16, ak16, av16))

    mix_b = _nsa(bq, kvflat, ksw, gates, cmp_w1(w_kc1[0]), wk2d, cmp_pe(pe_kc[0]),
                 cmp_w1(w_vc1[0]), wv2t, cmp_pe(pe_vc[0]), cc, cs1, cs2, ovl_t, expand_t)

    wr_t = jnp.pad(jnp.concatenate([w_rg[0].T, w_re[0].reshape(d, -1).T], axis=0),
                   ((0, ROUTER_ROWS - N_ROUTER), (0, 0)))
    br = jnp.broadcast_to(jnp.pad(jnp.concatenate([b_rg[0], b_re[0].reshape(-1)]),
                                  (0, ROUTER_ROWS - N_ROUTER))[:, None], (ROUTER_ROWS, LANES))
    tri = jnp.asarray(np.triu(np.ones((MOE_TM, MOE_TM), np.float32), 1), BF16)
    tp = t // MOE_PARTS
    n_pad = 2 * tp + N_GROUPS * EXPERTS_PER_GROUP * FFN_ROWS
    w_o_bf = w_o[0].astype(BF16)
    gf = norm_f_g.reshape(1, d)
    routed = []
    for part in range(MOE_PARTS):
        x1, hpa, hpb, e1, e2, r1, r2, wtok, cnt = _post(x2, mix_a, mix_b, w_o_bf, norm2_g, wr_t, br, tri, part)
        pos1, pos2, chunk_expert, chunk_valid = _moe_plan(cnt[:, :, 0], e1[:, 0], e2[:, 0], r1[:, 0], r2[:, 0])
        xsa = _sc_scatter_rows(hpa, pos1, pos2, n_pad)
        xsb = _sc_scatter_rows(hpb, pos1, pos2, n_pad)
        routed.append((x1, wtok, pos1, pos2, chunk_expert, chunk_valid, xsa, xsb))
    gathered = []
    for x1, wtok, pos1, pos2, chunk_expert, chunk_valid, xsa, xsb in routed:
        ysa, ysb = _ffn(chunk_expert, chunk_valid, xsa, xsb, w_gate[0], w_up[0], w_down[0])
        pos12 = jnp.concatenate([pos1, pos2])
        gathered.append((x1, wtok, _sc_gather_rows(ysa, pos12), _sc_gather_rows(ysb, pos12)))
    out = None
    for part, (x1, wtok, yga, ygb) in enumerate(gathered):
        out = _final(x1, yga.reshape(2, tp, PACK_W), ygb.reshape(2, tp, PACK_W), wtok, gf, out, part)
    return out.reshape(nb, s, d)
```

```python
import numpy as np
import jax
import jax.numpy as jnp
from jax import lax
from jax.experimental import pallas as pl
from jax.experimental.pallas import tpu as pltpu
from jax.experimental.pallas import tpu_sc as plsc

F32 = jnp.float32
BF16 = jnp.bfloat16

D_MODEL = 1024
SEQ = 2048
HEAD_DIM = 64
N_HEADS_DIL = 8
DIL_PATTERNS = ((128, 1), (512, 4), (2048, 16))
N_HEADS_NSA = 8
N_KV_NSA = 2
NSA_REP = N_HEADS_NSA // N_KV_NSA
CMP_STRIDE = 16
CMP_LEN = 32
CMP_HIDDEN = 256
SEL_LEN = 64
N_SEL = 8
WIN = 512
ROPE_THETA = 500000.0
ROT_DIM = HEAD_DIM // 4
N_GROUPS = 8
EXPERTS_PER_GROUP = 8
D_FF_EXPERT = 256
EPS = 1e-6
NEG = -1e30
BIG = 1e9
LOWEST = -3.0e38
LOG2_E = 1.4426950408889634

A_COLS = N_HEADS_DIL * HEAD_DIM
B_COLS = N_HEADS_NSA * HEAD_DIM
KV_COLS = N_KV_NSA * HEAD_DIM
N_EXPERTS = N_GROUPS * EXPERTS_PER_GROUP
N_ROUTER = N_GROUPS + N_EXPERTS

LANES = 128
QBLK = 128
N_QBLK = SEQ // QBLK
N_CMP = SEQ // CMP_STRIDE
N_SELBLK = SEQ // SEL_LEN
D_IN_PAD = 23 * LANES
PROJ_TM = 1024
SEL_GROUP = 4
MOE_TM = 1024
MOE_PARTS = 1
POST_SPLIT = 2
ROUTER_ROWS = 80
FFN_ROWS = 512
FINAL_TM = 1024
PACK_W = D_MODEL // 4
BF16_SUBLANES = 16
ONES_ROWS = BF16_SUBLANES
PE_ROWS = BF16_SUBLANES
GATE_ROWS = 16
SC_WINDOW = 128
VMEM_LIMIT = 56 * 1024 * 1024

_NT = (((1,), (1,)), ((), ()))


def _lane_iota(shape):
    return lax.broadcasted_iota(jnp.int32, shape, 1)


def _row_iota(shape):
    return lax.broadcasted_iota(jnp.int32, shape, 0)


def _rope_lanes(y, c, s1, s2):
    return y * c + pltpu.roll(y, LANES - ROT_DIM // 2, axis=1) * s1 + pltpu.roll(y, ROT_DIM // 2, axis=1) * s2


def _proj_kernel(x_ref, g_ref, w_ref, c_ref, s1_ref, s2_ref,
                 aq_ref, ak_ref, av_ref, aq4_ref, ak4_ref, av4_ref, aq16_ref, ak16_ref, av16_ref,
                 bq_ref, kvflat_ref, ksw_ref, gate_ref, nat_f32, d4_f32):
    x = x_ref[...]
    ms = jnp.mean(x * x, axis=-1, keepdims=True)
    h = (x * lax.rsqrt(ms + EPS) * g_ref[...]).astype(BF16)
    c = c_ref[...]
    s1 = s1_ref[...]
    s2 = s2_ref[...]

    def seg(lo, width):
        return jnp.dot(h, w_ref[:, lo:lo + width], preferred_element_type=F32)

    def store(dst, col, y, rope, scale, dilated=None):
        for j in range(y.shape[1] // LANES):
            yj = y[:, j * LANES:(j + 1) * LANES]
            if rope:
                yj = _rope_lanes(yj, c, s1, s2)
            if scale != 1.0:
                yj = yj * scale
            cols = slice(col + j * LANES, col + (j + 1) * LANES)
            dst[:, cols] = yj.astype(dst.dtype)
            if dilated is not None:
                d4_ref, d16_ref = dilated
                quarter = PROJ_TM // 4
                nat_f32[...] = yj
                for r in range(4):
                    part = nat_f32[pl.ds(r, quarter, stride=4), :]
                    d4_ref[r, :, cols] = part.astype(d4_ref.dtype)
                    d4_f32[r * quarter:(r + 1) * quarter, :] = part
                for r in range(4):
                    for a_ in range(4):
                        part = d4_f32[pl.ds(r * quarter + a_, quarter // 4, stride=4), :]
                        d16_ref[4 * a_ + r, :, cols] = part.astype(d16_ref.dtype)

    qscale = HEAD_DIM ** -0.5 * LOG2_E
    store(aq_ref, 0, seg(0, A_COLS), True, qscale, (aq4_ref, aq16_ref))
    store(ak_ref, 0, seg(A_COLS, A_COLS), True, 1.0, (ak4_ref, ak16_ref))
    store(av_ref, 0, seg(2 * A_COLS, A_COLS), False, 1.0, (av4_ref, av16_ref))
    nsa0 = 3 * A_COLS
    store(bq_ref, 0, seg(nsa0, B_COLS), True, qscale)
    y = seg(nsa0 + B_COLS, 2 * KV_COLS)
    quarter = PROJ_TM // 4
    left_half = _lane_iota((quarter // 4, LANES)) < HEAD_DIM
    for kind in range(2):
        nat_f32[...] = y[:, kind * KV_COLS:(kind + 1) * KV_COLS]
        for r in range(4):
            d4_f32[r * quarter:(r + 1) * quarter, :] = nat_f32[pl.ds(r, quarter, stride=4), :]

        def chunk_pos(i):
            return d4_f32[pl.ds((i % 4) * quarter + i // 4, quarter // 4, stride=4), :]

        for pair in range(CMP_STRIDE // 2):
            even, odd = chunk_pos(2 * pair), chunk_pos(2 * pair + 1)
            cols = slice(pair * LANES, (pair + 1) * LANES)
            kvflat_ref[2 * kind, :, cols] = jnp.where(left_half, even, pltpu.roll(odd, HEAD_DIM, axis=1)).astype(BF16)
            kvflat_ref[2 * kind + 1, :, cols] = jnp.where(left_half, pltpu.roll(even, HEAD_DIM, axis=1), odd).astype(BF16)
    kv0 = nsa0 + B_COLS + 2 * KV_COLS
    for n, rope in enumerate((True, False, True, False)):
        store(ksw_ref, n * KV_COLS, seg(kv0 + n * KV_COLS, KV_COLS), rope, 1.0)
    gate_ref[...] = jax.nn.sigmoid(seg(kv0 + 4 * KV_COLS, LANES))


def _proj(x2, g1, w_in_p, rc, rs1, rs2):
    t = x2.shape[0]
    nb = t // SEQ
    nblk_s = SEQ // PROJ_TM
    row = lambda i: (i, 0)
    pos = lambda i: (i % nblk_s, 0)
    const = lambda i: (0, 0)
    perm = lambda i: (i // nblk_s, 0, i % nblk_s, 0)
    wide = jax.ShapeDtypeStruct((t, A_COLS), BF16)
    wide_spec = pl.BlockSpec((PROJ_TM, A_COLS), row)
    d4 = jax.ShapeDtypeStruct((nb, 4, SEQ // 4, A_COLS), BF16)
    d4_spec = pl.BlockSpec((None, 4, PROJ_TM // 4, A_COLS), perm)
    d16 = jax.ShapeDtypeStruct((nb, 16, SEQ // 16, A_COLS), BF16)
    d16_spec = pl.BlockSpec((None, 16, PROJ_TM // 16, A_COLS), perm)
    flat = jax.ShapeDtypeStruct((nb, 2 * N_KV_NSA, N_CMP, CMP_STRIDE * HEAD_DIM), BF16)
    flat_spec = pl.BlockSpec((None, 2 * N_KV_NSA, PROJ_TM // CMP_STRIDE, CMP_STRIDE * HEAD_DIM), perm)
    outs = [wide] * 3 + [d4] * 3 + [d16] * 3 + [wide, flat, wide,
                                               jax.ShapeDtypeStruct((t, LANES), F32)]
    out_specs = [wide_spec] * 3 + [d4_spec] * 3 + [d16_spec] * 3 + [
        wide_spec, flat_spec, wide_spec, pl.BlockSpec((PROJ_TM, LANES), row)]
    return pl.pallas_call(
        _proj_kernel,
        grid=(t // PROJ_TM,),
        in_specs=[pl.BlockSpec((PROJ_TM, D_MODEL), row), pl.BlockSpec((1, D_MODEL), const),
                  pl.BlockSpec((D_MODEL, D_IN_PAD), const),
                  pl.BlockSpec((PROJ_TM, LANES), pos), pl.BlockSpec((PROJ_TM, LANES), pos),
                  pl.BlockSpec((PROJ_TM, LANES), pos)],
        out_specs=out_specs,
        out_shape=outs,
        scratch_shapes=[pltpu.VMEM((PROJ_TM, LANES), F32)] * 2,
        compiler_params=pltpu.CompilerParams(dimension_semantics=("arbitrary",),
                                             vmem_limit_bytes=VMEM_LIMIT),
        name="proj",
    )(x2, g1, w_in_p, rc, rs1, rs2)


def _dilated_kernel(q1_ref, k1_ref, v1_ref, q4_ref, k4_ref, v4_ref, q16_ref, k16_ref, v16_ref, o_ref,
                    qt, vt, op, lp, on0, on1, on2, ln0, ln1, ln2):
    c2 = _row_iota((2 * QBLK, 2 * QBLK))
    a2 = _lane_iota((2 * QBLK, 2 * QBLK)) % QBLK
    band = jnp.where((c2 >= a2) & (c2 <= a2 + QBLK), 0.0, NEG)
    band_noprev = jnp.where(c2 < QBLK, NEG, band)
    c1 = _row_iota((QBLK, 2 * QBLK))
    a1 = _lane_iota((QBLK, 2 * QBLK)) % QBLK
    band_own = jnp.where(c1 <= a1, 0.0, NEG)
    top_sq = _row_iota((LANES, QBLK)) < HEAD_DIM

    for j in range(N_QBLK):
        vt[j, LANES:, :] = jnp.ones((ONES_ROWS, QBLK), BF16)

    inputs = ((q1_ref, k1_ref, v1_ref), (q4_ref, k4_ref, v4_ref), (q16_ref, k16_ref, v16_ref))
    for pi, (window, dil) in enumerate(DIL_PATTERNS):
        seg_len = SEQ // dil
        nseg_blk = seg_len // QBLK
        q_ref, k_ref, v_ref = inputs[pi]

        def block(ref, j, nseg_blk=nseg_blk, dil=dil):
            if dil == 1:
                return ref[j * QBLK:(j + 1) * QBLK, :]
            return ref[j // nseg_blk, (j % nseg_blk) * QBLK:(j % nseg_blk + 1) * QBLK, :]

        for j in range(N_QBLK):
            qt[j] = block(q_ref, j).T
            vt[j, 0:LANES, :] = block(v_ref, j).T

        prevs, scores = [], []
        for j in range(N_QBLK):
            qtb = qt[j]
            zero = jnp.zeros_like(qtb)
            q_both = jnp.concatenate([jnp.where(top_sq, qtb, zero), jnp.where(top_sq, zero, qtb)], axis=1)
            if nseg_blk == 1:
                keys, bias, jp = block(k_ref, j), band_own, j
            else:
                first = j % nseg_blk == 0
                jp = j if first else j - 1
                bias = band_noprev if first else band
                keys = jnp.concatenate([block(k_ref, jp), block(k_ref, j)], axis=0)
            prevs.append(jp)
            scores.append(jnp.dot(keys, q_both, preferred_element_type=F32) + bias)
        stats = []
        for s in scores:
            m = jnp.max(s, axis=0, keepdims=True)
            stats.append((m, jnp.exp2((s - m).astype(BF16))))
        outs = []
        for j, (jp, (m, p)) in enumerate(zip(prevs, stats)):
            v_span = vt[j] if nseg_blk == 1 else jnp.concatenate([vt[jp], vt[j]], axis=1)
            outs.append(jnp.dot(v_span, p, preferred_element_type=F32))
        for j, ((m, p), res) in enumerate(zip(stats, outs)):
            l = res[LANES:LANES + 1, :]
            lse = m + jnp.log2(l)
            o_t = jnp.where(top_sq, res[0:LANES, 0:QBLK] / l[:, 0:QBLK], res[0:LANES, QBLK:] / l[:, QBLK:])
            lse_t = jnp.where(top_sq, jnp.broadcast_to(lse[:, 0:QBLK], (LANES, QBLK)),
                              jnp.broadcast_to(lse[:, QBLK:], (LANES, QBLK)))
            op[j * QBLK:(j + 1) * QBLK, :] = o_t.T
            lp[j * QBLK:(j + 1) * QBLK, :] = lse_t.T

        for r in range(dil):
            dst = pl.ds(r, seg_len, stride=dil) if dil > 1 else pl.ds(0, SEQ)
            src = pl.ds(r * seg_len, seg_len)
            (on0, on1, on2)[pi][dst, :] = op[src, :]
            (ln0, ln1, ln2)[pi][dst, :] = lp[src, :]

    l0, l1, l2 = ln0[...], ln1[...], ln2[...]
    mx = jnp.maximum(jnp.maximum(l0, l1), l2)
    e0, e1, e2 = jnp.exp2(l0 - mx), jnp.exp2(l1 - mx), jnp.exp2(l2 - mx)
    den = e0 + e1 + e2
    out = (e0 / den) * on0[...] + (e1 / den) * on1[...] + (e2 / den) * on2[...]
    o_ref[...] = out.astype(o_ref.dtype)


def _dilated(a1, a4, a16):
    t = a1[0].shape[0]
    nb = t // SEQ
    spec1 = pl.BlockSpec((SEQ, LANES), lambda b, hp: (b, hp))
    spec4 = pl.BlockSpec((None, 4, SEQ // 4, LANES), lambda b, hp: (b, 0, 0, hp))
    spec16 = pl.BlockSpec((None, 16, SEQ // 16, LANES), lambda b, hp: (b, 0, 0, hp))
    return pl.pallas_call(
        _dilated_kernel,
        grid=(nb, N_HEADS_DIL // 2),
        in_specs=[spec1] * 3 + [spec4] * 3 + [spec16] * 3,
        out_specs=spec1,
        out_shape=jax.ShapeDtypeStruct((t, A_COLS), BF16),
        scratch_shapes=[pltpu.VMEM((N_QBLK, LANES, QBLK), BF16), pltpu.VMEM((N_QBLK, LANES + ONES_ROWS, QBLK), BF16)]
        + [pltpu.VMEM((SEQ, LANES), F32)] * 8,
        compiler_params=pltpu.CompilerParams(dimension_semantics=("arbitrary", "arbitrary"),
                                             vmem_limit_bytes=VMEM_LIMIT),
        name="dilated",
    )(*a1, *a4, *a16)


def _gelu_tanh(x):
    return 0.5 * x * (1.0 + jnp.tanh(np.sqrt(2.0 / np.pi).astype(np.float32) * (x + 0.044715 * (x * x * x))))


def _softmax_step(state, s, v_t):
    m, l, acc = state
    mn = jnp.maximum(m, jnp.max(s, axis=0, keepdims=True))
    al = jnp.exp2(m - mn)
    p = jnp.exp2(s - mn)
    l = al * l + jnp.sum(p, axis=0, keepdims=True)
    acc = al * acc + jnp.dot(v_t, p.astype(BF16), preferred_element_type=F32)
    return mn, l, acc


def _nsa_kernel(q_ref, kflat_ref, vflat_ref, ksw_ref, gate_ref,
                wk1_ref, wk2d_ref, pek_ref, wv1_ref, wv2t_ref, pev_ref,
                cc_ref, cs1_ref, cs2_ref, ovl_ref, exp_ref,
                o_ref,
                qt, vst, vwt, gtt, kc2, vct):
    g = pl.program_id(1)
    nq = NSA_REP * QBLK

    for i in range(N_QBLK):
        rows = pl.ds(i * QBLK, QBLK)
        qt[i, 0:LANES, :] = q_ref[rows, 0:LANES].T
        qt[i, LANES:2 * LANES, :] = q_ref[rows, LANES:2 * LANES].T
        vs_t = ksw_ref[rows, LANES:2 * LANES].T
        vst[i] = jnp.where(g == 0, vs_t[0:HEAD_DIM], vs_t[HEAD_DIM:])
        vw_t = ksw_ref[rows, 3 * LANES:4 * LANES].T
        vwt[i] = jnp.where(g == 0, vw_t[0:HEAD_DIM], vw_t[HEAD_DIM:])
        gt = gate_ref[rows, :]
        gt = jnp.where(g == 0, gt, pltpu.roll(gt, LANES - 3 * NSA_REP, axis=1))
        gtt[i] = gt.T[0:GATE_ROWS, :]

    def first_layer(flat_ref, w1_ref, pe_ref):
        ab = jnp.dot(flat_ref[...], w1_ref[...], preferred_element_type=F32)
        pb = (jnp.dot(pe_ref[0:PE_ROWS, :], w1_ref[:, 0:CMP_HIDDEN], preferred_element_type=F32)
              + jnp.dot(pe_ref[PE_ROWS:, :], w1_ref[:, CMP_HIDDEN:], preferred_element_type=F32))
        return ab, pb

    def activate(ab, pb):
        hid = ab[:, 0:CMP_HIDDEN] + pltpu.roll(ab[:, CMP_HIDDEN:], N_CMP - 1, axis=0) + pb[0:1, :]
        return _gelu_tanh(hid).astype(BF16)

    first_k = first_layer(kflat_ref, wk1_ref, pek_ref)
    first_v = first_layer(vflat_ref, wv1_ref, pev_ref)
    hk = activate(*first_k)
    hv = activate(*first_v)
    kc2[...] = _rope_lanes(jnp.dot(hk, wk2d_ref[...], preferred_element_type=F32),
                           cc_ref[...], cs1_ref[...], cs2_ref[...]).astype(BF16)
    vct[...] = lax.dot_general(wv2t_ref[...], hv, _NT, preferred_element_type=F32).astype(BF16)

    key = _row_iota((QBLK, nq))
    qry = _lane_iota((QBLK, nq)) % QBLK
    causal = jnp.where(key <= qry, 0.0, NEG)
    upper = jnp.where(key > qry, 0.0, NEG)
    blk = _row_iota((N_SELBLK, QBLK))
    qry1 = _lane_iota((N_SELBLK, QBLK))
    grp_keys = SEL_GROUP * QBLK
    win_blocks = WIN // QBLK

    def qgroup(w, full_window):
        blocks = [SEL_GROUP * w + d for d in range(SEL_GROUP)]
        row0s = [pl.multiple_of(i * QBLK, QBLK) for i in blocks]

        q_pads = []
        for i in blocks:
            q_t = jnp.concatenate([qt[i, r * HEAD_DIM:(r + 1) * HEAD_DIM, :] for r in range(NSA_REP)], axis=1)
            zero = jnp.zeros_like(q_t)
            q_pads.append(jnp.where(g == 0, jnp.concatenate([q_t, zero], axis=0),
                                    jnp.concatenate([zero, q_t], axis=0)))
        s_cmp = [jnp.dot(kc2[...], qp, preferred_element_type=F32) for qp in q_pads]
        s_win, v_win = [], []
        for d, (i, qp) in enumerate(zip(blocks, q_pads)):
            if full_window:
                first = i - win_blocks
                keys = ksw_ref[pl.ds(pl.multiple_of(first * QBLK, QBLK), (win_blocks + 1) * QBLK), 2 * LANES:3 * LANES]
                s = jnp.dot(keys, qp, preferred_element_type=F32)
                s = jnp.concatenate([s[0:QBLK] + upper, s[QBLK:win_blocks * QBLK], s[win_blocks * QBLK:] + causal],
                                    axis=0)
                v_t = jnp.concatenate([vwt[first + e] for e in range(win_blocks + 1)], axis=1)
            else:
                s = jnp.dot(ksw_ref[0:(d + 1) * QBLK, 2 * LANES:3 * LANES], qp, preferred_element_type=F32)
                s = jnp.concatenate([s[0:d * QBLK], s[d * QBLK:] + causal], axis=0) if d else s + causal
                v_t = jnp.concatenate([vwt[e] for e in range(d + 1)], axis=1) if d else vwt[0]
            s_win.append(s)
            v_win.append(v_t)

        pcs = []
        for row0, s in zip(row0s, s_cmp):
            cvalid = (CMP_STRIDE * key + (CMP_LEN - 1) <= row0 + qry) & (key < N_CMP - 1)
            s = jnp.where(cvalid, s, NEG)
            m = jnp.max(s, axis=0, keepdims=True)
            p = jnp.where(cvalid, jnp.exp2(s - m), 0.0)
            l = jnp.sum(p, axis=0, keepdims=True)
            pcs.append(p / jnp.where(l > 0, l, 1.0))
        o_cmp = [jnp.dot(vct[...], pc.astype(BF16), preferred_element_type=F32) for pc in pcs]

        ovl = ovl_ref[...]
        imps = []
        for pc in pcs:
            psum = pc[:, 0:QBLK] + pc[:, QBLK:2 * QBLK] + pc[:, 2 * QBLK:3 * QBLK] + pc[:, 3 * QBLK:]
            p_hi = psum.astype(BF16)
            p_lo = (psum - p_hi.astype(F32)).astype(BF16)
            imps.append(jnp.dot(ovl, p_hi, preferred_element_type=F32) + jnp.dot(ovl, p_lo, preferred_element_type=F32))

        o_win = []
        for s, v_t in zip(s_win, v_win):
            m = jnp.max(s, axis=0, keepdims=True)
            p = jnp.exp2(s - m)
            l = jnp.sum(p, axis=0, keepdims=True)
            o_win.append(jnp.dot(v_t, p.astype(BF16), preferred_element_type=F32) / l)

        q_sel = []
        for row0, imp, qp in zip(row0s, imps, q_pads):
            t1 = row0 + qry1
            blk_t = t1 // SEL_LEN
            forced = (blk == 0) | (blk == blk_t) | (blk == blk_t - 1)
            v = jnp.where(blk * SEL_LEN <= t1, jnp.where(forced, BIG, imp), -BIG)
            selm = jnp.zeros((N_SELBLK, QBLK), F32)
            for _ in range(N_SEL):
                mx = jnp.max(v, axis=0, keepdims=True)
                first_hit = jnp.min(jnp.where(v == mx, blk, N_SELBLK), axis=0, keepdims=True)
                hit = blk == first_hit
                selm = jnp.where(hit, 1.0, selm)
                v = jnp.where(hit, LOWEST, v)
            notsel = jnp.concatenate([selm - 1.0, jnp.zeros((LANES - N_SELBLK, QBLK), F32)], axis=0).astype(BF16)
            q_sel.append(jnp.concatenate([qp, jnp.concatenate([notsel] * NSA_REP, axis=1)], axis=0))

        def sel_keys(t, n_rows):
            rows_t = pl.ds(pl.multiple_of(t * grp_keys, grp_keys), n_rows)
            return jnp.concatenate([ksw_ref[rows_t, 0:LANES], exp_ref[t, 0:n_rows, :]], axis=1)

        def sel_values(t, n_blk):
            return jnp.concatenate([vst[t * SEL_GROUP + e] for e in range(n_blk)], axis=1) if n_blk > 1 else vst[t * SEL_GROUP]

        wide = SEL_GROUP * nq
        st = (jnp.full((1, wide), LOWEST, F32), jnp.zeros((1, wide), F32), jnp.zeros((HEAD_DIM, wide), F32))
        if full_window:
            q_all = jnp.concatenate(q_sel, axis=1)

            def sel_body(t, st):
                s = jnp.dot(sel_keys(t, grp_keys), q_all, preferred_element_type=F32)
                return _softmax_step(st, s, sel_values(t, SEL_GROUP))

            for t_ in range(w):
                st = sel_body(t_, st)
        diag = []
        for d, qs in enumerate(q_sel):
            s = jnp.dot(sel_keys(w, (d + 1) * QBLK), qs, preferred_element_type=F32)
            diag.append(jnp.concatenate([s[0:d * QBLK], s[d * QBLK:] + causal], axis=0) if d else s + causal)
        o_sel = []
        for d, s in enumerate(diag):
            lanes = slice(d * nq, (d + 1) * nq)
            m, l, acc = _softmax_step((st[0][:, lanes], st[1][:, lanes], st[2][:, lanes]), s, sel_values(w, d + 1))
            o_sel.append(acc / l)

        for n, (i, row0) in enumerate(zip(blocks, row0s)):
            gti = gtt[i]

            def gate(br):
                return jnp.concatenate([gti[r * 3 + br:r * 3 + br + 1, :] for r in range(NSA_REP)], axis=1)

            out_t = gate(0) * o_cmp[n] + gate(1) * o_sel[n] + gate(2) * o_win[n]
            for pair in range(NSA_REP // 2):
                both = jnp.concatenate([out_t[:, (2 * pair) * QBLK:(2 * pair + 1) * QBLK],
                                        out_t[:, (2 * pair + 1) * QBLK:(2 * pair + 2) * QBLK]], axis=0)
                o_ref[pl.ds(row0, QBLK), pair * LANES:(pair + 1) * LANES] = both.T.astype(o_ref.dtype)

    qgroup(0, False)
    for w in range(1, N_QBLK // SEL_GROUP):
        qgroup(w, True)


def _nsa(bq, kvflat, ksw, gates, wk1, wk2d, pek, wv1, wv2t, pev, cc, cs1, cs2, ovl_t, expand_t):
    t = bq.shape[0]
    nb = t // SEQ
    const2 = lambda b, g: (0, 0)
    return pl.pallas_call(
        _nsa_kernel,
        grid=(nb, N_KV_NSA),
        in_specs=[
            pl.BlockSpec((SEQ, 2 * LANES), lambda b, g: (b, g)),
            pl.BlockSpec((None, None, N_CMP, CMP_STRIDE * HEAD_DIM), lambda b, g: (b, g, 0, 0)),
            pl.BlockSpec((None, None, N_CMP, CMP_STRIDE * HEAD_DIM), lambda b, g: (b, N_KV_NSA + g, 0, 0)),
            pl.BlockSpec((SEQ, 4 * LANES), lambda b, g: (b, 0)),
            pl.BlockSpec((SEQ, LANES), lambda b, g: (b, 0)),
            pl.BlockSpec((CMP_STRIDE * HEAD_DIM, 2 * CMP_HIDDEN), const2),
            pl.BlockSpec((CMP_HIDDEN, LANES), const2),
            pl.BlockSpec((2 * PE_ROWS, CMP_STRIDE * HEAD_DIM), const2),
            pl.BlockSpec((CMP_STRIDE * HEAD_DIM, 2 * CMP_HIDDEN), const2),
            pl.BlockSpec((HEAD_DIM, CMP_HIDDEN), const2),
            pl.BlockSpec((2 * PE_ROWS, CMP_STRIDE * HEAD_DIM), const2),
            pl.BlockSpec((N_CMP, LANES), const2),
            pl.BlockSpec((N_CMP, LANES), const2),
            pl.BlockSpec((N_CMP, LANES), const2),
            pl.BlockSpec((N_SELBLK, LANES), const2),
            pl.BlockSpec((N_QBLK // SEL_GROUP, SEL_GROUP * QBLK, LANES), lambda b, g: (0, 0, 0)),
        ],
        out_specs=pl.BlockSpec((SEQ, 2 * LANES), lambda b, g: (b, g)),
        out_shape=jax.ShapeDtypeStruct((t, B_COLS), BF16),
        scratch_shapes=[pltpu.VMEM((N_QBLK, 2 * LANES, QBLK), BF16),
                        pltpu.VMEM((N_QBLK, HEAD_DIM, QBLK), BF16),
                        pltpu.VMEM((N_QBLK, HEAD_DIM, QBLK), BF16),
                        pltpu.VMEM((N_QBLK, GATE_ROWS, QBLK), F32),
                        pltpu.VMEM((N_CMP, LANES), BF16),
                        pltpu.VMEM((HEAD_DIM, N_CMP), BF16)],
        compiler_params=pltpu.CompilerParams(dimension_semantics=("arbitrary", "arbitrary"),
                                             vmem_limit_bytes=VMEM_LIMIT),
        name="nsa",
    )(bq, kvflat, kvflat, ksw, gates, wk1, wk2d, pek, wv1, wv2t, pev, cc, cs1, cs2, ovl_t, expand_t)


def _pack_bf16_pairs(lo, hi):
    lo_b = lax.bitcast_convert_type(lo.astype(BF16).astype(F32), jnp.uint32)
    hi_b = lax.bitcast_convert_type(hi.astype(BF16).astype(F32), jnp.uint32)
    return (hi_b & jnp.uint32(0xFFFF0000)) | (lo_b >> 16)


def _unpack_bf16_pairs(p):
    lo = lax.bitcast_convert_type(p << 16, F32)
    hi = lax.bitcast_convert_type(p & jnp.uint32(0xFFFF0000), F32)
    return lo, hi


def _post_kernel(x_ref, ma_ref, mb_ref, wo_ref, g2_ref, wr_ref, br_ref, tri_ref,
                 x1_ref, ha_ref, hb_ref, e1_ref, e2_ref, r1_ref, r2_ref, w_ref, cnt_ref):
    sub_rows = MOE_TM // POST_SPLIT
    ranges = [slice(n * sub_rows, (n + 1) * sub_rows) for n in range(POST_SPLIT)]
    x1s = [x_ref[rows, :] + jnp.dot(ma_ref[rows, :], wo_ref[0:A_COLS, :], preferred_element_type=F32)
           + jnp.dot(mb_ref[rows, :], wo_ref[A_COLS:, :], preferred_element_type=F32) for rows in ranges]
    wr = wr_ref[...]
    w_hi = wr.astype(BF16)
    w_lo = (wr - w_hi.astype(F32)).astype(BF16)
    logits = []
    for rows, x1 in zip(ranges, x1s):
        x1_ref[rows, :] = x1
        ms = jnp.mean(x1 * x1, axis=-1, keepdims=True)
        h2 = x1 * lax.rsqrt(ms + EPS) * g2_ref[...]
        ha_ref[rows, :] = _pack_bf16_pairs(h2[:, 0:PACK_W], h2[:, 2 * PACK_W:3 * PACK_W])
        hb_ref[rows, :] = _pack_bf16_pairs(h2[:, PACK_W:2 * PACK_W], h2[:, 3 * PACK_W:])
        h_hi = h2.astype(BF16)
        h_lo = (h2 - h_hi.astype(F32)).astype(BF16)
        logits.append(lax.dot_general(w_hi, h_hi, _NT, preferred_element_type=F32)
                      + lax.dot_general(w_hi, h_lo, _NT, preferred_element_type=F32)
                      + lax.dot_general(w_lo, h_hi, _NT, preferred_element_type=F32))
    lg = jnp.concatenate(logits, axis=1) + br_ref[:, 0:1]
    gl = lg[0:N_GROUPS, :]
    sub = _row_iota(gl.shape)
    gmax = jnp.max(gl, axis=0, keepdims=True)
    p_g = 1.0 / jnp.sum(jnp.exp(gl - gmax), axis=0, keepdims=True)
    g_star = jnp.min(jnp.where(gl == gmax, sub, N_GROUPS), axis=0, keepdims=True)
    el = jnp.zeros_like(gl)
    for gi in range(N_GROUPS):
        lo = N_GROUPS + gi * EXPERTS_PER_GROUP
        el = el + jnp.where(g_star == gi, lg[lo:lo + EXPERTS_PER_GROUP, :], 0.0)
    emax = jnp.max(el, axis=0, keepdims=True)
    ee = jnp.exp(el - emax)
    pe = ee / jnp.sum(ee, axis=0, keepdims=True)
    v1 = jnp.max(pe, axis=0, keepdims=True)
    i1 = jnp.min(jnp.where(pe == v1, sub, EXPERTS_PER_GROUP), axis=0, keepdims=True)
    pe2 = jnp.where(sub == i1, -1.0, pe)
    v2 = jnp.max(pe2, axis=0, keepdims=True)
    i2 = jnp.min(jnp.where(pe2 == v2, sub, EXPERTS_PER_GROUP), axis=0, keepdims=True)
    tot = v1 + v2
    e1 = g_star * EXPERTS_PER_GROUP + i1
    e2 = g_star * EXPERTS_PER_GROUP + i2
    e1_ref[0] = e1
    e2_ref[0] = e2

    wslab = jnp.concatenate([v1 / tot * p_g, v2 / tot * p_g, jnp.zeros((LANES - 2, MOE_TM), F32)], axis=0)
    for j in range(MOE_TM // LANES):
        w_ref[j * LANES:(j + 1) * LANES, :] = wslab[:, j * LANES:(j + 1) * LANES].T

    n_exp = N_EXPERTS
    sub_e = _row_iota((n_exp, MOE_TM))
    hit1 = sub_e == e1
    hit2 = sub_e == e2
    assigned = jnp.where(hit1, 1.0, 0.0) + jnp.where(hit2, 1.0, 0.0)
    before = jnp.dot(assigned.astype(BF16), tri_ref[...], preferred_element_type=F32)
    r1_ref[0] = jnp.sum(jnp.where(hit1, before, 0.0), axis=0, keepdims=True).astype(jnp.int32)
    r2_ref[0] = jnp.sum(jnp.where(hit2, before, 0.0), axis=0, keepdims=True).astype(jnp.int32)
    cnt = jnp.sum(assigned, axis=1, keepdims=True).astype(jnp.int32)
    cnt_ref[0] = jnp.broadcast_to(cnt, (n_exp, LANES))


def _post(x2, mix_a, mix_b, w_o, g2, wr_t, br, tri, part):
    t = x2.shape[0] // MOE_PARTS
    nt = t // MOE_TM
    first = part * nt
    n_exp = N_EXPERTS
    row = lambda i: (i, 0)
    const = lambda i: (0, 0)
    tok = lambda i: (i, 0, 0)
    src = lambda i: (first + i, 0)
    tok_spec = pl.BlockSpec((1, 1, MOE_TM), tok)
    tok_shape = jax.ShapeDtypeStruct((nt, 1, MOE_TM), jnp.int32)
    return pl.pallas_call(
        _post_kernel,
        grid=(nt,),
        in_specs=[pl.BlockSpec((MOE_TM, D_MODEL), src), pl.BlockSpec((MOE_TM, A_COLS), src),
                  pl.BlockSpec((MOE_TM, B_COLS), src), pl.BlockSpec((D_MODEL, D_MODEL), const),
                  pl.BlockSpec((1, D_MODEL), const), pl.BlockSpec((ROUTER_ROWS, D_MODEL), const),
                  pl.BlockSpec((ROUTER_ROWS, LANES), const), pl.BlockSpec((MOE_TM, MOE_TM), const)],
        out_specs=[pl.BlockSpec((MOE_TM, D_MODEL), row),
                   pl.BlockSpec((MOE_TM, PACK_W), row), pl.BlockSpec((MOE_TM, PACK_W), row),
                   tok_spec, tok_spec, tok_spec, tok_spec,
                   pl.BlockSpec((MOE_TM, LANES), row),
                   pl.BlockSpec((1, n_exp, LANES), tok)],
        out_shape=[jax.ShapeDtypeStruct((t, D_MODEL), F32),
                   jax.ShapeDtypeStruct((t, PACK_W), jnp.uint32), jax.ShapeDtypeStruct((t, PACK_W), jnp.uint32),
                   tok_shape, tok_shape, tok_shape, tok_shape,
                   jax.ShapeDtypeStruct((t, LANES), F32),
                   jax.ShapeDtypeStruct((nt, n_exp, LANES), jnp.int32)],
        compiler_params=pltpu.CompilerParams(dimension_semantics=("arbitrary",),
                                             vmem_limit_bytes=VMEM_LIMIT),
        name="post",
    )(x2, mix_a, mix_b, w_o, g2, wr_t, br, tri)


def _sc_mesh():
    return plsc.VectorSubcoreMesh(core_axis_name="core", subcore_axis_name="subcore")


def _sc_scatter_rows(x, pos1, pos2, n_out):
    r, c = x.shape

    @pl.kernel(out_type=jax.ShapeDtypeStruct((n_out, c), x.dtype), mesh=_sc_mesh(), scratch_types=[])
    def scatter_kernel(x_hbm, p1_hbm, p2_hbm, o_hbm):
        def body(x_vmem, p1_vmem, p2_vmem):
            pltpu.sync_copy(x_vmem, o_hbm.at[p1_vmem.at[0]])
            pltpu.sync_copy(x_vmem, o_hbm.at[p2_vmem.at[0]])

        idx = pl.BlockSpec((1, SC_WINDOW), lambda i: (0, i))
        pltpu.emit_pipeline(body, grid=(r // SC_WINDOW,),
                            in_specs=[pl.BlockSpec((SC_WINDOW, c), lambda i: (i, 0)), idx, idx], out_specs=[],
                            core_axis_name=("core", "subcore"),
                            dimension_semantics=(pltpu.PARALLEL,))(x_hbm, p1_hbm, p2_hbm)

    return scatter_kernel(x, pos1.reshape(1, r), pos2.reshape(1, r))


def _sc_gather_rows(y, idx):
    n = idx.shape[0]
    c = y.shape[1]

    @pl.kernel(out_type=jax.ShapeDtypeStruct((n, c), y.dtype), mesh=_sc_mesh(), scratch_types=[])
    def gather_kernel(y_hbm, i_hbm, o_hbm):
        def body(i_vmem, o_vmem):
            pltpu.sync_copy(y_hbm.at[i_vmem.at[0]], o_vmem)

        pltpu.emit_pipeline(body, grid=(n // SC_WINDOW,),
                            in_specs=[pl.BlockSpec((1, SC_WINDOW), lambda i: (0, i))],
                            out_specs=[pl.BlockSpec((SC_WINDOW, c), lambda i: (i, 0))],
                            core_axis_name=("core", "subcore"),
                            dimension_semantics=(pltpu.PARALLEL,))(i_hbm, o_hbm)

    return gather_kernel(y, idx.reshape(1, n))


def _ffn_kernel(ce_ref, cv_ref, xa_ref, xb_ref, wg_ref, wu_ref, wd_ref, ya_ref, yb_ref, wg_bf, wu_bf, wd_bf):
    c = pl.program_id(0)
    prev = ce_ref[jnp.maximum(c - 1, 0)]

    @pl.when((c == 0) | (ce_ref[c] != prev))
    def _():
        wg_bf[...] = wg_ref[0].astype(BF16)
        wu_bf[...] = wu_ref[0].astype(BF16)
        wd_bf[...] = wd_ref[0].astype(BF16)

    @pl.when(cv_ref[c] == 0)
    def _():
        ya_ref[...] = jnp.zeros_like(ya_ref)
        yb_ref[...] = jnp.zeros_like(yb_ref)

    @pl.when(cv_ref[c] != 0)
    def _():
        a_lo, a_hi = _unpack_bf16_pairs(xa_ref[...])
        b_lo, b_hi = _unpack_bf16_pairs(xb_ref[...])
        xs = jnp.concatenate([a_lo, b_lo, a_hi, b_hi], axis=1).astype(BF16)
        gate = jnp.dot(xs, wg_bf[...], preferred_element_type=F32)
        up = jnp.dot(xs, wu_bf[...], preferred_element_type=F32)
        he = (gate * jax.nn.sigmoid(gate) * up).astype(BF16)
        y = jnp.dot(he, wd_bf[...], preferred_element_type=F32)
        ya_ref[...] = _pack_bf16_pairs(y[:, 0:PACK_W], y[:, 2 * PACK_W:3 * PACK_W])
        yb_ref[...] = _pack_bf16_pairs(y[:, PACK_W:2 * PACK_W], y[:, 3 * PACK_W:])


def _ffn(chunk_expert, chunk_valid, xsa, xsb, wg, wu, wd):
    n_pad = xsa.shape[0]
    rows = pl.BlockSpec((FFN_ROWS, PACK_W), lambda c, ce, cv: (c, 0))
    grid_spec = pltpu.PrefetchScalarGridSpec(
        num_scalar_prefetch=2,
        grid=(n_pad // FFN_ROWS,),
        in_specs=[rows, rows,
                  pl.BlockSpec((1, D_MODEL, D_FF_EXPERT), lambda c, ce, cv: (ce[c], 0, 0)),
                  pl.BlockSpec((1, D_MODEL, D_FF_EXPERT), lambda c, ce, cv: (ce[c], 0, 0)),
                  pl.BlockSpec((1, D_FF_EXPERT, D_MODEL), lambda c, ce, cv: (ce[c], 0, 0))],
        out_specs=[rows, rows],
        scratch_shapes=[pltpu.VMEM((D_MODEL, D_FF_EXPERT), BF16), pltpu.VMEM((D_MODEL, D_FF_EXPERT), BF16),
                        pltpu.VMEM((D_FF_EXPERT, D_MODEL), BF16)],
    )
    out = jax.ShapeDtypeStruct((n_pad, PACK_W), jnp.uint32)
    return pl.pallas_call(
        _ffn_kernel,
        grid_spec=grid_spec,
        out_shape=[out, out],
        compiler_params=pltpu.CompilerParams(dimension_semantics=("arbitrary",),
                                             vmem_limit_bytes=VMEM_LIMIT),
        name="ffn",
    )(chunk_expert, chunk_valid, xsa, xsb, wg, wu, wd)


def _final_kernel(x1_ref, y1a_ref, y1b_ref, y2a_ref, y2b_ref, w_ref, gf_ref, o_ref):
    w1 = w_ref[:, 0:1]
    w2 = w_ref[:, 1:2]
    a1_lo, a1_hi = _unpack_bf16_pairs(y1a_ref[...])
    b1_lo, b1_hi = _unpack_bf16_pairs(y1b_ref[...])
    a2_lo, a2_hi = _unpack_bf16_pairs(y2a_ref[...])
    b2_lo, b2_hi = _unpack_bf16_pairs(y2b_ref[...])
    y = jnp.concatenate([w1 * a1_lo + w2 * a2_lo, w1 * b1_lo + w2 * b2_lo,
                         w1 * a1_hi + w2 * a2_hi, w1 * b1_hi + w2 * b2_hi], axis=1)
    xo = x1_ref[...] + y
    ms = jnp.mean(xo * xo, axis=-1, keepdims=True)
    o_ref[...] = xo * lax.rsqrt(ms + EPS) * gf_ref[...]


def _final(x1, yga, ygb, wtok, gf, out_prev, part):
    t = x1.shape[0]
    first = part * (t // FINAL_TM)
    row = lambda i: (i, 0)
    one = pl.BlockSpec((None, FINAL_TM, PACK_W), lambda i: (0, i, 0))
    two = pl.BlockSpec((None, FINAL_TM, PACK_W), lambda i: (1, i, 0))
    in_specs = [pl.BlockSpec((FINAL_TM, D_MODEL), row), one, one, two, two,
                pl.BlockSpec((FINAL_TM, LANES), row), pl.BlockSpec((1, D_MODEL), lambda i: (0, 0))]
    args = [x1, yga, ygb, yga, ygb, wtok, gf]
    kern = _final_kernel
    aliases = {}
    if out_prev is not None:
        in_specs.append(pl.BlockSpec(memory_space=pl.ANY))
        args.append(out_prev)
        aliases = {len(args) - 1: 0}
        kern = lambda *refs: _final_kernel(*refs[:7], refs[8])
    return pl.pallas_call(
        kern,
        grid=(t // FINAL_TM,),
        in_specs=in_specs,
        out_specs=pl.BlockSpec((FINAL_TM, D_MODEL), lambda i: (first + i, 0)),
        out_shape=jax.ShapeDtypeStruct((t * MOE_PARTS, D_MODEL), F32),
        input_output_aliases=aliases,
        compiler_params=pltpu.CompilerParams(dimension_semantics=("arbitrary",),
                                             vmem_limit_bytes=VMEM_LIMIT),
        name="final",
    )(*args)


def _moe_plan(cnt, e1, e2, r1, r2):
    nt, n_exp = cnt.shape
    tot = jnp.sum(cnt, axis=0)
    padded = (tot + FFN_ROWS - 1) // FFN_ROWS * FFN_ROWS
    seg_end = jnp.cumsum(padded)
    seg_start = seg_end - padded
    off = seg_start[None, :] + jnp.cumsum(cnt, axis=0) - cnt
    experts = jnp.arange(n_exp, dtype=jnp.int32)

    def lookup(e):
        return jnp.sum(jnp.where(e[:, :, None] == experts, off[:, None, :], 0), axis=2)

    pos1 = (lookup(e1) + r1).reshape(-1).astype(jnp.int32)
    pos2 = (lookup(e2) + r2).reshape(-1).astype(jnp.int32)
    n_chunks = (nt * MOE_TM * 2 + n_exp * FFN_ROWS) // FFN_ROWS
    cstart = jnp.arange(n_chunks, dtype=jnp.int32) * FFN_ROWS
    ce = jnp.sum((seg_end[None, :] <= cstart[:, None]).astype(jnp.int32), axis=1)
    cec = jnp.minimum(ce, n_exp - 1)
    seg_used = seg_start + tot
    valid = jnp.any((ce[:, None] == experts[None, :]) & (cstart[:, None] < seg_used[None, :]), axis=1)
    return pos1, pos2, cec.astype(jnp.int32), valid.astype(jnp.int32)


def _rope_tables(pos):
    half = ROT_DIM // 2
    inv_freq = ROPE_THETA ** (-jnp.arange(0, ROT_DIM, 2, dtype=F32) / ROT_DIM)
    ang = pos.astype(F32)[:, None] * inv_freq[None, :]
    cos, sin = jnp.cos(ang), jnp.sin(ang)
    l64 = np.arange(LANES) % HEAD_DIM
    f = l64 % half
    first = jnp.asarray(l64 < half)[None, :]
    second = jnp.asarray((l64 >= half) & (l64 < ROT_DIM))[None, :]
    c = jnp.where(first | second, cos[:, f], 1.0)
    s1 = jnp.where(first, -sin[:, f], 0.0)
    s2 = jnp.where(second, sin[:, f], 0.0)
    return c.astype(F32), s1.astype(F32), s2.astype(F32)


def _overlap_table_t():
    cs = np.arange(N_CMP)[None, :] * CMP_STRIDE
    js = np.arange(N_SELBLK)[:, None] * SEL_LEN
    ov = np.clip(np.minimum(cs + CMP_LEN, js + SEL_LEN) - np.maximum(cs, js), 0, None) / CMP_LEN
    ov[:, N_CMP - 1] = 0.0
    return ov.astype(np.float32)


def _expand_table_t():
    out = np.zeros((SEQ, LANES), np.float32)
    out[np.arange(SEQ), np.arange(SEQ) // SEL_LEN] = -NEG
    return out.reshape(N_QBLK // SEL_GROUP, SEL_GROUP * QBLK, LANES)


def kernel(x, norm1_g, w_in, pe_kc, w_kc1, w_kc2, pe_vc, w_vc1, w_vc2, w_o, norm2_g, w_rg, b_rg, w_re, b_re,
           w_gate, w_up, w_down, norm_f_g):
    nb, s, d = x.shape
    assert (s, d) == (SEQ, D_MODEL) and norm1_g.shape[0] == 1
    t = nb * s
    x2 = x.reshape(t, d)

    w_in_p = jnp.pad(w_in[0], ((0, 0), (0, D_IN_PAD - w_in.shape[2]))).astype(BF16)
    rc, rs1, rs2 = _rope_tables(jnp.arange(SEQ))
    cc, cs1, cs2 = _rope_tables(jnp.arange(N_CMP) * CMP_STRIDE + CMP_LEN - 1)
    half_flat = CMP_STRIDE * HEAD_DIM

    def cmp_w1(w1):
        return jnp.concatenate([w1[:half_flat], w1[half_flat:]], axis=1).astype(BF16)

    def cmp_pe(pe):
        rows = pe.reshape(2, half_flat)
        return jnp.concatenate([jnp.broadcast_to(rows[0:1], (PE_ROWS, half_flat)),
                                jnp.broadcast_to(rows[1:2], (PE_ROWS, half_flat))], axis=0).astype(BF16)

    wk2d = jnp.concatenate([w_kc2[0], w_kc2[0]], axis=1).astype(BF16)
    wv2t = w_vc2[0].T.astype(BF16)
    ovl_t = jnp.asarray(_overlap_table_t(), BF16)
    expand_t = jnp.asarray(_expand_table_t(), BF16)

    (aq, ak, av, aq4, ak4, av4, aq16, ak16, av16,
     bq, kvflat, ksw, gates) = _proj(x2, norm1_g, w_in_p, rc, rs1, rs2)
    mix_a = _dilated((aq, ak, av), (aq4, ak4, av4), (aq16, ak16, av16))

    mix_b = _nsa(bq, kvflat, ksw, gates, cmp_w1(w_kc1[0]), wk2d, cmp_pe(pe_kc[0]),
                 cmp_w1(w_vc1[0]), wv2t, cmp_pe(pe_vc[0]), cc, cs1, cs2, ovl_t, expand_t)

    wr_t = jnp.pad(jnp.concatenate([w_rg[0].T, w_re[0].reshape(d, -1).T], axis=0),
                   ((0, ROUTER_ROWS - N_ROUTER), (0, 0)))
    br = jnp.broadcast_to(jnp.pad(jnp.concatenate([b_rg[0], b_re[0].reshape(-1)]),
                                  (0, ROUTER_ROWS - N_ROUTER))[:, None], (ROUTER_ROWS, LANES))
    tri = jnp.asarray(np.triu(np.ones((MOE_TM, MOE_TM), np.float32), 1), BF16)
    tp = t // MOE_PARTS
    n_pad = 2 * tp + N_GROUPS * EXPERTS_PER_GROUP * FFN_ROWS
    w_o_bf = w_o[0].astype(BF16)
    gf = norm_f_g.reshape(1, d)
    routed = []
    for part in range(MOE_PARTS):
        x1, hpa, hpb, e1, e2, r1, r2, wtok, cnt = _post(x2, mix_a, mix_b, w_o_bf, norm2_g, wr_t, br, tri, part)
        pos1, pos2, chunk_expert, chunk_valid = _moe_plan(cnt[:, :, 0], e1[:, 0], e2[:, 0], r1[:, 0], r2[:, 0])
        xsa = _sc_scatter_rows(hpa, pos1, pos2, n_pad)
        xsb = _sc_scatter_rows(hpb, pos1, pos2, n_pad)
        routed.append((x1, wtok, pos1, pos2, chunk_expert, chunk_valid, xsa, xsb))
    gathered = []
    for x1, wtok, pos1, pos2, chunk_expert, chunk_valid, xsa, xsb in routed:
        ysa, ysb = _ffn(chunk_expert, chunk_valid, xsa, xsb, w_gate[0], w_up[0], w_down[0])
        pos12 = jnp.concatenate([pos1, pos2])
        gathered.append((x1, wtok, _sc_gather_rows(ysa, pos12), _sc_gather_rows(ysb, pos12)))
    out = None
    for part, (x1, wtok, yga, ygb) in enumerate(gathered):
        out = _final(x1, yga.reshape(2, tp, PACK_W), ygb.reshape(2, tp, PACK_W), wtok, gf, out, part)
    return out.reshape(nb, s, d)
```

```python
import numpy as np
import jax
import jax.numpy as jnp
from jax import lax
from jax.experimental import pallas as pl
from jax.experimental.pallas import tpu as pltpu
from jax.experimental.pallas import tpu_sc as plsc

F32 = jnp.float32
BF16 = jnp.bfloat16

D_MODEL = 1024
SEQ = 2048
HEAD_DIM = 64
N_HEADS_DIL = 8
DIL_PATTERNS = ((128, 1), (512, 4), (2048, 16))
N_HEADS_NSA = 8
N_KV_NSA = 2
NSA_REP = N_HEADS_NSA // N_KV_NSA
CMP_STRIDE = 16
CMP_LEN = 32
CMP_HIDDEN = 256
SEL_LEN = 64
N_SEL = 8
WIN = 512
ROPE_THETA = 500000.0
ROT_DIM = HEAD_DIM // 4
N_GROUPS = 8
EXPERTS_PER_GROUP = 8
D_FF_EXPERT = 256
EPS = 1e-6
NEG = -1e30
BIG = 1e9
LOWEST = -3.0e38
LOG2_E = 1.4426950408889634

A_COLS = N_HEADS_DIL * HEAD_DIM
B_COLS = N_HEADS_NSA * HEAD_DIM
KV_COLS = N_KV_NSA * HEAD_DIM
N_EXPERTS = N_GROUPS * EXPERTS_PER_GROUP
N_ROUTER = N_GROUPS + N_EXPERTS

LANES = 128
QBLK = 128
N_QBLK = SEQ // QBLK
N_CMP = SEQ // CMP_STRIDE
N_SELBLK = SEQ // SEL_LEN
D_IN_PAD = 23 * LANES
PROJ_TM = 1024
SEL_GROUP = 4
MOE_TM = 1024
POST_SPLIT = 2
ROUTER_ROWS = 80
FFN_ROWS = 512
FINAL_TM = 1024
PACK_W = D_MODEL // 4
BF16_SUBLANES = 16
ONES_ROWS = BF16_SUBLANES
PE_ROWS = BF16_SUBLANES
GATE_ROWS = 16
SC_WINDOW = 128
VMEM_LIMIT = 56 * 1024 * 1024

_NT = (((1,), (1,)), ((), ()))


def _lane_iota(shape):
    return lax.broadcasted_iota(jnp.int32, shape, 1)


def _row_iota(shape):
    return lax.broadcasted_iota(jnp.int32, shape, 0)


def _rope_lanes(y, c, s1, s2):
    return y * c + pltpu.roll(y, LANES - ROT_DIM // 2, axis=1) * s1 + pltpu.roll(y, ROT_DIM // 2, axis=1) * s2


def _proj_kernel(x_ref, g_ref, w_ref, c_ref, s1_ref, s2_ref,
                 aq_ref, ak_ref, av_ref, aq4_ref, ak4_ref, av4_ref, aq16_ref, ak16_ref, av16_ref,
                 bq_ref, kvflat_ref, ksw_ref, gate_ref, nat_f32, d4_f32):
    x = x_ref[...]
    ms = jnp.mean(x * x, axis=-1, keepdims=True)
    h = (x * lax.rsqrt(ms + EPS) * g_ref[...]).astype(BF16)
    c = c_ref[...]
    s1 = s1_ref[...]
    s2 = s2_ref[...]

    def seg(lo, width):
        return jnp.dot(h, w_ref[:, lo:lo + width], preferred_element_type=F32)

    def store(dst, col, y, rope, scale, dilated=None):
        for j in range(y.shape[1] // LANES):
            yj = y[:, j * LANES:(j + 1) * LANES]
            if rope:
                yj = _rope_lanes(yj, c, s1, s2)
            if scale != 1.0:
                yj = yj * scale
            cols = slice(col + j * LANES, col + (j + 1) * LANES)
            dst[:, cols] = yj.astype(dst.dtype)
            if dilated is not None:
                d4_ref, d16_ref = dilated
                quarter = PROJ_TM // 4
                nat_f32[...] = yj
                for r in range(4):
                    part = nat_f32[pl.ds(r, quarter, stride=4), :]
                    d4_ref[r, :, cols] = part.astype(d4_ref.dtype)
                    d4_f32[r * quarter:(r + 1) * quarter, :] = part
                for r in range(4):
                    for a_ in range(4):
                        part = d4_f32[pl.ds(r * quarter + a_, quarter // 4, stride=4), :]
                        d16_ref[4 * a_ + r, :, cols] = part.astype(d16_ref.dtype)

    qscale = HEAD_DIM ** -0.5 * LOG2_E
    store(aq_ref, 0, seg(0, A_COLS), True, qscale, (aq4_ref, aq16_ref))
    store(ak_ref, 0, seg(A_COLS, A_COLS), True, 1.0, (ak4_ref, ak16_ref))
    store(av_ref, 0, seg(2 * A_COLS, A_COLS), False, 1.0, (av4_ref, av16_ref))
    nsa0 = 3 * A_COLS
    store(bq_ref, 0, seg(nsa0, B_COLS), True, qscale)
    y = seg(nsa0 + B_COLS, 2 * KV_COLS)
    quarter = PROJ_TM // 4
    left_half = _lane_iota((quarter // 4, LANES)) < HEAD_DIM
    for kind in range(2):
        nat_f32[...] = y[:, kind * KV_COLS:(kind + 1) * KV_COLS]
        for r in range(4):
            d4_f32[r * quarter:(r + 1) * quarter, :] = nat_f32[pl.ds(r, quarter, stride=4), :]

        def chunk_pos(i):
            return d4_f32[pl.ds((i % 4) * quarter + i // 4, quarter // 4, stride=4), :]

        for pair in range(CMP_STRIDE // 2):
            even, odd = chunk_pos(2 * pair), chunk_pos(2 * pair + 1)
            cols = slice(pair * LANES, (pair + 1) * LANES)
            kvflat_ref[2 * kind, :, cols] = jnp.where(left_half, even, pltpu.roll(odd, HEAD_DIM, axis=1)).astype(BF16)
            kvflat_ref[2 * kind + 1, :, cols] = jnp.where(left_half, pltpu.roll(even, HEAD_DIM, axis=1), odd).astype(BF16)
    kv0 = nsa0 + B_COLS + 2 * KV_COLS
    for n, rope in enumerate((True, False, True, False)):
        store(ksw_ref, n * KV_COLS, seg(kv0 + n * KV_COLS, KV_COLS), rope, 1.0)
    gate_ref[...] = jax.nn.sigmoid(seg(kv0 + 4 * KV_COLS, LANES))


def _proj(x2, g1, w_in_p, rc, rs1, rs2):
    t = x2.shape[0]
    nb = t // SEQ
    nblk_s = SEQ // PROJ_TM
    row = lambda i: (i, 0)
    pos = lambda i: (i % nblk_s, 0)
    const = lambda i: (0, 0)
    perm = lambda i: (i // nblk_s, 0, i % nblk_s, 0)
    wide = jax.ShapeDtypeStruct((t, A_COLS), BF16)
    wide_spec = pl.BlockSpec((PROJ_TM, A_COLS), row)
    d4 = jax.ShapeDtypeStruct((nb, 4, SEQ // 4, A_COLS), BF16)
    d4_spec = pl.BlockSpec((None, 4, PROJ_TM // 4, A_COLS), perm)
    d16 = jax.ShapeDtypeStruct((nb, 16, SEQ // 16, A_COLS), BF16)
    d16_spec = pl.BlockSpec((None, 16, PROJ_TM // 16, A_COLS), perm)
    flat = jax.ShapeDtypeStruct((nb, 2 * N_KV_NSA, N_CMP, CMP_STRIDE * HEAD_DIM), BF16)
    flat_spec = pl.BlockSpec((None, 2 * N_KV_NSA, PROJ_TM // CMP_STRIDE, CMP_STRIDE * HEAD_DIM), perm)
    outs = [wide] * 3 + [d4] * 3 + [d16] * 3 + [wide, flat, wide,
                                               jax.ShapeDtypeStruct((t, LANES), F32)]
    out_specs = [wide_spec] * 3 + [d4_spec] * 3 + [d16_spec] * 3 + [
        wide_spec, flat_spec, wide_spec, pl.BlockSpec((PROJ_TM, LANES), row)]
    return pl.pallas_call(
        _proj_kernel,
        grid=(t // PROJ_TM,),
        in_specs=[pl.BlockSpec((PROJ_TM, D_MODEL), row), pl.BlockSpec((1, D_MODEL), const),
                  pl.BlockSpec((D_MODEL, D_IN_PAD), const),
                  pl.BlockSpec((PROJ_TM, LANES), pos), pl.BlockSpec((PROJ_TM, LANES), pos),
                  pl.BlockSpec((PROJ_TM, LANES), pos)],
        out_specs=out_specs,
        out_shape=outs,
        scratch_shapes=[pltpu.VMEM((PROJ_TM, LANES), F32)] * 2,
        compiler_params=pltpu.CompilerParams(dimension_semantics=("arbitrary",),
                                             vmem_limit_bytes=VMEM_LIMIT),
        name="proj",
    )(x2, g1, w_in_p, rc, rs1, rs2)


def _dilated_kernel(q1_ref, k1_ref, v1_ref, q4_ref, k4_ref, v4_ref, q16_ref, k16_ref, v16_ref, o_ref,
                    qt, vt, op, lp, on0, on1, on2, ln0, ln1, ln2):
    c2 = _row_iota((2 * QBLK, 2 * QBLK))
    a2 = _lane_iota((2 * QBLK, 2 * QBLK)) % QBLK
    band = jnp.where((c2 >= a2) & (c2 <= a2 + QBLK), 0.0, NEG)
    band_noprev = jnp.where(c2 < QBLK, NEG, band)
    c1 = _row_iota((QBLK, 2 * QBLK))
    a1 = _lane_iota((QBLK, 2 * QBLK)) % QBLK
    band_own = jnp.where(c1 <= a1, 0.0, NEG)
    top_sq = _row_iota((LANES, QBLK)) < HEAD_DIM

    for j in range(N_QBLK):
        vt[j, LANES:, :] = jnp.ones((ONES_ROWS, QBLK), BF16)

    inputs = ((q1_ref, k1_ref, v1_ref), (q4_ref, k4_ref, v4_ref), (q16_ref, k16_ref, v16_ref))
    for pi, (window, dil) in enumerate(DIL_PATTERNS):
        seg_len = SEQ // dil
        nseg_blk = seg_len // QBLK
        q_ref, k_ref, v_ref = inputs[pi]

        def block(ref, j, nseg_blk=nseg_blk, dil=dil):
            if dil == 1:
                return ref[j * QBLK:(j + 1) * QBLK, :]
            return ref[j // nseg_blk, (j % nseg_blk) * QBLK:(j % nseg_blk + 1) * QBLK, :]

        for j in range(N_QBLK):
            qt[j] = block(q_ref, j).T
            vt[j, 0:LANES, :] = block(v_ref, j).T

        prevs, scores = [], []
        for j in range(N_QBLK):
            qtb = qt[j]
            zero = jnp.zeros_like(qtb)
            q_both = jnp.concatenate([jnp.where(top_sq, qtb, zero), jnp.where(top_sq, zero, qtb)], axis=1)
            if nseg_blk == 1:
                keys, bias, jp = block(k_ref, j), band_own, j
            else:
                first = j % nseg_blk == 0
                jp = j if first else j - 1
                bias = band_noprev if first else band
                keys = jnp.concatenate([block(k_ref, jp), block(k_ref, j)], axis=0)
            prevs.append(jp)
            scores.append(jnp.dot(keys, q_both, preferred_element_type=F32) + bias)
        stats = []
        for s in scores:
            m = jnp.max(s, axis=0, keepdims=True)
            stats.append((m, jnp.exp2((s - m).astype(BF16))))
        outs = []
        for j, (jp, (m, p)) in enumerate(zip(prevs, stats)):
            v_span = vt[j] if nseg_blk == 1 else jnp.concatenate([vt[jp], vt[j]], axis=1)
            outs.append(jnp.dot(v_span, p, preferred_element_type=F32))
        for j, ((m, p), res) in enumerate(zip(stats, outs)):
            l = res[LANES:LANES + 1, :]
            lse = m + jnp.log2(l)
            o_t = jnp.where(top_sq, res[0:LANES, 0:QBLK] / l[:, 0:QBLK], res[0:LANES, QBLK:] / l[:, QBLK:])
            lse_t = jnp.where(top_sq, jnp.broadcast_to(lse[:, 0:QBLK], (LANES, QBLK)),
                              jnp.broadcast_to(lse[:, QBLK:], (LANES, QBLK)))
            op[j * QBLK:(j + 1) * QBLK, :] = o_t.T
            lp[j * QBLK:(j + 1) * QBLK, :] = lse_t.T

        for r in range(dil):
            dst = pl.ds(r, seg_len, stride=dil) if dil > 1 else pl.ds(0, SEQ)
            src = pl.ds(r * seg_len, seg_len)
            (on0, on1, on2)[pi][dst, :] = op[src, :]
            (ln0, ln1, ln2)[pi][dst, :] = lp[src, :]

    l0, l1, l2 = ln0[...], ln1[...], ln2[...]
    mx = jnp.maximum(jnp.maximum(l0, l1), l2)
    e0, e1, e2 = jnp.exp2(l0 - mx), jnp.exp2(l1 - mx), jnp.exp2(l2 - mx)
    den = e0 + e1 + e2
    out = (e0 / den) * on0[...] + (e1 / den) * on1[...] + (e2 / den) * on2[...]
    o_ref[...] = out.astype(o_ref.dtype)


def _dilated(a1, a4, a16):
    t = a1[0].shape[0]
    nb = t // SEQ
    spec1 = pl.BlockSpec((SEQ, LANES), lambda b, hp: (b, hp))
    spec4 = pl.BlockSpec((None, 4, SEQ // 4, LANES), lambda b, hp: (b, 0, 0, hp))
    spec16 = pl.BlockSpec((None, 16, SEQ // 16, LANES), lambda b, hp: (b, 0, 0, hp))
    return pl.pallas_call(
        _dilated_kernel,
        grid=(nb, N_HEADS_DIL // 2),
        in_specs=[spec1] * 3 + [spec4] * 3 + [spec16] * 3,
        out_specs=spec1,
        out_shape=jax.ShapeDtypeStruct((t, A_COLS), BF16),
        scratch_shapes=[pltpu.VMEM((N_QBLK, LANES, QBLK), BF16), pltpu.VMEM((N_QBLK, LANES + ONES_ROWS, QBLK), BF16)]
        + [pltpu.VMEM((SEQ, LANES), F32)] * 8,
        compiler_params=pltpu.CompilerParams(dimension_semantics=("arbitrary", "arbitrary"),
                                             vmem_limit_bytes=VMEM_LIMIT),
        name="dilated",
    )(*a1, *a4, *a16)


def _gelu_tanh(x):
    return 0.5 * x * (1.0 + jnp.tanh(np.sqrt(2.0 / np.pi).astype(np.float32) * (x + 0.044715 * (x * x * x))))


def _softmax_step(state, s, v_t):
    m, l, acc = state
    mn = jnp.maximum(m, jnp.max(s, axis=0, keepdims=True))
    al = jnp.exp2(m - mn)
    p = jnp.exp2(s - mn)
    l = al * l + jnp.sum(p, axis=0, keepdims=True)
    acc = al * acc + jnp.dot(v_t, p.astype(BF16), preferred_element_type=F32)
    return mn, l, acc


def _nsa_kernel(q_ref, kflat_ref, vflat_ref, ksw_ref, gate_ref,
                wk1_ref, wk2d_ref, pek_ref, wv1_ref, wv2t_ref, pev_ref,
                cc_ref, cs1_ref, cs2_ref, ovl_ref, exp_ref,
                o_ref,
                qt, vst, vwt, gtt, kc2, vct):
    g = pl.program_id(1)
    nq = NSA_REP * QBLK

    for i in range(N_QBLK):
        rows = pl.ds(i * QBLK, QBLK)
        qt[i, 0:LANES, :] = q_ref[rows, 0:LANES].T
        qt[i, LANES:2 * LANES, :] = q_ref[rows, LANES:2 * LANES].T
        vs_t = ksw_ref[rows, LANES:2 * LANES].T
        vst[i] = jnp.where(g == 0, vs_t[0:HEAD_DIM], vs_t[HEAD_DIM:])
        vw_t = ksw_ref[rows, 3 * LANES:4 * LANES].T
        vwt[i] = jnp.where(g == 0, vw_t[0:HEAD_DIM], vw_t[HEAD_DIM:])
        gt = gate_ref[rows, :]
        gt = jnp.where(g == 0, gt, pltpu.roll(gt, LANES - 3 * NSA_REP, axis=1))
        gtt[i] = gt.T[0:GATE_ROWS, :]

    def first_layer(flat_ref, w1_ref, pe_ref):
        ab = jnp.dot(flat_ref[...], w1_ref[...], preferred_element_type=F32)
        pb = (jnp.dot(pe_ref[0:PE_ROWS, :], w1_ref[:, 0:CMP_HIDDEN], preferred_element_type=F32)
              + jnp.dot(pe_ref[PE_ROWS:, :], w1_ref[:, CMP_HIDDEN:], preferred_element_type=F32))
        return ab, pb

    def activate(ab, pb):
        hid = ab[:, 0:CMP_HIDDEN] + pltpu.roll(ab[:, CMP_HIDDEN:], N_CMP - 1, axis=0) + pb[0:1, :]
        return _gelu_tanh(hid).astype(BF16)

    first_k = first_layer(kflat_ref, wk1_ref, pek_ref)
    first_v = first_layer(vflat_ref, wv1_ref, pev_ref)
    hk = activate(*first_k)
    hv = activate(*first_v)
    kc2[...] = _rope_lanes(jnp.dot(hk, wk2d_ref[...], preferred_element_type=F32),
                           cc_ref[...], cs1_ref[...], cs2_ref[...]).astype(BF16)
    vct[...] = lax.dot_general(wv2t_ref[...], hv, _NT, preferred_element_type=F32).astype(BF16)

    key = _row_iota((QBLK, nq))
    qry = _lane_iota((QBLK, nq)) % QBLK
    causal = jnp.where(key <= qry, 0.0, NEG)
    upper = jnp.where(key > qry, 0.0, NEG)
    blk = _row_iota((N_SELBLK, QBLK))
    qry1 = _lane_iota((N_SELBLK, QBLK))
    grp_keys = SEL_GROUP * QBLK
    win_blocks = WIN // QBLK

    def qgroup(w, full_window):
        blocks = [SEL_GROUP * w + d for d in range(SEL_GROUP)]
        row0s = [pl.multiple_of(i * QBLK, QBLK) for i in blocks]

        q_pads = []
        for i in blocks:
            q_t = jnp.concatenate([qt[i, r * HEAD_DIM:(r + 1) * HEAD_DIM, :] for r in range(NSA_REP)], axis=1)
            zero = jnp.zeros_like(q_t)
            q_pads.append(jnp.where(g == 0, jnp.concatenate([q_t, zero], axis=0),
                                    jnp.concatenate([zero, q_t], axis=0)))
        s_cmp = [jnp.dot(kc2[...], qp, preferred_element_type=F32) for qp in q_pads]
        s_win, v_win = [], []
        for d, (i, qp) in enumerate(zip(blocks, q_pads)):
            if full_window:
                first = i - win_blocks
                keys = ksw_ref[pl.ds(pl.multiple_of(first * QBLK, QBLK), (win_blocks + 1) * QBLK), 2 * LANES:3 * LANES]
                s = jnp.dot(keys, qp, preferred_element_type=F32)
                s = jnp.concatenate([s[0:QBLK] + upper, s[QBLK:win_blocks * QBLK], s[win_blocks * QBLK:] + causal],
                                    axis=0)
                v_t = jnp.concatenate([vwt[first + e] for e in range(win_blocks + 1)], axis=1)
            else:
                s = jnp.dot(ksw_ref[0:(d + 1) * QBLK, 2 * LANES:3 * LANES], qp, preferred_element_type=F32)
                s = jnp.concatenate([s[0:d * QBLK], s[d * QBLK:] + causal], axis=0) if d else s + causal
                v_t = jnp.concatenate([vwt[e] for e in range(d + 1)], axis=1) if d else vwt[0]
            s_win.append(s)
            v_win.append(v_t)

        pcs = []
        for row0, s in zip(row0s, s_cmp):
            cvalid = (CMP_STRIDE * key + (CMP_LEN - 1) <= row0 + qry) & (key < N_CMP - 1)
            s = jnp.where(cvalid, s, NEG)
            m = jnp.max(s, axis=0, keepdims=True)
            p = jnp.where(cvalid, jnp.exp2(s - m), 0.0)
            l = jnp.sum(p, axis=0, keepdims=True)
            pcs.append(p / jnp.where(l > 0, l, 1.0))
        o_cmp = [jnp.dot(vct[...], pc.astype(BF16), preferred_element_type=F32) for pc in pcs]

        ovl = ovl_ref[...]
        imps = []
        for pc in pcs:
            psum = pc[:, 0:QBLK] + pc[:, QBLK:2 * QBLK] + pc[:, 2 * QBLK:3 * QBLK] + pc[:, 3 * QBLK:]
            p_hi = psum.astype(BF16)
            p_lo = (psum - p_hi.astype(F32)).astype(BF16)
            imps.append(jnp.dot(ovl, p_hi, preferred_element_type=F32) + jnp.dot(ovl, p_lo, preferred_element_type=F32))

        o_win = []
        for s, v_t in zip(s_win, v_win):
            m = jnp.max(s, axis=0, keepdims=True)
            p = jnp.exp2(s - m)
            l = jnp.sum(p, axis=0, keepdims=True)
            o_win.append(jnp.dot(v_t, p.astype(BF16), preferred_element_type=F32) / l)

        q_sel = []
        for row0, imp, qp in zip(row0s, imps, q_pads):
            t1 = row0 + qry1
            blk_t = t1 // SEL_LEN
            forced = (blk == 0) | (blk == blk_t) | (blk == blk_t - 1)
            v = jnp.where(blk * SEL_LEN <= t1, jnp.where(forced, BIG, imp), -BIG)
            selm = jnp.zeros((N_SELBLK, QBLK), F32)
            for _ in range(N_SEL):
                mx = jnp.max(v, axis=0, keepdims=True)
                first_hit = jnp.min(jnp.where(v == mx, blk, N_SELBLK), axis=0, keepdims=True)
                hit = blk == first_hit
                selm = jnp.where(hit, 1.0, selm)
                v = jnp.where(hit, LOWEST, v)
            notsel = jnp.concatenate([selm - 1.0, jnp.zeros((LANES - N_SELBLK, QBLK), F32)], axis=0).astype(BF16)
            q_sel.append(jnp.concatenate([qp, jnp.concatenate([notsel] * NSA_REP, axis=1)], axis=0))

        def sel_keys(t, n_rows):
            rows_t = pl.ds(pl.multiple_of(t * grp_keys, grp_keys), n_rows)
            return jnp.concatenate([ksw_ref[rows_t, 0:LANES], exp_ref[t, 0:n_rows, :]], axis=1)

        def sel_values(t, n_blk):
            return jnp.concatenate([vst[t * SEL_GROUP + e] for e in range(n_blk)], axis=1) if n_blk > 1 else vst[t * SEL_GROUP]

        wide = SEL_GROUP * nq
        st = (jnp.full((1, wide), LOWEST, F32), jnp.zeros((1, wide), F32), jnp.zeros((HEAD_DIM, wide), F32))
        if full_window:
            q_all = jnp.concatenate(q_sel, axis=1)

            def sel_body(t, st):
                s = jnp.dot(sel_keys(t, grp_keys), q_all, preferred_element_type=F32)
                return _softmax_step(st, s, sel_values(t, SEL_GROUP))

            for t_ in range(w):
                st = sel_body(t_, st)
        diag = []
        for d, qs in enumerate(q_sel):
            s = jnp.dot(sel_keys(w, (d + 1) * QBLK), qs, preferred_element_type=F32)
            diag.append(jnp.concatenate([s[0:d * QBLK], s[d * QBLK:] + causal], axis=0) if d else s + causal)
        o_sel = []
        for d, s in enumerate(diag):
            lanes = slice(d * nq, (d + 1) * nq)
            m, l, acc = _softmax_step((st[0][:, lanes], st[1][:, lanes], st[2][:, lanes]), s, sel_values(w, d + 1))
            o_sel.append(acc / l)

        for n, (i, row0) in enumerate(zip(blocks, row0s)):
            gti = gtt[i]

            def gate(br):
                return jnp.concatenate([gti[r * 3 + br:r * 3 + br + 1, :] for r in range(NSA_REP)], axis=1)

            out_t = gate(0) * o_cmp[n] + gate(1) * o_sel[n] + gate(2) * o_win[n]
            for pair in range(NSA_REP // 2):
                both = jnp.concatenate([out_t[:, (2 * pair) * QBLK:(2 * pair + 1) * QBLK],
                                        out_t[:, (2 * pair + 1) * QBLK:(2 * pair + 2) * QBLK]], axis=0)
                o_ref[pl.ds(row0, QBLK), pair * LANES:(pair + 1) * LANES] = both.T.astype(o_ref.dtype)

    qgroup(0, False)
    for w in range(1, N_QBLK // SEL_GROUP):
        qgroup(w, True)


def _nsa(bq, kvflat, ksw, gates, wk1, wk2d, pek, wv1, wv2t, pev, cc, cs1, cs2, ovl_t, expand_t):
    t = bq.shape[0]
    nb = t // SEQ
    const2 = lambda b, g: (0, 0)
    return pl.pallas_call(
        _nsa_kernel,
        grid=(nb, N_KV_NSA),
        in_specs=[
            pl.BlockSpec((SEQ, 2 * LANES), lambda b, g: (b, g)),
            pl.BlockSpec((None, None, N_CMP, CMP_STRIDE * HEAD_DIM), lambda b, g: (b, g, 0, 0)),
            pl.BlockSpec((None, None, N_CMP, CMP_STRIDE * HEAD_DIM), lambda b, g: (b, N_KV_NSA + g, 0, 0)),
            pl.BlockSpec((SEQ, 4 * LANES), lambda b, g: (b, 0)),
            pl.BlockSpec((SEQ, LANES), lambda b, g: (b, 0)),
            pl.BlockSpec((CMP_STRIDE * HEAD_DIM, 2 * CMP_HIDDEN), const2),
            pl.BlockSpec((CMP_HIDDEN, LANES), const2),
            pl.BlockSpec((2 * PE_ROWS, CMP_STRIDE * HEAD_DIM), const2),
            pl.BlockSpec((CMP_STRIDE * HEAD_DIM, 2 * CMP_HIDDEN), const2),
            pl.BlockSpec((HEAD_DIM, CMP_HIDDEN), const2),
            pl.BlockSpec((2 * PE_ROWS, CMP_STRIDE * HEAD_DIM), const2),
            pl.BlockSpec((N_CMP, LANES), const2),
            pl.BlockSpec((N_CMP, LANES), const2),
            pl.BlockSpec((N_CMP, LANES), const2),
            pl.BlockSpec((N_SELBLK, LANES), const2),
            pl.BlockSpec((N_QBLK // SEL_GROUP, SEL_GROUP * QBLK, LANES), lambda b, g: (0, 0, 0)),
        ],
        out_specs=pl.BlockSpec((SEQ, 2 * LANES), lambda b, g: (b, g)),
        out_shape=jax.ShapeDtypeStruct((t, B_COLS), BF16),
        scratch_shapes=[pltpu.VMEM((N_QBLK, 2 * LANES, QBLK), BF16),
                        pltpu.VMEM((N_QBLK, HEAD_DIM, QBLK), BF16),
                        pltpu.VMEM((N_QBLK, HEAD_DIM, QBLK), BF16),
                        pltpu.VMEM((N_QBLK, GATE_ROWS, QBLK), F32),
                        pltpu.VMEM((N_CMP, LANES), BF16),
                        pltpu.VMEM((HEAD_DIM, N_CMP), BF16)],
        compiler_params=pltpu.CompilerParams(dimension_semantics=("arbitrary", "arbitrary"),
                                             vmem_limit_bytes=VMEM_LIMIT),
        name="nsa",
    )(bq, kvflat, kvflat, ksw, gates, wk1, wk2d, pek, wv1, wv2t, pev, cc, cs1, cs2, ovl_t, expand_t)


def _pack_bf16_pairs(lo, hi):
    lo_b = lax.bitcast_convert_type(lo.astype(BF16).astype(F32), jnp.uint32)
    hi_b = lax.bitcast_convert_type(hi.astype(BF16).astype(F32), jnp.uint32)
    return (hi_b & jnp.uint32(0xFFFF0000)) | (lo_b >> 16)


def _unpack_bf16_pairs(p):
    lo = lax.bitcast_convert_type(p << 16, F32)
    hi = lax.bitcast_convert_type(p & jnp.uint32(0xFFFF0000), F32)
    return lo, hi


def _post_kernel(x_ref, ma_ref, mb_ref, wo_ref, g2_ref, wr_ref, br_ref, tri_ref,
                 x1_ref, ha_ref, hb_ref, e1_ref, e2_ref, r1_ref, r2_ref, w_ref, cnt_ref):
    sub_rows = MOE_TM // POST_SPLIT
    ranges = [slice(n * sub_rows, (n + 1) * sub_rows) for n in range(POST_SPLIT)]
    x1s = [x_ref[rows, :] + jnp.dot(ma_ref[rows, :], wo_ref[0:A_COLS, :], preferred_element_type=F32)
           + jnp.dot(mb_ref[rows, :], wo_ref[A_COLS:, :], preferred_element_type=F32) for rows in ranges]
    wr = wr_ref[...]
    w_hi = wr.astype(BF16)
    w_lo = (wr - w_hi.astype(F32)).astype(BF16)
    logits = []
    for rows, x1 in zip(ranges, x1s):
        x1_ref[rows, :] = x1
        ms = jnp.mean(x1 * x1, axis=-1, keepdims=True)
        h2 = x1 * lax.rsqrt(ms + EPS) * g2_ref[...]
        ha_ref[rows, :] = _pack_bf16_pairs(h2[:, 0:PACK_W], h2[:, 2 * PACK_W:3 * PACK_W])
        hb_ref[rows, :] = _pack_bf16_pairs(h2[:, PACK_W:2 * PACK_W], h2[:, 3 * PACK_W:])
        h_hi = h2.astype(BF16)
        h_lo = (h2 - h_hi.astype(F32)).astype(BF16)
        logits.append(lax.dot_general(w_hi, h_hi, _NT, preferred_element_type=F32)
                      + lax.dot_general(w_hi, h_lo, _NT, preferred_element_type=F32)
                      + lax.dot_general(w_lo, h_hi, _NT, preferred_element_type=F32))
    lg = jnp.concatenate(logits, axis=1) + br_ref[:, 0:1]
    gl = lg[0:N_GROUPS, :]
    sub = _row_iota(gl.shape)
    gmax = jnp.max(gl, axis=0, keepdims=True)
    p_g = 1.0 / jnp.sum(jnp.exp(gl - gmax), axis=0, keepdims=True)
    g_star = jnp.min(jnp.where(gl == gmax, sub, N_GROUPS), axis=0, keepdims=True)
    el = jnp.zeros_like(gl)
    for gi in range(N_GROUPS):
        lo = N_GROUPS + gi * EXPERTS_PER_GROUP
        el = el + jnp.where(g_star == gi, lg[lo:lo + EXPERTS_PER_GROUP, :], 0.0)
    emax = jnp.max(el, axis=0, keepdims=True)
    ee = jnp.exp(el - emax)
    pe = ee / jnp.sum(ee, axis=0, keepdims=True)
    v1 = jnp.max(pe, axis=0, keepdims=True)
    i1 = jnp.min(jnp.where(pe == v1, sub, EXPERTS_PER_GROUP), axis=0, keepdims=True)
    pe2 = jnp.where(sub == i1, -1.0, pe)
    v2 = jnp.max(pe2, axis=0, keepdims=True)
    i2 = jnp.min(jnp.where(pe2 == v2, sub, EXPERTS_PER_GROUP), axis=0, keepdims=True)
    tot = v1 + v2
    e1 = g_star * EXPERTS_PER_GROUP + i1
    e2 = g_star * EXPERTS_PER_GROUP + i2
    e1_ref[0] = e1
    e2_ref[0] = e2

    wslab = jnp.concatenate([v1 / tot * p_g, v2 / tot * p_g, jnp.zeros((LANES - 2, MOE_TM), F32)], axis=0)
    for j in range(MOE_TM // LANES):
        w_ref[j * LANES:(j + 1) * LANES, :] = wslab[:, j * LANES:(j + 1) * LANES].T

    n_exp = N_EXPERTS
    sub_e = _row_iota((n_exp, MOE_TM))
    hit1 = sub_e == e1
    hit2 = sub_e == e2
    assigned = jnp.where(hit1, 1.0, 0.0) + jnp.where(hit2, 1.0, 0.0)
    before = jnp.dot(assigned.astype(BF16), tri_ref[...], preferred_element_type=F32)
    r1_ref[0] = jnp.sum(jnp.where(hit1, before, 0.0), axis=0, keepdims=True).astype(jnp.int32)
    r2_ref[0] = jnp.sum(jnp.where(hit2, before, 0.0), axis=0, keepdims=True).astype(jnp.int32)
    cnt = jnp.sum(assigned, axis=1, keepdims=True).astype(jnp.int32)
    cnt_ref[0] = jnp.broadcast_to(cnt, (n_exp, LANES))


def _post(x2, mix_a, mix_b, w_o, g2, wr_t, br, tri):
    t = x2.shape[0]
    nt = t // MOE_TM
    n_exp = N_EXPERTS
    row = lambda i: (i, 0)
    const = lambda i: (0, 0)
    tok = lambda i: (i, 0, 0)
    tok_spec = pl.BlockSpec((1, 1, MOE_TM), tok)
    tok_shape = jax.ShapeDtypeStruct((nt, 1, MOE_TM), jnp.int32)
    return pl.pallas_call(
        _post_kernel,
        grid=(nt,),
        in_specs=[pl.BlockSpec((MOE_TM, D_MODEL), row), pl.BlockSpec((MOE_TM, A_COLS), row),
                  pl.BlockSpec((MOE_TM, B_COLS), row), pl.BlockSpec((D_MODEL, D_MODEL), const),
                  pl.BlockSpec((1, D_MODEL), const), pl.BlockSpec((ROUTER_ROWS, D_MODEL), const),
                  pl.BlockSpec((ROUTER_ROWS, LANES), const), pl.BlockSpec((MOE_TM, MOE_TM), const)],
        out_specs=[pl.BlockSpec((MOE_TM, D_MODEL), row),
                   pl.BlockSpec((MOE_TM, PACK_W), row), pl.BlockSpec((MOE_TM, PACK_W), row),
                   tok_spec, tok_spec, tok_spec, tok_spec,
                   pl.BlockSpec((MOE_TM, LANES), row),
                   pl.BlockSpec((1, n_exp, LANES), tok)],
        out_shape=[jax.ShapeDtypeStruct((t, D_MODEL), F32),
                   jax.ShapeDtypeStruct((t, PACK_W), jnp.uint32), jax.ShapeDtypeStruct((t, PACK_W), jnp.uint32),
                   tok_shape, tok_shape, tok_shape, tok_shape,
                   jax.ShapeDtypeStruct((t, LANES), F32),
                   jax.ShapeDtypeStruct((nt, n_exp, LANES), jnp.int32)],
        compiler_params=pltpu.CompilerParams(dimension_semantics=("arbitrary",),
                                             vmem_limit_bytes=VMEM_LIMIT),
        name="post",
    )(x2, mix_a, mix_b, w_o, g2, wr_t, br, tri)


def _sc_mesh():
    return plsc.VectorSubcoreMesh(core_axis_name="core", subcore_axis_name="subcore")


def _sc_scatter_rows(x, pos1, pos2, n_out):
    r, c = x.shape

    @pl.kernel(out_type=jax.ShapeDtypeStruct((n_out, c), x.dtype), mesh=_sc_mesh(), scratch_types=[])
    def scatter_kernel(x_hbm, p1_hbm, p2_hbm, o_hbm):
        def body(x_vmem, p1_vmem, p2_vmem):
            pltpu.sync_copy(x_vmem, o_hbm.at[p1_vmem.at[0]])
            pltpu.sync_copy(x_vmem, o_hbm.at[p2_vmem.at[0]])

        idx = pl.BlockSpec((1, SC_WINDOW), lambda i: (0, i))
        pltpu.emit_pipeline(body, grid=(r // SC_WINDOW,),
                            in_specs=[pl.BlockSpec((SC_WINDOW, c), lambda i: (i, 0)), idx, idx], out_specs=[],
                            core_axis_name=("core", "subcore"),
                            dimension_semantics=(pltpu.PARALLEL,))(x_hbm, p1_hbm, p2_hbm)

    return scatter_kernel(x, pos1.reshape(1, r), pos2.reshape(1, r))


def _sc_gather_rows(y, idx):
    n = idx.shape[0]
    c = y.shape[1]

    @pl.kernel(out_type=jax.ShapeDtypeStruct((n, c), y.dtype), mesh=_sc_mesh(), scratch_types=[])
    def gather_kernel(y_hbm, i_hbm, o_hbm):
        def body(i_vmem, o_vmem):
            pltpu.sync_copy(y_hbm.at[i_vmem.at[0]], o_vmem)

        pltpu.emit_pipeline(body, grid=(n // SC_WINDOW,),
                            in_specs=[pl.BlockSpec((1, SC_WINDOW), lambda i: (0, i))],
                            out_specs=[pl.BlockSpec((SC_WINDOW, c), lambda i: (i, 0))],
                            core_axis_name=("core", "subcore"),
                            dimension_semantics=(pltpu.PARALLEL,))(i_hbm, o_hbm)

    return gather_kernel(y, idx.reshape(1, n))


def _ffn_kernel(ce_ref, cv_ref, xa_ref, xb_ref, wg_ref, wu_ref, wd_ref, ya_ref, yb_ref, wg_bf, wu_bf, wd_bf):
    c = pl.program_id(0)
    prev = ce_ref[jnp.maximum(c - 1, 0)]

    @pl.when((c == 0) | (ce_ref[c] != prev))
    def _():
        wg_bf[...] = wg_ref[0].astype(BF16)
        wu_bf[...] = wu_ref[0].astype(BF16)
        wd_bf[...] = wd_ref[0].astype(BF16)

    @pl.when(cv_ref[c] == 0)
    def _():
        ya_ref[...] = jnp.zeros_like(ya_ref)
        yb_ref[...] = jnp.zeros_like(yb_ref)

    @pl.when(cv_ref[c] != 0)
    def _():
        a_lo, a_hi = _unpack_bf16_pairs(xa_ref[...])
        b_lo, b_hi = _unpack_bf16_pairs(xb_ref[...])
        xs = jnp.concatenate([a_lo, b_lo, a_hi, b_hi], axis=1).astype(BF16)
        gate = jnp.dot(xs, wg_bf[...], preferred_element_type=F32)
        up = jnp.dot(xs, wu_bf[...], preferred_element_type=F32)
        he = (gate * jax.nn.sigmoid(gate) * up).astype(BF16)
        y = jnp.dot(he, wd_bf[...], preferred_element_type=F32)
        ya_ref[...] = _pack_bf16_pairs(y[:, 0:PACK_W], y[:, 2 * PACK_W:3 * PACK_W])
        yb_ref[...] = _pack_bf16_pairs(y[:, PACK_W:2 * PACK_W], y[:, 3 * PACK_W:])


def _ffn(chunk_expert, chunk_valid, xsa, xsb, wg, wu, wd):
    n_pad = xsa.shape[0]
    rows = pl.BlockSpec((FFN_ROWS, PACK_W), lambda c, ce, cv: (c, 0))
    grid_spec = pltpu.PrefetchScalarGridSpec(
        num_scalar_prefetch=2,
        grid=(n_pad // FFN_ROWS,),
        in_specs=[rows, rows,
                  pl.BlockSpec((1, D_MODEL, D_FF_EXPERT), lambda c, ce, cv: (ce[c], 0, 0)),
                  pl.BlockSpec((1, D_MODEL, D_FF_EXPERT), lambda c, ce, cv: (ce[c], 0, 0)),
                  pl.BlockSpec((1, D_FF_EXPERT, D_MODEL), lambda c, ce, cv: (ce[c], 0, 0))],
        out_specs=[rows, rows],
        scratch_shapes=[pltpu.VMEM((D_MODEL, D_FF_EXPERT), BF16), pltpu.VMEM((D_MODEL, D_FF_EXPERT), BF16),
                        pltpu.VMEM((D_FF_EXPERT, D_MODEL), BF16)],
    )
    out = jax.ShapeDtypeStruct((n_pad, PACK_W), jnp.uint32)
    return pl.pallas_call(
        _ffn_kernel,
        grid_spec=grid_spec,
        out_shape=[out, out],
        compiler_params=pltpu.CompilerParams(dimension_semantics=("arbitrary",),
                                             vmem_limit_bytes=VMEM_LIMIT),
        name="ffn",
    )(chunk_expert, chunk_valid, xsa, xsb, wg, wu, wd)


def _final_kernel(x1_ref, y1a_ref, y1b_ref, y2a_ref, y2b_ref, w_ref, gf_ref, o_ref):
    w1 = w_ref[:, 0:1]
    w2 = w_ref[:, 1:2]
    a1_lo, a1_hi = _unpack_bf16_pairs(y1a_ref[...])
    b1_lo, b1_hi = _unpack_bf16_pairs(y1b_ref[...])
    a2_lo, a2_hi = _unpack_bf16_pairs(y2a_ref[...])
    b2_lo, b2_hi = _unpack_bf16_pairs(y2b_ref[...])
    y = jnp.concatenate([w1 * a1_lo + w2 * a2_lo, w1 * b1_lo + w2 * b2_lo,
                         w1 * a1_hi + w2 * a2_hi, w1 * b1_hi + w2 * b2_hi], axis=1)
    xo = x1_ref[...] + y
    ms = jnp.mean(xo * xo, axis=-1, keepdims=True)
    o_ref[...] = xo * lax.rsqrt(ms + EPS) * gf_ref[...]


def _final(x1, yga, ygb, wtok, gf):
    t = x1.shape[0]
    row = lambda i: (i, 0)
    one = pl.BlockSpec((None, FINAL_TM, PACK_W), lambda i: (0, i, 0))
    two = pl.BlockSpec((None, FINAL_TM, PACK_W), lambda i: (1, i, 0))
    return pl.pallas_call(
        _final_kernel,
        grid=(t // FINAL_TM,),
        in_specs=[pl.BlockSpec((FINAL_TM, D_MODEL), row), one, one, two, two,
                  pl.BlockSpec((FINAL_TM, LANES), row), pl.BlockSpec((1, D_MODEL), lambda i: (0, 0))],
        out_specs=pl.BlockSpec((FINAL_TM, D_MODEL), row),
        out_shape=jax.ShapeDtypeStruct((t, D_MODEL), F32),
        compiler_params=pltpu.CompilerParams(dimension_semantics=("arbitrary",),
                                             vmem_limit_bytes=VMEM_LIMIT),
        name="final",
    )(x1, yga, ygb, yga, ygb, wtok, gf)


def _moe_plan(cnt, e1, e2, r1, r2):
    nt, n_exp = cnt.shape
    tot = jnp.sum(cnt, axis=0)
    padded = (tot + FFN_ROWS - 1) // FFN_ROWS * FFN_ROWS
    seg_end = jnp.cumsum(padded)
    seg_start = seg_end - padded
    off = seg_start[None, :] + jnp.cumsum(cnt, axis=0) - cnt
    experts = jnp.arange(n_exp, dtype=jnp.int32)

    def lookup(e):
        return jnp.sum(jnp.where(e[:, :, None] == experts, off[:, None, :], 0), axis=2)

    pos1 = (lookup(e1) + r1).reshape(-1).astype(jnp.int32)
    pos2 = (lookup(e2) + r2).reshape(-1).astype(jnp.int32)
    n_chunks = (nt * MOE_TM * 2 + n_exp * FFN_ROWS) // FFN_ROWS
    cstart = jnp.arange(n_chunks, dtype=jnp.int32) * FFN_ROWS
    ce = jnp.sum((seg_end[None, :] <= cstart[:, None]).astype(jnp.int32), axis=1)
    cec = jnp.minimum(ce, n_exp - 1)
    seg_used = seg_start + tot
    valid = jnp.any((ce[:, None] == experts[None, :]) & (cstart[:, None] < seg_used[None, :]), axis=1)
    return pos1, pos2, cec.astype(jnp.int32), valid.astype(jnp.int32)


def _rope_tables(pos):
    half = ROT_DIM // 2
    inv_freq = ROPE_THETA ** (-jnp.arange(0, ROT_DIM, 2, dtype=F32) / ROT_DIM)
    ang = pos.astype(F32)[:, None] * inv_freq[None, :]
    cos, sin = jnp.cos(ang), jnp.sin(ang)
    l64 = np.arange(LANES) % HEAD_DIM
    f = l64 % half
    first = jnp.asarray(l64 < half)[None, :]
    second = jnp.asarray((l64 >= half) & (l64 < ROT_DIM))[None, :]
    c = jnp.where(first | second, cos[:, f], 1.0)
    s1 = jnp.where(first, -sin[:, f], 0.0)
    s2 = jnp.where(second, sin[:, f], 0.0)
    return c.astype(F32), s1.astype(F32), s2.astype(F32)


def _overlap_table_t():
    cs = np.arange(N_CMP)[None, :] * CMP_STRIDE
    js = np.arange(N_SELBLK)[:, None] * SEL_LEN
    ov = np.clip(np.minimum(cs + CMP_LEN, js + SEL_LEN) - np.maximum(cs, js), 0, None) / CMP_LEN
    ov[:, N_CMP - 1] = 0.0
    return ov.astype(np.float32)


def _expand_table_t():
    out = np.zeros((SEQ, LANES), np.float32)
    out[np.arange(SEQ), np.arange(SEQ) // SEL_LEN] = -NEG
    return out.reshape(N_QBLK // SEL_GROUP, SEL_GROUP * QBLK, LANES)


def kernel(x, norm1_g, w_in, pe_kc, w_kc1, w_kc2, pe_vc, w_vc1, w_vc2, w_o, norm2_g, w_rg, b_rg, w_re, b_re,
           w_gate, w_up, w_down, norm_f_g):
    nb, s, d = x.shape
    assert (s, d) == (SEQ, D_MODEL) and norm1_g.shape[0] == 1
    t = nb * s
    x2 = x.reshape(t, d)

    w_in_p = jnp.pad(w_in[0], ((0, 0), (0, D_IN_PAD - w_in.shape[2]))).astype(BF16)
    rc, rs1, rs2 = _rope_tables(jnp.arange(SEQ))
    cc, cs1, cs2 = _rope_tables(jnp.arange(N_CMP) * CMP_STRIDE + CMP_LEN - 1)
    half_flat = CMP_STRIDE * HEAD_DIM

    def cmp_w1(w1):
        return jnp.concatenate([w1[:half_flat], w1[half_flat:]], axis=1).astype(BF16)

    def cmp_pe(pe):
        rows = pe.reshape(2, half_flat)
        return jnp.concatenate([jnp.broadcast_to(rows[0:1], (PE_ROWS, half_flat)),
                                jnp.broadcast_to(rows[1:2], (PE_ROWS, half_flat))], axis=0).astype(BF16)

    wk2d = jnp.concatenate([w_kc2[0], w_kc2[0]], axis=1).astype(BF16)
    wv2t = w_vc2[0].T.astype(BF16)
    ovl_t = jnp.asarray(_overlap_table_t(), BF16)
    expand_t = jnp.asarray(_expand_table_t(), BF16)

    (aq, ak, av, aq4, ak4, av4, aq16, ak16, av16,
     bq, kvflat, ksw, gates) = _proj(x2, norm1_g, w_in_p, rc, rs1, rs2)
    mix_a = _dilated((aq, ak, av), (aq4, ak4, av4), (aq16, ak16, av16))

    mix_b = _nsa(bq, kvflat, ksw, gates, cmp_w1(w_kc1[0]), wk2d, cmp_pe(pe_kc[0]),
                 cmp_w1(w_vc1[0]), wv2t, cmp_pe(pe_vc[0]), cc, cs1, cs2, ovl_t, expand_t)

    wr_t = jnp.pad(jnp.concatenate([w_rg[0].T, w_re[0].reshape(d, -1).T], axis=0),
                   ((0, ROUTER_ROWS - N_ROUTER), (0, 0)))
    br = jnp.broadcast_to(jnp.pad(jnp.concatenate([b_rg[0], b_re[0].reshape(-1)]),
                                  (0, ROUTER_ROWS - N_ROUTER))[:, None], (ROUTER_ROWS, LANES))
    tri = jnp.asarray(np.triu(np.ones((MOE_TM, MOE_TM), np.float32), 1), BF16)
    n_pad = 2 * t + N_GROUPS * EXPERTS_PER_GROUP * FFN_ROWS
    x1, hpa, hpb, e1, e2, r1, r2, wtok, cnt = _post(x2, mix_a, mix_b, w_o[0].astype(BF16), norm2_g, wr_t, br, tri)
    pos1, pos2, chunk_expert, chunk_valid = _moe_plan(cnt[:, :, 0], e1[:, 0], e2[:, 0], r1[:, 0], r2[:, 0])
    xsa = _sc_scatter_rows(hpa, pos1, pos2, n_pad)
    xsb = _sc_scatter_rows(hpb, pos1, pos2, n_pad)
    ysa, ysb = _ffn(chunk_expert, chunk_valid, xsa, xsb, w_gate[0], w_up[0], w_down[0])
    pos12 = jnp.concatenate([pos1, pos2])
    yga = _sc_gather_rows(ysa, pos12)
    ygb = _sc_gather_rows(ysb, pos12)
    out = _final(x1, yga.reshape(2, t, PACK_W), ygb.reshape(2, t, PACK_W), wtok, norm_f_g.reshape(1, d))
    return out.reshape(nb, s, d)
```
